```python
import math
import jax, jax.numpy as jnp
from jax import lax
import numpy as np

D_MODEL = 1024
BATCH = 4
SEQ = 4096
DEPTH = 1

ATTN_HEADS = 8
HEAD_DIM = 64
ATTN_WIDTH = ATTN_HEADS * HEAD_DIM
MOBA_BLOCK = 256
MOBA_TOPK = 3
Q_CHUNK = 64
NUM_BUCKETS = 32
MAX_DISTANCE = 128
SSM_WIDTH = 512
SSM_GROUP = 16
SSM_GROUPS = SSM_WIDTH // SSM_GROUP
SSM_STATE = 64
DT_MIN = 0.001
DT_MAX = 0.1
IN_SPLITS = [ATTN_WIDTH, 2 * ATTN_WIDTH, 3 * ATTN_WIDTH, 3 * ATTN_WIDTH + SSM_WIDTH,
             3 * ATTN_WIDTH + SSM_WIDTH + D_MODEL]
IN_WIDTH = 3 * ATTN_WIDTH + SSM_WIDTH + 2 * D_MODEL
N_EXPERTS = 256
TOP_K = 8
N_EXPERT_GROUPS = 8
TOPK_GROUPS = 4
EXPERT_DIM = 256
SHARED_DIM = 256
ROUTED_SCALE = 2.5
EXPERT_BLOCK = 128
EPS = 1e-6
MASK_VALUE = -1e30

kernel_name = "griffin_moba_s5_moe_hybrid"


def rms_norm(x, g):
    xf = x.astype(jnp.float32)
    y = xf * lax.rsqrt(jnp.mean(xf * xf, axis=-1, keepdims=True) + EPS)
    return (y * g.astype(jnp.float32)).astype(x.dtype)


def t5_bucket(rel):
    n = jnp.maximum(rel, 0)
    max_exact = NUM_BUCKETS // 2
    nf = jnp.maximum(n, 1).astype(jnp.float32)
    large = max_exact + (jnp.log(nf / max_exact) / math.log(MAX_DISTANCE / max_exact)
                         * (NUM_BUCKETS - max_exact)).astype(jnp.int32)
    large = jnp.minimum(large, NUM_BUCKETS - 1)
    return jnp.where(n < max_exact, n, large)


def moba_attention(q, k, v, rel_bias):
    B, H, S, Dh = q.shape
    nb = -(-S // MOBA_BLOCK)
    pad = nb * MOBA_BLOCK - S
    kp = jnp.pad(k, ((0, 0), (0, 0), (0, pad), (0, 0)))
    vp = jnp.pad(v, ((0, 0), (0, 0), (0, pad), (0, 0)))
    kb = kp.reshape(B, H, nb, MOBA_BLOCK, Dh)
    vb = vp.reshape(B, H, nb, MOBA_BLOCK, Dh)
    kmean = jnp.mean(kb.astype(jnp.float32), axis=3)
    n_sel = min(MOBA_TOPK, nb)
    scale = Dh ** -0.5
    bidx = jnp.arange(B)[:, None, None, None]
    hidx = jnp.arange(H)[None, :, None, None]
    blk_off = jnp.arange(MOBA_BLOCK)

    def chunk(ci):
        start = ci * Q_CHUNK
        qc = lax.dynamic_slice_in_dim(q, start, Q_CHUNK, axis=2)
        qpos = start + jnp.arange(Q_CHUNK)
        own = start // MOBA_BLOCK
        gate = jnp.einsum('bhqd,bhnd->bhqn', qc.astype(jnp.float32), kmean)
        gate = jnp.where(jnp.arange(nb) < own, gate, MASK_VALUE)
        _, sel = lax.top_k(gate, n_sel)
        valid = sel < own
        ksel = kb[bidx, hidx, sel]
        vsel = vb[bidx, hidx, sel]
        s_sel = jnp.einsum('bhqd,bhqjkd->bhqjk', qc, ksel).astype(jnp.float32) * scale
        kpos_sel = sel[..., None] * MOBA_BLOCK + blk_off
        bucket_sel = t5_bucket(qpos[None, None, :, None, None] - kpos_sel)
        bias_sel = rel_bias[bucket_sel, hidx[..., None]].astype(jnp.float32)
        s_sel = jnp.where(valid[..., None], s_sel + bias_sel, MASK_VALUE)
        kown = lax.dynamic_index_in_dim(kb, own, axis=2, keepdims=False)
        vown = lax.dynamic_index_in_dim(vb, own, axis=2, keepdims=False)
        s_own = jnp.einsum('bhqd,bhkd->bhqk', qc, kown).astype(jnp.float32) * scale
        rel_own = qpos[:, None] - (own * MOBA_BLOCK + blk_off)[None, :]
        bias_own = rel_bias[t5_bucket(rel_own)].astype(jnp.float32).transpose(2, 0, 1)
        s_own = jnp.where(rel_own >= 0, s_own + bias_own, MASK_VALUE)
        logits = jnp.concatenate(
            [s_sel.reshape(B, H, Q_CHUNK, n_sel * MOBA_BLOCK), s_own], axis=-1)
        p = jax.nn.softmax(logits, axis=-1).astype(v.dtype)
        p_sel = p[..., :n_sel * MOBA_BLOCK].reshape(B, H, Q_CHUNK, n_sel, MOBA_BLOCK)
        p_own = p[..., n_sel * MOBA_BLOCK:]
        return (jnp.einsum('bhqjk,bhqjkd->bhqd', p_sel, vsel)
                + jnp.einsum('bhqk,bhkd->bhqd', p_own, vown))

    out = lax.map(chunk, jnp.arange(S // Q_CHUNK))
    return out.transpose(1, 0, 3, 2, 4).reshape(B, S, H * Dh)


def _complex_affine_combine(e1, e2):
    a1r, a1i, b1r, b1i = e1
    a2r, a2i, b2r, b2i = e2
    return (a2r * a1r - a2i * a1i,
            a2r * a1i + a2i * a1r,
            a2r * b1r - a2i * b1i + b2r,
            a2r * b1i + a2i * b1r + b2i)


def s5_branch(u, lambda_re, lambda_im, log_dt, b_re, b_im, c_re, c_im, d_skip, w_glu, b_glu):
    B, S, _ = u.shape
    f32 = jnp.float32
    uf = u.astype(f32)
    ug = uf.reshape(B, S, SSM_GROUPS, SSM_GROUP)
    lam_re = jnp.minimum(lambda_re.astype(f32), -1e-4)
    lam_im = lambda_im.astype(f32)
    dt = jnp.exp(log_dt.astype(f32))[:, None]
    mag = jnp.exp(lam_re * dt)
    a_re = mag * jnp.cos(lam_im * dt)
    a_im = mag * jnp.sin(lam_im * dt)
    den = lam_re * lam_re + lam_im * lam_im
    nr = a_re - 1.0
    coef_re = (nr * lam_re + a_im * lam_im) / den
    coef_im = (a_im * lam_re - nr * lam_im) / den
    br = b_re.astype(f32)
    bi = b_im.astype(f32)
    bbar_re = coef_re[..., None] * br - coef_im[..., None] * bi
    bbar_im = coef_re[..., None] * bi + coef_im[..., None] * br
    bu_re = jnp.einsum('bsgc,gpc->bsgp', ug, bbar_re)
    bu_im = jnp.einsum('bsgc,gpc->bsgp', ug, bbar_im)
    a_re_b = jnp.broadcast_to(a_re, bu_re.shape)
    a_im_b = jnp.broadcast_to(a_im, bu_re.shape)
    _, _, h_re, h_im = lax.associative_scan(
        _complex_affine_combine, (a_re_b, a_im_b, bu_re, bu_im), axis=1)
    y = (jnp.einsum('gcp,bsgp->bsgc', c_re.astype(f32), h_re)
         - jnp.einsum('gcp,bsgp->bsgc', c_im.astype(f32), h_im))
    y = y.reshape(B, S, SSM_WIDTH) + d_skip.astype(f32) * uf
    g = jax.nn.gelu(y)
    out = g * jax.nn.sigmoid(g @ w_glu.astype(f32) + b_glu.astype(f32))
    return out.astype(u.dtype)


def routed_experts(xf, top_idx, top_w, w_gate, w_up, w_down):
    n_tok = xf.shape[0]
    n_assign = n_tok * TOP_K
    flat_e = top_idx.reshape(n_assign).astype(jnp.int32)
    flat_tok = jnp.arange(n_assign, dtype=jnp.int32) // TOP_K
    flat_w = top_w.reshape(n_assign)
    order = jnp.argsort(flat_e, stable=True)
    se = flat_e[order]
    stok = flat_tok[order]
    sw = flat_w[order]
    counts = jnp.bincount(flat_e, length=N_EXPERTS)
    offs = jnp.cumsum(counts) - counts
    padded = (counts + EXPERT_BLOCK - 1) // EXPERT_BLOCK * EXPERT_BLOCK
    pend = jnp.cumsum(padded)
    poffs = pend - padded
    dest = poffs[se] + jnp.arange(n_assign, dtype=jnp.int32) - offs[se]
    n_blocks = -(-n_assign // EXPERT_BLOCK) + N_EXPERTS
    rows = n_blocks * EXPERT_BLOCK
    blk_tok = jnp.zeros((rows,), jnp.int32).at[dest].set(stok).reshape(n_blocks, EXPERT_BLOCK)
    blk_w = jnp.zeros((rows,), xf.dtype).at[dest].set(sw).reshape(n_blocks, EXPERT_BLOCK)
    blk_e = jnp.minimum(
        jnp.searchsorted(pend, jnp.arange(n_blocks) * EXPERT_BLOCK, side='right'),
        N_EXPERTS - 1).astype(jnp.int32)

    def body(acc, blk):
        tok, wgt, e = blk
        xb = xf[tok]
        hb = jax.nn.silu(xb @ w_gate[e]) * (xb @ w_up[e])
        return acc.at[tok].add((hb @ w_down[e]) * wgt[:, None]), None

    out, _ = lax.scan(body, jnp.zeros_like(xf), (blk_tok, blk_w, blk_e))
    return out


def moe_ffn(h, w_router, router_bias, w_exp_gate, w_exp_up, w_exp_down,
            w_sh_gate, w_sh_up, w_sh_down):
    B, S, D = h.shape
    n_tok = B * S
    xf = h.reshape(n_tok, D)
    scores = jax.nn.sigmoid((xf @ w_router).astype(jnp.float32))
    biased = scores + router_bias.astype(jnp.float32)
    grouped = biased.reshape(n_tok, N_EXPERT_GROUPS, N_EXPERTS // N_EXPERT_GROUPS)
    group_score = jnp.sum(lax.top_k(grouped, 2)[0], axis=-1)
    _, top_groups = lax.top_k(group_score, TOPK_GROUPS)
    group_mask = jnp.sum(jax.nn.one_hot(top_groups, N_EXPERT_GROUPS, dtype=jnp.float32), axis=1) > 0
    expert_mask = jnp.repeat(group_mask, N_EXPERTS // N_EXPERT_GROUPS, axis=1)
    _, top_idx = lax.top_k(jnp.where(expert_mask, biased, MASK_VALUE), TOP_K)
    top_w = jnp.take_along_axis(scores, top_idx, axis=-1)
    top_w = top_w / jnp.sum(top_w, axis=-1, keepdims=True) * ROUTED_SCALE
    routed = routed_experts(xf, top_idx, top_w.astype(xf.dtype), w_exp_gate, w_exp_up, w_exp_down)
    shared = (jax.nn.silu(xf @ w_sh_gate) * (xf @ w_sh_up)) @ w_sh_down
    return (routed + shared).reshape(B, S, D)


def hybrid_layer(x, cond, rel_bias, w_ada, b_ada, ln1_g, w_in, q_norm_g, k_norm_g,
                 ssm_lambda_re, ssm_lambda_im, ssm_log_dt, ssm_b_re, ssm_b_im, ssm_c_re, ssm_c_im,
                 ssm_d, ssm_w_glu, ssm_b_glu, w_up_attn, w_up_ssm, w_out, ln2_g,
                 w_router, router_bias, w_exp_gate, w_exp_up, w_exp_down,
                 w_sh_gate, w_sh_up, w_sh_down):
    B, S, D = x.shape
    mod = jax.nn.silu(cond) @ w_ada + b_ada
    sh1, sc1, g1, sh2, sc2, g2 = jnp.split(mod[:, None, :], 6, axis=-1)
    h = rms_norm(x, ln1_g) * (1.0 + sc1) + sh1
    proj = h @ w_in
    q, k, v, u, ga, gs = jnp.split(proj, IN_SPLITS, axis=-1)
    q = rms_norm(q.reshape(B, S, ATTN_HEADS, HEAD_DIM), q_norm_g).transpose(0, 2, 1, 3)
    k = rms_norm(k.reshape(B, S, ATTN_HEADS, HEAD_DIM), k_norm_g).transpose(0, 2, 1, 3)
    v = v.reshape(B, S, ATTN_HEADS, HEAD_DIM).transpose(0, 2, 1, 3)
    y_attn = moba_attention(q, k, v, rel_bias) @ w_up_attn
    y_ssm = s5_branch(u, ssm_lambda_re, ssm_lambda_im, ssm_log_dt, ssm_b_re, ssm_b_im,
                      ssm_c_re, ssm_c_im, ssm_d, ssm_w_glu, ssm_b_glu) @ w_up_ssm
    mixed = jax.nn.sigmoid(ga) * y_attn + jax.nn.sigmoid(gs) * y_ssm
    x = x + g1 * (mixed @ w_out)
    h2 = rms_norm(x, ln2_g) * (1.0 + sc2) + sh2
    y = moe_ffn(h2, w_router, router_bias, w_exp_gate, w_exp_up, w_exp_down,
                w_sh_gate, w_sh_up, w_sh_down)
    return x + g2 * y


def setup_inputs(seed: int = 0) -> dict:
    key = jax.random.key(seed)
    ks = jax.random.split(key, 40)
    L = DEPTH
    f32 = jnp.float32

    def nrm(k, shape, scale):
        return jax.random.normal(k, shape, f32) * scale

    n_idx = jnp.arange(SSM_STATE, dtype=f32)
    return {
        "x": nrm(ks[0], (BATCH, SEQ, D_MODEL), 1.0),
        "c": nrm(ks[1], (BATCH, D_MODEL), 1.0),
        "rel_bias": nrm(ks[2], (NUM_BUCKETS, ATTN_HEADS), 0.1),
        "w_ada": nrm(ks[3], (L, D_MODEL, 6 * D_MODEL), 0.5 * D_MODEL ** -0.5),
        "b_ada": nrm(ks[4], (L, 6 * D_MODEL), 0.02),
        "ln1_g": 1.0 + nrm(ks[5], (L, D_MODEL), 0.02),
        "w_in": nrm(ks[6], (L, D_MODEL, IN_WIDTH), D_MODEL ** -0.5),
        "q_norm_g": 1.0 + nrm(ks[7], (L, HEAD_DIM), 0.02),
        "k_norm_g": 1.0 + nrm(ks[8], (L, HEAD_DIM), 0.02),
        "ssm_lambda_re": -0.5 + nrm(ks[9], (L, SSM_GROUPS, SSM_STATE), 0.01),
        "ssm_lambda_im": jnp.pi * n_idx + nrm(ks[10], (L, SSM_GROUPS, SSM_STATE), 0.01),
        "ssm_log_dt": jax.random.uniform(ks[11], (L, SSM_GROUPS), f32,
                                         minval=math.log(DT_MIN), maxval=math.log(DT_MAX)),
        "ssm_b_re": nrm(ks[12], (L, SSM_GROUPS, SSM_STATE, SSM_GROUP), (2 * SSM_GROUP) ** -0.5),
        "ssm_b_im": nrm(ks[13], (L, SSM_GROUPS, SSM_STATE, SSM_GROUP), (2 * SSM_GROUP) ** -0.5),
        "ssm_c_re": nrm(ks[14], (L, SSM_GROUPS, SSM_GROUP, SSM_STATE), SSM_STATE ** -0.5),
        "ssm_c_im": nrm(ks[15], (L, SSM_GROUPS, SSM_GROUP, SSM_STATE), SSM_STATE ** -0.5),
        "ssm_d": nrm(ks[16], (L, SSM_WIDTH), 1.0),
        "ssm_w_glu": nrm(ks[17], (L, SSM_WIDTH, SSM_WIDTH), SSM_WIDTH ** -0.5),
        "ssm_b_glu": nrm(ks[18], (L, SSM_WIDTH), 0.02),
        "w_up_attn": nrm(ks[19], (L, ATTN_WIDTH, D_MODEL), ATTN_WIDTH ** -0.5),
        "w_up_ssm": nrm(ks[20], (L, SSM_WIDTH, D_MODEL), SSM_WIDTH ** -0.5),
        "w_out": nrm(ks[21], (L, D_MODEL, D_MODEL), D_MODEL ** -0.5),
        "ln2_g": 1.0 + nrm(ks[22], (L, D_MODEL), 0.02),
        "w_router": nrm(ks[23], (L, D_MODEL, N_EXPERTS), D_MODEL ** -0.5),
        "router_bias": nrm(ks[24], (L, N_EXPERTS), 0.01),
        "w_exp_gate": nrm(ks[25], (L, N_EXPERTS, D_MODEL, EXPERT_DIM), D_MODEL ** -0.5),
        "w_exp_up": nrm(ks[26], (L, N_EXPERTS, D_MODEL, EXPERT_DIM), D_MODEL ** -0.5),
        "w_exp_down": nrm(ks[27], (L, N_EXPERTS, EXPERT_DIM, D_MODEL), EXPERT_DIM ** -0.5),
        "w_sh_gate": nrm(ks[28], (L, D_MODEL, SHARED_DIM), D_MODEL ** -0.5),
        "w_sh_up": nrm(ks[29], (L, D_MODEL, SHARED_DIM), D_MODEL ** -0.5),
        "w_sh_down": nrm(ks[30], (L, SHARED_DIM, D_MODEL), SHARED_DIM ** -0.5),
    }


def reference(x, c, rel_bias, w_ada, b_ada, ln1_g, w_in, q_norm_g, k_norm_g,
              ssm_lambda_re, ssm_lambda_im, ssm_log_dt, ssm_b_re, ssm_b_im, ssm_c_re, ssm_c_im,
              ssm_d, ssm_w_glu, ssm_b_glu, w_up_attn, w_up_ssm, w_out, ln2_g,
              w_router, router_bias, w_exp_gate, w_exp_up, w_exp_down,
              w_sh_gate, w_sh_up, w_sh_down):
    for l in range(DEPTH):
        x = hybrid_layer(x, c, rel_bias, w_ada[l], b_ada[l], ln1_g[l], w_in[l],
                         q_norm_g[l], k_norm_g[l],
                         ssm_lambda_re[l], ssm_lambda_im[l], ssm_log_dt[l], ssm_b_re[l],
                         ssm_b_im[l], ssm_c_re[l], ssm_c_im[l], ssm_d[l], ssm_w_glu[l],
                         ssm_b_glu[l], w_up_attn[l], w_up_ssm[l], w_out[l], ln2_g[l],
                         w_router[l], router_bias[l], w_exp_gate[l], w_exp_up[l],
                         w_exp_down[l], w_sh_gate[l], w_sh_up[l], w_sh_down[l])
    return x
```

```python
import functools
import math

import numpy as np
import jax
import jax.numpy as jnp
from jax import lax
from jax.experimental import pallas as pl
from jax.experimental.pallas import tpu as pltpu

F32 = jnp.float32
BF16 = jnp.bfloat16

ATTN_HEADS = 8
HEAD_DIM = 64
ATTN_WIDTH = ATTN_HEADS * HEAD_DIM
MOBA_BLOCK = 256
MOBA_TOPK = 3
NUM_BUCKETS = 32
MAX_DISTANCE = 128
SSM_WIDTH = 512
SSM_GROUP = 16
SSM_GROUPS = SSM_WIDTH // SSM_GROUP
SSM_STATE = 64
N_EXPERTS = 256
TOP_K = 8
N_EXPERT_GROUPS = 8
TOPK_GROUPS = 4
GROUP_SIZE = N_EXPERTS // N_EXPERT_GROUPS
EXPERT_DIM = 256
ROUTED_SCALE = 2.5
EPS = 1e-6
MASK_VALUE = -1e30

LANES = 128
HEADS_PER_STEP = LANES // HEAD_DIM
SSM_CHUNK = 16
ROW_BLOCK = 256
TOKEN_TILE = 256
ROUTE_TILE = 256
GATHER_CHUNK = 512
COMBINE_TILE = 128
VMEM_LIMIT = 56 * 1024 * 1024


def _cparams(n_axes, vmem=VMEM_LIMIT):
    return pltpu.CompilerParams(dimension_semantics=("arbitrary",) * n_axes, vmem_limit_bytes=vmem)


def _sigmoid(x):
    return 1.0 / (1.0 + jnp.exp(-x))


def _silu(x):
    return x * _sigmoid(x)


def _bdot(a, b):
    return jnp.dot(a.astype(BF16), b.astype(BF16), preferred_element_type=F32)


def _bdot_nt(a, b):
    return lax.dot_general(a.astype(BF16), b.astype(BF16), (((1,), (1,)), ((), ())),
                           preferred_element_type=F32)


def _adaln_kernel(c_ref, w_ref, b_ref, o_ref):
    o_ref[...] = _bdot(_silu(c_ref[...]), w_ref[...]) + b_ref[...]


def _adaln(c, w_ada, b_ada):
    bsz, d = c.shape
    n_out = w_ada.shape[1]
    return pl.pallas_call(
        _adaln_kernel,
        out_shape=jax.ShapeDtypeStruct((bsz, n_out), F32),
        grid=(n_out // d,),
        in_specs=[pl.BlockSpec((bsz, d), lambda j: (0, 0)),
                  pl.BlockSpec((d, d), lambda j: (0, j)),
                  pl.BlockSpec((1, d), lambda j: (0, j))],
        out_specs=pl.BlockSpec((bsz, d), lambda j: (0, j)),
        compiler_params=_cparams(1),
        name="adaln",
    )(c, w_ada, b_ada.reshape(1, n_out))


def _modulated_norm(x, gain, shift, scale):
    y = x * lax.rsqrt(jnp.mean(x * x, axis=-1, keepdims=True) + EPS) * gain
    return y * (1.0 + scale) + shift


def _head_norm(t, seg, gain):
    ms = _bdot(t * t, seg)
    return t * lax.rsqrt(ms + EPS) * gain


def _inproj_kernel(x_ref, mod_ref, ln_ref, w_ref, seg_ref, qg_ref, kg_ref,
                   q_ref, k_ref, v_ref, u_ref, ga_ref, gs_ref, km_ref):
    aw, sw, d = ATTN_WIDTH, SSM_WIDTH, x_ref.shape[1]
    h = _modulated_norm(x_ref[...], ln_ref[...], mod_ref[0, 0:1, :], mod_ref[0, 1:2, :]).astype(BF16)
    seg = seg_ref[...]
    q = jnp.dot(h, w_ref[:, 0:aw], preferred_element_type=F32)
    q_ref[...] = _head_norm(q, seg, qg_ref[...])
    k = jnp.dot(h, w_ref[:, aw:2 * aw], preferred_element_type=F32)
    kn = _head_norm(k, seg, kg_ref[...])
    k_ref[...] = kn.astype(BF16)
    km_ref[0] = jnp.mean(kn, axis=0, keepdims=True)
    v_ref[...] = jnp.dot(h, w_ref[:, 2 * aw:3 * aw], preferred_element_type=F32).astype(BF16)
    o = 3 * aw
    u_ref[...] = jnp.dot(h, w_ref[:, o:o + sw], preferred_element_type=F32).astype(BF16)
    o += sw
    ga_ref[...] = jnp.dot(h, w_ref[:, o:o + d], preferred_element_type=F32).astype(BF16)
    o += d
    gs_ref[...] = jnp.dot(h, w_ref[:, o:o + d], preferred_element_type=F32).astype(BF16)


def _inproj(xf, mod3, ln1_g, w_in_b, q_gain, k_gain, seq):
    n, d = xf.shape
    tm = MOBA_BLOCK
    tiles_per_seq = seq // tm
    aw, sw = ATTN_WIDTH, SSM_WIDTH
    head_of_lane = np.arange(aw) // HEAD_DIM
    seg = jnp.asarray((head_of_lane[:, None] == head_of_lane[None, :]) / HEAD_DIM, BF16)
    row = lambda i: (i, 0)
    const = lambda i: (0, 0)
    return pl.pallas_call(
        _inproj_kernel,
        out_shape=(jax.ShapeDtypeStruct((n, aw), F32),
                   jax.ShapeDtypeStruct((n, aw), BF16),
                   jax.ShapeDtypeStruct((n, aw), BF16),
                   jax.ShapeDtypeStruct((n, sw), BF16),
                   jax.ShapeDtypeStruct((n, d), BF16),
                   jax.ShapeDtypeStruct((n, d), BF16),
                   jax.ShapeDtypeStruct((n // tm, 1, aw), F32)),
        grid=(n // tm,),
        in_specs=[pl.BlockSpec((tm, d), row),
                  pl.BlockSpec((1, 6, d), lambda i: (i // tiles_per_seq, 0, 0)),
                  pl.BlockSpec((1, d), const),
                  pl.BlockSpec(w_in_b.shape, const),
                  pl.BlockSpec((aw, aw), const),
                  pl.BlockSpec((1, aw), const),
                  pl.BlockSpec((1, aw), const)],
        out_specs=(pl.BlockSpec((tm, aw), row), pl.BlockSpec((tm, aw), row), pl.BlockSpec((tm, aw), row),
                   pl.BlockSpec((tm, sw), row), pl.BlockSpec((tm, d), row), pl.BlockSpec((tm, d), row),
                   pl.BlockSpec((1, 1, aw), lambda i: (i, 0, 0))),
        compiler_params=_cparams(1),
        name="inproj",
    )(xf, mod3, ln1_g.reshape(1, d), w_in_b, seg, q_gain, k_gain)


def _t5_bucket(rel):
    n = jnp.maximum(rel, 0)
    max_exact = NUM_BUCKETS // 2
    nf = jnp.maximum(n, 1).astype(F32)
    large = max_exact + (jnp.log(nf / max_exact) / math.log(MAX_DISTANCE / max_exact)
                         * (NUM_BUCKETS - max_exact)).astype(jnp.int32)
    large = jnp.minimum(large, NUM_BUCKETS - 1)
    return jnp.where(n < max_exact, n, large)


def _bias_tables(rel_bias):
    blk = MOBA_BLOCK
    rel = jnp.arange(blk)[:, None] - jnp.arange(blk)[None, :]
    own = rel_bias[_t5_bucket(rel)].astype(F32).transpose(2, 0, 1)
    prev = rel_bias[_t5_bucket(rel + blk)].astype(F32).transpose(2, 0, 1)
    assert blk + 1 >= MAX_DISTANCE
    far = rel_bias[NUM_BUCKETS - 1].astype(F32)
    return own, prev, far


def _select_blocks(gate, n_past):
    tq, nb = gate.shape
    blk = lax.broadcasted_iota(jnp.int32, (tq, nb), 1)
    beaten = jnp.zeros((tq, nb), jnp.int32)
    for m in range(nb):
        gm = gate[:, m:m + 1]
        wins = (gm > gate) | ((gm == gate) & (m < blk))
        beaten = beaten + jnp.where(wins & (m < n_past), 1, 0)
    return jnp.where((blk < n_past) & (beaten < MOBA_TOPK), 1.0, 0.0)


def _attn_kernel(far_ref, q_ref, k_ref, v_ref, km_ref, bown_ref, bprev_ref, o_ref):
    hp = pl.program_id(1)
    qi = pl.program_id(2)
    tq = q_ref.shape[0]
    blk = MOBA_BLOCK
    nb = km_ref.shape[1]
    q = q_ref[...]
    lane = lax.broadcasted_iota(jnp.int32, (tq, LANES), 1)
    qpos = lax.broadcasted_iota(jnp.int32, (tq, blk), 0)
    kpos = lax.broadcasted_iota(jnp.int32, (tq, blk), 1)
    blk_lane = lax.broadcasted_iota(jnp.int32, (tq, nb), 1)
    scale = HEAD_DIM ** -0.5
    own_start = pl.multiple_of(qi * blk, blk)
    k_own = k_ref[pl.ds(own_start, blk), :]
    v_own = v_ref[pl.ds(own_start, blk), :]
    outs = []
    for h in range(HEADS_PER_STEP):
        in_head = (lane >= h * HEAD_DIM) & (lane < (h + 1) * HEAD_DIM)
        qm = jnp.where(in_head, q, 0.0)
        gate = lax.dot_general(qm, km_ref[0], (((1,), (1,)), ((), ())),
                               precision=lax.Precision.HIGHEST, preferred_element_type=F32)
        sel = _select_blocks(gate, qi)
        qb = (qm * scale).astype(BF16)
        far_bias = far_ref[hp * HEADS_PER_STEP + h]

        s = _bdot_nt(qb, k_own) + bown_ref[h]
        s = jnp.where(qpos >= kpos, s, MASK_VALUE)
        m0 = jnp.max(s, axis=1, keepdims=True)
        p = jnp.exp(s - m0)
        l0 = jnp.sum(p, axis=1, keepdims=True)
        acc0 = _bdot(p, v_own)

        def past_block(j, carry, h=h, qb=qb, sel=sel, far_bias=far_bias):
            m, l, acc = carry
            start = pl.multiple_of(j * blk, blk)
            kb = k_ref[pl.ds(start, blk), :]
            vb = v_ref[pl.ds(start, blk), :]
            bias = jnp.where(j == qi - 1, bprev_ref[h], far_bias)
            s = _bdot_nt(qb, kb) + bias
            chosen = jnp.sum(jnp.where(blk_lane == j, sel, 0.0), axis=1, keepdims=True)
            s = jnp.where(chosen > 0.5, s, MASK_VALUE)
            m_new = jnp.maximum(m, jnp.max(s, axis=1, keepdims=True))
            alpha = jnp.exp(m - m_new)
            p = jnp.exp(s - m_new)
            l = alpha * l + jnp.sum(p, axis=1, keepdims=True)
            acc = alpha * acc + _bdot(p, vb)
            return m_new, l, acc

        _, l, acc = lax.fori_loop(0, qi, past_block, (m0, l0, acc0))
        outs.append(acc / l)
    o_ref[...] = jnp.where(lane < HEAD_DIM, outs[0], outs[1]).astype(o_ref.dtype)


def _moba_attention(q, k, v, kmean, rel_bias, bsz, seq):
    n, aw = q.shape
    blk = MOBA_BLOCK
    nb = seq // blk
    own, prev, far = _bias_tables(rel_bias)
    hps = HEADS_PER_STEP
    grid_spec = pltpu.PrefetchScalarGridSpec(
        num_scalar_prefetch=1,
        grid=(bsz, aw // LANES, nb),
        in_specs=[pl.BlockSpec((blk, LANES), lambda b, hp, qi, far: (b * nb + qi, hp)),
                  pl.BlockSpec((seq, LANES), lambda b, hp, qi, far: (b, hp)),
                  pl.BlockSpec((seq, LANES), lambda b, hp, qi, far: (b, hp)),
                  pl.BlockSpec((1, nb, LANES), lambda b, hp, qi, far: (b, 0, hp)),
                  pl.BlockSpec((hps, blk, blk), lambda b, hp, qi, far: (hp, 0, 0)),
                  pl.BlockSpec((hps, blk, blk), lambda b, hp, qi, far: (hp, 0, 0))],
        out_specs=pl.BlockSpec((blk, LANES), lambda b, hp, qi, far: (b * nb + qi, hp)),
    )
    return pl.pallas_call(
        _attn_kernel,
        out_shape=jax.ShapeDtypeStruct((n, aw), BF16),
        grid_spec=grid_spec,
        compiler_params=_cparams(3),
        name="moba_attention",
    )(far, q, k, v, kmean.reshape(bsz, nb, aw), own, prev)


def _s5_operators(lambda_re, lambda_im, log_dt, b_re, b_im, c_re, c_im, d_skip, n_chunks):
    hi = lax.Precision.HIGHEST
    L, G, P, C = SSM_CHUNK, SSM_GROUPS, SSM_STATE, SSM_GROUP
    lam_re = jnp.minimum(lambda_re.astype(F32), -1e-4)
    lam_im = lambda_im.astype(F32)
    dt = jnp.exp(log_dt.astype(F32))[:, None]
    z_re, z_im = lam_re * dt, lam_im * dt

    def a_pow(nvec):
        nv = jnp.asarray(nvec, F32)[:, None, None]
        mag = jnp.exp(nv * z_re)
        return mag * jnp.cos(nv * z_im), mag * jnp.sin(nv * z_im)

    a_re, a_im = a_pow([1.0])
    a_re, a_im = a_re[0], a_im[0]
    den = lam_re * lam_re + lam_im * lam_im
    nr = a_re - 1.0
    coef_re = (nr * lam_re + a_im * lam_im) / den
    coef_im = (a_im * lam_re - nr * lam_im) / den
    br, bi = b_re.astype(F32), b_im.astype(F32)
    bbar_re = coef_re[..., None] * br - coef_im[..., None] * bi
    bbar_im = coef_re[..., None] * bi + coef_im[..., None] * br
    cr, ci = c_re.astype(F32), c_im.astype(F32)

    pw_re, pw_im = a_pow(np.arange(L + 1))
    cb_re = cr[None] * pw_re[:, :, None, :] - ci[None] * pw_im[:, :, None, :]
    cb_im = cr[None] * pw_im[:, :, None, :] + ci[None] * pw_re[:, :, None, :]
    kern = (jnp.einsum('jgop,gpi->jgoi', cb_re[:L], bbar_re, precision=hi)
            - jnp.einsum('jgop,gpi->jgoi', cb_im[:L], bbar_im, precision=hi))
    sig = np.arange(L)[:, None]
    tau = np.arange(L)[None, :]
    lag = np.clip(tau - sig, 0, L - 1)
    causal = jnp.asarray((tau >= sig), F32)
    t_op = kern[lag] * causal[:, :, None, None, None]
    t_op = t_op.transpose(2, 0, 4, 1, 3).reshape(G, L * C, L * C)
    d_g = d_skip.astype(F32).reshape(G, C)
    t_op = t_op + jnp.eye(L * C, dtype=F32)[None] * jnp.tile(d_g, (1, L))[:, None, :]

    rp_re, rp_im = pw_re[L - 1 - np.arange(L)], pw_im[L - 1 - np.arange(L)]
    p_re = rp_re[..., None] * bbar_re[None] - rp_im[..., None] * bbar_im[None]
    p_im = rp_re[..., None] * bbar_im[None] + rp_im[..., None] * bbar_re[None]
    p_op = jnp.concatenate([p_re, p_im], axis=2)
    p_op = p_op.transpose(1, 0, 3, 2).reshape(G, L * C, 2 * P)

    q_re = cb_re[1:].transpose(1, 3, 0, 2)
    q_im = -cb_im[1:].transpose(1, 3, 0, 2)
    q_op = jnp.concatenate([q_re, q_im], axis=1).reshape(G, 2 * P, L * C)

    n_steps = max(1, int(math.ceil(math.log2(n_chunks))))
    dk_re, dk_im = a_pow([float(L * 2 ** k) for k in range(n_steps)])
    a1 = jnp.concatenate([dk_re, dk_re], axis=-1).transpose(1, 0, 2)
    a2 = jnp.concatenate([-dk_im, dk_im], axis=-1).transpose(1, 0, 2)
    return t_op.astype(BF16), p_op.astype(BF16), q_op.astype(BF16), a1, a2


def _s5_kernel(u_ref, t_ref, p_ref, q_ref, a1_ref, a2_ref, y_ref, *, n_chunks):
    u = u_ref[0]
    s = jnp.dot(u, p_ref[0], preferred_element_type=F32)
    rows, width = s.shape
    chunk = lax.broadcasted_iota(jnp.int32, (rows, width), 0) % n_chunks
    a1 = a1_ref[0]
    a2 = a2_ref[0]
    x = jnp.where(chunk >= 1, pltpu.roll(s, 1, axis=0), 0.0)
    for kk in range(a1.shape[0]):
        dist = 2 ** kk
        if dist >= n_chunks:
            break
        xs = jnp.where(chunk >= dist, pltpu.roll(x, dist, axis=0), 0.0)
        x = x + a1[kk:kk + 1, :] * xs + a2[kk:kk + 1, :] * pltpu.roll(xs, SSM_STATE, axis=1)
    y = jnp.dot(u, t_ref[0], preferred_element_type=F32)
    y_ref[0] = y + jnp.dot(x.astype(BF16), q_ref[0], preferred_element_type=F32)


def _s5_scan(u, ops, bsz, seq):
    t_op, p_op, q_op, a1, a2 = ops
    L, G, C = SSM_CHUNK, SSM_GROUPS, SSM_GROUP
    nc = seq // L
    rows = bsz * nc
    w = L * C
    ug = u.reshape(rows, L, G, C).transpose(2, 0, 1, 3).reshape(G, rows, w)
    grp = lambda g: (g, 0, 0)
    y = pl.pallas_call(
        functools.partial(_s5_kernel, n_chunks=nc),
        out_shape=jax.ShapeDtypeStruct((G, rows, w), F32),
        grid=(G,),
        in_specs=[pl.BlockSpec((1, rows, w), grp),
                  pl.BlockSpec((1, w, w), grp),
                  pl.BlockSpec((1, w, 2 * SSM_STATE), grp),
                  pl.BlockSpec((1, 2 * SSM_STATE, w), grp),
                  pl.BlockSpec((1,) + a1.shape[1:], grp),
                  pl.BlockSpec((1,) + a2.shape[1:], grp)],
        out_specs=pl.BlockSpec((1, rows, w), grp),
        compiler_params=_cparams(1),
        name="s5_scan",
    )(ug, t_op, p_op, q_op, a1, a2)
    return y.reshape(G, rows, L, C).transpose(1, 2, 0, 3).reshape(bsz * seq, G * C)


def _gelu_tanh(x):
    return 0.5 * x * (1.0 + jnp.tanh(math.sqrt(2.0 / math.pi) * (x + 0.044715 * (x * x * x))))


def _mix_kernel(x_ref, attn_ref, yssm_ref, ga_ref, gs_ref, mod_ref, ln_ref,
                wglu_ref, bglu_ref, wua_ref, wus_ref, wout_ref, wrt_ref, wsgu_ref, wsd_ref,
                h2_ref, base_ref, score_ref):
    g = _gelu_tanh(yssm_ref[...])
    glu = g * _sigmoid(_bdot(g, wglu_ref[...]) + bglu_ref[...])
    y_attn = jnp.dot(attn_ref[...], wua_ref[...], preferred_element_type=F32)
    y_ssm = _bdot(glu, wus_ref[...])
    mixed = _sigmoid(ga_ref[...].astype(F32)) * y_attn + _sigmoid(gs_ref[...].astype(F32)) * y_ssm
    gate1 = mod_ref[0, 2:3, :]
    x1 = x_ref[...] + gate1 * _bdot(mixed, wout_ref[...])
    h2 = _modulated_norm(x1, ln_ref[...], mod_ref[0, 3:4, :], mod_ref[0, 4:5, :])
    h2_ref[...] = h2
    h2b = h2.astype(BF16)
    score_ref[...] = _sigmoid(_bdot_nt(wrt_ref[...], h2b))
    gu = jnp.dot(h2b, wsgu_ref[...], preferred_element_type=F32)
    sd = wsd_ref.shape[0]
    shared = _bdot(_silu(gu[:, :sd]) * gu[:, sd:], wsd_ref[...])
    base_ref[...] = x1 + mod_ref[0, 5:6, :] * shared


def _mix(xf, attn, yssm, ga, gs, mod3, ln2_g, w, seq):
    n, d = xf.shape
    tm = TOKEN_TILE
    tiles_per_seq = seq // tm
    row = lambda i: (i, 0)
    const = lambda i: (0, 0)
    weights = [w["glu"], w["b_glu"], w["up_attn"], w["up_ssm"], w["out"], w["router_t"], w["sh_gu"], w["sh_down"]]
    return pl.pallas_call(
        _mix_kernel,
        out_shape=(jax.ShapeDtypeStruct((n, d), F32),
                   jax.ShapeDtypeStruct((n, d), F32),
                   jax.ShapeDtypeStruct((N_EXPERTS, n), F32)),
        grid=(n // tm,),
        in_specs=[pl.BlockSpec((tm, d), row),
                  pl.BlockSpec((tm, attn.shape[1]), row),
                  pl.BlockSpec((tm, yssm.shape[1]), row),
                  pl.BlockSpec((tm, d), row),
                  pl.BlockSpec((tm, d), row),
                  pl.BlockSpec((1, 6, d), lambda i: (i // tiles_per_seq, 0, 0)),
                  pl.BlockSpec((1, d), const)] + [pl.BlockSpec(a.shape, const) for a in weights],
        out_specs=(pl.BlockSpec((tm, d), row), pl.BlockSpec((tm, d), row),
                   pl.BlockSpec((N_EXPERTS, tm), lambda i: (0, i))),
        compiler_params=_cparams(1),
        name="mix",
    )(xf, attn, yssm, ga, gs, mod3, ln2_g.reshape(1, d), *weights)


def _route_kernel(score_ref, bias_ref, idx_ref, w_ref):
    scores = score_ref[...]
    ne, tn = scores.shape
    biased = scores + bias_ref[...]
    gsz = GROUP_SIZE
    sub = lax.broadcasted_iota(jnp.int32, (gsz, tn), 0)
    group_score = []
    for g in range(N_EXPERT_GROUPS):
        sg = biased[g * gsz:(g + 1) * gsz, :]
        m1 = jnp.max(sg, axis=0, keepdims=True)
        first = jnp.min(jnp.where(sg == m1, sub, gsz), axis=0, keepdims=True)
        m2 = jnp.max(jnp.where(sub == first, -jnp.inf, sg), axis=0, keepdims=True)
        group_score.append(m1 + m2)
    group_rows = []
    for g in range(N_EXPERT_GROUPS):
        beaten = jnp.zeros((1, tn), jnp.int32)
        for o in range(N_EXPERT_GROUPS):
            if o == g:
                continue
            wins = (group_score[o] > group_score[g])
            if o < g:
                wins = wins | (group_score[o] == group_score[g])
            beaten = beaten + jnp.where(wins, 1, 0)
        group_rows.append(jnp.broadcast_to(beaten < TOPK_GROUPS, (gsz, tn)))
    allowed = jnp.concatenate(group_rows, axis=0)
    cur = jnp.where(allowed, biased, MASK_VALUE)
    eio = lax.broadcasted_iota(jnp.int32, (ne, tn), 0)
    idx_rows, w_rows = [], []
    for _ in range(TOP_K):
        vmax = jnp.max(cur, axis=0, keepdims=True)
        eidx = jnp.min(jnp.where(cur == vmax, eio, ne), axis=0, keepdims=True)
        hit = eio == eidx
        w_rows.append(jnp.sum(jnp.where(hit, scores, 0.0), axis=0, keepdims=True))
        idx_rows.append(eidx)
        cur = jnp.where(hit, -jnp.inf, cur)
    wts = jnp.concatenate(w_rows, axis=0)
    idx_ref[...] = jnp.concatenate(idx_rows, axis=0)
    w_ref[...] = wts / jnp.sum(wts, axis=0, keepdims=True) * ROUTED_SCALE


def _route(scores_t, router_bias):
    ne, n = scores_t.shape
    tn = ROUTE_TILE
    return pl.pallas_call(
        _route_kernel,
        out_shape=(jax.ShapeDtypeStruct((TOP_K, n), jnp.int32), jax.ShapeDtypeStruct((TOP_K, n), F32)),
        grid=(n // tn,),
        in_specs=[pl.BlockSpec((ne, tn), lambda i: (0, i)), pl.BlockSpec((ne, 1), lambda i: (0, 0))],
        out_specs=(pl.BlockSpec((TOP_K, tn), lambda i: (0, i)), pl.BlockSpec((TOP_K, tn), lambda i: (0, i))),
        compiler_params=_cparams(1),
        name="route",
    )(scores_t, router_bias.reshape(ne, 1).astype(F32))


def _dispatch_tables(idx_t, w_t, n_blocks):
    k, n = idx_t.shape
    n_assign = k * n
    rb = ROW_BLOCK
    flat_e = idx_t.reshape(n_assign)
    flat_w = w_t.reshape(n_assign)
    order = jnp.argsort(flat_e).astype(jnp.int32)
    rank = jnp.argsort(order).astype(jnp.int32)
    se = flat_e[order]
    experts = jnp.arange(N_EXPERTS, dtype=jnp.int32)
    offs = jnp.searchsorted(se, experts, side='left').astype(jnp.int32)
    ends = jnp.searchsorted(se, experts, side='right').astype(jnp.int32)
    counts = ends - offs
    padded = (counts + rb - 1) // rb * rb
    pend = jnp.cumsum(padded)
    poffs = pend - padded
    blk_e = jnp.minimum(jnp.searchsorted(pend, jnp.arange(n_blocks, dtype=jnp.int32) * rb, side='right'),
                        N_EXPERTS - 1).astype(jnp.int32)
    slot = jnp.arange(n_blocks * rb, dtype=jnp.int32)
    e_of_slot = blk_e[slot // rb]
    j = slot - poffs[e_of_slot]
    valid = (j >= 0) & (j < counts[e_of_slot])
    src = jnp.clip(offs[e_of_slot] + j, 0, n_assign - 1)
    a_of_slot = order[src]
    slot_tok = jnp.where(valid, a_of_slot % n, slot % n).astype(jnp.int32)
    slot_w = jnp.where(valid, flat_w[a_of_slot], 0.0).astype(F32)
    dest = (poffs[flat_e] + rank - offs[flat_e]).astype(jnp.int32)
    return slot_tok, slot_w, blk_e, dest


def _gather_kernel(idx_hbm, src_hbm, dst_hbm, idx_smem, idx_sem, row_sem):
    i = pl.program_id(0)
    n_steps = pl.num_programs(0)
    ch = idx_smem.shape[0]
    fetch = pltpu.make_async_copy(idx_hbm.at[i], idx_smem, idx_sem)
    fetch.start()
    fetch.wait()
    base = i * ch

    def row_copy(src_row, dst_row):
        return pltpu.make_async_copy(src_hbm.at[pl.ds(src_row, 1)], dst_hbm.at[pl.ds(dst_row, 1)], row_sem)

    def issue(r, _):
        row_copy(idx_smem[r], base + r).start()
        return 0

    lax.fori_loop(0, ch, issue, 0)

    def drain(r, _):
        row_copy(0, 0).wait()
        return 0

    @pl.when(i > 0)
    def _():
        lax.fori_loop(0, ch, drain, 0)

    @pl.when(i == n_steps - 1)
    def _():
        lax.fori_loop(0, ch, drain, 0)


def _gather_rows(src, slot_tok):
    rows = slot_tok.shape[0]
    d = src.shape[1]
    ch = GATHER_CHUNK
    return pl.pallas_call(
        _gather_kernel,
        out_shape=jax.ShapeDtypeStruct((rows, d), src.dtype),
        grid=(rows // ch,),
        in_specs=[pl.BlockSpec(memory_space=pl.ANY), pl.BlockSpec(memory_space=pl.ANY)],
        out_specs=pl.BlockSpec(memory_space=pl.ANY),
        scratch_shapes=[pltpu.SMEM((ch,), jnp.int32), pltpu.SemaphoreType.DMA, pltpu.SemaphoreType.DMA],
        compiler_params=_cparams(1),
        name="gather_rows",
    )(slot_tok.reshape(rows // ch, ch), src)


def _ffn_kernel(blk_e_ref, x_ref, sw_ref, wg_ref, wu_ref, wd_ref, y_ref):
    xb = x_ref[...].astype(BF16)
    hg = jnp.dot(xb, wg_ref[0].astype(BF16), preferred_element_type=F32)
    hu = jnp.dot(xb, wu_ref[0].astype(BF16), preferred_element_type=F32)
    hb = _silu(hg) * hu
    y_ref[...] = _bdot(hb, wd_ref[0]) * sw_ref[...]


def _expert_ffn(x_sorted, slot_w, blk_e, w_gate, w_up, w_down):
    rows, d = x_sorted.shape
    rb = ROW_BLOCK
    ed = w_gate.shape[2]
    grid_spec = pltpu.PrefetchScalarGridSpec(
        num_scalar_prefetch=1,
        grid=(rows // rb,),
        in_specs=[pl.BlockSpec((rb, d), lambda i, be: (i, 0)),
                  pl.BlockSpec((rb, 1), lambda i, be: (i, 0)),
                  pl.BlockSpec((1, d, ed), lambda i, be: (be[i], 0, 0)),
                  pl.BlockSpec((1, d, ed), lambda i, be: (be[i], 0, 0)),
                  pl.BlockSpec((1, ed, d), lambda i, be: (be[i], 0, 0))],
        out_specs=pl.BlockSpec((rb, d), lambda i, be: (i, 0)),
    )
    return pl.pallas_call(
        _ffn_kernel,
        out_shape=jax.ShapeDtypeStruct((rows, d), F32),
        grid_spec=grid_spec,
        compiler_params=_cparams(1),
        name="expert_ffn",
    )(blk_e, x_sorted, slot_w.reshape(rows, 1), w_gate, w_up, w_down)


def _combine_kernel(dest_hbm, y_hbm, base_ref, mod_ref, o_ref, idx_smem, buf, idx_sem, row_sem):
    i = pl.program_id(0)
    tm = base_ref.shape[0]
    n_rows = idx_smem.shape[0]
    fetch = pltpu.make_async_copy(dest_hbm.at[i], idx_smem, idx_sem)
    fetch.start()
    fetch.wait()

    def row_copy(src_row, dst_row):
        return pltpu.make_async_copy(y_hbm.at[pl.ds(src_row, 1)], buf.at[pl.ds(dst_row, 1)], row_sem)

    def issue(r, _):
        row_copy(idx_smem[r], r).start()
        return 0

    lax.fori_loop(0, n_rows, issue, 0)

    def drain(r, _):
        row_copy(0, 0).wait()
        return 0

    lax.fori_loop(0, n_rows, drain, 0)
    routed = buf[0:tm, :]
    for kk in range(1, n_rows // tm):
        routed = routed + buf[kk * tm:(kk + 1) * tm, :]
    o_ref[...] = base_ref[...] + mod_ref[0, 5:6, :] * routed


def _combine(y_sorted, dest, base, mod3, seq):
    n, d = base.shape
    tm = COMBINE_TILE
    k = dest.shape[0] // n
    tiles_per_seq = seq // tm
    dest_tiles = dest.reshape(k, n // tm, tm).transpose(1, 0, 2).reshape(n // tm, k * tm)
    return pl.pallas_call(
        _combine_kernel,
        out_shape=jax.ShapeDtypeStruct((n, d), F32),
        grid=(n // tm,),
        in_specs=[pl.BlockSpec(memory_space=pl.ANY),
                  pl.BlockSpec(memory_space=pl.ANY),
                  pl.BlockSpec((tm, d), lambda i: (i, 0)),
                  pl.BlockSpec((1, 6, d), lambda i: (i // tiles_per_seq, 0, 0))],
        out_specs=pl.BlockSpec((tm, d), lambda i: (i, 0)),
        scratch_shapes=[pltpu.SMEM((k * tm,), jnp.int32), pltpu.VMEM((k * tm, d), F32),
                        pltpu.SemaphoreType.DMA, pltpu.SemaphoreType.DMA],
        compiler_params=_cparams(1),
        name="combine",
    )(dest_tiles, y_sorted, base, mod3)


def _hybrid_layer(x, cond, rel_bias, w_ada, b_ada, ln1_g, w_in, q_norm_g, k_norm_g,
                  ssm_lambda_re, ssm_lambda_im, ssm_log_dt, ssm_b_re, ssm_b_im, ssm_c_re, ssm_c_im,
                  ssm_d, ssm_w_glu, ssm_b_glu, w_up_attn, w_up_ssm, w_out, ln2_g,
                  w_router, router_bias, w_exp_gate, w_exp_up, w_exp_down,
                  w_sh_gate, w_sh_up, w_sh_down):
    bsz, seq, d = x.shape
    n = bsz * seq
    xf = x.reshape(n, d)
    mod3 = _adaln(cond, w_ada, b_ada).reshape(bsz, 6, d)

    q_gain = jnp.tile(q_norm_g.astype(F32), ATTN_HEADS).reshape(1, ATTN_WIDTH)
    k_gain = jnp.tile(k_norm_g.astype(F32), ATTN_HEADS).reshape(1, ATTN_WIDTH)
    q, k, v, u, ga, gs, kmean = _inproj(xf, mod3, ln1_g, w_in.astype(BF16), q_gain, k_gain, seq)

    attn = _moba_attention(q, k, v, kmean, rel_bias, bsz, seq)
    ops = _s5_operators(ssm_lambda_re, ssm_lambda_im, ssm_log_dt, ssm_b_re, ssm_b_im,
                        ssm_c_re, ssm_c_im, ssm_d, seq // SSM_CHUNK)
    yssm = _s5_scan(u, ops, bsz, seq)

    weights = {
        "glu": ssm_w_glu.astype(BF16), "b_glu": ssm_b_glu.astype(F32).reshape(1, -1),
        "up_attn": w_up_attn.astype(BF16), "up_ssm": w_up_ssm.astype(BF16), "out": w_out.astype(BF16),
        "router_t": w_router.T.astype(BF16),
        "sh_gu": jnp.concatenate([w_sh_gate, w_sh_up], axis=1).astype(BF16),
        "sh_down": w_sh_down.astype(BF16),
    }
    h2, base, scores_t = _mix(xf, attn, yssm, ga, gs, mod3, ln2_g, weights, seq)

    idx_t, w_t = _route(scores_t, router_bias)
    n_blocks = -(-(n * TOP_K) // ROW_BLOCK) + N_EXPERTS
    n_blocks = -(-n_blocks * ROW_BLOCK // GATHER_CHUNK) * GATHER_CHUNK // ROW_BLOCK
    slot_tok, slot_w, blk_e, dest = _dispatch_tables(idx_t, w_t, n_blocks)
    x_sorted = _gather_rows(h2, slot_tok)
    y_sorted = _expert_ffn(x_sorted, slot_w, blk_e, w_exp_gate, w_exp_up, w_exp_down)
    out = _combine(y_sorted, dest, base, mod3, seq)
    return out.reshape(bsz, seq, d)


def kernel(x, c, rel_bias, w_ada, b_ada, ln1_g, w_in, q_norm_g, k_norm_g, ssm_lambda_re, ssm_lambda_im, ssm_log_dt, ssm_b_re, ssm_b_im, ssm_c_re, ssm_c_im, ssm_d, ssm_w_glu, ssm_b_glu, w_up_attn, w_up_ssm, w_out, ln2_g, w_router, router_bias, w_exp_gate, w_exp_up, w_exp_down, w_sh_gate, w_sh_up, w_sh_down):
    for l in range(w_ada.shape[0]):
        x = _hybrid_layer(x, c, rel_bias, w_ada[l], b_ada[l], ln1_g[l], w_in[l], q_norm_g[l], k_norm_g[l],
                          ssm_lambda_re[l], ssm_lambda_im[l], ssm_log_dt[l], ssm_b_re[l], ssm_b_im[l],
                          ssm_c_re[l], ssm_c_im[l], ssm_d[l], ssm_w_glu[l], ssm_b_glu[l],
                          w_up_attn[l], w_up_ssm[l], w_out[l], ln2_g[l], w_router[l], router_bias[l],
                          w_exp_gate[l], w_exp_up[l], w_exp_down[l], w_sh_gate[l], w_sh_up[l], w_sh_down[l])
    return x
```

```python
import functools
import math

import numpy as np
import jax
import jax.numpy as jnp
from jax import lax
from jax.experimental import pallas as pl
from jax.experimental.pallas import tpu as pltpu

F32 = jnp.float32
BF16 = jnp.bfloat16

ATTN_HEADS = 8
HEAD_DIM = 64
ATTN_WIDTH = ATTN_HEADS * HEAD_DIM
MOBA_BLOCK = 256
MOBA_TOPK = 3
NUM_BUCKETS = 32
MAX_DISTANCE = 128
SSM_WIDTH = 512
SSM_GROUP = 16
SSM_GROUPS = SSM_WIDTH // SSM_GROUP
SSM_STATE = 64
N_EXPERTS = 256
TOP_K = 8
N_EXPERT_GROUPS = 8
TOPK_GROUPS = 4
GROUP_SIZE = N_EXPERTS // N_EXPERT_GROUPS
EXPERT_DIM = 256
ROUTED_SCALE = 2.5
EPS = 1e-6
MASK_VALUE = -1e30

LANES = 128
HEADS_PER_STEP = LANES // HEAD_DIM
SSM_CHUNK = 16
SUBLANES = 8
ROW_BLOCK = 256
TOKEN_TILE = 256
ROUTE_TILE = 256
COMBINE_TILE = 128
ISSUE_UNROLL = 8
VMEM_LIMIT = 56 * 1024 * 1024


def _store_rows_as_tiles(ref, val):
    rows, d = val.shape
    nt = d // LANES
    for c in range(nt):
        ref[pl.ds(c, rows, stride=nt), :] = val[:, c * LANES:(c + 1) * LANES]


def _load_rows_from_tiles(ref, rows, d):
    nt = d // LANES
    return jnp.concatenate([ref[pl.ds(c, rows, stride=nt), :] for c in range(nt)], axis=1)


def _cparams(n_axes, vmem=VMEM_LIMIT):
    return pltpu.CompilerParams(dimension_semantics=("arbitrary",) * n_axes, vmem_limit_bytes=vmem)


def _sigmoid(x):
    return 1.0 / (1.0 + jnp.exp(-x))


def _silu(x):
    return x * _sigmoid(x)


def _bdot(a, b):
    return jnp.dot(a.astype(BF16), b.astype(BF16), preferred_element_type=F32)


def _bdot_nt(a, b):
    return lax.dot_general(a.astype(BF16), b.astype(BF16), (((1,), (1,)), ((), ())),
                           preferred_element_type=F32)


def _adaln_kernel(c_ref, w_ref, b_ref, o_ref):
    o_ref[...] = _bdot(_silu(c_ref[...]), w_ref[...]) + b_ref[...]


def _adaln(c, w_ada, b_ada):
    bsz, d = c.shape
    n_out = w_ada.shape[1]
    return pl.pallas_call(
        _adaln_kernel,
        out_shape=jax.ShapeDtypeStruct((bsz, n_out), F32),
        grid=(n_out // d,),
        in_specs=[pl.BlockSpec((bsz, d), lambda j: (0, 0)),
                  pl.BlockSpec((d, d), lambda j: (0, j)),
                  pl.BlockSpec((1, d), lambda j: (0, j))],
        out_specs=pl.BlockSpec((bsz, d), lambda j: (0, j)),
        compiler_params=_cparams(1),
        name="adaln",
    )(c, w_ada, b_ada.reshape(1, n_out))


def _modulated_norm(x, gain, shift, scale):
    y = x * lax.rsqrt(jnp.mean(x * x, axis=-1, keepdims=True) + EPS) * gain
    return y * (1.0 + scale) + shift


def _head_norm(t, seg, gain):
    ms = _bdot(t * t, seg)
    return t * lax.rsqrt(ms + EPS) * gain


def _inproj_kernel(x_ref, mod_ref, ln_ref, w_ref, seg_ref, qg_ref, kg_ref,
                   q_ref, k_ref, v_ref, u_ref, ga_ref, gs_ref, km_ref):
    aw, sw, d = ATTN_WIDTH, SSM_WIDTH, x_ref.shape[1]
    h = _modulated_norm(x_ref[...], ln_ref[...], mod_ref[0, 0:1, :], mod_ref[0, 1:2, :]).astype(BF16)
    seg = seg_ref[...]
    q = jnp.dot(h, w_ref[:, 0:aw], preferred_element_type=F32)
    q_ref[...] = _head_norm(q, seg, qg_ref[...])
    k = jnp.dot(h, w_ref[:, aw:2 * aw], preferred_element_type=F32)
    kn = _head_norm(k, seg, kg_ref[...])
    k_ref[...] = kn.astype(BF16)
    km_ref[0] = jnp.mean(kn, axis=0, keepdims=True)
    v_ref[...] = jnp.dot(h, w_ref[:, 2 * aw:3 * aw], preferred_element_type=F32).astype(BF16)
    o = 3 * aw
    u_ref[...] = jnp.dot(h, w_ref[:, o:o + sw], preferred_element_type=F32).astype(BF16)
    o += sw
    ga_ref[...] = jnp.dot(h, w_ref[:, o:o + d], preferred_element_type=F32).astype(BF16)
    o += d
    gs_ref[...] = jnp.dot(h, w_ref[:, o:o + d], preferred_element_type=F32).astype(BF16)


def _inproj(xf, mod3, ln1_g, w_in_b, q_gain, k_gain, seq):
    n, d = xf.shape
    tm = MOBA_BLOCK
    tiles_per_seq = seq // tm
    aw, sw = ATTN_WIDTH, SSM_WIDTH
    head_of_lane = np.arange(aw) // HEAD_DIM
    seg = jnp.asarray((head_of_lane[:, None] == head_of_lane[None, :]) / HEAD_DIM, BF16)
    row = lambda i: (i, 0)
    const = lambda i: (0, 0)
    return pl.pallas_call(
        _inproj_kernel,
        out_shape=(jax.ShapeDtypeStruct((n, aw), F32),
                   jax.ShapeDtypeStruct((n, aw), BF16),
                   jax.ShapeDtypeStruct((n, aw), BF16),
                   jax.ShapeDtypeStruct((n, sw), BF16),
                   jax.ShapeDtypeStruct((n, d), BF16),
                   jax.ShapeDtypeStruct((n, d), BF16),
                   jax.ShapeDtypeStruct((n // tm, 1, aw), F32)),
        grid=(n // tm,),
        in_specs=[pl.BlockSpec((tm, d), row),
                  pl.BlockSpec((1, 6, d), lambda i: (i // tiles_per_seq, 0, 0)),
                  pl.BlockSpec((1, d), const),
                  pl.BlockSpec(w_in_b.shape, const),
                  pl.BlockSpec((aw, aw), const),
                  pl.BlockSpec((1, aw), const),
                  pl.BlockSpec((1, aw), const)],
        out_specs=(pl.BlockSpec((tm, aw), row), pl.BlockSpec((tm, aw), row), pl.BlockSpec((tm, aw), row),
                   pl.BlockSpec((tm, sw), row), pl.BlockSpec((tm, d), row), pl.BlockSpec((tm, d), row),
                   pl.BlockSpec((1, 1, aw), lambda i: (i, 0, 0))),
        compiler_params=_cparams(1),
        name="inproj",
    )(xf, mod3, ln1_g.reshape(1, d), w_in_b, seg, q_gain, k_gain)


def _t5_bucket(rel):
    n = jnp.maximum(rel, 0)
    max_exact = NUM_BUCKETS // 2
    nf = jnp.maximum(n, 1).astype(F32)
    large = max_exact + (jnp.log(nf / max_exact) / math.log(MAX_DISTANCE / max_exact)
                         * (NUM_BUCKETS - max_exact)).astype(jnp.int32)
    large = jnp.minimum(large, NUM_BUCKETS - 1)
    return jnp.where(n < max_exact, n, large)


def _bias_tables(rel_bias):
    blk = MOBA_BLOCK
    rel = jnp.arange(blk)[:, None] - jnp.arange(blk)[None, :]
    own = rel_bias[_t5_bucket(rel)].astype(F32).transpose(2, 0, 1)
    prev = rel_bias[_t5_bucket(rel + blk)].astype(F32).transpose(2, 0, 1)
    assert blk + 1 >= MAX_DISTANCE
    far = rel_bias[NUM_BUCKETS - 1].astype(F32)
    return own, prev, far


def _select_blocks(gate, n_past):
    tq, nb = gate.shape
    blk = lax.broadcasted_iota(jnp.int32, (tq, nb), 1)
    beaten = jnp.zeros((tq, nb), jnp.int32)
    for m in range(nb):
        gm = gate[:, m:m + 1]
        wins = (gm > gate) | ((gm == gate) & (m < blk))
        beaten = beaten + jnp.where(wins & (m < n_past), 1, 0)
    return jnp.where((blk < n_past) & (beaten < MOBA_TOPK), 1.0, 0.0)


def _attn_kernel(far_ref, q_ref, k_ref, v_ref, km_ref, bown_ref, bprev_ref, o_ref):
    hp = pl.program_id(1)
    qi = pl.program_id(2)
    tq = q_ref.shape[0]
    blk = MOBA_BLOCK
    nb = km_ref.shape[1]
    q = q_ref[...]
    lane = lax.broadcasted_iota(jnp.int32, (tq, LANES), 1)
    qpos = lax.broadcasted_iota(jnp.int32, (tq, blk), 0)
    kpos = lax.broadcasted_iota(jnp.int32, (tq, blk), 1)
    blk_lane = lax.broadcasted_iota(jnp.int32, (tq, nb), 1)
    scale = HEAD_DIM ** -0.5
    own_start = pl.multiple_of(qi * blk, blk)
    k_own = k_ref[pl.ds(own_start, blk), :]
    v_own = v_ref[pl.ds(own_start, blk), :]
    outs = []
    for h in range(HEADS_PER_STEP):
        in_head = (lane >= h * HEAD_DIM) & (lane < (h + 1) * HEAD_DIM)
        qm = jnp.where(in_head, q, 0.0)
        gate = lax.dot_general(qm, km_ref[0], (((1,), (1,)), ((), ())),
                               precision=lax.Precision.HIGHEST, preferred_element_type=F32)
        sel = _select_blocks(gate, qi)
        qb = (qm * scale).astype(BF16)
        far_bias = far_ref[hp * HEADS_PER_STEP + h]

        s = _bdot_nt(qb, k_own) + bown_ref[h]
        s = jnp.where(qpos >= kpos, s, MASK_VALUE)
        m0 = jnp.max(s, axis=1, keepdims=True)
        p = jnp.exp(s - m0)
        l0 = jnp.sum(p, axis=1, keepdims=True)
        acc0 = _bdot(p, v_own)

        def past_block(j, carry, h=h, qb=qb, sel=sel, far_bias=far_bias):
            m, l, acc = carry
            start = pl.multiple_of(j * blk, blk)
            kb = k_ref[pl.ds(start, blk), :]
            vb = v_ref[pl.ds(start, blk), :]
            bias = jnp.where(j == qi - 1, bprev_ref[h], far_bias)
            s = _bdot_nt(qb, kb) + bias
            chosen = jnp.sum(jnp.where(blk_lane == j, sel, 0.0), axis=1, keepdims=True)
            s = jnp.where(chosen > 0.5, s, MASK_VALUE)
            m_new = jnp.maximum(m, jnp.max(s, axis=1, keepdims=True))
            alpha = jnp.exp(m - m_new)
            p = jnp.exp(s - m_new)
            l = alpha * l + jnp.sum(p, axis=1, keepdims=True)
            acc = alpha * acc + _bdot(p, vb)
            return m_new, l, acc

        _, l, acc = lax.fori_loop(0, qi, past_block, (m0, l0, acc0))
        outs.append(acc / l)
    o_ref[...] = jnp.where(lane < HEAD_DIM, outs[0], outs[1]).astype(o_ref.dtype)


def _moba_attention(q, k, v, kmean, rel_bias, bsz, seq):
    n, aw = q.shape
    blk = MOBA_BLOCK
    nb = seq // blk
    own, prev, far = _bias_tables(rel_bias)
    hps = HEADS_PER_STEP
    grid_spec = pltpu.PrefetchScalarGridSpec(
        num_scalar_prefetch=1,
        grid=(bsz, aw // LANES, nb),
        in_specs=[pl.BlockSpec((blk, LANES), lambda b, hp, qi, far: (b * nb + qi, hp)),
                  pl.BlockSpec((seq, LANES), lambda b, hp, qi, far: (b, hp)),
                  pl.BlockSpec((seq, LANES), lambda b, hp, qi, far: (b, hp)),
                  pl.BlockSpec((1, nb, LANES), lambda b, hp, qi, far: (b, 0, hp)),
                  pl.BlockSpec((hps, blk, blk), lambda b, hp, qi, far: (hp, 0, 0)),
                  pl.BlockSpec((hps, blk, blk), lambda b, hp, qi, far: (hp, 0, 0))],
        out_specs=pl.BlockSpec((blk, LANES), lambda b, hp, qi, far: (b * nb + qi, hp)),
    )
    return pl.pallas_call(
        _attn_kernel,
        out_shape=jax.ShapeDtypeStruct((n, aw), BF16),
        grid_spec=grid_spec,
        compiler_params=_cparams(3),
        name="moba_attention",
    )(far, q, k, v, kmean.reshape(bsz, nb, aw), own, prev)


def _s5_operators(lambda_re, lambda_im, log_dt, b_re, b_im, c_re, c_im, d_skip, n_chunks):
    hi = lax.Precision.HIGHEST
    L, G, P, C = SSM_CHUNK, SSM_GROUPS, SSM_STATE, SSM_GROUP
    lam_re = jnp.minimum(lambda_re.astype(F32), -1e-4)
    lam_im = lambda_im.astype(F32)
    dt = jnp.exp(log_dt.astype(F32))[:, None]
    z_re, z_im = lam_re * dt, lam_im * dt

    def a_pow(nvec):
        nv = jnp.asarray(nvec, F32)[:, None, None]
        mag = jnp.exp(nv * z_re)
        return mag * jnp.cos(nv * z_im), mag * jnp.sin(nv * z_im)

    a_re, a_im = a_pow([1.0])
    a_re, a_im = a_re[0], a_im[0]
    den = lam_re * lam_re + lam_im * lam_im
    nr = a_re - 1.0
    coef_re = (nr * lam_re + a_im * lam_im) / den
    coef_im = (a_im * lam_re - nr * lam_im) / den
    br, bi = b_re.astype(F32), b_im.astype(F32)
    bbar_re = coef_re[..., None] * br - coef_im[..., None] * bi
    bbar_im = coef_re[..., None] * bi + coef_im[..., None] * br
    cr, ci = c_re.astype(F32), c_im.astype(F32)

    pw_re, pw_im = a_pow(np.arange(L + 1))
    cb_re = cr[None] * pw_re[:, :, None, :] - ci[None] * pw_im[:, :, None, :]
    cb_im = cr[None] * pw_im[:, :, None, :] + ci[None] * pw_re[:, :, None, :]
    kern = (jnp.einsum('jgop,gpi->jgoi', cb_re[:L], bbar_re, precision=hi)
            - jnp.einsum('jgop,gpi->jgoi', cb_im[:L], bbar_im, precision=hi))
    sig = np.arange(L)[:, None]
    tau = np.arange(L)[None, :]
    lag = np.clip(tau - sig, 0, L - 1)
    causal = jnp.asarray((tau >= sig), F32)
    t_op = kern[lag] * causal[:, :, None, None, None]
    t_op = t_op.transpose(2, 0, 4, 1, 3).reshape(G, L * C, L * C)
    d_g = d_skip.astype(F32).reshape(G, C)
    t_op = t_op + jnp.eye(L * C, dtype=F32)[None] * jnp.tile(d_g, (1, L))[:, None, :]

    rp_re, rp_im = pw_re[L - 1 - np.arange(L)], pw_im[L - 1 - np.arange(L)]
    p_re = rp_re[..., None] * bbar_re[None] - rp_im[..., None] * bbar_im[None]
    p_im = rp_re[..., None] * bbar_im[None] + rp_im[..., None] * bbar_re[None]
    p_op = jnp.concatenate([p_re, p_im], axis=2)
    p_op = p_op.transpose(1, 0, 3, 2).reshape(G, L * C, 2 * P)

    q_re = cb_re[1:].transpose(1, 3, 0, 2)
    q_im = -cb_im[1:].transpose(1, 3, 0, 2)
    q_op = jnp.concatenate([q_re, q_im], axis=1).reshape(G, 2 * P, L * C)

    n_steps = max(1, int(math.ceil(math.log2(n_chunks))))
    dk_re, dk_im = a_pow([float(L * 2 ** k) for k in range(n_steps)])
    a1 = jnp.concatenate([dk_re, dk_re], axis=-1).transpose(1, 0, 2)
    a2 = jnp.concatenate([-dk_im, dk_im], axis=-1).transpose(1, 0, 2)
    return t_op.astype(BF16), p_op.astype(BF16), q_op.astype(BF16), a1, a2


def _s5_kernel(u_ref, t_ref, p_ref, q_ref, a1_ref, a2_ref, y_ref, *, n_chunks):
    u = u_ref[0]
    s = jnp.dot(u, p_ref[0], preferred_element_type=F32)
    rows, width = s.shape
    chunk = lax.broadcasted_iota(jnp.int32, (rows, width), 0) % n_chunks
    a1 = a1_ref[0]
    a2 = a2_ref[0]
    x = jnp.where(chunk >= 1, pltpu.roll(s, 1, axis=0), 0.0)
    for kk in range(a1.shape[0]):
        dist = 2 ** kk
        if dist >= n_chunks:
            break
        xs = jnp.where(chunk >= dist, pltpu.roll(x, dist, axis=0), 0.0)
        x = x + a1[kk:kk + 1, :] * xs + a2[kk:kk + 1, :] * pltpu.roll(xs, SSM_STATE, axis=1)
    y = jnp.dot(u, t_ref[0], preferred_element_type=F32)
    y_ref[0] = y + jnp.dot(x.astype(BF16), q_ref[0], preferred_element_type=F32)


def _s5_scan(u, ops, bsz, seq):
    t_op, p_op, q_op, a1, a2 = ops
    L, G, C = SSM_CHUNK, SSM_GROUPS, SSM_GROUP
    nc = seq // L
    rows = bsz * nc
    w = L * C
    ug = u.reshape(rows, L, G, C).transpose(2, 0, 1, 3).reshape(G, rows, w)
    grp = lambda g: (g, 0, 0)
    y = pl.pallas_call(
        functools.partial(_s5_kernel, n_chunks=nc),
        out_shape=jax.ShapeDtypeStruct((G, rows, w), F32),
        grid=(G,),
        in_specs=[pl.BlockSpec((1, rows, w), grp),
                  pl.BlockSpec((1, w, w), grp),
                  pl.BlockSpec((1, w, 2 * SSM_STATE), grp),
                  pl.BlockSpec((1, 2 * SSM_STATE, w), grp),
                  pl.BlockSpec((1,) + a1.shape[1:], grp),
                  pl.BlockSpec((1,) + a2.shape[1:], grp)],
        out_specs=pl.BlockSpec((1, rows, w), grp),
        compiler_params=_cparams(1),
        name="s5_scan",
    )(ug, t_op, p_op, q_op, a1, a2)
    return y.reshape(G, rows, L, C).transpose(1, 2, 0, 3).reshape(bsz * seq, G * C)


def _gelu_tanh(x):
    return 0.5 * x * (1.0 + jnp.tanh(math.sqrt(2.0 / math.pi) * (x + 0.044715 * (x * x * x))))


def _mix_kernel(x_ref, attn_ref, yssm_ref, ga_ref, gs_ref, mod_ref, ln_ref,
                wglu_ref, bglu_ref, wua_ref, wus_ref, wout_ref, wrt_ref, wsgu_ref, wsd_ref,
                h2_ref, base_ref, score_ref):
    g = _gelu_tanh(yssm_ref[...])
    glu = g * _sigmoid(_bdot(g, wglu_ref[...]) + bglu_ref[...])
    y_attn = jnp.dot(attn_ref[...], wua_ref[...], preferred_element_type=F32)
    y_ssm = _bdot(glu, wus_ref[...])
    mixed = _sigmoid(ga_ref[...].astype(F32)) * y_attn + _sigmoid(gs_ref[...].astype(F32)) * y_ssm
    gate1 = mod_ref[0, 2:3, :]
    x1 = x_ref[...] + gate1 * _bdot(mixed, wout_ref[...])
    h2 = _modulated_norm(x1, ln_ref[...], mod_ref[0, 3:4, :], mod_ref[0, 4:5, :])
    _store_rows_as_tiles(h2_ref, h2)
    h2b = h2.astype(BF16)
    score_ref[...] = _sigmoid(_bdot_nt(wrt_ref[...], h2b))
    gu = jnp.dot(h2b, wsgu_ref[...], preferred_element_type=F32)
    sd = wsd_ref.shape[0]
    shared = _bdot(_silu(gu[:, :sd]) * gu[:, sd:], wsd_ref[...])
    base_ref[...] = x1 + mod_ref[0, 5:6, :] * shared


def _mix(xf, attn, yssm, ga, gs, mod3, ln2_g, w, seq):
    n, d = xf.shape
    tm = TOKEN_TILE
    tiles_per_seq = seq // tm
    row = lambda i: (i, 0)
    const = lambda i: (0, 0)
    nt = d // LANES
    weights = [w["glu"], w["b_glu"], w["up_attn"], w["up_ssm"], w["out"], w["router_t"], w["sh_gu"], w["sh_down"]]
    return pl.pallas_call(
        _mix_kernel,
        out_shape=(jax.ShapeDtypeStruct((n * nt, LANES), F32),
                   jax.ShapeDtypeStruct((n, d), F32),
                   jax.ShapeDtypeStruct((N_EXPERTS, n), F32)),
        grid=(n // tm,),
        in_specs=[pl.BlockSpec((tm, d), row),
                  pl.BlockSpec((tm, attn.shape[1]), row),
                  pl.BlockSpec((tm, yssm.shape[1]), row),
                  pl.BlockSpec((tm, d), row),
                  pl.BlockSpec((tm, d), row),
                  pl.BlockSpec((1, 6, d), lambda i: (i // tiles_per_seq, 0, 0)),
                  pl.BlockSpec((1, d), const)] + [pl.BlockSpec(a.shape, const) for a in weights],
        out_specs=(pl.BlockSpec((tm * nt, LANES), row), pl.BlockSpec((tm, d), row),
                   pl.BlockSpec((N_EXPERTS, tm), lambda i: (0, i))),
        compiler_params=_cparams(1),
        name="mix",
    )(xf, attn, yssm, ga, gs, mod3, ln2_g.reshape(1, d), *weights)


def _route_kernel(score_ref, bias_ref, tri_ref, idx_ref, w_ref, rank_ref, cnt_ref, carry_ref):
    @pl.when(pl.program_id(0) == 0)
    def _():
        carry_ref[...] = jnp.zeros_like(carry_ref)

    scores = score_ref[...]
    ne, tn = scores.shape
    biased = scores + bias_ref[...]
    gsz = GROUP_SIZE
    sub = lax.broadcasted_iota(jnp.int32, (gsz, tn), 0)
    group_score = []
    for g in range(N_EXPERT_GROUPS):
        sg = biased[g * gsz:(g + 1) * gsz, :]
        m1 = jnp.max(sg, axis=0, keepdims=True)
        first = jnp.min(jnp.where(sg == m1, sub, gsz), axis=0, keepdims=True)
        m2 = jnp.max(jnp.where(sub == first, -jnp.inf, sg), axis=0, keepdims=True)
        group_score.append(m1 + m2)
    group_rows = []
    for g in range(N_EXPERT_GROUPS):
        beaten = jnp.zeros((1, tn), jnp.int32)
        for o in range(N_EXPERT_GROUPS):
            if o == g:
                continue
            wins = (group_score[o] > group_score[g])
            if o < g:
                wins = wins | (group_score[o] == group_score[g])
            beaten = beaten + jnp.where(wins, 1, 0)
        group_rows.append(jnp.broadcast_to(beaten < TOPK_GROUPS, (gsz, tn)))
    allowed = jnp.concatenate(group_rows, axis=0)
    cur = jnp.where(allowed, biased, MASK_VALUE)
    eio = lax.broadcasted_iota(jnp.int32, (ne, tn), 0)
    idx_rows, w_rows, hits = [], [], []
    for _ in range(TOP_K):
        vmax = jnp.max(cur, axis=0, keepdims=True)
        eidx = jnp.min(jnp.where(cur == vmax, eio, ne), axis=0, keepdims=True)
        hit = eio == eidx
        w_rows.append(jnp.sum(jnp.where(hit, scores, 0.0), axis=0, keepdims=True))
        idx_rows.append(eidx)
        hits.append(hit)
        cur = jnp.where(hit, -jnp.inf, cur)
    wts = jnp.concatenate(w_rows, axis=0)
    idx_ref[...] = jnp.concatenate(idx_rows, axis=0)
    w_ref[...] = wts / jnp.sum(wts, axis=0, keepdims=True) * ROUTED_SCALE

    onehot = jnp.zeros((ne, tn), F32)
    for hit in hits:
        onehot = onehot + jnp.where(hit, 1.0, 0.0)
    earlier = carry_ref[...] + _bdot(onehot, tri_ref[...])
    rank_rows = [jnp.sum(jnp.where(hit, earlier, 0.0), axis=0, keepdims=True) for hit in hits]
    rank_ref[...] = jnp.concatenate(rank_rows, axis=0).astype(jnp.int32)
    carry_ref[...] = carry_ref[...] + jnp.sum(onehot, axis=1, keepdims=True)
    cnt_ref[...] = carry_ref[...]


def _route(scores_t, router_bias):
    ne, n = scores_t.shape
    tn = ROUTE_TILE
    tri = jnp.asarray(np.arange(tn)[:, None] < np.arange(tn)[None, :], BF16)
    tok = lambda i: (0, i)
    const = lambda i: (0, 0)
    return pl.pallas_call(
        _route_kernel,
        out_shape=(jax.ShapeDtypeStruct((TOP_K, n), jnp.int32), jax.ShapeDtypeStruct((TOP_K, n), F32),
                   jax.ShapeDtypeStruct((TOP_K, n), jnp.int32), jax.ShapeDtypeStruct((ne, 1), F32)),
        grid=(n // tn,),
        in_specs=[pl.BlockSpec((ne, tn), tok), pl.BlockSpec((ne, 1), const), pl.BlockSpec((tn, tn), const)],
        out_specs=(pl.BlockSpec((TOP_K, tn), tok), pl.BlockSpec((TOP_K, tn), tok),
                   pl.BlockSpec((TOP_K, tn), tok), pl.BlockSpec((ne, 1), const)),
        scratch_shapes=[pltpu.VMEM((ne, 1), F32)],
        compiler_params=_cparams(1),
        name="route",
    )(scores_t, router_bias.reshape(ne, 1).astype(F32), tri)


def _dest_kernel(idx_ref, rank_ref, poffs_ref, dest_ref):
    idx = idx_ref[...]
    ne = poffs_ref.shape[0]
    eio = lax.broadcasted_iota(jnp.int32, (ne, idx.shape[1]), 0)
    poffs = poffs_ref[...]
    rows = [jnp.sum(jnp.where(eio == idx[kk:kk + 1, :], poffs, 0.0), axis=0, keepdims=True)
            for kk in range(idx.shape[0])]
    dest_ref[...] = rank_ref[...] + jnp.concatenate(rows, axis=0).astype(jnp.int32)


def _dest_slots(idx_t, rank_t, poffs):
    k, n = idx_t.shape
    tn = ROUTE_TILE
    ne = poffs.shape[0]
    tok = lambda i: (0, i)
    return pl.pallas_call(
        _dest_kernel,
        out_shape=jax.ShapeDtypeStruct((k, n), jnp.int32),
        grid=(n // tn,),
        in_specs=[pl.BlockSpec((k, tn), tok), pl.BlockSpec((k, tn), tok), pl.BlockSpec((ne, 1), lambda i: (0, 0))],
        out_specs=pl.BlockSpec((k, tn), tok),
        compiler_params=_cparams(1),
        name="dest_slots",
    )(idx_t, rank_t, poffs.astype(F32).reshape(ne, 1))


def _dispatch_tables(idx_t, counts, n_blocks):
    k, n = idx_t.shape
    n_assign = k * n
    rb = ROW_BLOCK
    ne = counts.shape[0]
    counts = counts.reshape(ne).astype(jnp.int32)
    offs = jnp.cumsum(counts) - counts
    nblk = (counts + rb - 1) // rb
    bend = jnp.cumsum(nblk)
    bstart = bend - nblk
    blocks = jnp.arange(n_blocks, dtype=jnp.int32)
    blk_e = jnp.minimum(jnp.searchsorted(bend, blocks, side='right'), ne - 1).astype(jnp.int32)
    src_start = (offs[blk_e] + (blocks - bstart[blk_e]) * rb).astype(jnp.int32)
    tok = lax.broadcasted_iota(jnp.int32, (k, n), 1)
    kk = lax.broadcasted_iota(jnp.int32, (k, n), 0)
    assert ne * n_assign < 2 ** 31
    keys = jnp.sort((idx_t * n_assign + tok * k + kk).reshape(n_assign))
    tok_sorted = (keys % n_assign) // k
    return tok_sorted, blk_e, src_start, bend[ne - 1:ne], bstart * rb


def _ffn_kernel(tok_ref, blk_e_ref, start_ref, used_ref, h2_hbm, wg_ref, wu_ref, wd_ref, y_ref, xbuf, sems):
    b = pl.program_id(0)
    rb = ROW_BLOCK
    d = wg_ref.shape[1]
    nt = d // LANES
    n_assign = tok_ref.shape[0]
    n_used = used_ref[0]
    slot = b % 2

    def issue(blk, s):
        start = start_ref[blk]

        def body(r, _):
            tok = tok_ref[jnp.minimum(start + r, n_assign - 1)]
            pltpu.make_async_copy(h2_hbm.at[pl.ds(pl.multiple_of(tok * nt, nt), nt)],
                                  xbuf.at[s, pl.ds(pl.multiple_of(r * nt, nt), nt)], sems.at[s]).start()
            return 0

        lax.fori_loop(0, rb, body, 0, unroll=ISSUE_UNROLL)

    @pl.when(b == 0)
    def _():
        issue(0, 0)

    @pl.when(b + 1 < n_used)
    def _():
        issue(b + 1, 1 - slot)

    @pl.when(b < n_used)
    def _():
        pltpu.make_async_copy(h2_hbm.at[pl.ds(0, rb * nt)], xbuf.at[slot], sems.at[slot]).wait()
        xb = _load_rows_from_tiles(xbuf.at[slot], rb, d).astype(BF16)
        hg = jnp.dot(xb, wg_ref[0].astype(BF16), preferred_element_type=F32)
        hu = jnp.dot(xb, wu_ref[0].astype(BF16), preferred_element_type=F32)
        _store_rows_as_tiles(y_ref, _bdot(_silu(hg) * hu, wd_ref[0]))

    @pl.when(b >= n_used)
    def _():
        y_ref[...] = jnp.zeros_like(y_ref)


def _expert_ffn(h2_tiles, tok_sorted, blk_e, src_start, n_used, w_gate, w_up, w_down):
    n_blocks = blk_e.shape[0]
    rb = ROW_BLOCK
    _, d, ed = w_gate.shape
    nt = d // LANES
    expert = lambda i, tok, be, st, nu: (be[i], 0, 0)
    grid_spec = pltpu.PrefetchScalarGridSpec(
        num_scalar_prefetch=4,
        grid=(n_blocks,),
        in_specs=[pl.BlockSpec(memory_space=pl.ANY),
                  pl.BlockSpec((1, d, ed), expert),
                  pl.BlockSpec((1, d, ed), expert),
                  pl.BlockSpec((1, ed, d), expert)],
        out_specs=pl.BlockSpec((rb * nt, LANES), lambda i, tok, be, st, nu: (i, 0)),
        scratch_shapes=[pltpu.VMEM((2, rb * nt, LANES), F32), pltpu.SemaphoreType.DMA((2,))],
    )
    return pl.pallas_call(
        _ffn_kernel,
        out_shape=jax.ShapeDtypeStruct((n_blocks * rb * nt, LANES), F32),
        grid_spec=grid_spec,
        compiler_params=_cparams(1),
        name="expert_ffn",
    )(tok_sorted, blk_e, src_start, n_used, h2_tiles, w_gate, w_up, w_down)


def _combine_kernel(dest_ref, y_hbm, base_ref, wt_ref, mod_ref, o_ref, buf, sems):
    i = pl.program_id(0)
    n_steps = pl.num_programs(0)
    tm, d = base_ref.shape
    k = wt_ref.shape[1]
    nt = d // LANES
    n_rows = k * tm
    slot = i % 2

    def issue(step, s):
        first = step * n_rows

        def body(r, _):
            src = dest_ref[first + r]
            pltpu.make_async_copy(y_hbm.at[pl.ds(pl.multiple_of(src * nt, nt), nt)],
                                  buf.at[s, pl.ds(pl.multiple_of(r * nt, nt), nt)], sems.at[s]).start()
            return 0

        lax.fori_loop(0, n_rows, body, 0, unroll=ISSUE_UNROLL)

    @pl.when(i == 0)
    def _():
        issue(0, 0)

    @pl.when(i + 1 < n_steps)
    def _():
        issue(i + 1, 1 - slot)

    pltpu.make_async_copy(y_hbm.at[pl.ds(0, n_rows * nt)], buf.at[slot], sems.at[slot]).wait()
    rows = buf.at[slot]
    wts = wt_ref[...]
    gate2 = mod_ref[0, 5:6, :]
    for c in range(nt):
        cols = slice(c * LANES, (c + 1) * LANES)
        routed = jnp.zeros((tm, LANES), F32)
        for kk in range(k):
            routed = routed + wts[:, kk:kk + 1] * rows[pl.ds(kk * tm * nt + c, tm, stride=nt), :]
        o_ref[:, cols] = base_ref[:, cols] + gate2[:, cols] * routed


def _combine(y_tiles, dest_t, w_t, base, mod3, seq):
    n, d = base.shape
    tm = COMBINE_TILE
    k = dest_t.shape[0]
    nt = d // LANES
    tiles_per_seq = seq // tm
    dest_tiles = dest_t.reshape(k, n // tm, tm).transpose(1, 0, 2).reshape(n * k)
    grid_spec = pltpu.PrefetchScalarGridSpec(
        num_scalar_prefetch=1,
        grid=(n // tm,),
        in_specs=[pl.BlockSpec(memory_space=pl.ANY),
                  pl.BlockSpec((tm, d), lambda i, dst: (i, 0)),
                  pl.BlockSpec((tm, k), lambda i, dst: (i, 0)),
                  pl.BlockSpec((1, 6, d), lambda i, dst: (i // tiles_per_seq, 0, 0))],
        out_specs=pl.BlockSpec((tm, d), lambda i, dst: (i, 0)),
        scratch_shapes=[pltpu.VMEM((2, k * tm * nt, LANES), F32), pltpu.SemaphoreType.DMA((2,))],
    )
    return pl.pallas_call(
        _combine_kernel,
        out_shape=jax.ShapeDtypeStruct((n, d), F32),
        grid_spec=grid_spec,
        compiler_params=_cparams(1),
        name="combine",
    )(dest_tiles, y_tiles, base, w_t.T, mod3)


def _hybrid_layer(x, cond, rel_bias, w_ada, b_ada, ln1_g, w_in, q_norm_g, k_norm_g,
                  ssm_lambda_re, ssm_lambda_im, ssm_log_dt, ssm_b_re, ssm_b_im, ssm_c_re, ssm_c_im,
                  ssm_d, ssm_w_glu, ssm_b_glu, w_up_attn, w_up_ssm, w_out, ln2_g,
                  w_router, router_bias, w_exp_gate, w_exp_up, w_exp_down,
                  w_sh_gate, w_sh_up, w_sh_down):
    bsz, seq, d = x.shape
    n = bsz * seq
    xf = x.reshape(n, d)
    mod3 = _adaln(cond, w_ada, b_ada).reshape(bsz, 6, d)

    q_gain = jnp.tile(q_norm_g.astype(F32), ATTN_HEADS).reshape(1, ATTN_WIDTH)
    k_gain = jnp.tile(k_norm_g.astype(F32), ATTN_HEADS).reshape(1, ATTN_WIDTH)
    q, k, v, u, ga, gs, kmean = _inproj(xf, mod3, ln1_g, w_in.astype(BF16), q_gain, k_gain, seq)

    attn = _moba_attention(q, k, v, kmean, rel_bias, bsz, seq)
    ops = _s5_operators(ssm_lambda_re, ssm_lambda_im, ssm_log_dt, ssm_b_re, ssm_b_im,
                        ssm_c_re, ssm_c_im, ssm_d, seq // SSM_CHUNK)
    yssm = _s5_scan(u, ops, bsz, seq)

    weights = {
        "glu": ssm_w_glu.astype(BF16), "b_glu": ssm_b_glu.astype(F32).reshape(1, -1),
        "up_attn": w_up_attn.astype(BF16), "up_ssm": w_up_ssm.astype(BF16), "out": w_out.astype(BF16),
        "router_t": w_router.T.astype(BF16),
        "sh_gu": jnp.concatenate([w_sh_gate, w_sh_up], axis=1).astype(BF16),
        "sh_down": w_sh_down.astype(BF16),
    }
    h2, base, scores_t = _mix(xf, attn, yssm, ga, gs, mod3, ln2_g, weights, seq)

    idx_t, w_t, rank_t, counts = _route(scores_t, router_bias)
    n_blocks = -(-(n * TOP_K) // ROW_BLOCK) + N_EXPERTS
    tok_sorted, blk_e, src_start, n_used, poffs = _dispatch_tables(idx_t, counts, n_blocks)
    dest_t = _dest_slots(idx_t, rank_t, poffs)
    y_tiles = _expert_ffn(h2, tok_sorted, blk_e, src_start, n_used, w_exp_gate, w_exp_up, w_exp_down)
    out = _combine(y_tiles, dest_t, w_t, base, mod3, seq)
    return out.reshape(bsz, seq, d)


def kernel(x, c, rel_bias, w_ada, b_ada, ln1_g, w_in, q_norm_g, k_norm_g, ssm_lambda_re, ssm_lambda_im, ssm_log_dt, ssm_b_re, ssm_b_im, ssm_c_re, ssm_c_im, ssm_d, ssm_w_glu, ssm_b_glu, w_up_attn, w_up_ssm, w_out, ln2_g, w_router, router_bias, w_exp_gate, w_exp_up, w_exp_down, w_sh_gate, w_sh_up, w_sh_down):
    for l in range(w_ada.shape[0]):
        x = _hybrid_layer(x, c, rel_bias, w_ada[l], b_ada[l], ln1_g[l], w_in[l], q_norm_g[l], k_norm_g[l],
                          ssm_lambda_re[l], ssm_lambda_im[l], ssm_log_dt[l], ssm_b_re[l], ssm_b_im[l],
                          ssm_c_re[l], ssm_c_im[l], ssm_d[l], ssm_w_glu[l], ssm_b_glu[l],
                          w_up_attn[l], w_up_ssm[l], w_out[l], ln2_g[l], w_router[l], router_bias[l],
                          w_exp_gate[l], w_exp_up[l], w_exp_down[l], w_sh_gate[l], w_sh_up[l], w_sh_down[l])
    return x
```

```python
import functools
import math

import numpy as np
import jax
import jax.numpy as jnp
from jax import lax
from jax.experimental import pallas as pl
from jax.experimental.pallas import tpu as pltpu

F32 = jnp.float32
BF16 = jnp.bfloat16

ATTN_HEADS = 8
HEAD_DIM = 64
ATTN_WIDTH = ATTN_HEADS * HEAD_DIM
MOBA_BLOCK = 256
MOBA_TOPK = 3
NUM_BUCKETS = 32
MAX_DISTANCE = 128
SSM_WIDTH = 512
SSM_GROUP = 16
SSM_GROUPS = SSM_WIDTH // SSM_GROUP
SSM_STATE = 64
N_EXPERTS = 256
TOP_K = 8
N_EXPERT_GROUPS = 8
TOPK_GROUPS = 4
GROUP_SIZE = N_EXPERTS // N_EXPERT_GROUPS
EXPERT_DIM = 256
ROUTED_SCALE = 2.5
EPS = 1e-6
MASK_VALUE = -1e30

LANES = 128
HEADS_PER_STEP = LANES // HEAD_DIM
SSM_CHUNK = 16
SUBLANES = 8
ROW_BLOCK = 256
TOKEN_TILE = 256
ROUTE_TILE = 256
COMBINE_TILE = 128
ISSUE_UNROLL = 8
VMEM_LIMIT = 56 * 1024 * 1024


def _store_rows_as_tiles(ref, val):
    rows, d = val.shape
    nt = d // LANES
    for c in range(nt):
        ref[pl.ds(c, rows, stride=nt), :] = val[:, c * LANES:(c + 1) * LANES]


def _load_rows_from_tiles(ref, rows, d):
    nt = d // LANES
    return jnp.concatenate([ref[pl.ds(c, rows, stride=nt), :] for c in range(nt)], axis=1)


def _cparams(n_axes, vmem=VMEM_LIMIT):
    return pltpu.CompilerParams(dimension_semantics=("arbitrary",) * n_axes, vmem_limit_bytes=vmem)


def _sigmoid(x):
    return 1.0 / (1.0 + jnp.exp(-x))


def _silu(x):
    return x * _sigmoid(x)


def _bdot(a, b):
    return jnp.dot(a.astype(BF16), b.astype(BF16), preferred_element_type=F32)


def _bdot_nt(a, b):
    return lax.dot_general(a.astype(BF16), b.astype(BF16), (((1,), (1,)), ((), ())),
                           preferred_element_type=F32)


def _adaln_kernel(c_ref, w_ref, b_ref, o_ref):
    o_ref[...] = _bdot(_silu(c_ref[...]), w_ref[...]) + b_ref[...]


def _adaln(c, w_ada, b_ada):
    bsz, d = c.shape
    n_out = w_ada.shape[1]
    return pl.pallas_call(
        _adaln_kernel,
        out_shape=jax.ShapeDtypeStruct((bsz, n_out), F32),
        grid=(n_out // d,),
        in_specs=[pl.BlockSpec((bsz, d), lambda j: (0, 0)),
                  pl.BlockSpec((d, d), lambda j: (0, j)),
                  pl.BlockSpec((1, d), lambda j: (0, j))],
        out_specs=pl.BlockSpec((bsz, d), lambda j: (0, j)),
        compiler_params=_cparams(1),
        name="adaln",
    )(c, w_ada, b_ada.reshape(1, n_out))


def _modulated_norm(x, gain, shift, scale):
    y = x * lax.rsqrt(jnp.mean(x * x, axis=-1, keepdims=True) + EPS) * gain
    return y * (1.0 + scale) + shift


def _head_norm(t, seg, gain):
    ms = _bdot(t * t, seg)
    return t * lax.rsqrt(ms + EPS) * gain


def _inproj_kernel(x_ref, mod_ref, ln_ref, w_ref, seg_ref, qg_ref, kg_ref,
                   q_ref, k_ref, v_ref, u_ref, ga_ref, gs_ref, km_ref):
    aw, sw, d = ATTN_WIDTH, SSM_WIDTH, x_ref.shape[1]
    h = _modulated_norm(x_ref[...], ln_ref[...], mod_ref[0, 0:1, :], mod_ref[0, 1:2, :]).astype(BF16)
    seg = seg_ref[...]
    q = jnp.dot(h, w_ref[:, 0:aw], preferred_element_type=F32)
    q_ref[...] = _head_norm(q, seg, qg_ref[...])
    k = jnp.dot(h, w_ref[:, aw:2 * aw], preferred_element_type=F32)
    kn = _head_norm(k, seg, kg_ref[...])
    k_ref[...] = kn.astype(BF16)
    km_ref[0] = jnp.mean(kn, axis=0, keepdims=True)
    v_ref[...] = jnp.dot(h, w_ref[:, 2 * aw:3 * aw], preferred_element_type=F32).astype(BF16)
    o = 3 * aw
    u_ref[...] = jnp.dot(h, w_ref[:, o:o + sw], preferred_element_type=F32).astype(BF16)
    o += sw
    ga_ref[...] = jnp.dot(h, w_ref[:, o:o + d], preferred_element_type=F32).astype(BF16)
    o += d
    gs_ref[...] = jnp.dot(h, w_ref[:, o:o + d], preferred_element_type=F32).astype(BF16)


def _inproj(xf, mod3, ln1_g, w_in_b, q_gain, k_gain, seq):
    n, d = xf.shape
    tm = MOBA_BLOCK
    tiles_per_seq = seq // tm
    aw, sw = ATTN_WIDTH, SSM_WIDTH
    head_of_lane = np.arange(aw) // HEAD_DIM
    seg = jnp.asarray((head_of_lane[:, None] == head_of_lane[None, :]) / HEAD_DIM, BF16)
    row = lambda i: (i, 0)
    const = lambda i: (0, 0)
    return pl.pallas_call(
        _inproj_kernel,
        out_shape=(jax.ShapeDtypeStruct((n, aw), F32),
                   jax.ShapeDtypeStruct((n, aw), BF16),
                   jax.ShapeDtypeStruct((n, aw), BF16),
                   jax.ShapeDtypeStruct((n, sw), BF16),
                   jax.ShapeDtypeStruct((n, d), BF16),
                   jax.ShapeDtypeStruct((n, d), BF16),
                   jax.ShapeDtypeStruct((n // tm, 1, aw), F32)),
        grid=(n // tm,),
        in_specs=[pl.BlockSpec((tm, d), row),
                  pl.BlockSpec((1, 6, d), lambda i: (i // tiles_per_seq, 0, 0)),
                  pl.BlockSpec((1, d), const),
                  pl.BlockSpec(w_in_b.shape, const),
                  pl.BlockSpec((aw, aw), const),
                  pl.BlockSpec((1, aw), const),
                  pl.BlockSpec((1, aw), const)],
        out_specs=(pl.BlockSpec((tm, aw), row), pl.BlockSpec((tm, aw), row), pl.BlockSpec((tm, aw), row),
                   pl.BlockSpec((tm, sw), row), pl.BlockSpec((tm, d), row), pl.BlockSpec((tm, d), row),
                   pl.BlockSpec((1, 1, aw), lambda i: (i, 0, 0))),
        compiler_params=_cparams(1),
        name="inproj",
    )(xf, mod3, ln1_g.reshape(1, d), w_in_b, seg, q_gain, k_gain)


def _t5_bucket(rel):
    n = jnp.maximum(rel, 0)
    max_exact = NUM_BUCKETS // 2
    nf = jnp.maximum(n, 1).astype(F32)
    large = max_exact + (jnp.log(nf / max_exact) / math.log(MAX_DISTANCE / max_exact)
                         * (NUM_BUCKETS - max_exact)).astype(jnp.int32)
    large = jnp.minimum(large, NUM_BUCKETS - 1)
    return jnp.where(n < max_exact, n, large)


def _bias_tables(rel_bias):
    blk = MOBA_BLOCK
    assert blk + 1 >= MAX_DISTANCE
    rel = jnp.arange(blk)[None, :] - jnp.arange(blk)[:, None]
    table = rel_bias.astype(F32)
    table = table - table[NUM_BUCKETS - 1][None, :]

    def lookup(r):
        onehot = jax.nn.one_hot(_t5_bucket(r), NUM_BUCKETS, dtype=F32)
        return jnp.einsum('kqn,nh->hkq', onehot, table, precision=lax.Precision.HIGHEST)

    return lookup(rel), lookup(rel + blk)


def _select_blocks(gate_t, n_past):
    nb, tq = gate_t.shape
    blk = lax.broadcasted_iota(jnp.int32, (nb, tq), 0)
    beaten = jnp.zeros((nb, tq), jnp.int32)
    for m in range(nb):
        gm = gate_t[m:m + 1, :]
        wins = (gm > gate_t) | ((gm == gate_t) & (m < blk))
        beaten = beaten + jnp.where(wins & (m < n_past), 1, 0)
    return jnp.where((blk < n_past) & (beaten < MOBA_TOPK), 1.0, 0.0)


def _attn_kernel(q_ref, k_ref, vt_ref, km_ref, bown_ref, bprev_ref, o_ref, sel_ref):
    qi = pl.program_id(2)
    tq = q_ref.shape[0]
    blk = MOBA_BLOCK
    hd = HEAD_DIM
    heads = range(HEADS_PER_STEP)
    q = q_ref[...]
    lane = lax.broadcasted_iota(jnp.int32, (tq, LANES), 1)
    kpos = lax.broadcasted_iota(jnp.int32, (blk, tq), 0)
    qpos = lax.broadcasted_iota(jnp.int32, (blk, tq), 1)
    scale = hd ** -0.5
    k_own = k_ref[pl.ds(pl.multiple_of(qi * blk, blk), blk), :]
    n_far = jnp.maximum(qi - 1, 0)

    qbs = []
    for h in heads:
        in_head = (lane >= h * hd) & (lane < (h + 1) * hd)
        qm = jnp.where(in_head, q, 0.0)
        gate_t = lax.dot_general(km_ref[0], qm, (((1,), (1,)), ((), ())),
                                 precision=lax.Precision.HIGHEST, preferred_element_type=F32)
        sel_ref[h] = _select_blocks(gate_t, qi)
        qbs.append((qm * scale).astype(BF16))

    def first(h):
        s = _bdot_nt(k_own, qbs[h]) + bown_ref[h]
        s = jnp.where(kpos <= qpos, s, MASK_VALUE)
        m = jnp.max(s, axis=0, keepdims=True)
        p = jnp.exp(s - m)
        l = jnp.sum(p, axis=0, keepdims=True)
        acc = jnp.dot(vt_ref[0, qi, h * hd:(h + 1) * hd, :], p.astype(BF16), preferred_element_type=F32)
        return m, l, acc

    def update(carry, h, s, vts):
        m, l, acc = carry
        m_new = jnp.maximum(m, jnp.max(s, axis=0, keepdims=True))
        alpha = jnp.exp(m - m_new)
        p = jnp.exp(s - m_new)
        l = alpha * l + jnp.sum(p, axis=0, keepdims=True)
        p = p.astype(BF16)
        acc = alpha * acc
        for i, vt in enumerate(vts):
            acc = acc + jnp.dot(vt[h * hd:(h + 1) * hd, :], p[i * blk:(i + 1) * blk, :],
                                preferred_element_type=F32)
        return m_new, l, acc

    def far_pair(pi, carries):
        j = 2 * pi
        kb = k_ref[pl.ds(pl.multiple_of(j * blk, blk), 2 * blk), :]
        vts = (vt_ref[0, j], vt_ref[0, j + 1])
        second_is_far = j + 1 < n_far
        out = []
        for h in heads:
            s = _bdot_nt(kb, qbs[h])
            c0 = sel_ref[h, pl.ds(j, 1), :] > 0.5
            c1 = (sel_ref[h, pl.ds(j + 1, 1), :] > 0.5) & second_is_far
            chosen = jnp.concatenate([jnp.broadcast_to(c0, (blk, tq)), jnp.broadcast_to(c1, (blk, tq))], axis=0)
            out.append(update(carries[h], h, jnp.where(chosen, s, MASK_VALUE), vts))
        return tuple(out)

    def prev_block(carries):
        j = qi - 1
        kb = k_ref[pl.ds(pl.multiple_of(j * blk, blk), blk), :]
        out = []
        for h in heads:
            s = _bdot_nt(kb, qbs[h]) + bprev_ref[h]
            s = jnp.where(sel_ref[h, pl.ds(j, 1), :] > 0.5, s, MASK_VALUE)
            out.append(update(carries[h], h, s, (vt_ref[0, j],)))
        return tuple(out)

    carries = tuple(first(h) for h in heads)
    carries = lax.fori_loop(0, (n_far + 1) // 2, far_pair, carries)
    carries = lax.cond(qi >= 1, prev_block, lambda c: c, carries)
    out_t = jnp.concatenate([acc / l for _, l, acc in carries], axis=0)
    o_ref[...] = out_t.T.astype(o_ref.dtype)


def _moba_attention(q, k, v, kmean, rel_bias, bsz, seq):
    n, aw = q.shape
    blk = MOBA_BLOCK
    nb = seq // blk
    own, prev = _bias_tables(rel_bias)
    hps = HEADS_PER_STEP
    npair = aw // LANES
    vt = v.reshape(bsz, nb, blk, aw).transpose(0, 1, 3, 2)
    return pl.pallas_call(
        _attn_kernel,
        out_shape=jax.ShapeDtypeStruct((n, aw), BF16),
        grid=(bsz, npair, nb),
        in_specs=[pl.BlockSpec((blk, LANES), lambda b, hp, qi: (b * nb + qi, hp)),
                  pl.BlockSpec((seq, LANES), lambda b, hp, qi: (b, hp)),
                  pl.BlockSpec((1, nb, LANES, blk), lambda b, hp, qi: (b, 0, hp, 0)),
                  pl.BlockSpec((1, nb, LANES), lambda b, hp, qi: (b, 0, hp)),
                  pl.BlockSpec((hps, blk, blk), lambda b, hp, qi: (hp, 0, 0)),
                  pl.BlockSpec((hps, blk, blk), lambda b, hp, qi: (hp, 0, 0))],
        out_specs=pl.BlockSpec((blk, LANES), lambda b, hp, qi: (b * nb + qi, hp)),
        scratch_shapes=[pltpu.VMEM((hps, nb, blk), F32)],
        compiler_params=_cparams(3),
        name="moba_attention",
    )(q, k, vt, kmean.reshape(bsz, nb, aw), own, prev)


def _s5_operators(lambda_re, lambda_im, log_dt, b_re, b_im, c_re, c_im, d_skip, n_chunks):
    hi = lax.Precision.HIGHEST
    L, G, P, C = SSM_CHUNK, SSM_GROUPS, SSM_STATE, SSM_GROUP
    lam_re = jnp.minimum(lambda_re.astype(F32), -1e-4)
    lam_im = lambda_im.astype(F32)
    dt = jnp.exp(log_dt.astype(F32))[:, None]
    z_re, z_im = lam_re * dt, lam_im * dt

    def a_pow(nvec):
        nv = jnp.asarray(nvec, F32)[:, None, None]
        mag = jnp.exp(nv * z_re)
        return mag * jnp.cos(nv * z_im), mag * jnp.sin(nv * z_im)

    a_re, a_im = a_pow([1.0])
    a_re, a_im = a_re[0], a_im[0]
    den = lam_re * lam_re + lam_im * lam_im
    nr = a_re - 1.0
    coef_re = (nr * lam_re + a_im * lam_im) / den
    coef_im = (a_im * lam_re - nr * lam_im) / den
    br, bi = b_re.astype(F32), b_im.astype(F32)
    bbar_re = coef_re[..., None] * br - coef_im[..., None] * bi
    bbar_im = coef_re[..., None] * bi + coef_im[..., None] * br
    cr, ci = c_re.astype(F32), c_im.astype(F32)

    pw_re, pw_im = a_pow(np.arange(L + 1))
    cb_re = cr[None] * pw_re[:, :, None, :] - ci[None] * pw_im[:, :, None, :]
    cb_im = cr[None] * pw_im[:, :, None, :] + ci[None] * pw_re[:, :, None, :]
    kern = (jnp.einsum('jgop,gpi->jgoi', cb_re[:L], bbar_re, precision=hi)
            - jnp.einsum('jgop,gpi->jgoi', cb_im[:L], bbar_im, precision=hi))
    sig = np.arange(L)[:, None]
    tau = np.arange(L)[None, :]
    lag = np.clip(tau - sig, 0, L - 1)
    causal = jnp.asarray((tau >= sig), F32)
    t_op = kern[lag] * causal[:, :, None, None, None]
    t_op = t_op.transpose(2, 0, 4, 1, 3).reshape(G, L * C, L * C)
    d_g = d_skip.astype(F32).reshape(G, C)
    t_op = t_op + jnp.eye(L * C, dtype=F32)[None] * jnp.tile(d_g, (1, L))[:, None, :]

    rp_re, rp_im = pw_re[L - 1 - np.arange(L)], pw_im[L - 1 - np.arange(L)]
    p_re = rp_re[..., None] * bbar_re[None] - rp_im[..., None] * bbar_im[None]
    p_im = rp_re[..., None] * bbar_im[None] + rp_im[..., None] * bbar_re[None]
    p_op = jnp.concatenate([p_re, p_im], axis=2)
    p_op = p_op.transpose(1, 0, 3, 2).reshape(G, L * C, 2 * P)

    q_re = cb_re[1:].transpose(1, 3, 0, 2)
    q_im = -cb_im[1:].transpose(1, 3, 0, 2)
    q_op = jnp.concatenate([q_re, q_im], axis=1).reshape(G, 2 * P, L * C)

    n_steps = max(1, int(math.ceil(math.log2(n_chunks))))
    dk_re, dk_im = a_pow([float(L * 2 ** k) for k in range(n_steps)])
    a1 = jnp.concatenate([dk_re, dk_re], axis=-1).transpose(1, 0, 2)
    a2 = jnp.concatenate([-dk_im, dk_im], axis=-1).transpose(1, 0, 2)
    return t_op.astype(BF16), p_op.astype(BF16), q_op.astype(BF16), a1, a2


def _s5_kernel(u_ref, t_ref, p_ref, q_ref, a1_ref, a2_ref, y_ref, *, n_chunks):
    u = u_ref[0]
    s = jnp.dot(u, p_ref[0], preferred_element_type=F32)
    rows, width = s.shape
    chunk = lax.broadcasted_iota(jnp.int32, (rows, width), 0) % n_chunks
    a1 = a1_ref[0]
    a2 = a2_ref[0]
    x = jnp.where(chunk >= 1, pltpu.roll(s, 1, axis=0), 0.0)
    for kk in range(a1.shape[0]):
        dist = 2 ** kk
        if dist >= n_chunks:
            break
        xs = jnp.where(chunk >= dist, pltpu.roll(x, dist, axis=0), 0.0)
        x = x + a1[kk:kk + 1, :] * xs + a2[kk:kk + 1, :] * pltpu.roll(xs, SSM_STATE, axis=1)
    y = jnp.dot(u, t_ref[0], preferred_element_type=F32)
    y_ref[0] = y + jnp.dot(x.astype(BF16), q_ref[0], preferred_element_type=F32)


def _s5_scan(u, ops, bsz, seq):
    t_op, p_op, q_op, a1, a2 = ops
    L, G, C = SSM_CHUNK, SSM_GROUPS, SSM_GROUP
    nc = seq // L
    rows = bsz * nc
    w = L * C
    ug = u.reshape(rows, L, G, C).transpose(2, 0, 1, 3).reshape(G, rows, w)
    grp = lambda g: (g, 0, 0)
    y = pl.pallas_call(
        functools.partial(_s5_kernel, n_chunks=nc),
        out_shape=jax.ShapeDtypeStruct((G, rows, w), F32),
        grid=(G,),
        in_specs=[pl.BlockSpec((1, rows, w), grp),
                  pl.BlockSpec((1, w, w), grp),
                  pl.BlockSpec((1, w, 2 * SSM_STATE), grp),
                  pl.BlockSpec((1, 2 * SSM_STATE, w), grp),
                  pl.BlockSpec((1,) + a1.shape[1:], grp),
                  pl.BlockSpec((1,) + a2.shape[1:], grp)],
        out_specs=pl.BlockSpec((1, rows, w), grp),
        compiler_params=_cparams(1),
        name="s5_scan",
    )(ug, t_op, p_op, q_op, a1, a2)
    return y.reshape(G, rows, L, C).transpose(1, 2, 0, 3).reshape(bsz * seq, G * C)


def _gelu_tanh(x):
    return 0.5 * x * (1.0 + jnp.tanh(math.sqrt(2.0 / math.pi) * (x + 0.044715 * (x * x * x))))


def _mix_kernel(x_ref, attn_ref, yssm_ref, ga_ref, gs_ref, mod_ref, ln_ref,
                wglu_ref, bglu_ref, wua_ref, wus_ref, wout_ref, wrt_ref, wsgu_ref, wsd_ref,
                h2_ref, base_ref, score_ref):
    g = _gelu_tanh(yssm_ref[...])
    glu = g * _sigmoid(_bdot(g, wglu_ref[...]) + bglu_ref[...])
    y_attn = jnp.dot(attn_ref[...], wua_ref[...], preferred_element_type=F32)
    y_ssm = _bdot(glu, wus_ref[...])
    mixed = _sigmoid(ga_ref[...].astype(F32)) * y_attn + _sigmoid(gs_ref[...].astype(F32)) * y_ssm
    gate1 = mod_ref[0, 2:3, :]
    x1 = x_ref[...] + gate1 * _bdot(mixed, wout_ref[...])
    h2 = _modulated_norm(x1, ln_ref[...], mod_ref[0, 3:4, :], mod_ref[0, 4:5, :])
    _store_rows_as_tiles(h2_ref, h2)
    h2b = h2.astype(BF16)
    score_ref[...] = _sigmoid(_bdot_nt(wrt_ref[...], h2b))
    gu = jnp.dot(h2b, wsgu_ref[...], preferred_element_type=F32)
    sd = wsd_ref.shape[0]
    shared = _bdot(_silu(gu[:, :sd]) * gu[:, sd:], wsd_ref[...])
    base_ref[...] = x1 + mod_ref[0, 5:6, :] * shared


def _mix(xf, attn, yssm, ga, gs, mod3, ln2_g, w, seq):
    n, d = xf.shape
    tm = TOKEN_TILE
    tiles_per_seq = seq // tm
    row = lambda i: (i, 0)
    const = lambda i: (0, 0)
    nt = d // LANES
    weights = [w["glu"], w["b_glu"], w["up_attn"], w["up_ssm"], w["out"], w["router_t"], w["sh_gu"], w["sh_down"]]
    return pl.pallas_call(
        _mix_kernel,
        out_shape=(jax.ShapeDtypeStruct((n * nt, LANES), F32),
                   jax.ShapeDtypeStruct((n, d), F32),
                   jax.ShapeDtypeStruct((N_EXPERTS, n), F32)),
        grid=(n // tm,),
        in_specs=[pl.BlockSpec((tm, d), row),
                  pl.BlockSpec((tm, attn.shape[1]), row),
                  pl.BlockSpec((tm, yssm.shape[1]), row),
                  pl.BlockSpec((tm, d), row),
                  pl.BlockSpec((tm, d), row),
                  pl.BlockSpec((1, 6, d), lambda i: (i // tiles_per_seq, 0, 0)),
                  pl.BlockSpec((1, d), const)] + [pl.BlockSpec(a.shape, const) for a in weights],
        out_specs=(pl.BlockSpec((tm * nt, LANES), row), pl.BlockSpec((tm, d), row),
                   pl.BlockSpec((N_EXPERTS, tm), lambda i: (0, i))),
        compiler_params=_cparams(1),
        name="mix",
    )(xf, attn, yssm, ga, gs, mod3, ln2_g.reshape(1, d), *weights)


def _route_kernel(score_ref, bias_ref, tri_ref, idx_ref, w_ref, rank_ref, cnt_ref, carry_ref):
    @pl.when(pl.program_id(0) == 0)
    def _():
        carry_ref[...] = jnp.zeros_like(carry_ref)

    scores = score_ref[...]
    ne, tn = scores.shape
    biased = scores + bias_ref[...]
    gsz = GROUP_SIZE
    sub = lax.broadcasted_iota(jnp.int32, (gsz, tn), 0)
    group_score = []
    for g in range(N_EXPERT_GROUPS):
        sg = biased[g * gsz:(g + 1) * gsz, :]
        m1 = jnp.max(sg, axis=0, keepdims=True)
        first = jnp.min(jnp.where(sg == m1, sub, gsz), axis=0, keepdims=True)
        m2 = jnp.max(jnp.where(sub == first, -jnp.inf, sg), axis=0, keepdims=True)
        group_score.append(m1 + m2)
    group_rows = []
    for g in range(N_EXPERT_GROUPS):
        beaten = jnp.zeros((1, tn), jnp.int32)
        for o in range(N_EXPERT_GROUPS):
            if o == g:
                continue
            wins = (group_score[o] > group_score[g])
            if o < g:
                wins = wins | (group_score[o] == group_score[g])
            beaten = beaten + jnp.where(wins, 1, 0)
        group_rows.append(jnp.broadcast_to(beaten < TOPK_GROUPS, (gsz, tn)))
    allowed = jnp.concatenate(group_rows, axis=0)
    cur = jnp.where(allowed, biased, MASK_VALUE)
    eio = lax.broadcasted_iota(jnp.int32, (ne, tn), 0)
    idx_rows, w_rows, hits = [], [], []
    for _ in range(TOP_K):
        vmax = jnp.max(cur, axis=0, keepdims=True)
        eidx = jnp.min(jnp.where(cur == vmax, eio, ne), axis=0, keepdims=True)
        hit = eio == eidx
        w_rows.append(jnp.sum(jnp.where(hit, scores, 0.0), axis=0, keepdims=True))
        idx_rows.append(eidx)
        hits.append(hit)
        cur = jnp.where(hit, -jnp.inf, cur)
    wts = jnp.concatenate(w_rows, axis=0)
    idx_ref[...] = jnp.concatenate(idx_rows, axis=0)
    w_ref[...] = wts / jnp.sum(wts, axis=0, keepdims=True) * ROUTED_SCALE

    onehot = jnp.zeros((ne, tn), F32)
    for hit in hits:
        onehot = onehot + jnp.where(hit, 1.0, 0.0)
    earlier = carry_ref[...] + _bdot(onehot, tri_ref[...])
    rank_rows = [jnp.sum(jnp.where(hit, earlier, 0.0), axis=0, keepdims=True) for hit in hits]
    rank_ref[...] = jnp.concatenate(rank_rows, axis=0).astype(jnp.int32)
    carry_ref[...] = carry_ref[...] + jnp.sum(onehot, axis=1, keepdims=True)
    cnt_ref[...] = carry_ref[...]


def _route(scores_t, router_bias):
    ne, n = scores_t.shape
    tn = ROUTE_TILE
    tri = jnp.asarray(np.arange(tn)[:, None] < np.arange(tn)[None, :], BF16)
    tok = lambda i: (0, i)
    const = lambda i: (0, 0)
    return pl.pallas_call(
        _route_kernel,
        out_shape=(jax.ShapeDtypeStruct((TOP_K, n), jnp.int32), jax.ShapeDtypeStruct((TOP_K, n), F32),
                   jax.ShapeDtypeStruct((TOP_K, n), jnp.int32), jax.ShapeDtypeStruct((ne, 1), F32)),
        grid=(n // tn,),
        in_specs=[pl.BlockSpec((ne, tn), tok), pl.BlockSpec((ne, 1), const), pl.BlockSpec((tn, tn), const)],
        out_specs=(pl.BlockSpec((TOP_K, tn), tok), pl.BlockSpec((TOP_K, tn), tok),
                   pl.BlockSpec((TOP_K, tn), tok), pl.BlockSpec((ne, 1), const)),
        scratch_shapes=[pltpu.VMEM((ne, 1), F32)],
        compiler_params=_cparams(1),
        name="route",
    )(scores_t, router_bias.reshape(ne, 1).astype(F32), tri)


def _dest_kernel(idx_ref, rank_ref, poffs_ref, dest_ref):
    idx = idx_ref[...]
    ne = poffs_ref.shape[0]
    eio = lax.broadcasted_iota(jnp.int32, (ne, idx.shape[1]), 0)
    poffs = poffs_ref[...]
    rows = [jnp.sum(jnp.where(eio == idx[kk:kk + 1, :], poffs, 0.0), axis=0, keepdims=True)
            for kk in range(idx.shape[0])]
    dest_ref[...] = rank_ref[...] + jnp.concatenate(rows, axis=0).astype(jnp.int32)


def _dest_slots(idx_t, rank_t, poffs):
    k, n = idx_t.shape
    tn = ROUTE_TILE
    ne = poffs.shape[0]
    tok = lambda i: (0, i)
    return pl.pallas_call(
        _dest_kernel,
        out_shape=jax.ShapeDtypeStruct((k, n), jnp.int32),
        grid=(n // tn,),
        in_specs=[pl.BlockSpec((k, tn), tok), pl.BlockSpec((k, tn), tok), pl.BlockSpec((ne, 1), lambda i: (0, 0))],
        out_specs=pl.BlockSpec((k, tn), tok),
        compiler_params=_cparams(1),
        name="dest_slots",
    )(idx_t, rank_t, poffs.astype(F32).reshape(ne, 1))


def _dispatch_tables(idx_t, counts, n_blocks):
    k, n = idx_t.shape
    n_assign = k * n
    rb = ROW_BLOCK
    ne = counts.shape[0]
    counts = counts.reshape(ne).astype(jnp.int32)
    offs = jnp.cumsum(counts) - counts
    nblk = (counts + rb - 1) // rb
    bend = jnp.cumsum(nblk)
    bstart = bend - nblk
    blocks = jnp.arange(n_blocks, dtype=jnp.int32)
    blk_e = jnp.minimum(jnp.searchsorted(bend, blocks, side='right'), ne - 1).astype(jnp.int32)
    src_start = (offs[blk_e] + (blocks - bstart[blk_e]) * rb).astype(jnp.int32)
    tok = lax.broadcasted_iota(jnp.int32, (k, n), 1)
    kk = lax.broadcasted_iota(jnp.int32, (k, n), 0)
    assert ne * n_assign < 2 ** 31
    keys = jnp.sort((idx_t * n_assign + tok * k + kk).reshape(n_assign))
    tok_sorted = (keys % n_assign) // k
    return tok_sorted, blk_e, src_start, bend[ne - 1:ne], bstart * rb


def _ffn_kernel(tok_ref, blk_e_ref, start_ref, used_ref, h2_hbm, wg_ref, wu_ref, wd_ref, y_ref, xbuf, sems):
    b = pl.program_id(0)
    rb = ROW_BLOCK
    d = wg_ref.shape[1]
    nt = d // LANES
    n_assign = tok_ref.shape[0]
    n_used = used_ref[0]
    slot = b % 2

    def issue(blk, s):
        start = start_ref[blk]

        def body(r, _):
            tok = tok_ref[jnp.minimum(start + r, n_assign - 1)]
            pltpu.make_async_copy(h2_hbm.at[pl.ds(pl.multiple_of(tok * nt, nt), nt)],
                                  xbuf.at[s, pl.ds(pl.multiple_of(r * nt, nt), nt)], sems.at[s]).start()
            return 0

        lax.fori_loop(0, rb, body, 0, unroll=ISSUE_UNROLL)

    @pl.when(b == 0)
    def _():
        issue(0, 0)

    @pl.when(b + 1 < n_used)
    def _():
        issue(b + 1, 1 - slot)

    @pl.when(b < n_used)
    def _():
        pltpu.make_async_copy(h2_hbm.at[pl.ds(0, rb * nt)], xbuf.at[slot], sems.at[slot]).wait()
        xb = _load_rows_from_tiles(xbuf.at[slot], rb, d).astype(BF16)
        hg = jnp.dot(xb, wg_ref[0].astype(BF16), preferred_element_type=F32)
        hu = jnp.dot(xb, wu_ref[0].astype(BF16), preferred_element_type=F32)
        _store_rows_as_tiles(y_ref, _bdot(_silu(hg) * hu, wd_ref[0]))

    @pl.when(b >= n_used)
    def _():
        y_ref[...] = jnp.zeros_like(y_ref)


def _expert_ffn(h2_tiles, tok_sorted, blk_e, src_start, n_used, w_gate, w_up, w_down):
    n_blocks = blk_e.shape[0]
    rb = ROW_BLOCK
    _, d, ed = w_gate.shape
    nt = d // LANES
    expert = lambda i, tok, be, st, nu: (be[i], 0, 0)
    grid_spec = pltpu.PrefetchScalarGridSpec(
        num_scalar_prefetch=4,
        grid=(n_blocks,),
        in_specs=[pl.BlockSpec(memory_space=pl.ANY),
                  pl.BlockSpec((1, d, ed), expert),
                  pl.BlockSpec((1, d, ed), expert),
                  pl.BlockSpec((1, ed, d), expert)],
        out_specs=pl.BlockSpec((rb * nt, LANES), lambda i, tok, be, st, nu: (i, 0)),
        scratch_shapes=[pltpu.VMEM((2, rb * nt, LANES), F32), pltpu.SemaphoreType.DMA((2,))],
    )
    return pl.pallas_call(
        _ffn_kernel,
        out_shape=jax.ShapeDtypeStruct((n_blocks * rb * nt, LANES), F32),
        grid_spec=grid_spec,
        compiler_params=_cparams(1),
        name="expert_ffn",
    )(tok_sorted, blk_e, src_start, n_used, h2_tiles, w_gate, w_up, w_down)


def _combine_kernel(dest_ref, y_hbm, base_ref, wt_ref, mod_ref, o_ref, buf, sems):
    i = pl.program_id(0)
    n_steps = pl.num_programs(0)
    tm, d = base_ref.shape
    k = wt_ref.shape[1]
    nt = d // LANES
    n_rows = k * tm
    slot = i % 2

    def issue(step, s):
        first = step * n_rows

        def body(r, _):
            src = dest_ref[first + r]
            pltpu.make_async_copy(y_hbm.at[pl.ds(pl.multiple_of(src * nt, nt), nt)],
                                  buf.at[s, pl.ds(pl.multiple_of(r * nt, nt), nt)], sems.at[s]).start()
            return 0

        lax.fori_loop(0, n_rows, body, 0, unroll=ISSUE_UNROLL)

    @pl.when(i == 0)
    def _():
        issue(0, 0)

    @pl.when(i + 1 < n_steps)
    def _():
        issue(i + 1, 1 - slot)

    pltpu.make_async_copy(y_hbm.at[pl.ds(0, n_rows * nt)], buf.at[slot], sems.at[slot]).wait()
    rows = buf.at[slot]
    wts = wt_ref[...]
    gate2 = mod_ref[0, 5:6, :]
    for c in range(nt):
        cols = slice(c * LANES, (c + 1) * LANES)
        routed = jnp.zeros((tm, LANES), F32)
        for kk in range(k):
            routed = routed + wts[:, kk:kk + 1] * rows[pl.ds(kk * tm * nt + c, tm, stride=nt), :]
        o_ref[:, cols] = base_ref[:, cols] + gate2[:, cols] * routed


def _combine(y_tiles, dest_t, w_t, base, mod3, seq):
    n, d = base.shape
    tm = COMBINE_TILE
    k = dest_t.shape[0]
    nt = d // LANES
    tiles_per_seq = seq // tm
    dest_tiles = dest_t.reshape(k, n // tm, tm).transpose(1, 0, 2).reshape(n * k)
    grid_spec = pltpu.PrefetchScalarGridSpec(
        num_scalar_prefetch=1,
        grid=(n // tm,),
        in_specs=[pl.BlockSpec(memory_space=pl.ANY),
                  pl.BlockSpec((tm, d), lambda i, dst: (i, 0)),
                  pl.BlockSpec((tm, k), lambda i, dst: (i, 0)),
                  pl.BlockSpec((1, 6, d), lambda i, dst: (i // tiles_per_seq, 0, 0))],
        out_specs=pl.BlockSpec((tm, d), lambda i, dst: (i, 0)),
        scratch_shapes=[pltpu.VMEM((2, k * tm * nt, LANES), F32), pltpu.SemaphoreType.DMA((2,))],
    )
    return pl.pallas_call(
        _combine_kernel,
        out_shape=jax.ShapeDtypeStruct((n, d), F32),
        grid_spec=grid_spec,
        compiler_params=_cparams(1),
        name="combine",
    )(dest_tiles, y_tiles, base, w_t.T, mod3)


def _hybrid_layer(x, cond, rel_bias, w_ada, b_ada, ln1_g, w_in, q_norm_g, k_norm_g,
                  ssm_lambda_re, ssm_lambda_im, ssm_log_dt, ssm_b_re, ssm_b_im, ssm_c_re, ssm_c_im,
                  ssm_d, ssm_w_glu, ssm_b_glu, w_up_attn, w_up_ssm, w_out, ln2_g,
                  w_router, router_bias, w_exp_gate, w_exp_up, w_exp_down,
                  w_sh_gate, w_sh_up, w_sh_down):
    bsz, seq, d = x.shape
    n = bsz * seq
    xf = x.reshape(n, d)
    mod3 = _adaln(cond, w_ada, b_ada).reshape(bsz, 6, d)

    q_gain = jnp.tile(q_norm_g.astype(F32), ATTN_HEADS).reshape(1, ATTN_WIDTH)
    k_gain = jnp.tile(k_norm_g.astype(F32), ATTN_HEADS).reshape(1, ATTN_WIDTH)
    q, k, v, u, ga, gs, kmean = _inproj(xf, mod3, ln1_g, w_in.astype(BF16), q_gain, k_gain, seq)

    attn = _moba_attention(q, k, v, kmean, rel_bias, bsz, seq)
    ops = _s5_operators(ssm_lambda_re, ssm_lambda_im, ssm_log_dt, ssm_b_re, ssm_b_im,
                        ssm_c_re, ssm_c_im, ssm_d, seq // SSM_CHUNK)
    yssm = _s5_scan(u, ops, bsz, seq)

    weights = {
        "glu": ssm_w_glu.astype(BF16), "b_glu": ssm_b_glu.astype(F32).reshape(1, -1),
        "up_attn": w_up_attn.astype(BF16), "up_ssm": w_up_ssm.astype(BF16), "out": w_out.astype(BF16),
        "router_t": w_router.T.astype(BF16),
        "sh_gu": jnp.concatenate([w_sh_gate, w_sh_up], axis=1).astype(BF16),
        "sh_down": w_sh_down.astype(BF16),
    }
    h2, base, scores_t = _mix(xf, attn, yssm, ga, gs, mod3, ln2_g, weights, seq)

    idx_t, w_t, rank_t, counts = _route(scores_t, router_bias)
    n_blocks = -(-(n * TOP_K) // ROW_BLOCK) + N_EXPERTS
    tok_sorted, blk_e, src_start, n_used, poffs = _dispatch_tables(idx_t, counts, n_blocks)
    dest_t = _dest_slots(idx_t, rank_t, poffs)
    y_tiles = _expert_ffn(h2, tok_sorted, blk_e, src_start, n_used, w_exp_gate, w_exp_up, w_exp_down)
    out = _combine(y_tiles, dest_t, w_t, base, mod3, seq)
    return out.reshape(bsz, seq, d)


def kernel(x, c, rel_bias, w_ada, b_ada, ln1_g, w_in, q_norm_g, k_norm_g, ssm_lambda_re, ssm_lambda_im, ssm_log_dt, ssm_b_re, ssm_b_im, ssm_c_re, ssm_c_im, ssm_d, ssm_w_glu, ssm_b_glu, w_up_attn, w_up_ssm, w_out, ln2_g, w_router, router_bias, w_exp_gate, w_exp_up, w_exp_down, w_sh_gate, w_sh_up, w_sh_down):
    for l in range(w_ada.shape[0]):
        x = _hybrid_layer(x, c, rel_bias, w_ada[l], b_ada[l], ln1_g[l], w_in[l], q_norm_g[l], k_norm_g[l],
                          ssm_lambda_re[l], ssm_lambda_im[l], ssm_log_dt[l], ssm_b_re[l], ssm_b_im[l],
                          ssm_c_re[l], ssm_c_im[l], ssm_d[l], ssm_w_glu[l], ssm_b_glu[l],
                          w_up_attn[l], w_up_ssm[l], w_out[l], ln2_g[l], w_router[l], router_bias[l],
                          w_exp_gate[l], w_exp_up[l], w_exp_down[l], w_sh_gate[l], w_sh_up[l], w_sh_down[l])
    return x
```

```python
import functools
import math

import numpy as np
import jax
import jax.numpy as jnp
from jax import lax
from jax.experimental import pallas as pl
from jax.experimental.pallas import tpu as pltpu

F32 = jnp.float32
BF16 = jnp.bfloat16

ATTN_HEADS = 8
HEAD_DIM = 64
ATTN_WIDTH = ATTN_HEADS * HEAD_DIM
MOBA_BLOCK = 256
MOBA_TOPK = 3
NUM_BUCKETS = 32
MAX_DISTANCE = 128
SSM_WIDTH = 512
SSM_GROUP = 16
SSM_GROUPS = SSM_WIDTH // SSM_GROUP
SSM_STATE = 64
N_EXPERTS = 256
TOP_K = 8
N_EXPERT_GROUPS = 8
TOPK_GROUPS = 4
GROUP_SIZE = N_EXPERTS // N_EXPERT_GROUPS
EXPERT_DIM = 256
ROUTED_SCALE = 2.5
EPS = 1e-6
MASK_VALUE = -1e30

LANES = 128
HEADS_PER_STEP = LANES // HEAD_DIM
SSM_CHUNK = 16
SUBLANES = 8
ROW_BLOCK = 256
TOKEN_TILE = 256
ROUTE_TILE = 256
COMBINE_TILE = 128
ISSUE_UNROLL = 8
VMEM_LIMIT = 56 * 1024 * 1024


def _store_rows_as_tiles(ref, val):
    rows, d = val.shape
    nt = d // LANES
    for c in range(nt):
        ref[pl.ds(c, rows, stride=nt), :] = val[:, c * LANES:(c + 1) * LANES]


def _load_rows_from_tiles(ref, rows, d):
    nt = d // LANES
    return jnp.concatenate([ref[pl.ds(c, rows, stride=nt), :] for c in range(nt)], axis=1)


def _cparams(n_axes, vmem=VMEM_LIMIT):
    return pltpu.CompilerParams(dimension_semantics=("arbitrary",) * n_axes, vmem_limit_bytes=vmem)


def _sigmoid(x):
    return 1.0 / (1.0 + jnp.exp(-x))


def _silu(x):
    return x * _sigmoid(x)


def _bdot(a, b):
    return jnp.dot(a.astype(BF16), b.astype(BF16), preferred_element_type=F32)


def _bdot_nt(a, b):
    return lax.dot_general(a.astype(BF16), b.astype(BF16), (((1,), (1,)), ((), ())),
                           preferred_element_type=F32)


def _adaln_kernel(c_ref, w_ref, b_ref, o_ref):
    o_ref[...] = _bdot(_silu(c_ref[...]), w_ref[...]) + b_ref[...]


def _adaln(c, w_ada, b_ada):
    bsz, d = c.shape
    n_out = w_ada.shape[1]
    return pl.pallas_call(
        _adaln_kernel,
        out_shape=jax.ShapeDtypeStruct((bsz, n_out), F32),
        grid=(n_out // d,),
        in_specs=[pl.BlockSpec((bsz, d), lambda j: (0, 0)),
                  pl.BlockSpec((d, d), lambda j: (0, j)),
                  pl.BlockSpec((1, d), lambda j: (0, j))],
        out_specs=pl.BlockSpec((bsz, d), lambda j: (0, j)),
        compiler_params=_cparams(1),
        name="adaln",
    )(c, w_ada, b_ada.reshape(1, n_out))


def _modulated_norm(x, gain, shift, scale):
    y = x * lax.rsqrt(jnp.mean(x * x, axis=-1, keepdims=True) + EPS) * gain
    return y * (1.0 + scale) + shift


def _head_norm(t, seg, gain):
    ms = _bdot(t * t, seg)
    return t * lax.rsqrt(ms + EPS) * gain


def _inproj_kernel(x_ref, mod_ref, ln_ref, w_ref, seg_ref, qg_ref, kg_ref,
                   q_ref, k_ref, v_ref, u_ref, ga_ref, gs_ref, km_ref):
    aw, sw, d = ATTN_WIDTH, SSM_WIDTH, x_ref.shape[1]
    h = _modulated_norm(x_ref[...], ln_ref[...], mod_ref[0, 0:1, :], mod_ref[0, 1:2, :]).astype(BF16)
    seg = seg_ref[...]
    q = jnp.dot(h, w_ref[:, 0:aw], preferred_element_type=F32)
    q_ref[...] = _head_norm(q, seg, qg_ref[...])
    k = jnp.dot(h, w_ref[:, aw:2 * aw], preferred_element_type=F32)
    kn = _head_norm(k, seg, kg_ref[...])
    k_ref[...] = kn.astype(BF16)
    km_ref[0] = jnp.mean(kn, axis=0, keepdims=True)
    v_ref[...] = jnp.dot(h, w_ref[:, 2 * aw:3 * aw], preferred_element_type=F32).astype(BF16)
    o = 3 * aw
    u_ref[...] = jnp.dot(h, w_ref[:, o:o + sw], preferred_element_type=F32).astype(BF16)
    o += sw
    ga_ref[...] = jnp.dot(h, w_ref[:, o:o + d], preferred_element_type=F32).astype(BF16)
    o += d
    gs_ref[...] = jnp.dot(h, w_ref[:, o:o + d], preferred_element_type=F32).astype(BF16)


def _inproj(xf, mod3, ln1_g, w_in_b, q_gain, k_gain, seq):
    n, d = xf.shape
    tm = MOBA_BLOCK
    tiles_per_seq = seq // tm
    aw, sw = ATTN_WIDTH, SSM_WIDTH
    head_of_lane = np.arange(aw) // HEAD_DIM
    seg = jnp.asarray((head_of_lane[:, None] == head_of_lane[None, :]) / HEAD_DIM, BF16)
    row = lambda i: (i, 0)
    const = lambda i: (0, 0)
    return pl.pallas_call(
        _inproj_kernel,
        out_shape=(jax.ShapeDtypeStruct((n, aw), F32),
                   jax.ShapeDtypeStruct((n, aw), BF16),
                   jax.ShapeDtypeStruct((n, aw), BF16),
                   jax.ShapeDtypeStruct((n, sw), BF16),
                   jax.ShapeDtypeStruct((n, d), BF16),
                   jax.ShapeDtypeStruct((n, d), BF16),
                   jax.ShapeDtypeStruct((n // tm, 1, aw), F32)),
        grid=(n // tm,),
        in_specs=[pl.BlockSpec((tm, d), row),
                  pl.BlockSpec((1, 6, d), lambda i: (i // tiles_per_seq, 0, 0)),
                  pl.BlockSpec((1, d), const),
                  pl.BlockSpec(w_in_b.shape, const),
                  pl.BlockSpec((aw, aw), const),
                  pl.BlockSpec((1, aw), const),
                  pl.BlockSpec((1, aw), const)],
        out_specs=(pl.BlockSpec((tm, aw), row), pl.BlockSpec((tm, aw), row), pl.BlockSpec((tm, aw), row),
                   pl.BlockSpec((tm, sw), row), pl.BlockSpec((tm, d), row), pl.BlockSpec((tm, d), row),
                   pl.BlockSpec((1, 1, aw), lambda i: (i, 0, 0))),
        compiler_params=_cparams(1),
        name="inproj",
    )(xf, mod3, ln1_g.reshape(1, d), w_in_b, seg, q_gain, k_gain)


def _t5_bucket(rel):
    n = jnp.maximum(rel, 0)
    max_exact = NUM_BUCKETS // 2
    nf = jnp.maximum(n, 1).astype(F32)
    large = max_exact + (jnp.log(nf / max_exact) / math.log(MAX_DISTANCE / max_exact)
                         * (NUM_BUCKETS - max_exact)).astype(jnp.int32)
    large = jnp.minimum(large, NUM_BUCKETS - 1)
    return jnp.where(n < max_exact, n, large)


def _bias_tables(rel_bias):
    blk = MOBA_BLOCK
    assert blk + 1 >= MAX_DISTANCE
    rel = jnp.arange(blk)[None, :] - jnp.arange(blk)[:, None]
    table = rel_bias.astype(F32)
    table = table - table[NUM_BUCKETS - 1][None, :]

    def lookup(r):
        onehot = jax.nn.one_hot(_t5_bucket(r), NUM_BUCKETS, dtype=F32)
        return jnp.einsum('kqn,nh->hkq', onehot, table, precision=lax.Precision.HIGHEST)

    return lookup(rel), lookup(rel + blk)


def _select_blocks(gate_t, n_past):
    nb, tq = gate_t.shape
    blk = lax.broadcasted_iota(jnp.int32, (nb, tq), 0)
    beaten = jnp.zeros((nb, tq), jnp.int32)
    for m in range(nb):
        gm = gate_t[m:m + 1, :]
        wins = (gm > gate_t) | ((gm == gate_t) & (m < blk))
        beaten = beaten + jnp.where(wins & (m < n_past), 1, 0)
    return jnp.where((blk < n_past) & (beaten < MOBA_TOPK), 1.0, 0.0)


def _attn_kernel(q_ref, k_ref, vt_ref, km_ref, bown_ref, bprev_ref, o_ref, sel_ref):
    qi = pl.program_id(2)
    tq = q_ref.shape[0]
    blk = MOBA_BLOCK
    hd = HEAD_DIM
    heads = range(HEADS_PER_STEP)
    q = q_ref[...]
    lane = lax.broadcasted_iota(jnp.int32, (tq, LANES), 1)
    kpos = lax.broadcasted_iota(jnp.int32, (blk, tq), 0)
    qpos = lax.broadcasted_iota(jnp.int32, (blk, tq), 1)
    scale = hd ** -0.5
    k_own = k_ref[pl.ds(pl.multiple_of(qi * blk, blk), blk), :]
    n_far = jnp.maximum(qi - 1, 0)

    qbs = []
    for h in heads:
        in_head = (lane >= h * hd) & (lane < (h + 1) * hd)
        qm = jnp.where(in_head, q, 0.0)
        gate_t = lax.dot_general(km_ref[0], qm, (((1,), (1,)), ((), ())),
                                 precision=lax.Precision.HIGHEST, preferred_element_type=F32)
        sel_ref[h] = _select_blocks(gate_t, qi)
        qbs.append((qm * scale).astype(BF16))

    def first(h):
        s = _bdot_nt(k_own, qbs[h]) + bown_ref[h]
        s = jnp.where(kpos <= qpos, s, MASK_VALUE)
        m = jnp.max(s, axis=0, keepdims=True)
        p = jnp.exp(s - m)
        l = jnp.sum(p, axis=0, keepdims=True)
        acc = jnp.dot(vt_ref[0, qi, h * hd:(h + 1) * hd, :], p.astype(BF16), preferred_element_type=F32)
        return m, l, acc

    def update(carry, h, s, vts):
        m, l, acc = carry
        m_new = jnp.maximum(m, jnp.max(s, axis=0, keepdims=True))
        alpha = jnp.exp(m - m_new)
        p = jnp.exp(s - m_new)
        l = alpha * l + jnp.sum(p, axis=0, keepdims=True)
        p = p.astype(BF16)
        acc = alpha * acc
        for i, vt in enumerate(vts):
            acc = acc + jnp.dot(vt[h * hd:(h + 1) * hd, :], p[i * blk:(i + 1) * blk, :],
                                preferred_element_type=F32)
        return m_new, l, acc

    def far_pair(pi, carries):
        j = 2 * pi
        kb = k_ref[pl.ds(pl.multiple_of(j * blk, blk), 2 * blk), :]
        vts = (vt_ref[0, j], vt_ref[0, j + 1])
        second_is_far = j + 1 < n_far
        out = []
        for h in heads:
            s = _bdot_nt(kb, qbs[h])
            c0 = sel_ref[h, pl.ds(j, 1), :] > 0.5
            c1 = (sel_ref[h, pl.ds(j + 1, 1), :] > 0.5) & second_is_far
            chosen = jnp.concatenate([jnp.broadcast_to(c0, (blk, tq)), jnp.broadcast_to(c1, (blk, tq))], axis=0)
            out.append(update(carries[h], h, jnp.where(chosen, s, MASK_VALUE), vts))
        return tuple(out)

    def prev_block(carries):
        j = qi - 1
        kb = k_ref[pl.ds(pl.multiple_of(j * blk, blk), blk), :]
        out = []
        for h in heads:
            s = _bdot_nt(kb, qbs[h]) + bprev_ref[h]
            s = jnp.where(sel_ref[h, pl.ds(j, 1), :] > 0.5, s, MASK_VALUE)
            out.append(update(carries[h], h, s, (vt_ref[0, j],)))
        return tuple(out)

    carries = tuple(first(h) for h in heads)
    carries = lax.fori_loop(0, (n_far + 1) // 2, far_pair, carries)
    carries = lax.cond(qi >= 1, prev_block, lambda c: c, carries)
    out_t = jnp.concatenate([acc / l for _, l, acc in carries], axis=0)
    o_ref[...] = out_t.T.astype(o_ref.dtype)


def _moba_attention(q, k, v, kmean, rel_bias, bsz, seq):
    n, aw = q.shape
    blk = MOBA_BLOCK
    nb = seq // blk
    own, prev = _bias_tables(rel_bias)
    hps = HEADS_PER_STEP
    npair = aw // LANES
    vt = v.reshape(bsz, nb, blk, aw).transpose(0, 1, 3, 2)
    return pl.pallas_call(
        _attn_kernel,
        out_shape=jax.ShapeDtypeStruct((n, aw), BF16),
        grid=(bsz, npair, nb),
        in_specs=[pl.BlockSpec((blk, LANES), lambda b, hp, qi: (b * nb + qi, hp)),
                  pl.BlockSpec((seq, LANES), lambda b, hp, qi: (b, hp)),
                  pl.BlockSpec((1, nb, LANES, blk), lambda b, hp, qi: (b, 0, hp, 0)),
                  pl.BlockSpec((1, nb, LANES), lambda b, hp, qi: (b, 0, hp)),
                  pl.BlockSpec((hps, blk, blk), lambda b, hp, qi: (hp, 0, 0)),
                  pl.BlockSpec((hps, blk, blk), lambda b, hp, qi: (hp, 0, 0))],
        out_specs=pl.BlockSpec((blk, LANES), lambda b, hp, qi: (b * nb + qi, hp)),
        scratch_shapes=[pltpu.VMEM((hps, nb, blk), F32)],
        compiler_params=_cparams(3),
        name="moba_attention",
    )(q, k, vt, kmean.reshape(bsz, nb, aw), own, prev)


def _s5_operators(lambda_re, lambda_im, log_dt, b_re, b_im, c_re, c_im, d_skip, n_chunks):
    hi = lax.Precision.HIGHEST
    L, G, P, C = SSM_CHUNK, SSM_GROUPS, SSM_STATE, SSM_GROUP
    lam_re = jnp.minimum(lambda_re.astype(F32), -1e-4)
    lam_im = lambda_im.astype(F32)
    dt = jnp.exp(log_dt.astype(F32))[:, None]
    z_re, z_im = lam_re * dt, lam_im * dt

    def a_pow(nvec):
        nv = jnp.asarray(nvec, F32)[:, None, None]
        mag = jnp.exp(nv * z_re)
        return mag * jnp.cos(nv * z_im), mag * jnp.sin(nv * z_im)

    a_re, a_im = a_pow([1.0])
    a_re, a_im = a_re[0], a_im[0]
    den = lam_re * lam_re + lam_im * lam_im
    nr = a_re - 1.0
    coef_re = (nr * lam_re + a_im * lam_im) / den
    coef_im = (a_im * lam_re - nr * lam_im) / den
    br, bi = b_re.astype(F32), b_im.astype(F32)
    bbar_re = coef_re[..., None] * br - coef_im[..., None] * bi
    bbar_im = coef_re[..., None] * bi + coef_im[..., None] * br
    cr, ci = c_re.astype(F32), c_im.astype(F32)

    pw_re, pw_im = a_pow(np.arange(L + 1))
    cb_re = cr[None] * pw_re[:, :, None, :] - ci[None] * pw_im[:, :, None, :]
    cb_im = cr[None] * pw_im[:, :, None, :] + ci[None] * pw_re[:, :, None, :]
    kern = (jnp.einsum('jgop,gpi->jgoi', cb_re[:L], bbar_re, precision=hi)
            - jnp.einsum('jgop,gpi->jgoi', cb_im[:L], bbar_im, precision=hi))
    sig = np.arange(L)[:, None]
    tau = np.arange(L)[None, :]
    lag = np.clip(tau - sig, 0, L - 1)
    causal = jnp.asarray((tau >= sig), F32)
    t_op = kern[lag] * causal[:, :, None, None, None]
    t_op = t_op.transpose(2, 0, 4, 1, 3).reshape(G, L * C, L * C)
    d_g = d_skip.astype(F32).reshape(G, C)
    t_op = t_op + jnp.eye(L * C, dtype=F32)[None] * jnp.tile(d_g, (1, L))[:, None, :]

    rp_re, rp_im = pw_re[L - 1 - np.arange(L)], pw_im[L - 1 - np.arange(L)]
    p_re = rp_re[..., None] * bbar_re[None] - rp_im[..., None] * bbar_im[None]
    p_im = rp_re[..., None] * bbar_im[None] + rp_im[..., None] * bbar_re[None]
    p_op = jnp.concatenate([p_re, p_im], axis=2)
    p_op = p_op.transpose(1, 0, 3, 2).reshape(G, L * C, 2 * P)

    q_re = cb_re[1:].transpose(1, 3, 0, 2)
    q_im = -cb_im[1:].transpose(1, 3, 0, 2)
    q_op = jnp.concatenate([q_re, q_im], axis=1).reshape(G, 2 * P, L * C)

    n_steps = max(1, int(math.ceil(math.log2(n_chunks))))
    dk_re, dk_im = a_pow([float(L * 2 ** k) for k in range(n_steps)])
    a1 = jnp.concatenate([dk_re, dk_re], axis=-1).transpose(1, 0, 2)
    a2 = jnp.concatenate([-dk_im, dk_im], axis=-1).transpose(1, 0, 2)
    return t_op.astype(BF16), p_op.astype(BF16), q_op.astype(BF16), a1, a2


def _s5_kernel(u_ref, t_ref, p_ref, q_ref, a1_ref, a2_ref, y_ref, *, n_chunks):
    u = u_ref[0]
    s = jnp.dot(u, p_ref[0], preferred_element_type=F32)
    rows, width = s.shape
    chunk = lax.broadcasted_iota(jnp.int32, (rows, width), 0) % n_chunks
    a1 = a1_ref[0]
    a2 = a2_ref[0]
    x = jnp.where(chunk >= 1, pltpu.roll(s, 1, axis=0), 0.0)
    for kk in range(a1.shape[0]):
        dist = 2 ** kk
        if dist >= n_chunks:
            break
        xs = jnp.where(chunk >= dist, pltpu.roll(x, dist, axis=0), 0.0)
        x = x + a1[kk:kk + 1, :] * xs + a2[kk:kk + 1, :] * pltpu.roll(xs, SSM_STATE, axis=1)
    y = jnp.dot(u, t_ref[0], preferred_element_type=F32)
    y_ref[0] = y + jnp.dot(x.astype(BF16), q_ref[0], preferred_element_type=F32)


def _s5_scan(u, ops, bsz, seq):
    t_op, p_op, q_op, a1, a2 = ops
    L, G, C = SSM_CHUNK, SSM_GROUPS, SSM_GROUP
    nc = seq // L
    rows = bsz * nc
    w = L * C
    ug = u.reshape(rows, L, G, C).transpose(2, 0, 1, 3).reshape(G, rows, w)
    grp = lambda g: (g, 0, 0)
    y = pl.pallas_call(
        functools.partial(_s5_kernel, n_chunks=nc),
        out_shape=jax.ShapeDtypeStruct((G, rows, w), F32),
        grid=(G,),
        in_specs=[pl.BlockSpec((1, rows, w), grp),
                  pl.BlockSpec((1, w, w), grp),
                  pl.BlockSpec((1, w, 2 * SSM_STATE), grp),
                  pl.BlockSpec((1, 2 * SSM_STATE, w), grp),
                  pl.BlockSpec((1,) + a1.shape[1:], grp),
                  pl.BlockSpec((1,) + a2.shape[1:], grp)],
        out_specs=pl.BlockSpec((1, rows, w), grp),
        compiler_params=_cparams(1),
        name="s5_scan",
    )(ug, t_op, p_op, q_op, a1, a2)
    return y.reshape(G, rows, L, C).transpose(1, 2, 0, 3).reshape(bsz * seq, G * C)


def _gelu_tanh(x):
    return 0.5 * x * (1.0 + jnp.tanh(math.sqrt(2.0 / math.pi) * (x + 0.044715 * (x * x * x))))


def _mix_kernel(x_ref, attn_ref, yssm_ref, ga_ref, gs_ref, mod_ref, ln_ref,
                wglu_ref, bglu_ref, wua_ref, wus_ref, wout_ref, wrt_ref, wsgu_ref, wsd_ref,
                h2_ref, base_ref, score_ref):
    g = _gelu_tanh(yssm_ref[...])
    glu = g * _sigmoid(_bdot(g, wglu_ref[...]) + bglu_ref[...])
    y_attn = jnp.dot(attn_ref[...], wua_ref[...], preferred_element_type=F32)
    y_ssm = _bdot(glu, wus_ref[...])
    mixed = _sigmoid(ga_ref[...].astype(F32)) * y_attn + _sigmoid(gs_ref[...].astype(F32)) * y_ssm
    gate1 = mod_ref[0, 2:3, :]
    x1 = x_ref[...] + gate1 * _bdot(mixed, wout_ref[...])
    h2 = _modulated_norm(x1, ln_ref[...], mod_ref[0, 3:4, :], mod_ref[0, 4:5, :])
    _store_rows_as_tiles(h2_ref, h2)
    h2b = h2.astype(BF16)
    score_ref[...] = _sigmoid(_bdot_nt(wrt_ref[...], h2b))
    gu = jnp.dot(h2b, wsgu_ref[...], preferred_element_type=F32)
    sd = wsd_ref.shape[0]
    shared = _bdot(_silu(gu[:, :sd]) * gu[:, sd:], wsd_ref[...])
    base_ref[...] = x1 + mod_ref[0, 5:6, :] * shared


def _mix(xf, attn, yssm, ga, gs, mod3, ln2_g, w, seq):
    n, d = xf.shape
    tm = TOKEN_TILE
    tiles_per_seq = seq // tm
    row = lambda i: (i, 0)
    const = lambda i: (0, 0)
    nt = d // LANES
    weights = [w["glu"], w["b_glu"], w["up_attn"], w["up_ssm"], w["out"], w["router_t"], w["sh_gu"], w["sh_down"]]
    return pl.pallas_call(
        _mix_kernel,
        out_shape=(jax.ShapeDtypeStruct((n * nt, LANES), F32),
                   jax.ShapeDtypeStruct((n, d), F32),
                   jax.ShapeDtypeStruct((N_EXPERTS, n), F32)),
        grid=(n // tm,),
        in_specs=[pl.BlockSpec((tm, d), row),
                  pl.BlockSpec((tm, attn.shape[1]), row),
                  pl.BlockSpec((tm, yssm.shape[1]), row),
                  pl.BlockSpec((tm, d), row),
                  pl.BlockSpec((tm, d), row),
                  pl.BlockSpec((1, 6, d), lambda i: (i // tiles_per_seq, 0, 0)),
                  pl.BlockSpec((1, d), const)] + [pl.BlockSpec(a.shape, const) for a in weights],
        out_specs=(pl.BlockSpec((tm * nt, LANES), row), pl.BlockSpec((tm, d), row),
                   pl.BlockSpec((N_EXPERTS, tm), lambda i: (0, i))),
        compiler_params=_cparams(1),
        name="mix",
    )(xf, attn, yssm, ga, gs, mod3, ln2_g.reshape(1, d), *weights)


def _route_kernel(score_ref, bias_ref, tri_ref, idx_ref, w_ref, rank_ref, cnt_ref, carry_ref):
    @pl.when(pl.program_id(0) == 0)
    def _():
        carry_ref[...] = jnp.zeros_like(carry_ref)

    scores = score_ref[...]
    ne, tn = scores.shape
    biased = scores + bias_ref[...]
    gsz = GROUP_SIZE
    sub = lax.broadcasted_iota(jnp.int32, (gsz, tn), 0)
    group_score = []
    for g in range(N_EXPERT_GROUPS):
        sg = biased[g * gsz:(g + 1) * gsz, :]
        m1 = jnp.max(sg, axis=0, keepdims=True)
        first = jnp.min(jnp.where(sg == m1, sub, gsz), axis=0, keepdims=True)
        m2 = jnp.max(jnp.where(sub == first, -jnp.inf, sg), axis=0, keepdims=True)
        group_score.append(m1 + m2)
    group_rows = []
    for g in range(N_EXPERT_GROUPS):
        beaten = jnp.zeros((1, tn), jnp.int32)
        for o in range(N_EXPERT_GROUPS):
            if o == g:
                continue
            wins = (group_score[o] > group_score[g])
            if o < g:
                wins = wins | (group_score[o] == group_score[g])
            beaten = beaten + jnp.where(wins, 1, 0)
        group_rows.append(jnp.broadcast_to(beaten < TOPK_GROUPS, (gsz, tn)))
    allowed = jnp.concatenate(group_rows, axis=0)
    cur = jnp.where(allowed, biased, MASK_VALUE)
    eio = lax.broadcasted_iota(jnp.int32, (ne, tn), 0)
    idx_rows, w_rows, hits = [], [], []
    for _ in range(TOP_K):
        vmax = jnp.max(cur, axis=0, keepdims=True)
        eidx = jnp.min(jnp.where(cur == vmax, eio, ne), axis=0, keepdims=True)
        hit = eio == eidx
        w_rows.append(jnp.sum(jnp.where(hit, scores, 0.0), axis=0, keepdims=True))
        idx_rows.append(eidx)
        hits.append(hit)
        cur = jnp.where(hit, -jnp.inf, cur)
    wts = jnp.concatenate(w_rows, axis=0)
    idx_ref[...] = jnp.concatenate(idx_rows, axis=0)
    w_ref[...] = wts / jnp.sum(wts, axis=0, keepdims=True) * ROUTED_SCALE

    onehot = jnp.zeros((ne, tn), F32)
    for hit in hits:
        onehot = onehot + jnp.where(hit, 1.0, 0.0)
    earlier = carry_ref[...] + _bdot(onehot, tri_ref[...])
    rank_rows = [jnp.sum(jnp.where(hit, earlier, 0.0), axis=0, keepdims=True) for hit in hits]
    rank_ref[...] = jnp.concatenate(rank_rows, axis=0).astype(jnp.int32)
    carry_ref[...] = carry_ref[...] + jnp.sum(onehot, axis=1, keepdims=True)
    cnt_ref[...] = carry_ref[...]


def _route(scores_t, router_bias):
    ne, n = scores_t.shape
    tn = ROUTE_TILE
    tri = jnp.asarray(np.arange(tn)[:, None] < np.arange(tn)[None, :], BF16)
    tok = lambda i: (0, i)
    const = lambda i: (0, 0)
    return pl.pallas_call(
        _route_kernel,
        out_shape=(jax.ShapeDtypeStruct((TOP_K, n), jnp.int32), jax.ShapeDtypeStruct((TOP_K, n), F32),
                   jax.ShapeDtypeStruct((TOP_K, n), jnp.int32), jax.ShapeDtypeStruct((ne, 1), F32)),
        grid=(n // tn,),
        in_specs=[pl.BlockSpec((ne, tn), tok), pl.BlockSpec((ne, 1), const), pl.BlockSpec((tn, tn), const)],
        out_specs=(pl.BlockSpec((TOP_K, tn), tok), pl.BlockSpec((TOP_K, tn), tok),
                   pl.BlockSpec((TOP_K, tn), tok), pl.BlockSpec((ne, 1), const)),
        scratch_shapes=[pltpu.VMEM((ne, 1), F32)],
        compiler_params=_cparams(1),
        name="route",
    )(scores_t, router_bias.reshape(ne, 1).astype(F32), tri)


def _dest_kernel(idx_ref, rank_ref, poffs_ref, dest_ref):
    idx = idx_ref[...]
    ne = poffs_ref.shape[0]
    eio = lax.broadcasted_iota(jnp.int32, (ne, idx.shape[1]), 0)
    poffs = poffs_ref[...]
    rows = [jnp.sum(jnp.where(eio == idx[kk:kk + 1, :], poffs, 0.0), axis=0, keepdims=True)
            for kk in range(idx.shape[0])]
    dest_ref[...] = rank_ref[...] + jnp.concatenate(rows, axis=0).astype(jnp.int32)


def _dest_slots(idx_t, rank_t, poffs):
    k, n = idx_t.shape
    tn = ROUTE_TILE
    ne = poffs.shape[0]
    tok = lambda i: (0, i)
    return pl.pallas_call(
        _dest_kernel,
        out_shape=jax.ShapeDtypeStruct((k, n), jnp.int32),
        grid=(n // tn,),
        in_specs=[pl.BlockSpec((k, tn), tok), pl.BlockSpec((k, tn), tok), pl.BlockSpec((ne, 1), lambda i: (0, 0))],
        out_specs=pl.BlockSpec((k, tn), tok),
        compiler_params=_cparams(1),
        name="dest_slots",
    )(idx_t, rank_t, poffs.astype(F32).reshape(ne, 1))


def _dispatch_tables(idx_t, counts, n_blocks):
    k, n = idx_t.shape
    n_assign = k * n
    rb = ROW_BLOCK
    ne = counts.shape[0]
    counts = counts.reshape(ne).astype(jnp.int32)
    offs = jnp.cumsum(counts) - counts
    nblk = (counts + rb - 1) // rb
    bend = jnp.cumsum(nblk)
    bstart = bend - nblk
    blocks = jnp.arange(n_blocks, dtype=jnp.int32)
    blk_e = jnp.minimum(jnp.searchsorted(bend, blocks, side='right'), ne - 1).astype(jnp.int32)
    src_start = (offs[blk_e] + (blocks - bstart[blk_e]) * rb).astype(jnp.int32)
    tok = lax.broadcasted_iota(jnp.int32, (k, n), 1)
    kk = lax.broadcasted_iota(jnp.int32, (k, n), 0)
    assert ne * n_assign < 2 ** 31
    keys = jnp.sort((idx_t * n_assign + tok * k + kk).reshape(n_assign))
    tok_sorted = (keys % n_assign) // k
    return tok_sorted, blk_e, src_start, bend[ne - 1:ne], bstart * rb


def _ffn_kernel(tok_ref, blk_e_ref, start_ref, used_ref, h2_hbm, wg_ref, wu_ref, wd_ref, y_ref, xbuf, sems):
    b = pl.program_id(0)
    rb = ROW_BLOCK
    d = wg_ref.shape[1]
    nt = d // LANES
    n_assign = tok_ref.shape[0]
    n_used = used_ref[0]
    n_blocks = pl.num_programs(0)

    def start_rows(blk, s, inline):
        start = start_ref[blk]

        def one(r):
            tok = tok_ref[jnp.minimum(start + r, n_assign - 1)]
            pltpu.make_async_copy(h2_hbm.at[pl.ds(pl.multiple_of(tok * nt, nt), nt)],
                                  xbuf.at[s, pl.ds(pl.multiple_of(r * nt, nt), nt)], sems.at[s]).start()

        if inline:
            for r in range(rb):
                one(r)
        else:
            lax.fori_loop(0, rb, lambda r, c: (one(r), c)[1], 0, unroll=ISSUE_UNROLL)

    def wait_rows(s):
        pltpu.make_async_copy(h2_hbm.at[pl.ds(0, rb * nt)], xbuf.at[s], sems.at[s]).wait()

    @pl.when(b == 0)
    def _():
        start_rows(0, 0, False)

    for s in range(2):
        @pl.when((b < n_used) & (b % 2 == s))
        def _(s=s):
            wait_rows(s)
            xb = _load_rows_from_tiles(xbuf.at[s], rb, d).astype(BF16)
            start_rows(jnp.minimum(b + 1, n_blocks - 1), 1 - s, True)
            hg = jnp.dot(xb, wg_ref[0].astype(BF16), preferred_element_type=F32)
            hu = jnp.dot(xb, wu_ref[0].astype(BF16), preferred_element_type=F32)
            _store_rows_as_tiles(y_ref, _bdot(_silu(hg) * hu, wd_ref[0]))

    @pl.when(b == n_used - 1)
    def _():
        wait_rows((b + 1) % 2)

    @pl.when(b >= n_used)
    def _():
        y_ref[...] = jnp.zeros_like(y_ref)


def _expert_ffn(h2_tiles, tok_sorted, blk_e, src_start, n_used, w_gate, w_up, w_down):
    n_blocks = blk_e.shape[0]
    rb = ROW_BLOCK
    _, d, ed = w_gate.shape
    nt = d // LANES
    expert = lambda i, tok, be, st, nu: (be[i], 0, 0)
    grid_spec = pltpu.PrefetchScalarGridSpec(
        num_scalar_prefetch=4,
        grid=(n_blocks,),
        in_specs=[pl.BlockSpec(memory_space=pl.ANY),
                  pl.BlockSpec((1, d, ed), expert),
                  pl.BlockSpec((1, d, ed), expert),
                  pl.BlockSpec((1, ed, d), expert)],
        out_specs=pl.BlockSpec((rb * nt, LANES), lambda i, tok, be, st, nu: (i, 0)),
        scratch_shapes=[pltpu.VMEM((2, rb * nt, LANES), F32), pltpu.SemaphoreType.DMA((2,))],
    )
    return pl.pallas_call(
        _ffn_kernel,
        out_shape=jax.ShapeDtypeStruct((n_blocks * rb * nt, LANES), F32),
        grid_spec=grid_spec,
        compiler_params=_cparams(1),
        name="expert_ffn",
    )(tok_sorted, blk_e, src_start, n_used, h2_tiles, w_gate, w_up, w_down)


def _combine_kernel(dest_ref, y_hbm, base_ref, wt_ref, mod_ref, o_ref, buf, sems):
    i = pl.program_id(0)
    n_steps = pl.num_programs(0)
    tm, d = base_ref.shape
    k = wt_ref.shape[1]
    nt = d // LANES
    n_rows = k * tm
    slot = i % 2

    def issue(step, s):
        first = step * n_rows

        def body(r, _):
            src = dest_ref[first + r]
            pltpu.make_async_copy(y_hbm.at[pl.ds(pl.multiple_of(src * nt, nt), nt)],
                                  buf.at[s, pl.ds(pl.multiple_of(r * nt, nt), nt)], sems.at[s]).start()
            return 0

        lax.fori_loop(0, n_rows, body, 0, unroll=ISSUE_UNROLL)

    @pl.when(i == 0)
    def _():
        issue(0, 0)

    @pl.when(i + 1 < n_steps)
    def _():
        issue(i + 1, 1 - slot)

    pltpu.make_async_copy(y_hbm.at[pl.ds(0, n_rows * nt)], buf.at[slot], sems.at[slot]).wait()
    rows = buf.at[slot]
    wts = wt_ref[...]
    gate2 = mod_ref[0, 5:6, :]
    for c in range(nt):
        cols = slice(c * LANES, (c + 1) * LANES)
        routed = jnp.zeros((tm, LANES), F32)
        for kk in range(k):
            routed = routed + wts[:, kk:kk + 1] * rows[pl.ds(kk * tm * nt + c, tm, stride=nt), :]
        o_ref[:, cols] = base_ref[:, cols] + gate2[:, cols] * routed


def _combine(y_tiles, dest_t, w_t, base, mod3, seq):
    n, d = base.shape
    tm = COMBINE_TILE
    k = dest_t.shape[0]
    nt = d // LANES
    tiles_per_seq = seq // tm
    dest_tiles = dest_t.reshape(k, n // tm, tm).transpose(1, 0, 2).reshape(n * k)
    grid_spec = pltpu.PrefetchScalarGridSpec(
        num_scalar_prefetch=1,
        grid=(n // tm,),
        in_specs=[pl.BlockSpec(memory_space=pl.ANY),
                  pl.BlockSpec((tm, d), lambda i, dst: (i, 0)),
                  pl.BlockSpec((tm, k), lambda i, dst: (i, 0)),
                  pl.BlockSpec((1, 6, d), lambda i, dst: (i // tiles_per_seq, 0, 0))],
        out_specs=pl.BlockSpec((tm, d), lambda i, dst: (i, 0)),
        scratch_shapes=[pltpu.VMEM((2, k * tm * nt, LANES), F32), pltpu.SemaphoreType.DMA((2,))],
    )
    return pl.pallas_call(
        _combine_kernel,
        out_shape=jax.ShapeDtypeStruct((n, d), F32),
        grid_spec=grid_spec,
        compiler_params=_cparams(1),
        name="combine",
    )(dest_tiles, y_tiles, base, w_t.T, mod3)


def _hybrid_layer(x, cond, rel_bias, w_ada, b_ada, ln1_g, w_in, q_norm_g, k_norm_g,
                  ssm_lambda_re, ssm_lambda_im, ssm_log_dt, ssm_b_re, ssm_b_im, ssm_c_re, ssm_c_im,
                  ssm_d, ssm_w_glu, ssm_b_glu, w_up_attn, w_up_ssm, w_out, ln2_g,
                  w_router, router_bias, w_exp_gate, w_exp_up, w_exp_down,
                  w_sh_gate, w_sh_up, w_sh_down):
    bsz, seq, d = x.shape
    n = bsz * seq
    xf = x.reshape(n, d)
    mod3 = _adaln(cond, w_ada, b_ada).reshape(bsz, 6, d)

    q_gain = jnp.tile(q_norm_g.astype(F32), ATTN_HEADS).reshape(1, ATTN_WIDTH)
    k_gain = jnp.tile(k_norm_g.astype(F32), ATTN_HEADS).reshape(1, ATTN_WIDTH)
    q, k, v, u, ga, gs, kmean = _inproj(xf, mod3, ln1_g, w_in.astype(BF16), q_gain, k_gain, seq)

    attn = _moba_attention(q, k, v, kmean, rel_bias, bsz, seq)
    ops = _s5_operators(ssm_lambda_re, ssm_lambda_im, ssm_log_dt, ssm_b_re, ssm_b_im,
                        ssm_c_re, ssm_c_im, ssm_d, seq // SSM_CHUNK)
    yssm = _s5_scan(u, ops, bsz, seq)

    weights = {
        "glu": ssm_w_glu.astype(BF16), "b_glu": ssm_b_glu.astype(F32).reshape(1, -1),
        "up_attn": w_up_attn.astype(BF16), "up_ssm": w_up_ssm.astype(BF16), "out": w_out.astype(BF16),
        "router_t": w_router.T.astype(BF16),
        "sh_gu": jnp.concatenate([w_sh_gate, w_sh_up], axis=1).astype(BF16),
        "sh_down": w_sh_down.astype(BF16),
    }
    h2, base, scores_t = _mix(xf, attn, yssm, ga, gs, mod3, ln2_g, weights, seq)

    idx_t, w_t, rank_t, counts = _route(scores_t, router_bias)
    n_blocks = -(-(n * TOP_K) // ROW_BLOCK) + N_EXPERTS
    tok_sorted, blk_e, src_start, n_used, poffs = _dispatch_tables(idx_t, counts, n_blocks)
    dest_t = _dest_slots(idx_t, rank_t, poffs)
    y_tiles = _expert_ffn(h2, tok_sorted, blk_e, src_start, n_used, w_exp_gate, w_exp_up, w_exp_down)
    out = _combine(y_tiles, dest_t, w_t, base, mod3, seq)
    return out.reshape(bsz, seq, d)


def kernel(x, c, rel_bias, w_ada, b_ada, ln1_g, w_in, q_norm_g, k_norm_g, ssm_lambda_re, ssm_lambda_im, ssm_log_dt, ssm_b_re, ssm_b_im, ssm_c_re, ssm_c_im, ssm_d, ssm_w_glu, ssm_b_glu, w_up_attn, w_up_ssm, w_out, ln2_g, w_router, router_bias, w_exp_gate, w_exp_up, w_exp_down, w_sh_gate, w_sh_up, w_sh_down):
    for l in range(w_ada.shape[0]):
        x = _hybrid_layer(x, c, rel_bias, w_ada[l], b_ada[l], ln1_g[l], w_in[l], q_norm_g[l], k_norm_g[l],
                          ssm_lambda_re[l], ssm_lambda_im[l], ssm_log_dt[l], ssm_b_re[l], ssm_b_im[l],
                          ssm_c_re[l], ssm_c_im[l], ssm_d[l], ssm_w_glu[l], ssm_b_glu[l],
                          w_up_attn[l], w_up_ssm[l], w_out[l], ln2_g[l], w_router[l], router_bias[l],
                          w_exp_gate[l], w_exp_up[l], w_exp_down[l], w_sh_gate[l], w_sh_up[l], w_sh_down[l])
    return x
```

```python
import functools
import math

import numpy as np
import jax
import jax.numpy as jnp
from jax import lax
from jax.experimental import pallas as pl
from jax.experimental.pallas import tpu as pltpu

F32 = jnp.float32
BF16 = jnp.bfloat16

ATTN_HEADS = 8
HEAD_DIM = 64
ATTN_WIDTH = ATTN_HEADS * HEAD_DIM
MOBA_BLOCK = 256
MOBA_TOPK = 3
NUM_BUCKETS = 32
MAX_DISTANCE = 128
SSM_WIDTH = 512
SSM_GROUP = 16
SSM_GROUPS = SSM_WIDTH // SSM_GROUP
SSM_STATE = 64
N_EXPERTS = 256
TOP_K = 8
N_EXPERT_GROUPS = 8
TOPK_GROUPS = 4
GROUP_SIZE = N_EXPERTS // N_EXPERT_GROUPS
EXPERT_DIM = 256
ROUTED_SCALE = 2.5
EPS = 1e-6
MASK_VALUE = -1e30

LANES = 128
HEADS_PER_STEP = LANES // HEAD_DIM
SSM_CHUNK = 16
SUBLANES = 8
ROW_BLOCK = 256
TOKEN_TILE = 256
ROUTE_TILE = 256
COMBINE_TILE = 128
ISSUE_UNROLL = 8
FFN_ROW_BUFFERS = 3
VMEM_LIMIT = 56 * 1024 * 1024


def _store_rows_as_tiles(ref, val):
    rows, d = val.shape
    nt = d // LANES
    for c in range(nt):
        ref[pl.ds(c, rows, stride=nt), :] = val[:, c * LANES:(c + 1) * LANES]


def _load_rows_from_tiles(ref, rows, d):
    nt = d // LANES
    return jnp.concatenate([ref[pl.ds(c, rows, stride=nt), :] for c in range(nt)], axis=1)


def _cparams(n_axes, vmem=VMEM_LIMIT):
    return pltpu.CompilerParams(dimension_semantics=("arbitrary",) * n_axes, vmem_limit_bytes=vmem)


def _sigmoid(x):
    return 1.0 / (1.0 + jnp.exp(-x))


def _silu(x):
    return x * _sigmoid(x)


def _bdot(a, b):
    return jnp.dot(a.astype(BF16), b.astype(BF16), preferred_element_type=F32)


def _bdot_nt(a, b):
    return lax.dot_general(a.astype(BF16), b.astype(BF16), (((1,), (1,)), ((), ())),
                           preferred_element_type=F32)


def _adaln_kernel(c_ref, w_ref, b_ref, o_ref):
    o_ref[...] = _bdot(_silu(c_ref[...]), w_ref[...]) + b_ref[...]


def _adaln(c, w_ada, b_ada):
    bsz, d = c.shape
    n_out = w_ada.shape[1]
    return pl.pallas_call(
        _adaln_kernel,
        out_shape=jax.ShapeDtypeStruct((bsz, n_out), F32),
        grid=(n_out // d,),
        in_specs=[pl.BlockSpec((bsz, d), lambda j: (0, 0)),
                  pl.BlockSpec((d, d), lambda j: (0, j)),
                  pl.BlockSpec((1, d), lambda j: (0, j))],
        out_specs=pl.BlockSpec((bsz, d), lambda j: (0, j)),
        compiler_params=_cparams(1),
        name="adaln",
    )(c, w_ada, b_ada.reshape(1, n_out))


def _modulated_norm(x, gain, shift, scale):
    y = x * lax.rsqrt(jnp.mean(x * x, axis=-1, keepdims=True) + EPS) * gain
    return y * (1.0 + scale) + shift


def _head_norm(t, seg, gain):
    ms = _bdot(t * t, seg)
    return t * lax.rsqrt(ms + EPS) * gain


def _inproj_kernel(x_ref, mod_ref, ln_ref, w_ref, seg_ref, qg_ref, kg_ref,
                   q_ref, k_ref, v_ref, u_ref, ga_ref, gs_ref, km_ref):
    aw, sw, d = ATTN_WIDTH, SSM_WIDTH, x_ref.shape[1]
    h = _modulated_norm(x_ref[...], ln_ref[...], mod_ref[0, 0:1, :], mod_ref[0, 1:2, :]).astype(BF16)
    seg = seg_ref[...]
    q = jnp.dot(h, w_ref[:, 0:aw], preferred_element_type=F32)
    q_ref[...] = _head_norm(q, seg, qg_ref[...])
    k = jnp.dot(h, w_ref[:, aw:2 * aw], preferred_element_type=F32)
    kn = _head_norm(k, seg, kg_ref[...])
    k_ref[...] = kn.astype(BF16)
    km_ref[0] = jnp.mean(kn, axis=0, keepdims=True)
    v_ref[...] = jnp.dot(h, w_ref[:, 2 * aw:3 * aw], preferred_element_type=F32).astype(BF16)
    o = 3 * aw
    u_ref[...] = jnp.dot(h, w_ref[:, o:o + sw], preferred_element_type=F32).astype(BF16)
    o += sw
    ga_ref[...] = jnp.dot(h, w_ref[:, o:o + d], preferred_element_type=F32).astype(BF16)
    o += d
    gs_ref[...] = jnp.dot(h, w_ref[:, o:o + d], preferred_element_type=F32).astype(BF16)


def _inproj(xf, mod3, ln1_g, w_in_b, q_gain, k_gain, seq):
    n, d = xf.shape
    tm = MOBA_BLOCK
    tiles_per_seq = seq // tm
    aw, sw = ATTN_WIDTH, SSM_WIDTH
    head_of_lane = np.arange(aw) // HEAD_DIM
    seg = jnp.asarray((head_of_lane[:, None] == head_of_lane[None, :]) / HEAD_DIM, BF16)
    row = lambda i: (i, 0)
    const = lambda i: (0, 0)
    return pl.pallas_call(
        _inproj_kernel,
        out_shape=(jax.ShapeDtypeStruct((n, aw), F32),
                   jax.ShapeDtypeStruct((n, aw), BF16),
                   jax.ShapeDtypeStruct((n, aw), BF16),
                   jax.ShapeDtypeStruct((n, sw), BF16),
                   jax.ShapeDtypeStruct((n, d), BF16),
                   jax.ShapeDtypeStruct((n, d), BF16),
                   jax.ShapeDtypeStruct((n // tm, 1, aw), F32)),
        grid=(n // tm,),
        in_specs=[pl.BlockSpec((tm, d), row),
                  pl.BlockSpec((1, 6, d), lambda i: (i // tiles_per_seq, 0, 0)),
                  pl.BlockSpec((1, d), const),
                  pl.BlockSpec(w_in_b.shape, const),
                  pl.BlockSpec((aw, aw), const),
                  pl.BlockSpec((1, aw), const),
                  pl.BlockSpec((1, aw), const)],
        out_specs=(pl.BlockSpec((tm, aw), row), pl.BlockSpec((tm, aw), row), pl.BlockSpec((tm, aw), row),
                   pl.BlockSpec((tm, sw), row), pl.BlockSpec((tm, d), row), pl.BlockSpec((tm, d), row),
                   pl.BlockSpec((1, 1, aw), lambda i: (i, 0, 0))),
        compiler_params=_cparams(1),
        name="inproj",
    )(xf, mod3, ln1_g.reshape(1, d), w_in_b, seg, q_gain, k_gain)


def _t5_bucket(rel):
    n = jnp.maximum(rel, 0)
    max_exact = NUM_BUCKETS // 2
    nf = jnp.maximum(n, 1).astype(F32)
    large = max_exact + (jnp.log(nf / max_exact) / math.log(MAX_DISTANCE / max_exact)
                         * (NUM_BUCKETS - max_exact)).astype(jnp.int32)
    large = jnp.minimum(large, NUM_BUCKETS - 1)
    return jnp.where(n < max_exact, n, large)


def _bias_tables(rel_bias):
    blk = MOBA_BLOCK
    assert blk + 1 >= MAX_DISTANCE
    rel = jnp.arange(blk)[None, :] - jnp.arange(blk)[:, None]
    table = rel_bias.astype(F32)
    table = table - table[NUM_BUCKETS - 1][None, :]

    def lookup(r):
        onehot = jax.nn.one_hot(_t5_bucket(r), NUM_BUCKETS, dtype=F32)
        return jnp.einsum('kqn,nh->hkq', onehot, table, precision=lax.Precision.HIGHEST)

    return lookup(rel), lookup(rel + blk)


def _select_blocks(gate_t, n_past):
    nb, tq = gate_t.shape
    blk = lax.broadcasted_iota(jnp.int32, (nb, tq), 0)
    beaten = jnp.zeros((nb, tq), jnp.int32)
    for m in range(nb):
        gm = gate_t[m:m + 1, :]
        wins = (gm > gate_t) | ((gm == gate_t) & (m < blk))
        beaten = beaten + jnp.where(wins & (m < n_past), 1, 0)
    return jnp.where((blk < n_past) & (beaten < MOBA_TOPK), 1.0, 0.0)


def _attn_kernel(q_ref, k_ref, vt_ref, km_ref, bown_ref, bprev_ref, o_ref, sel_ref):
    qi = pl.program_id(2)
    tq = q_ref.shape[0]
    blk = MOBA_BLOCK
    hd = HEAD_DIM
    heads = range(HEADS_PER_STEP)
    q = q_ref[...]
    lane = lax.broadcasted_iota(jnp.int32, (tq, LANES), 1)
    kpos = lax.broadcasted_iota(jnp.int32, (blk, tq), 0)
    qpos = lax.broadcasted_iota(jnp.int32, (blk, tq), 1)
    scale = hd ** -0.5
    k_own = k_ref[pl.ds(pl.multiple_of(qi * blk, blk), blk), :]
    n_far = jnp.maximum(qi - 1, 0)

    qbs = []
    for h in heads:
        in_head = (lane >= h * hd) & (lane < (h + 1) * hd)
        qm = jnp.where(in_head, q, 0.0)
        gate_t = lax.dot_general(km_ref[0], qm, (((1,), (1,)), ((), ())),
                                 precision=lax.Precision.HIGHEST, preferred_element_type=F32)
        sel_ref[h] = _select_blocks(gate_t, qi)
        qbs.append((qm * scale).astype(BF16))

    def first(h):
        s = _bdot_nt(k_own, qbs[h]) + bown_ref[h]
        s = jnp.where(kpos <= qpos, s, MASK_VALUE)
        m = jnp.max(s, axis=0, keepdims=True)
        p = jnp.exp(s - m)
        l = jnp.sum(p, axis=0, keepdims=True)
        acc = jnp.dot(vt_ref[0, qi, h * hd:(h + 1) * hd, :], p.astype(BF16), preferred_element_type=F32)
        return m, l, acc

    def update(carry, h, s, vts):
        m, l, acc = carry
        m_new = jnp.maximum(m, jnp.max(s, axis=0, keepdims=True))
        alpha = jnp.exp(m - m_new)
        p = jnp.exp(s - m_new)
        l = alpha * l + jnp.sum(p, axis=0, keepdims=True)
        p = p.astype(BF16)
        acc = alpha * acc
        for i, vt in enumerate(vts):
            acc = acc + jnp.dot(vt[h * hd:(h + 1) * hd, :], p[i * blk:(i + 1) * blk, :],
                                preferred_element_type=F32)
        return m_new, l, acc

    def far_pair(pi, carries):
        j = 2 * pi
        kb = k_ref[pl.ds(pl.multiple_of(j * blk, blk), 2 * blk), :]
        vts = (vt_ref[0, j], vt_ref[0, j + 1])
        second_is_far = j + 1 < n_far
        out = []
        for h in heads:
            s = _bdot_nt(kb, qbs[h])
            c0 = sel_ref[h, pl.ds(j, 1), :] > 0.5
            c1 = (sel_ref[h, pl.ds(j + 1, 1), :] > 0.5) & second_is_far
            chosen = jnp.concatenate([jnp.broadcast_to(c0, (blk, tq)), jnp.broadcast_to(c1, (blk, tq))], axis=0)
            out.append(update(carries[h], h, jnp.where(chosen, s, MASK_VALUE), vts))
        return tuple(out)

    def prev_block(carries):
        j = qi - 1
        kb = k_ref[pl.ds(pl.multiple_of(j * blk, blk), blk), :]
        out = []
        for h in heads:
            s = _bdot_nt(kb, qbs[h]) + bprev_ref[h]
            s = jnp.where(sel_ref[h, pl.ds(j, 1), :] > 0.5, s, MASK_VALUE)
            out.append(update(carries[h], h, s, (vt_ref[0, j],)))
        return tuple(out)

    carries = tuple(first(h) for h in heads)
    carries = lax.fori_loop(0, (n_far + 1) // 2, far_pair, carries)
    carries = lax.cond(qi >= 1, prev_block, lambda c: c, carries)
    out_t = jnp.concatenate([acc / l for _, l, acc in carries], axis=0)
    o_ref[...] = out_t.T.astype(o_ref.dtype)


def _moba_attention(q, k, v, kmean, rel_bias, bsz, seq):
    n, aw = q.shape
    blk = MOBA_BLOCK
    nb = seq // blk
    own, prev = _bias_tables(rel_bias)
    hps = HEADS_PER_STEP
    npair = aw // LANES
    vt = v.reshape(bsz, nb, blk, aw).transpose(0, 1, 3, 2)
    return pl.pallas_call(
        _attn_kernel,
        out_shape=jax.ShapeDtypeStruct((n, aw), BF16),
        grid=(bsz, npair, nb),
        in_specs=[pl.BlockSpec((blk, LANES), lambda b, hp, qi: (b * nb + qi, hp)),
                  pl.BlockSpec((seq, LANES), lambda b, hp, qi: (b, hp)),
                  pl.BlockSpec((1, nb, LANES, blk), lambda b, hp, qi: (b, 0, hp, 0)),
                  pl.BlockSpec((1, nb, LANES), lambda b, hp, qi: (b, 0, hp)),
                  pl.BlockSpec((hps, blk, blk), lambda b, hp, qi: (hp, 0, 0)),
                  pl.BlockSpec((hps, blk, blk), lambda b, hp, qi: (hp, 0, 0))],
        out_specs=pl.BlockSpec((blk, LANES), lambda b, hp, qi: (b * nb + qi, hp)),
        scratch_shapes=[pltpu.VMEM((hps, nb, blk), F32)],
        compiler_params=_cparams(3),
        name="moba_attention",
    )(q, k, vt, kmean.reshape(bsz, nb, aw), own, prev)


def _s5_operators(lambda_re, lambda_im, log_dt, b_re, b_im, c_re, c_im, d_skip, n_chunks):
    hi = lax.Precision.HIGHEST
    L, G, P, C = SSM_CHUNK, SSM_GROUPS, SSM_STATE, SSM_GROUP
    lam_re = jnp.minimum(lambda_re.astype(F32), -1e-4)
    lam_im = lambda_im.astype(F32)
    dt = jnp.exp(log_dt.astype(F32))[:, None]
    z_re, z_im = lam_re * dt, lam_im * dt

    def a_pow(nvec):
        nv = jnp.asarray(nvec, F32)[:, None, None]
        mag = jnp.exp(nv * z_re)
        return mag * jnp.cos(nv * z_im), mag * jnp.sin(nv * z_im)

    a_re, a_im = a_pow([1.0])
    a_re, a_im = a_re[0], a_im[0]
    den = lam_re * lam_re + lam_im * lam_im
    nr = a_re - 1.0
    coef_re = (nr * lam_re + a_im * lam_im) / den
    coef_im = (a_im * lam_re - nr * lam_im) / den
    br, bi = b_re.astype(F32), b_im.astype(F32)
    bbar_re = coef_re[..., None] * br - coef_im[..., None] * bi
    bbar_im = coef_re[..., None] * bi + coef_im[..., None] * br
    cr, ci = c_re.astype(F32), c_im.astype(F32)

    pw_re, pw_im = a_pow(np.arange(L + 1))
    cb_re = cr[None] * pw_re[:, :, None, :] - ci[None] * pw_im[:, :, None, :]
    cb_im = cr[None] * pw_im[:, :, None, :] + ci[None] * pw_re[:, :, None, :]
    kern = (jnp.einsum('jgop,gpi->jgoi', cb_re[:L], bbar_re, precision=hi)
            - jnp.einsum('jgop,gpi->jgoi', cb_im[:L], bbar_im, precision=hi))
    sig = np.arange(L)[:, None]
    tau = np.arange(L)[None, :]
    lag = np.clip(tau - sig, 0, L - 1)
    causal = jnp.asarray((tau >= sig), F32)
    t_op = kern[lag] * causal[:, :, None, None, None]
    t_op = t_op.transpose(2, 0, 4, 1, 3).reshape(G, L * C, L * C)
    d_g = d_skip.astype(F32).reshape(G, C)
    t_op = t_op + jnp.eye(L * C, dtype=F32)[None] * jnp.tile(d_g, (1, L))[:, None, :]

    rp_re, rp_im = pw_re[L - 1 - np.arange(L)], pw_im[L - 1 - np.arange(L)]
    p_re = rp_re[..., None] * bbar_re[None] - rp_im[..., None] * bbar_im[None]
    p_im = rp_re[..., None] * bbar_im[None] + rp_im[..., None] * bbar_re[None]
    p_op = jnp.concatenate([p_re, p_im], axis=2)
    p_op = p_op.transpose(1, 0, 3, 2).reshape(G, L * C, 2 * P)

    q_re = cb_re[1:].transpose(1, 3, 0, 2)
    q_im = -cb_im[1:].transpose(1, 3, 0, 2)
    q_op = jnp.concatenate([q_re, q_im], axis=1).reshape(G, 2 * P, L * C)

    n_steps = max(1, int(math.ceil(math.log2(n_chunks))))
    dk_re, dk_im = a_pow([float(L * 2 ** k) for k in range(n_steps)])
    a1 = jnp.concatenate([dk_re, dk_re], axis=-1).transpose(1, 0, 2)
    a2 = jnp.concatenate([-dk_im, dk_im], axis=-1).transpose(1, 0, 2)
    return t_op.astype(BF16), p_op.astype(BF16), q_op.astype(BF16), a1, a2


def _s5_kernel(u_ref, t_ref, p_ref, q_ref, a1_ref, a2_ref, y_ref, *, n_chunks):
    u = u_ref[0]
    s = jnp.dot(u, p_ref[0], preferred_element_type=F32)
    rows, width = s.shape
    chunk = lax.broadcasted_iota(jnp.int32, (rows, width), 0) % n_chunks
    a1 = a1_ref[0]
    a2 = a2_ref[0]
    x = jnp.where(chunk >= 1, pltpu.roll(s, 1, axis=0), 0.0)
    for kk in range(a1.shape[0]):
        dist = 2 ** kk
        if dist >= n_chunks:
            break
        xs = jnp.where(chunk >= dist, pltpu.roll(x, dist, axis=0), 0.0)
        x = x + a1[kk:kk + 1, :] * xs + a2[kk:kk + 1, :] * pltpu.roll(xs, SSM_STATE, axis=1)
    y = jnp.dot(u, t_ref[0], preferred_element_type=F32)
    y_ref[0] = y + jnp.dot(x.astype(BF16), q_ref[0], preferred_element_type=F32)


def _s5_scan(u, ops, bsz, seq):
    t_op, p_op, q_op, a1, a2 = ops
    L, G, C = SSM_CHUNK, SSM_GROUPS, SSM_GROUP
    nc = seq // L
    rows = bsz * nc
    w = L * C
    ug = u.reshape(rows, L, G, C).transpose(2, 0, 1, 3).reshape(G, rows, w)
    grp = lambda g: (g, 0, 0)
    y = pl.pallas_call(
        functools.partial(_s5_kernel, n_chunks=nc),
        out_shape=jax.ShapeDtypeStruct((G, rows, w), F32),
        grid=(G,),
        in_specs=[pl.BlockSpec((1, rows, w), grp),
                  pl.BlockSpec((1, w, w), grp),
                  pl.BlockSpec((1, w, 2 * SSM_STATE), grp),
                  pl.BlockSpec((1, 2 * SSM_STATE, w), grp),
                  pl.BlockSpec((1,) + a1.shape[1:], grp),
                  pl.BlockSpec((1,) + a2.shape[1:], grp)],
        out_specs=pl.BlockSpec((1, rows, w), grp),
        compiler_params=_cparams(1),
        name="s5_scan",
    )(ug, t_op, p_op, q_op, a1, a2)
    return y.reshape(G, rows, L, C).transpose(1, 2, 0, 3).reshape(bsz * seq, G * C)


def _gelu_tanh(x):
    return 0.5 * x * (1.0 + jnp.tanh(math.sqrt(2.0 / math.pi) * (x + 0.044715 * (x * x * x))))


def _mix_kernel(x_ref, attn_ref, yssm_ref, ga_ref, gs_ref, mod_ref, ln_ref,
                wglu_ref, bglu_ref, wua_ref, wus_ref, wout_ref, wrt_ref, wsgu_ref, wsd_ref,
                h2_ref, base_ref, score_ref):
    g = _gelu_tanh(yssm_ref[...])
    glu = g * _sigmoid(_bdot(g, wglu_ref[...]) + bglu_ref[...])
    y_attn = jnp.dot(attn_ref[...], wua_ref[...], preferred_element_type=F32)
    y_ssm = _bdot(glu, wus_ref[...])
    mixed = _sigmoid(ga_ref[...].astype(F32)) * y_attn + _sigmoid(gs_ref[...].astype(F32)) * y_ssm
    gate1 = mod_ref[0, 2:3, :]
    x1 = x_ref[...] + gate1 * _bdot(mixed, wout_ref[...])
    h2 = _modulated_norm(x1, ln_ref[...], mod_ref[0, 3:4, :], mod_ref[0, 4:5, :])
    _store_rows_as_tiles(h2_ref, h2)
    h2b = h2.astype(BF16)
    score_ref[...] = _sigmoid(_bdot_nt(wrt_ref[...], h2b))
    gu = jnp.dot(h2b, wsgu_ref[...], preferred_element_type=F32)
    sd = wsd_ref.shape[0]
    shared = _bdot(_silu(gu[:, :sd]) * gu[:, sd:], wsd_ref[...])
    base_ref[...] = x1 + mod_ref[0, 5:6, :] * shared


def _mix(xf, attn, yssm, ga, gs, mod3, ln2_g, w, seq):
    n, d = xf.shape
    tm = TOKEN_TILE
    tiles_per_seq = seq // tm
    row = lambda i: (i, 0)
    const = lambda i: (0, 0)
    nt = d // LANES
    weights = [w["glu"], w["b_glu"], w["up_attn"], w["up_ssm"], w["out"], w["router_t"], w["sh_gu"], w["sh_down"]]
    return pl.pallas_call(
        _mix_kernel,
        out_shape=(jax.ShapeDtypeStruct((n * nt, LANES), F32),
                   jax.ShapeDtypeStruct((n, d), F32),
                   jax.ShapeDtypeStruct((N_EXPERTS, n), F32)),
        grid=(n // tm,),
        in_specs=[pl.BlockSpec((tm, d), row),
                  pl.BlockSpec((tm, attn.shape[1]), row),
                  pl.BlockSpec((tm, yssm.shape[1]), row),
                  pl.BlockSpec((tm, d), row),
                  pl.BlockSpec((tm, d), row),
                  pl.BlockSpec((1, 6, d), lambda i: (i // tiles_per_seq, 0, 0)),
                  pl.BlockSpec((1, d), const)] + [pl.BlockSpec(a.shape, const) for a in weights],
        out_specs=(pl.BlockSpec((tm * nt, LANES), row), pl.BlockSpec((tm, d), row),
                   pl.BlockSpec((N_EXPERTS, tm), lambda i: (0, i))),
        compiler_params=_cparams(1),
        name="mix",
    )(xf, attn, yssm, ga, gs, mod3, ln2_g.reshape(1, d), *weights)


def _route_kernel(score_ref, bias_ref, tri_ref, idx_ref, w_ref, rank_ref, cnt_ref, carry_ref):
    @pl.when(pl.program_id(0) == 0)
    def _():
        carry_ref[...] = jnp.zeros_like(carry_ref)

    scores = score_ref[...]
    ne, tn = scores.shape
    biased = scores + bias_ref[...]
    gsz = GROUP_SIZE
    sub = lax.broadcasted_iota(jnp.int32, (gsz, tn), 0)
    group_score = []
    for g in range(N_EXPERT_GROUPS):
        sg = biased[g * gsz:(g + 1) * gsz, :]
        m1 = jnp.max(sg, axis=0, keepdims=True)
        first = jnp.min(jnp.where(sg == m1, sub, gsz), axis=0, keepdims=True)
        m2 = jnp.max(jnp.where(sub == first, -jnp.inf, sg), axis=0, keepdims=True)
        group_score.append(m1 + m2)
    group_rows = []
    for g in range(N_EXPERT_GROUPS):
        beaten = jnp.zeros((1, tn), jnp.int32)
        for o in range(N_EXPERT_GROUPS):
            if o == g:
                continue
            wins = (group_score[o] > group_score[g])
            if o < g:
                wins = wins | (group_score[o] == group_score[g])
            beaten = beaten + jnp.where(wins, 1, 0)
        group_rows.append(jnp.broadcast_to(beaten < TOPK_GROUPS, (gsz, tn)))
    allowed = jnp.concatenate(group_rows, axis=0)
    cur = jnp.where(allowed, biased, MASK_VALUE)
    eio = lax.broadcasted_iota(jnp.int32, (ne, tn), 0)
    idx_rows, w_rows, hits = [], [], []
    for _ in range(TOP_K):
        vmax = jnp.max(cur, axis=0, keepdims=True)
        eidx = jnp.min(jnp.where(cur == vmax, eio, ne), axis=0, keepdims=True)
        hit = eio == eidx
        w_rows.append(jnp.sum(jnp.where(hit, scores, 0.0), axis=0, keepdims=True))
        idx_rows.append(eidx)
        hits.append(hit)
        cur = jnp.where(hit, -jnp.inf, cur)
    wts = jnp.concatenate(w_rows, axis=0)
    idx_ref[...] = jnp.concatenate(idx_rows, axis=0)
    w_ref[...] = wts / jnp.sum(wts, axis=0, keepdims=True) * ROUTED_SCALE

    onehot = jnp.zeros((ne, tn), F32)
    for hit in hits:
        onehot = onehot + jnp.where(hit, 1.0, 0.0)
    earlier = carry_ref[...] + _bdot(onehot, tri_ref[...])
    rank_rows = [jnp.sum(jnp.where(hit, earlier, 0.0), axis=0, keepdims=True) for hit in hits]
    rank_ref[...] = jnp.concatenate(rank_rows, axis=0).astype(jnp.int32)
    carry_ref[...] = carry_ref[...] + jnp.sum(onehot, axis=1, keepdims=True)
    cnt_ref[...] = carry_ref[...]


def _route(scores_t, router_bias):
    ne, n = scores_t.shape
    tn = ROUTE_TILE
    tri = jnp.asarray(np.arange(tn)[:, None] < np.arange(tn)[None, :], BF16)
    tok = lambda i: (0, i)
    const = lambda i: (0, 0)
    return pl.pallas_call(
        _route_kernel,
        out_shape=(jax.ShapeDtypeStruct((TOP_K, n), jnp.int32), jax.ShapeDtypeStruct((TOP_K, n), F32),
                   jax.ShapeDtypeStruct((TOP_K, n), jnp.int32), jax.ShapeDtypeStruct((ne, 1), F32)),
        grid=(n // tn,),
        in_specs=[pl.BlockSpec((ne, tn), tok), pl.BlockSpec((ne, 1), const), pl.BlockSpec((tn, tn), const)],
        out_specs=(pl.BlockSpec((TOP_K, tn), tok), pl.BlockSpec((TOP_K, tn), tok),
                   pl.BlockSpec((TOP_K, tn), tok), pl.BlockSpec((ne, 1), const)),
        scratch_shapes=[pltpu.VMEM((ne, 1), F32)],
        compiler_params=_cparams(1),
        name="route",
    )(scores_t, router_bias.reshape(ne, 1).astype(F32), tri)


def _dest_kernel(idx_ref, rank_ref, poffs_ref, dest_ref):
    idx = idx_ref[...]
    ne = poffs_ref.shape[0]
    eio = lax.broadcasted_iota(jnp.int32, (ne, idx.shape[1]), 0)
    poffs = poffs_ref[...]
    rows = [jnp.sum(jnp.where(eio == idx[kk:kk + 1, :], poffs, 0.0), axis=0, keepdims=True)
            for kk in range(idx.shape[0])]
    dest_ref[...] = rank_ref[...] + jnp.concatenate(rows, axis=0).astype(jnp.int32)


def _dest_slots(idx_t, rank_t, poffs):
    k, n = idx_t.shape
    tn = ROUTE_TILE
    ne = poffs.shape[0]
    tok = lambda i: (0, i)
    return pl.pallas_call(
        _dest_kernel,
        out_shape=jax.ShapeDtypeStruct((k, n), jnp.int32),
        grid=(n // tn,),
        in_specs=[pl.BlockSpec((k, tn), tok), pl.BlockSpec((k, tn), tok), pl.BlockSpec((ne, 1), lambda i: (0, 0))],
        out_specs=pl.BlockSpec((k, tn), tok),
        compiler_params=_cparams(1),
        name="dest_slots",
    )(idx_t, rank_t, poffs.astype(F32).reshape(ne, 1))


def _dispatch_tables(idx_t, counts, n_blocks):
    k, n = idx_t.shape
    n_assign = k * n
    rb = ROW_BLOCK
    ne = counts.shape[0]
    counts = counts.reshape(ne).astype(jnp.int32)
    offs = jnp.cumsum(counts) - counts
    nblk = (counts + rb - 1) // rb
    bend = jnp.cumsum(nblk)
    bstart = bend - nblk
    blocks = jnp.arange(n_blocks, dtype=jnp.int32)
    blk_e = jnp.minimum(jnp.searchsorted(bend, blocks, side='right'), ne - 1).astype(jnp.int32)
    src_start = (offs[blk_e] + (blocks - bstart[blk_e]) * rb).astype(jnp.int32)
    tok = lax.broadcasted_iota(jnp.int32, (k, n), 1)
    kk = lax.broadcasted_iota(jnp.int32, (k, n), 0)
    assert ne * n_assign < 2 ** 31
    keys = jnp.sort((idx_t * n_assign + tok * k + kk).reshape(n_assign))
    tok_sorted = (keys % n_assign) // k
    return tok_sorted, blk_e, src_start, bend[ne - 1:ne], bstart * rb


def _ffn_kernel(tok_ref, blk_e_ref, start_ref, used_ref, h2_hbm, wg_ref, wu_ref, wd_ref, y_ref, xbuf, sems):
    b = pl.program_id(0)
    rb = ROW_BLOCK
    d = wg_ref.shape[1]
    nt = d // LANES
    n_assign = tok_ref.shape[0]
    n_used = used_ref[0]
    n_blocks = pl.num_programs(0)

    def start_rows(blk, s, inline):
        start = start_ref[blk]

        def one(r):
            tok = tok_ref[jnp.minimum(start + r, n_assign - 1)]
            pltpu.make_async_copy(h2_hbm.at[pl.ds(pl.multiple_of(tok * nt, nt), nt)],
                                  xbuf.at[s, pl.ds(pl.multiple_of(r * nt, nt), nt)], sems.at[s]).start()

        if inline:
            for r in range(rb):
                one(r)
        else:
            lax.fori_loop(0, rb, lambda r, c: (one(r), c)[1], 0, unroll=ISSUE_UNROLL)

    def wait_rows(s):
        pltpu.make_async_copy(h2_hbm.at[pl.ds(0, rb * nt)], xbuf.at[s], sems.at[s]).wait()

    nbuf = xbuf.shape[0]
    ahead = nbuf - 1

    @pl.when(b == 0)
    def _():
        for a in range(ahead):
            start_rows(jnp.minimum(a, n_blocks - 1), a, False)

    for s in range(nbuf):
        @pl.when((b < n_used) & (b % nbuf == s))
        def _(s=s):
            wait_rows(s)
            xb = _load_rows_from_tiles(xbuf.at[s], rb, d).astype(BF16)
            start_rows(jnp.minimum(b + ahead, n_blocks - 1), (s + ahead) % nbuf, True)
            hg = jnp.dot(xb, wg_ref[0].astype(BF16), preferred_element_type=F32)
            hu = jnp.dot(xb, wu_ref[0].astype(BF16), preferred_element_type=F32)
            _store_rows_as_tiles(y_ref, _bdot(_silu(hg) * hu, wd_ref[0]))

    @pl.when(b == n_used - 1)
    def _():
        for a in range(1, nbuf):
            wait_rows((b + a) % nbuf)

    @pl.when(b >= n_used)
    def _():
        y_ref[...] = jnp.zeros_like(y_ref)


def _expert_ffn(h2_tiles, tok_sorted, blk_e, src_start, n_used, w_gate, w_up, w_down):
    n_blocks = blk_e.shape[0]
    rb = ROW_BLOCK
    _, d, ed = w_gate.shape
    nt = d // LANES
    expert = lambda i, tok, be, st, nu: (be[i], 0, 0)
    grid_spec = pltpu.PrefetchScalarGridSpec(
        num_scalar_prefetch=4,
        grid=(n_blocks,),
        in_specs=[pl.BlockSpec(memory_space=pl.ANY),
                  pl.BlockSpec((1, d, ed), expert),
                  pl.BlockSpec((1, d, ed), expert),
                  pl.BlockSpec((1, ed, d), expert)],
        out_specs=pl.BlockSpec((rb * nt, LANES), lambda i, tok, be, st, nu: (i, 0)),
        scratch_shapes=[pltpu.VMEM((FFN_ROW_BUFFERS, rb * nt, LANES), F32),
                        pltpu.SemaphoreType.DMA((FFN_ROW_BUFFERS,))],
    )
    return pl.pallas_call(
        _ffn_kernel,
        out_shape=jax.ShapeDtypeStruct((n_blocks * rb * nt, LANES), F32),
        grid_spec=grid_spec,
        compiler_params=_cparams(1),
        name="expert_ffn",
    )(tok_sorted, blk_e, src_start, n_used, h2_tiles, w_gate, w_up, w_down)


def _combine_kernel(dest_ref, y_hbm, base_ref, wt_ref, mod_ref, o_ref, buf, sems):
    i = pl.program_id(0)
    n_steps = pl.num_programs(0)
    tm, d = base_ref.shape
    k = wt_ref.shape[1]
    nt = d // LANES
    n_rows = k * tm
    slot = i % 2

    def issue(step, s):
        first = step * n_rows

        def body(r, _):
            src = dest_ref[first + r]
            pltpu.make_async_copy(y_hbm.at[pl.ds(pl.multiple_of(src * nt, nt), nt)],
                                  buf.at[s, pl.ds(pl.multiple_of(r * nt, nt), nt)], sems.at[s]).start()
            return 0

        lax.fori_loop(0, n_rows, body, 0, unroll=ISSUE_UNROLL)

    @pl.when(i == 0)
    def _():
        issue(0, 0)

    @pl.when(i + 1 < n_steps)
    def _():
        issue(i + 1, 1 - slot)

    pltpu.make_async_copy(y_hbm.at[pl.ds(0, n_rows * nt)], buf.at[slot], sems.at[slot]).wait()
    rows = buf.at[slot]
    wts = wt_ref[...]
    gate2 = mod_ref[0, 5:6, :]
    for c in range(nt):
        cols = slice(c * LANES, (c + 1) * LANES)
        routed = jnp.zeros((tm, LANES), F32)
        for kk in range(k):
            routed = routed + wts[:, kk:kk + 1] * rows[pl.ds(kk * tm * nt + c, tm, stride=nt), :]
        o_ref[:, cols] = base_ref[:, cols] + gate2[:, cols] * routed


def _combine(y_tiles, dest_t, w_t, base, mod3, seq):
    n, d = base.shape
    tm = COMBINE_TILE
    k = dest_t.shape[0]
    nt = d // LANES
    tiles_per_seq = seq // tm
    dest_tiles = dest_t.reshape(k, n // tm, tm).transpose(1, 0, 2).reshape(n * k)
    grid_spec = pltpu.PrefetchScalarGridSpec(
        num_scalar_prefetch=1,
        grid=(n // tm,),
        in_specs=[pl.BlockSpec(memory_space=pl.ANY),
                  pl.BlockSpec((tm, d), lambda i, dst: (i, 0)),
                  pl.BlockSpec((tm, k), lambda i, dst: (i, 0)),
                  pl.BlockSpec((1, 6, d), lambda i, dst: (i // tiles_per_seq, 0, 0))],
        out_specs=pl.BlockSpec((tm, d), lambda i, dst: (i, 0)),
        scratch_shapes=[pltpu.VMEM((2, k * tm * nt, LANES), F32), pltpu.SemaphoreType.DMA((2,))],
    )
    return pl.pallas_call(
        _combine_kernel,
        out_shape=jax.ShapeDtypeStruct((n, d), F32),
        grid_spec=grid_spec,
        compiler_params=_cparams(1),
        name="combine",
    )(dest_tiles, y_tiles, base, w_t.T, mod3)


def _hybrid_layer(x, cond, rel_bias, w_ada, b_ada, ln1_g, w_in, q_norm_g, k_norm_g,
                  ssm_lambda_re, ssm_lambda_im, ssm_log_dt, ssm_b_re, ssm_b_im, ssm_c_re, ssm_c_im,
                  ssm_d, ssm_w_glu, ssm_b_glu, w_up_attn, w_up_ssm, w_out, ln2_g,
                  w_router, router_bias, w_exp_gate, w_exp_up, w_exp_down,
                  w_sh_gate, w_sh_up, w_sh_down):
    bsz, seq, d = x.shape
    n = bsz * seq
    xf = x.reshape(n, d)
    mod3 = _adaln(cond, w_ada, b_ada).reshape(bsz, 6, d)

    q_gain = jnp.tile(q_norm_g.astype(F32), ATTN_HEADS).reshape(1, ATTN_WIDTH)
    k_gain = jnp.tile(k_norm_g.astype(F32), ATTN_HEADS).reshape(1, ATTN_WIDTH)
    q, k, v, u, ga, gs, kmean = _inproj(xf, mod3, ln1_g, w_in.astype(BF16), q_gain, k_gain, seq)

    attn = _moba_attention(q, k, v, kmean, rel_bias, bsz, seq)
    ops = _s5_operators(ssm_lambda_re, ssm_lambda_im, ssm_log_dt, ssm_b_re, ssm_b_im,
                        ssm_c_re, ssm_c_im, ssm_d, seq // SSM_CHUNK)
    yssm = _s5_scan(u, ops, bsz, seq)

    weights = {
        "glu": ssm_w_glu.astype(BF16), "b_glu": ssm_b_glu.astype(F32).reshape(1, -1),
        "up_attn": w_up_attn.astype(BF16), "up_ssm": w_up_ssm.astype(BF16), "out": w_out.astype(BF16),
        "router_t": w_router.T.astype(BF16),
        "sh_gu": jnp.concatenate([w_sh_gate, w_sh_up], axis=1).astype(BF16),
        "sh_down": w_sh_down.astype(BF16),
    }
    h2, base, scores_t = _mix(xf, attn, yssm, ga, gs, mod3, ln2_g, weights, seq)

    idx_t, w_t, rank_t, counts = _route(scores_t, router_bias)
    n_blocks = -(-(n * TOP_K) // ROW_BLOCK) + N_EXPERTS
    tok_sorted, blk_e, src_start, n_used, poffs = _dispatch_tables(idx_t, counts, n_blocks)
    dest_t = _dest_slots(idx_t, rank_t, poffs)
    y_tiles = _expert_ffn(h2, tok_sorted, blk_e, src_start, n_used, w_exp_gate, w_exp_up, w_exp_down)
    out = _combine(y_tiles, dest_t, w_t, base, mod3, seq)
    return out.reshape(bsz, seq, d)


def kernel(x, c, rel_bias, w_ada, b_ada, ln1_g, w_in, q_norm_g, k_norm_g, ssm_lambda_re, ssm_lambda_im, ssm_log_dt, ssm_b_re, ssm_b_im, ssm_c_re, ssm_c_im, ssm_d, ssm_w_glu, ssm_b_glu, w_up_attn, w_up_ssm, w_out, ln2_g, w_router, router_bias, w_exp_gate, w_exp_up, w_exp_down, w_sh_gate, w_sh_up, w_sh_down):
    for l in range(w_ada.shape[0]):
        x = _hybrid_layer(x, c, rel_bias, w_ada[l], b_ada[l], ln1_g[l], w_in[l], q_norm_g[l], k_norm_g[l],
                          ssm_lambda_re[l], ssm_lambda_im[l], ssm_log_dt[l], ssm_b_re[l], ssm_b_im[l],
                          ssm_c_re[l], ssm_c_im[l], ssm_d[l], ssm_w_glu[l], ssm_b_glu[l],
                          w_up_attn[l], w_up_ssm[l], w_out[l], ln2_g[l], w_router[l], router_bias[l],
                          w_exp_gate[l], w_exp_up[l], w_exp_down[l], w_sh_gate[l], w_sh_up[l], w_sh_down[l])
    return x
```

```python
import functools
import math

import numpy as np
import jax
import jax.numpy as jnp
from jax import lax
from jax.experimental import pallas as pl
from jax.experimental.pallas import tpu as pltpu

F32 = jnp.float32
BF16 = jnp.bfloat16

ATTN_HEADS = 8
HEAD_DIM = 64
ATTN_WIDTH = ATTN_HEADS * HEAD_DIM
MOBA_BLOCK = 256
MOBA_TOPK = 3
NUM_BUCKETS = 32
MAX_DISTANCE = 128
SSM_WIDTH = 512
SSM_GROUP = 16
SSM_GROUPS = SSM_WIDTH // SSM_GROUP
SSM_STATE = 64
N_EXPERTS = 256
TOP_K = 8
N_EXPERT_GROUPS = 8
TOPK_GROUPS = 4
GROUP_SIZE = N_EXPERTS // N_EXPERT_GROUPS
EXPERT_DIM = 256
ROUTED_SCALE = 2.5
EPS = 1e-6
MASK_VALUE = -1e30

LANES = 128
HEADS_PER_STEP = LANES // HEAD_DIM
SSM_CHUNK = 16
SUBLANES = 8
ROW_BLOCK = 256
TOKEN_TILE = 256
ROUTE_TILE = 256
COMBINE_TILE = 128
ISSUE_UNROLL = 8
FFN_ROW_BUFFERS = 3
VMEM_LIMIT = 56 * 1024 * 1024


def _store_rows_as_tiles(ref, val):
    rows, d = val.shape
    nt = d // LANES
    for c in range(nt):
        ref[pl.ds(c, rows, stride=nt), :] = val[:, c * LANES:(c + 1) * LANES]


def _load_rows_from_tiles(ref, rows, d):
    nt = d // LANES
    return jnp.concatenate([ref[pl.ds(c, rows, stride=nt), :] for c in range(nt)], axis=1)


def _cparams(n_axes, vmem=VMEM_LIMIT):
    return pltpu.CompilerParams(dimension_semantics=("arbitrary",) * n_axes, vmem_limit_bytes=vmem)


def _sigmoid(x):
    return 1.0 / (1.0 + jnp.exp(-x))


def _silu(x):
    return x * _sigmoid(x)


def _bdot(a, b):
    return jnp.dot(a.astype(BF16), b.astype(BF16), preferred_element_type=F32)


def _bdot_nt(a, b):
    return lax.dot_general(a.astype(BF16), b.astype(BF16), (((1,), (1,)), ((), ())),
                           preferred_element_type=F32)


def _adaln_kernel(c_ref, w_ref, b_ref, o_ref):
    o_ref[...] = _bdot(_silu(c_ref[...]), w_ref[...]) + b_ref[...]


def _adaln(c, w_ada, b_ada):
    bsz, d = c.shape
    n_out = w_ada.shape[1]
    return pl.pallas_call(
        _adaln_kernel,
        out_shape=jax.ShapeDtypeStruct((bsz, n_out), F32),
        grid=(n_out // d,),
        in_specs=[pl.BlockSpec((bsz, d), lambda j: (0, 0)),
                  pl.BlockSpec((d, d), lambda j: (0, j)),
                  pl.BlockSpec((1, d), lambda j: (0, j))],
        out_specs=pl.BlockSpec((bsz, d), lambda j: (0, j)),
        compiler_params=_cparams(1),
        name="adaln",
    )(c, w_ada, b_ada.reshape(1, n_out))


def _modulated_norm(x, gain, shift, scale):
    y = x * lax.rsqrt(jnp.mean(x * x, axis=-1, keepdims=True) + EPS) * gain
    return y * (1.0 + scale) + shift


def _head_norm(t, seg, gain):
    ms = _bdot(t * t, seg)
    return t * lax.rsqrt(ms + EPS) * gain


def _inproj_kernel(x_ref, mod_ref, ln_ref, w_ref, seg_ref, qg_ref, kg_ref,
                   q_ref, k_ref, v_ref, u_ref, ga_ref, gs_ref, km_ref):
    aw, sw, d = ATTN_WIDTH, SSM_WIDTH, x_ref.shape[1]
    h = _modulated_norm(x_ref[...], ln_ref[...], mod_ref[0, 0:1, :], mod_ref[0, 1:2, :]).astype(BF16)
    seg = seg_ref[...]
    q = jnp.dot(h, w_ref[:, 0:aw], preferred_element_type=F32)
    q_ref[...] = _head_norm(q, seg, qg_ref[...])
    k = jnp.dot(h, w_ref[:, aw:2 * aw], preferred_element_type=F32)
    kn = _head_norm(k, seg, kg_ref[...])
    k_ref[...] = kn.astype(BF16)
    km_ref[0] = jnp.mean(kn, axis=0, keepdims=True)
    v_ref[...] = jnp.dot(h, w_ref[:, 2 * aw:3 * aw], preferred_element_type=F32).astype(BF16)
    o = 3 * aw
    u_ref[...] = jnp.dot(h, w_ref[:, o:o + sw], preferred_element_type=F32).astype(BF16)
    o += sw
    ga_ref[...] = jnp.dot(h, w_ref[:, o:o + d], preferred_element_type=F32).astype(BF16)
    o += d
    gs_ref[...] = jnp.dot(h, w_ref[:, o:o + d], preferred_element_type=F32).astype(BF16)


def _inproj(xf, mod3, ln1_g, w_in_b, q_gain, k_gain, seq):
    n, d = xf.shape
    tm = MOBA_BLOCK
    tiles_per_seq = seq // tm
    aw, sw = ATTN_WIDTH, SSM_WIDTH
    head_of_lane = np.arange(aw) // HEAD_DIM
    seg = jnp.asarray((head_of_lane[:, None] == head_of_lane[None, :]) / HEAD_DIM, BF16)
    row = lambda i: (i, 0)
    const = lambda i: (0, 0)
    return pl.pallas_call(
        _inproj_kernel,
        out_shape=(jax.ShapeDtypeStruct((n, aw), F32),
                   jax.ShapeDtypeStruct((n, aw), BF16),
                   jax.ShapeDtypeStruct((n, aw), BF16),
                   jax.ShapeDtypeStruct((n, sw), BF16),
                   jax.ShapeDtypeStruct((n, d), BF16),
                   jax.ShapeDtypeStruct((n, d), BF16),
                   jax.ShapeDtypeStruct((n // tm, 1, aw), F32)),
        grid=(n // tm,),
        in_specs=[pl.BlockSpec((tm, d), row),
                  pl.BlockSpec((1, 6, d), lambda i: (i // tiles_per_seq, 0, 0)),
                  pl.BlockSpec((1, d), const),
                  pl.BlockSpec(w_in_b.shape, const),
                  pl.BlockSpec((aw, aw), const),
                  pl.BlockSpec((1, aw), const),
                  pl.BlockSpec((1, aw), const)],
        out_specs=(pl.BlockSpec((tm, aw), row), pl.BlockSpec((tm, aw), row), pl.BlockSpec((tm, aw), row),
                   pl.BlockSpec((tm, sw), row), pl.BlockSpec((tm, d), row), pl.BlockSpec((tm, d), row),
                   pl.BlockSpec((1, 1, aw), lambda i: (i, 0, 0))),
        compiler_params=_cparams(1),
        name="inproj",
    )(xf, mod3, ln1_g.reshape(1, d), w_in_b, seg, q_gain, k_gain)


def _t5_bucket(rel):
    n = jnp.maximum(rel, 0)
    max_exact = NUM_BUCKETS // 2
    nf = jnp.maximum(n, 1).astype(F32)
    large = max_exact + (jnp.log(nf / max_exact) / math.log(MAX_DISTANCE / max_exact)
                         * (NUM_BUCKETS - max_exact)).astype(jnp.int32)
    large = jnp.minimum(large, NUM_BUCKETS - 1)
    return jnp.where(n < max_exact, n, large)


def _bias_tables(rel_bias):
    blk = MOBA_BLOCK
    assert blk + 1 >= MAX_DISTANCE
    rel = jnp.arange(blk)[None, :] - jnp.arange(blk)[:, None]
    table = rel_bias.astype(F32)
    table = table - table[NUM_BUCKETS - 1][None, :]

    def lookup(r):
        onehot = jax.nn.one_hot(_t5_bucket(r), NUM_BUCKETS, dtype=F32)
        return jnp.einsum('kqn,nh->hkq', onehot, table, precision=lax.Precision.HIGHEST)

    return lookup(rel), lookup(rel + blk)


def _select_blocks(gate_t, n_past):
    nb, tq = gate_t.shape
    blk = lax.broadcasted_iota(jnp.int32, (nb, tq), 0)
    beaten = jnp.zeros((nb, tq), jnp.int32)
    for m in range(nb):
        gm = gate_t[m:m + 1, :]
        wins = (gm > gate_t) | ((gm == gate_t) & (m < blk))
        beaten = beaten + jnp.where(wins & (m < n_past), 1, 0)
    return jnp.where((blk < n_past) & (beaten < MOBA_TOPK), 1.0, 0.0)


def _attn_kernel(q_ref, k_ref, vt_ref, km_ref, bias_ref, o_ref, sel_ref, s_ref):
    qi = pl.program_id(2)
    tq = q_ref.shape[0]
    blk = MOBA_BLOCK
    hd = HEAD_DIM
    heads = range(HEADS_PER_STEP)
    q = q_ref[...]
    lane = lax.broadcasted_iota(jnp.int32, (tq, LANES), 1)
    kpos = lax.broadcasted_iota(jnp.int32, (blk, tq), 0)
    qpos = lax.broadcasted_iota(jnp.int32, (blk, tq), 1)
    scale = hd ** -0.5
    n_far = jnp.maximum(qi - 1, 0)
    n_pairs = (n_far + 1) // 2
    jp = jnp.maximum(qi - 1, 0)
    k_own = k_ref[pl.ds(pl.multiple_of(qi * blk, blk), blk), :]
    k_prev = k_ref[pl.ds(pl.multiple_of(jp * blk, blk), blk), :]

    qbs = []
    for h in heads:
        in_head = (lane >= h * hd) & (lane < (h + 1) * hd)
        qm = jnp.where(in_head, q, 0.0)
        gate_t = lax.dot_general(km_ref[0], qm, (((1,), (1,)), ((), ())),
                                 precision=lax.Precision.HIGHEST, preferred_element_type=F32)
        sel_ref[h] = _select_blocks(gate_t, qi)
        qbs.append((qm * scale).astype(BF16))

    def pair_scores(j):
        kb = k_ref[pl.ds(pl.multiple_of(j * blk, blk), 2 * blk), :]
        return [_bdot_nt(kb, qbs[h]) for h in heads]

    def attend(h, p, blocks):
        acc = None
        for i, j in enumerate(blocks):
            part = jnp.dot(vt_ref[0, j, h * hd:(h + 1) * hd, :], p[i * blk:(i + 1) * blk, :],
                           preferred_element_type=F32)
            acc = part if acc is None else acc + part
        return acc

    for h, s in enumerate(pair_scores(0)):
        s_ref[h] = s

    carries = []
    for h in heads:
        s_prev = _bdot_nt(k_prev, qbs[h]) + bias_ref[h, 0:blk, :]
        s_prev = jnp.where(sel_ref[h, pl.ds(jp, 1), :] > 0.5, s_prev, MASK_VALUE)
        s_own = _bdot_nt(k_own, qbs[h]) + bias_ref[h, blk:2 * blk, :]
        s_own = jnp.where(kpos <= qpos, s_own, MASK_VALUE)
        s = jnp.concatenate([s_prev, s_own], axis=0)
        m = jnp.max(s, axis=0, keepdims=True)
        p = jnp.exp(s - m)
        l = jnp.sum(p, axis=0, keepdims=True)
        carries.append((m, l, attend(h, p.astype(BF16), (jp, qi))))

    def far_pair(pi, carries):
        j = 2 * pi
        s_cur = [s_ref[h] for h in heads]
        for h, s in enumerate(pair_scores(2 * jnp.minimum(pi + 1, n_pairs - 1))):
            s_ref[h] = s
        second_is_far = j + 1 < n_far
        out = []
        for h in heads:
            m, l, acc = carries[h]
            c0 = sel_ref[h, pl.ds(j, 1), :] > 0.5
            c1 = (sel_ref[h, pl.ds(j + 1, 1), :] > 0.5) & second_is_far
            chosen = jnp.concatenate([jnp.broadcast_to(c0, (blk, tq)), jnp.broadcast_to(c1, (blk, tq))], axis=0)
            s = jnp.where(chosen, s_cur[h], MASK_VALUE)
            m_new = jnp.maximum(m, jnp.max(s, axis=0, keepdims=True))
            alpha = jnp.exp(m - m_new)
            p = jnp.exp(s - m_new)
            l = alpha * l + jnp.sum(p, axis=0, keepdims=True)
            acc = alpha * acc + attend(h, p.astype(BF16), (j, j + 1))
            out.append((m_new, l, acc))
        return tuple(out)

    carries = lax.fori_loop(0, n_pairs, far_pair, tuple(carries))
    out_t = jnp.concatenate([acc / l for _, l, acc in carries], axis=0)
    o_ref[...] = out_t.T.astype(o_ref.dtype)


def _moba_attention(q, k, v, kmean, rel_bias, bsz, seq):
    n, aw = q.shape
    blk = MOBA_BLOCK
    nb = seq // blk
    assert nb >= 2
    own, prev = _bias_tables(rel_bias)
    bias = jnp.concatenate([prev, own], axis=1)
    hps = HEADS_PER_STEP
    npair = aw // LANES
    vt = v.reshape(bsz, nb, blk, aw).transpose(0, 1, 3, 2)
    return pl.pallas_call(
        _attn_kernel,
        out_shape=jax.ShapeDtypeStruct((n, aw), BF16),
        grid=(bsz, npair, nb),
        in_specs=[pl.BlockSpec((blk, LANES), lambda b, hp, qi: (b * nb + qi, hp)),
                  pl.BlockSpec((seq, LANES), lambda b, hp, qi: (b, hp)),
                  pl.BlockSpec((1, nb, LANES, blk), lambda b, hp, qi: (b, 0, hp, 0)),
                  pl.BlockSpec((1, nb, LANES), lambda b, hp, qi: (b, 0, hp)),
                  pl.BlockSpec((hps, 2 * blk, blk), lambda b, hp, qi: (hp, 0, 0))],
        out_specs=pl.BlockSpec((blk, LANES), lambda b, hp, qi: (b * nb + qi, hp)),
        scratch_shapes=[pltpu.VMEM((hps, nb, blk), F32), pltpu.VMEM((hps, 2 * blk, blk), F32)],
        compiler_params=_cparams(3),
        name="moba_attention",
    )(q, k, vt, kmean.reshape(bsz, nb, aw), bias)


def _s5_operators(lambda_re, lambda_im, log_dt, b_re, b_im, c_re, c_im, d_skip, n_chunks):
    hi = lax.Precision.HIGHEST
    L, G, P, C = SSM_CHUNK, SSM_GROUPS, SSM_STATE, SSM_GROUP
    lam_re = jnp.minimum(lambda_re.astype(F32), -1e-4)
    lam_im = lambda_im.astype(F32)
    dt = jnp.exp(log_dt.astype(F32))[:, None]
    z_re, z_im = lam_re * dt, lam_im * dt

    def a_pow(nvec):
        nv = jnp.asarray(nvec, F32)[:, None, None]
        mag = jnp.exp(nv * z_re)
        return mag * jnp.cos(nv * z_im), mag * jnp.sin(nv * z_im)

    a_re, a_im = a_pow([1.0])
    a_re, a_im = a_re[0], a_im[0]
    den = lam_re * lam_re + lam_im * lam_im
    nr = a_re - 1.0
    coef_re = (nr * lam_re + a_im * lam_im) / den
    coef_im = (a_im * lam_re - nr * lam_im) / den
    br, bi = b_re.astype(F32), b_im.astype(F32)
    bbar_re = coef_re[..., None] * br - coef_im[..., None] * bi
    bbar_im = coef_re[..., None] * bi + coef_im[..., None] * br
    cr, ci = c_re.astype(F32), c_im.astype(F32)

    pw_re, pw_im = a_pow(np.arange(L + 1))
    cb_re = cr[None] * pw_re[:, :, None, :] - ci[None] * pw_im[:, :, None, :]
    cb_im = cr[None] * pw_im[:, :, None, :] + ci[None] * pw_re[:, :, None, :]
    kern = (jnp.einsum('jgop,gpi->jgoi', cb_re[:L], bbar_re, precision=hi)
            - jnp.einsum('jgop,gpi->jgoi', cb_im[:L], bbar_im, precision=hi))
    sig = np.arange(L)[:, None]
    tau = np.arange(L)[None, :]
    lag = np.clip(tau - sig, 0, L - 1)
    causal = jnp.asarray((tau >= sig), F32)
    t_op = kern[lag] * causal[:, :, None, None, None]
    t_op = t_op.transpose(2, 0, 4, 1, 3).reshape(G, L * C, L * C)
    d_g = d_skip.astype(F32).reshape(G, C)
    t_op = t_op + jnp.eye(L * C, dtype=F32)[None] * jnp.tile(d_g, (1, L))[:, None, :]

    rp_re, rp_im = pw_re[L - 1 - np.arange(L)], pw_im[L - 1 - np.arange(L)]
    p_re = rp_re[..., None] * bbar_re[None] - rp_im[..., None] * bbar_im[None]
    p_im = rp_re[..., None] * bbar_im[None] + rp_im[..., None] * bbar_re[None]
    p_op = jnp.concatenate([p_re, p_im], axis=2)
    p_op = p_op.transpose(1, 0, 3, 2).reshape(G, L * C, 2 * P)

    q_re = cb_re[1:].transpose(1, 3, 0, 2)
    q_im = -cb_im[1:].transpose(1, 3, 0, 2)
    q_op = jnp.concatenate([q_re, q_im], axis=1).reshape(G, 2 * P, L * C)

    n_steps = max(1, int(math.ceil(math.log2(n_chunks))))
    dk_re, dk_im = a_pow([float(L * 2 ** k) for k in range(n_steps)])
    a1 = jnp.concatenate([dk_re, dk_re], axis=-1).transpose(1, 0, 2)
    a2 = jnp.concatenate([-dk_im, dk_im], axis=-1).transpose(1, 0, 2)
    return t_op.astype(BF16), p_op.astype(BF16), q_op.astype(BF16), a1, a2


def _s5_kernel(u_ref, t_ref, p_ref, q_ref, a1_ref, a2_ref, y_ref, *, n_chunks):
    u = u_ref[0]
    s = jnp.dot(u, p_ref[0], preferred_element_type=F32)
    rows, width = s.shape
    chunk = lax.broadcasted_iota(jnp.int32, (rows, width), 0) % n_chunks
    a1 = a1_ref[0]
    a2 = a2_ref[0]
    x = jnp.where(chunk >= 1, pltpu.roll(s, 1, axis=0), 0.0)
    for kk in range(a1.shape[0]):
        dist = 2 ** kk
        if dist >= n_chunks:
            break
        xs = jnp.where(chunk >= dist, pltpu.roll(x, dist, axis=0), 0.0)
        x = x + a1[kk:kk + 1, :] * xs + a2[kk:kk + 1, :] * pltpu.roll(xs, SSM_STATE, axis=1)
    y = jnp.dot(u, t_ref[0], preferred_element_type=F32)
    y_ref[0] = y + jnp.dot(x.astype(BF16), q_ref[0], preferred_element_type=F32)


def _s5_scan(u, ops, bsz, seq):
    t_op, p_op, q_op, a1, a2 = ops
    L, G, C = SSM_CHUNK, SSM_GROUPS, SSM_GROUP
    nc = seq // L
    rows = bsz * nc
    w = L * C
    ug = u.reshape(rows, L, G, C).transpose(2, 0, 1, 3).reshape(G, rows, w)
    grp = lambda g: (g, 0, 0)
    y = pl.pallas_call(
        functools.partial(_s5_kernel, n_chunks=nc),
        out_shape=jax.ShapeDtypeStruct((G, rows, w), F32),
        grid=(G,),
        in_specs=[pl.BlockSpec((1, rows, w), grp),
                  pl.BlockSpec((1, w, w), grp),
                  pl.BlockSpec((1, w, 2 * SSM_STATE), grp),
                  pl.BlockSpec((1, 2 * SSM_STATE, w), grp),
                  pl.BlockSpec((1,) + a1.shape[1:], grp),
                  pl.BlockSpec((1,) + a2.shape[1:], grp)],
        out_specs=pl.BlockSpec((1, rows, w), grp),
        compiler_params=_cparams(1),
        name="s5_scan",
    )(ug, t_op, p_op, q_op, a1, a2)
    return y.reshape(G, rows, L, C).transpose(1, 2, 0, 3).reshape(bsz * seq, G * C)


def _gelu_tanh(x):
    return 0.5 * x * (1.0 + jnp.tanh(math.sqrt(2.0 / math.pi) * (x + 0.044715 * (x * x * x))))


def _mix_kernel(x_ref, attn_ref, yssm_ref, ga_ref, gs_ref, mod_ref, ln_ref,
                wglu_ref, bglu_ref, wua_ref, wus_ref, wout_ref, wrt_ref, wsgu_ref, wsd_ref,
                h2_ref, base_ref, score_ref):
    g = _gelu_tanh(yssm_ref[...])
    glu = g * _sigmoid(_bdot(g, wglu_ref[...]) + bglu_ref[...])
    y_attn = jnp.dot(attn_ref[...], wua_ref[...], preferred_element_type=F32)
    y_ssm = _bdot(glu, wus_ref[...])
    mixed = _sigmoid(ga_ref[...].astype(F32)) * y_attn + _sigmoid(gs_ref[...].astype(F32)) * y_ssm
    gate1 = mod_ref[0, 2:3, :]
    x1 = x_ref[...] + gate1 * _bdot(mixed, wout_ref[...])
    h2 = _modulated_norm(x1, ln_ref[...], mod_ref[0, 3:4, :], mod_ref[0, 4:5, :])
    _store_rows_as_tiles(h2_ref, h2)
    h2b = h2.astype(BF16)
    score_ref[...] = _sigmoid(_bdot_nt(wrt_ref[...], h2b))
    gu = jnp.dot(h2b, wsgu_ref[...], preferred_element_type=F32)
    sd = wsd_ref.shape[0]
    shared = _bdot(_silu(gu[:, :sd]) * gu[:, sd:], wsd_ref[...])
    base_ref[...] = x1 + mod_ref[0, 5:6, :] * shared


def _mix(xf, attn, yssm, ga, gs, mod3, ln2_g, w, seq):
    n, d = xf.shape
    tm = TOKEN_TILE
    tiles_per_seq = seq // tm
    row = lambda i: (i, 0)
    const = lambda i: (0, 0)
    nt = d // LANES
    weights = [w["glu"], w["b_glu"], w["up_attn"], w["up_ssm"], w["out"], w["router_t"], w["sh_gu"], w["sh_down"]]
    return pl.pallas_call(
        _mix_kernel,
        out_shape=(jax.ShapeDtypeStruct((n * nt, LANES), F32),
                   jax.ShapeDtypeStruct((n, d), F32),
                   jax.ShapeDtypeStruct((N_EXPERTS, n), F32)),
        grid=(n // tm,),
        in_specs=[pl.BlockSpec((tm, d), row),
                  pl.BlockSpec((tm, attn.shape[1]), row),
                  pl.BlockSpec((tm, yssm.shape[1]), row),
                  pl.BlockSpec((tm, d), row),
                  pl.BlockSpec((tm, d), row),
                  pl.BlockSpec((1, 6, d), lambda i: (i // tiles_per_seq, 0, 0)),
                  pl.BlockSpec((1, d), const)] + [pl.BlockSpec(a.shape, const) for a in weights],
        out_specs=(pl.BlockSpec((tm * nt, LANES), row), pl.BlockSpec((tm, d), row),
                   pl.BlockSpec((N_EXPERTS, tm), lambda i: (0, i))),
        compiler_params=_cparams(1),
        name="mix",
    )(xf, attn, yssm, ga, gs, mod3, ln2_g.reshape(1, d), *weights)


def _route_kernel(score_ref, bias_ref, tri_ref, idx_ref, w_ref, rank_ref, cnt_ref, carry_ref):
    @pl.when(pl.program_id(0) == 0)
    def _():
        carry_ref[...] = jnp.zeros_like(carry_ref)

    scores = score_ref[...]
    ne, tn = scores.shape
    biased = scores + bias_ref[...]
    gsz = GROUP_SIZE
    sub = lax.broadcasted_iota(jnp.int32, (gsz, tn), 0)
    group_score = []
    for g in range(N_EXPERT_GROUPS):
        sg = biased[g * gsz:(g + 1) * gsz, :]
        m1 = jnp.max(sg, axis=0, keepdims=True)
        first = jnp.min(jnp.where(sg == m1, sub, gsz), axis=0, keepdims=True)
        m2 = jnp.max(jnp.where(sub == first, -jnp.inf, sg), axis=0, keepdims=True)
        group_score.append(m1 + m2)
    group_rows = []
    for g in range(N_EXPERT_GROUPS):
        beaten = jnp.zeros((1, tn), jnp.int32)
        for o in range(N_EXPERT_GROUPS):
            if o == g:
                continue
            wins = (group_score[o] > group_score[g])
            if o < g:
                wins = wins | (group_score[o] == group_score[g])
            beaten = beaten + jnp.where(wins, 1, 0)
        group_rows.append(jnp.broadcast_to(beaten < TOPK_GROUPS, (gsz, tn)))
    allowed = jnp.concatenate(group_rows, axis=0)
    cur = jnp.where(allowed, biased, MASK_VALUE)
    eio = lax.broadcasted_iota(jnp.int32, (ne, tn), 0)
    idx_rows, w_rows, hits = [], [], []
    for _ in range(TOP_K):
        vmax = jnp.max(cur, axis=0, keepdims=True)
        eidx = jnp.min(jnp.where(cur == vmax, eio, ne), axis=0, keepdims=True)
        hit = eio == eidx
        w_rows.append(jnp.sum(jnp.where(hit, scores, 0.0), axis=0, keepdims=True))
        idx_rows.append(eidx)
        hits.append(hit)
        cur = jnp.where(hit, -jnp.inf, cur)
    wts = jnp.concatenate(w_rows, axis=0)
    idx_ref[...] = jnp.concatenate(idx_rows, axis=0)
    w_ref[...] = wts / jnp.sum(wts, axis=0, keepdims=True) * ROUTED_SCALE

    onehot = jnp.zeros((ne, tn), F32)
    for hit in hits:
        onehot = onehot + jnp.where(hit, 1.0, 0.0)
    earlier = carry_ref[...] + _bdot(onehot, tri_ref[...])
    rank_rows = [jnp.sum(jnp.where(hit, earlier, 0.0), axis=0, keepdims=True) for hit in hits]
    rank_ref[...] = jnp.concatenate(rank_rows, axis=0).astype(jnp.int32)
    carry_ref[...] = carry_ref[...] + jnp.sum(onehot, axis=1, keepdims=True)
    cnt_ref[...] = carry_ref[...]


def _route(scores_t, router_bias):
    ne, n = scores_t.shape
    tn = ROUTE_TILE
    tri = jnp.asarray(np.arange(tn)[:, None] < np.arange(tn)[None, :], BF16)
    tok = lambda i: (0, i)
    const = lambda i: (0, 0)
    return pl.pallas_call(
        _route_kernel,
        out_shape=(jax.ShapeDtypeStruct((TOP_K, n), jnp.int32), jax.ShapeDtypeStruct((TOP_K, n), F32),
                   jax.ShapeDtypeStruct((TOP_K, n), jnp.int32), jax.ShapeDtypeStruct((ne, 1), F32)),
        grid=(n // tn,),
        in_specs=[pl.BlockSpec((ne, tn), tok), pl.BlockSpec((ne, 1), const), pl.BlockSpec((tn, tn), const)],
        out_specs=(pl.BlockSpec((TOP_K, tn), tok), pl.BlockSpec((TOP_K, tn), tok),
                   pl.BlockSpec((TOP_K, tn), tok), pl.BlockSpec((ne, 1), const)),
        scratch_shapes=[pltpu.VMEM((ne, 1), F32)],
        compiler_params=_cparams(1),
        name="route",
    )(scores_t, router_bias.reshape(ne, 1).astype(F32), tri)


def _dest_kernel(idx_ref, rank_ref, poffs_ref, dest_ref):
    idx = idx_ref[...]
    ne = poffs_ref.shape[0]
    eio = lax.broadcasted_iota(jnp.int32, (ne, idx.shape[1]), 0)
    poffs = poffs_ref[...]
    rows = [jnp.sum(jnp.where(eio == idx[kk:kk + 1, :], poffs, 0.0), axis=0, keepdims=True)
            for kk in range(idx.shape[0])]
    dest_ref[...] = rank_ref[...] + jnp.concatenate(rows, axis=0).astype(jnp.int32)


def _dest_slots(idx_t, rank_t, poffs):
    k, n = idx_t.shape
    tn = ROUTE_TILE
    ne = poffs.shape[0]
    tok = lambda i: (0, i)
    return pl.pallas_call(
        _dest_kernel,
        out_shape=jax.ShapeDtypeStruct((k, n), jnp.int32),
        grid=(n // tn,),
        in_specs=[pl.BlockSpec((k, tn), tok), pl.BlockSpec((k, tn), tok), pl.BlockSpec((ne, 1), lambda i: (0, 0))],
        out_specs=pl.BlockSpec((k, tn), tok),
        compiler_params=_cparams(1),
        name="dest_slots",
    )(idx_t, rank_t, poffs.astype(F32).reshape(ne, 1))


def _dispatch_tables(idx_t, counts, n_blocks):
    k, n = idx_t.shape
    n_assign = k * n
    rb = ROW_BLOCK
    ne = counts.shape[0]
    counts = counts.reshape(ne).astype(jnp.int32)
    offs = jnp.cumsum(counts) - counts
    nblk = (counts + rb - 1) // rb
    bend = jnp.cumsum(nblk)
    bstart = bend - nblk
    blocks = jnp.arange(n_blocks, dtype=jnp.int32)
    blk_e = jnp.minimum(jnp.searchsorted(bend, blocks, side='right'), ne - 1).astype(jnp.int32)
    src_start = (offs[blk_e] + (blocks - bstart[blk_e]) * rb).astype(jnp.int32)
    tok = lax.broadcasted_iota(jnp.int32, (k, n), 1)
    kk = lax.broadcasted_iota(jnp.int32, (k, n), 0)
    assert ne * n_assign < 2 ** 31
    keys = jnp.sort((idx_t * n_assign + tok * k + kk).reshape(n_assign))
    tok_sorted = (keys % n_assign) // k
    return tok_sorted, blk_e, src_start, bend[ne - 1:ne], bstart * rb


def _ffn_kernel(tok_ref, blk_e_ref, start_ref, used_ref, h2_hbm, wg_ref, wu_ref, wd_ref, y_ref, xbuf, sems):
    b = pl.program_id(0)
    rb = ROW_BLOCK
    d = wg_ref.shape[1]
    nt = d // LANES
    n_assign = tok_ref.shape[0]
    n_used = used_ref[0]
    n_blocks = pl.num_programs(0)

    def start_rows(blk, s, inline):
        start = start_ref[blk]

        def one(r):
            tok = tok_ref[jnp.minimum(start + r, n_assign - 1)]
            pltpu.make_async_copy(h2_hbm.at[pl.ds(pl.multiple_of(tok * nt, nt), nt)],
                                  xbuf.at[s, pl.ds(pl.multiple_of(r * nt, nt), nt)], sems.at[s]).start()

        if inline:
            for r in range(rb):
                one(r)
        else:
            lax.fori_loop(0, rb, lambda r, c: (one(r), c)[1], 0, unroll=ISSUE_UNROLL)

    def wait_rows(s):
        pltpu.make_async_copy(h2_hbm.at[pl.ds(0, rb * nt)], xbuf.at[s], sems.at[s]).wait()

    nbuf = xbuf.shape[0]
    ahead = nbuf - 1

    @pl.when(b == 0)
    def _():
        for a in range(ahead):
            start_rows(jnp.minimum(a, n_blocks - 1), a, False)

    for s in range(nbuf):
        @pl.when((b < n_used) & (b % nbuf == s))
        def _(s=s):
            wait_rows(s)
            xb = _load_rows_from_tiles(xbuf.at[s], rb, d).astype(BF16)
            start_rows(jnp.minimum(b + ahead, n_blocks - 1), (s + ahead) % nbuf, True)
            hg = jnp.dot(xb, wg_ref[0].astype(BF16), preferred_element_type=F32)
            hu = jnp.dot(xb, wu_ref[0].astype(BF16), preferred_element_type=F32)
            _store_rows_as_tiles(y_ref, _bdot(_silu(hg) * hu, wd_ref[0]))

    @pl.when(b == n_used - 1)
    def _():
        for a in range(1, nbuf):
            wait_rows((b + a) % nbuf)

    @pl.when(b >= n_used)
    def _():
        y_ref[...] = jnp.zeros_like(y_ref)


def _expert_ffn(h2_tiles, tok_sorted, blk_e, src_start, n_used, w_gate, w_up, w_down):
    n_blocks = blk_e.shape[0]
    rb = ROW_BLOCK
    _, d, ed = w_gate.shape
    nt = d // LANES
    expert = lambda i, tok, be, st, nu: (be[i], 0, 0)
    grid_spec = pltpu.PrefetchScalarGridSpec(
        num_scalar_prefetch=4,
        grid=(n_blocks,),
        in_specs=[pl.BlockSpec(memory_space=pl.ANY),
                  pl.BlockSpec((1, d, ed), expert),
                  pl.BlockSpec((1, d, ed), expert),
                  pl.BlockSpec((1, ed, d), expert)],
        out_specs=pl.BlockSpec((rb * nt, LANES), lambda i, tok, be, st, nu: (i, 0)),
        scratch_shapes=[pltpu.VMEM((FFN_ROW_BUFFERS, rb * nt, LANES), F32),
                        pltpu.SemaphoreType.DMA((FFN_ROW_BUFFERS,))],
    )
    return pl.pallas_call(
        _ffn_kernel,
        out_shape=jax.ShapeDtypeStruct((n_blocks * rb * nt, LANES), F32),
        grid_spec=grid_spec,
        compiler_params=_cparams(1),
        name="expert_ffn",
    )(tok_sorted, blk_e, src_start, n_used, h2_tiles, w_gate, w_up, w_down)


def _combine_kernel(dest_ref, y_hbm, base_ref, wt_ref, mod_ref, o_ref, buf, sems):
    i = pl.program_id(0)
    n_steps = pl.num_programs(0)
    tm, d = base_ref.shape
    k = wt_ref.shape[1]
    nt = d // LANES
    n_rows = k * tm
    slot = i % 2

    def issue(step, s):
        first = step * n_rows

        def body(r, _):
            src = dest_ref[first + r]
            pltpu.make_async_copy(y_hbm.at[pl.ds(pl.multiple_of(src * nt, nt), nt)],
                                  buf.at[s, pl.ds(pl.multiple_of(r * nt, nt), nt)], sems.at[s]).start()
            return 0

        lax.fori_loop(0, n_rows, body, 0, unroll=ISSUE_UNROLL)

    @pl.when(i == 0)
    def _():
        issue(0, 0)

    @pl.when(i + 1 < n_steps)
    def _():
        issue(i + 1, 1 - slot)

    pltpu.make_async_copy(y_hbm.at[pl.ds(0, n_rows * nt)], buf.at[slot], sems.at[slot]).wait()
    rows = buf.at[slot]
    wts = wt_ref[...]
    gate2 = mod_ref[0, 5:6, :]
    for c in range(nt):
        cols = slice(c * LANES, (c + 1) * LANES)
        routed = jnp.zeros((tm, LANES), F32)
        for kk in range(k):
            routed = routed + wts[:, kk:kk + 1] * rows[pl.ds(kk * tm * nt + c, tm, stride=nt), :]
        o_ref[:, cols] = base_ref[:, cols] + gate2[:, cols] * routed


def _combine(y_tiles, dest_t, w_t, base, mod3, seq):
    n, d = base.shape
    tm = COMBINE_TILE
    k = dest_t.shape[0]
    nt = d // LANES
    tiles_per_seq = seq // tm
    dest_tiles = dest_t.reshape(k, n // tm, tm).transpose(1, 0, 2).reshape(n * k)
    grid_spec = pltpu.PrefetchScalarGridSpec(
        num_scalar_prefetch=1,
        grid=(n // tm,),
        in_specs=[pl.BlockSpec(memory_space=pl.ANY),
                  pl.BlockSpec((tm, d), lambda i, dst: (i, 0)),
                  pl.BlockSpec((tm, k), lambda i, dst: (i, 0)),
                  pl.BlockSpec((1, 6, d), lambda i, dst: (i // tiles_per_seq, 0, 0))],
        out_specs=pl.BlockSpec((tm, d), lambda i, dst: (i, 0)),
        scratch_shapes=[pltpu.VMEM((2, k * tm * nt, LANES), F32), pltpu.SemaphoreType.DMA((2,))],
    )
    return pl.pallas_call(
        _combine_kernel,
        out_shape=jax.ShapeDtypeStruct((n, d), F32),
        grid_spec=grid_spec,
        compiler_params=_cparams(1),
        name="combine",
    )(dest_tiles, y_tiles, base, w_t.T, mod3)


def _hybrid_layer(x, cond, rel_bias, w_ada, b_ada, ln1_g, w_in, q_norm_g, k_norm_g,
                  ssm_lambda_re, ssm_lambda_im, ssm_log_dt, ssm_b_re, ssm_b_im, ssm_c_re, ssm_c_im,
                  ssm_d, ssm_w_glu, ssm_b_glu, w_up_attn, w_up_ssm, w_out, ln2_g,
                  w_router, router_bias, w_exp_gate, w_exp_up, w_exp_down,
                  w_sh_gate, w_sh_up, w_sh_down):
    bsz, seq, d = x.shape
    n = bsz * seq
    xf = x.reshape(n, d)
    mod3 = _adaln(cond, w_ada, b_ada).reshape(bsz, 6, d)

    q_gain = jnp.tile(q_norm_g.astype(F32), ATTN_HEADS).reshape(1, ATTN_WIDTH)
    k_gain = jnp.tile(k_norm_g.astype(F32), ATTN_HEADS).reshape(1, ATTN_WIDTH)
    q, k, v, u, ga, gs, kmean = _inproj(xf, mod3, ln1_g, w_in.astype(BF16), q_gain, k_gain, seq)

    attn = _moba_attention(q, k, v, kmean, rel_bias, bsz, seq)
    ops = _s5_operators(ssm_lambda_re, ssm_lambda_im, ssm_log_dt, ssm_b_re, ssm_b_im,
                        ssm_c_re, ssm_c_im, ssm_d, seq // SSM_CHUNK)
    yssm = _s5_scan(u, ops, bsz, seq)

    weights = {
        "glu": ssm_w_glu.astype(BF16), "b_glu": ssm_b_glu.astype(F32).reshape(1, -1),
        "up_attn": w_up_attn.astype(BF16), "up_ssm": w_up_ssm.astype(BF16), "out": w_out.astype(BF16),
        "router_t": w_router.T.astype(BF16),
        "sh_gu": jnp.concatenate([w_sh_gate, w_sh_up], axis=1).astype(BF16),
        "sh_down": w_sh_down.astype(BF16),
    }
    h2, base, scores_t = _mix(xf, attn, yssm, ga, gs, mod3, ln2_g, weights, seq)

    idx_t, w_t, rank_t, counts = _route(scores_t, router_bias)
    n_blocks = -(-(n * TOP_K) // ROW_BLOCK) + N_EXPERTS
    tok_sorted, blk_e, src_start, n_used, poffs = _dispatch_tables(idx_t, counts, n_blocks)
    dest_t = _dest_slots(idx_t, rank_t, poffs)
    y_tiles = _expert_ffn(h2, tok_sorted, blk_e, src_start, n_used, w_exp_gate, w_exp_up, w_exp_down)
    out = _combine(y_tiles, dest_t, w_t, base, mod3, seq)
    return out.reshape(bsz, seq, d)


def kernel(x, c, rel_bias, w_ada, b_ada, ln1_g, w_in, q_norm_g, k_norm_g, ssm_lambda_re, ssm_lambda_im, ssm_log_dt, ssm_b_re, ssm_b_im, ssm_c_re, ssm_c_im, ssm_d, ssm_w_glu, ssm_b_glu, w_up_attn, w_up_ssm, w_out, ln2_g, w_router, router_bias, w_exp_gate, w_exp_up, w_exp_down, w_sh_gate, w_sh_up, w_sh_down):
    for l in range(w_ada.shape[0]):
        x = _hybrid_layer(x, c, rel_bias, w_ada[l], b_ada[l], ln1_g[l], w_in[l], q_norm_g[l], k_norm_g[l],
                          ssm_lambda_re[l], ssm_lambda_im[l], ssm_log_dt[l], ssm_b_re[l], ssm_b_im[l],
                          ssm_c_re[l], ssm_c_im[l], ssm_d[l], ssm_w_glu[l], ssm_b_glu[l],
                          w_up_attn[l], w_up_ssm[l], w_out[l], ln2_g[l], w_router[l], router_bias[l],
                          w_exp_gate[l], w_exp_up[l], w_exp_down[l], w_sh_gate[l], w_sh_up[l], w_sh_down[l])
    return x
```

```python
import functools
import math

import numpy as np
import jax
import jax.numpy as jnp
from jax import lax
from jax.experimental import pallas as pl
from jax.experimental.pallas import tpu as pltpu

F32 = jnp.float32
BF16 = jnp.bfloat16

ATTN_HEADS = 8
HEAD_DIM = 64
ATTN_WIDTH = ATTN_HEADS * HEAD_DIM
MOBA_BLOCK = 256
MOBA_TOPK = 3
NUM_BUCKETS = 32
MAX_DISTANCE = 128
SSM_WIDTH = 512
SSM_GROUP = 16
SSM_GROUPS = SSM_WIDTH // SSM_GROUP
SSM_STATE = 64
N_EXPERTS = 256
TOP_K = 8
N_EXPERT_GROUPS = 8
TOPK_GROUPS = 4
GROUP_SIZE = N_EXPERTS // N_EXPERT_GROUPS
EXPERT_DIM = 256
ROUTED_SCALE = 2.5
EPS = 1e-6
MASK_VALUE = -1e30

LANES = 128
HEADS_PER_STEP = LANES // HEAD_DIM
SSM_CHUNK = 16
SUBLANES = 8
ROW_BLOCK = 256
TOKEN_TILE = 256
ROUTE_TILE = 256
COMBINE_TILE = 128
ISSUE_UNROLL = 8
FFN_ROW_BUFFERS = 3
VMEM_LIMIT = 56 * 1024 * 1024


def _store_rows_as_tiles(ref, val):
    rows, d = val.shape
    nt = d // LANES
    for c in range(nt):
        ref[pl.ds(c, rows, stride=nt), :] = val[:, c * LANES:(c + 1) * LANES]


def _load_rows_from_tiles(ref, rows, d):
    nt = d // LANES
    return jnp.concatenate([ref[pl.ds(c, rows, stride=nt), :] for c in range(nt)], axis=1)


def _cparams(n_axes, vmem=VMEM_LIMIT):
    return pltpu.CompilerParams(dimension_semantics=("arbitrary",) * n_axes, vmem_limit_bytes=vmem)


def _sigmoid(x):
    return 1.0 / (1.0 + jnp.exp(-x))


def _silu(x):
    return x * _sigmoid(x)


def _bdot(a, b):
    return jnp.dot(a.astype(BF16), b.astype(BF16), preferred_element_type=F32)


def _bdot_nt(a, b):
    return lax.dot_general(a.astype(BF16), b.astype(BF16), (((1,), (1,)), ((), ())),
                           preferred_element_type=F32)


def _adaln_kernel(c_ref, w_ref, b_ref, o_ref):
    o_ref[...] = _bdot(_silu(c_ref[...]), w_ref[...]) + b_ref[...]


def _adaln(c, w_ada, b_ada):
    bsz, d = c.shape
    n_out = w_ada.shape[1]
    return pl.pallas_call(
        _adaln_kernel,
        out_shape=jax.ShapeDtypeStruct((bsz, n_out), F32),
        grid=(n_out // d,),
        in_specs=[pl.BlockSpec((bsz, d), lambda j: (0, 0)),
                  pl.BlockSpec((d, d), lambda j: (0, j)),
                  pl.BlockSpec((1, d), lambda j: (0, j))],
        out_specs=pl.BlockSpec((bsz, d), lambda j: (0, j)),
        compiler_params=_cparams(1),
        name="adaln",
    )(c, w_ada, b_ada.reshape(1, n_out))


def _modulated_norm(x, gain, shift, scale):
    y = x * lax.rsqrt(jnp.mean(x * x, axis=-1, keepdims=True) + EPS) * gain
    return y * (1.0 + scale) + shift


def _head_norm(t, seg, gain):
    ms = _bdot(t * t, seg)
    return t * lax.rsqrt(ms + EPS) * gain


def _inproj_kernel(x_ref, mod_ref, ln_ref, w_ref, seg_ref, qg_ref, kg_ref,
                   q_ref, k_ref, v_ref, u_ref, ga_ref, gs_ref, km_ref, u_scr):
    aw, sw, d = ATTN_WIDTH, SSM_WIDTH, x_ref.shape[1]
    h = _modulated_norm(x_ref[...], ln_ref[...], mod_ref[0, 0:1, :], mod_ref[0, 1:2, :]).astype(BF16)
    seg = seg_ref[...]
    q = jnp.dot(h, w_ref[:, 0:aw], preferred_element_type=F32)
    q_ref[...] = _head_norm(q, seg, qg_ref[...])
    k = jnp.dot(h, w_ref[:, aw:2 * aw], preferred_element_type=F32)
    kn = _head_norm(k, seg, kg_ref[...])
    k_ref[...] = kn.astype(BF16)
    km_ref[0] = jnp.mean(kn, axis=0, keepdims=True)
    v_ref[...] = jnp.dot(h, w_ref[:, 2 * aw:3 * aw], preferred_element_type=F32).astype(BF16)
    o = 3 * aw
    u = jnp.dot(h, w_ref[:, o:o + sw], preferred_element_type=F32)
    n_chunk = u_scr.shape[1] // SSM_CHUNK
    for cb in range(sw // LANES):
        u_scr[cb] = u[:, cb * LANES:(cb + 1) * LANES]
        for sg in range(SSM_CHUNK):
            u_ref[cb, :, sg * LANES:(sg + 1) * LANES] = (
                u_scr[cb, pl.ds(sg, n_chunk, stride=SSM_CHUNK), :].astype(BF16))
    o += sw
    ga_ref[...] = jnp.dot(h, w_ref[:, o:o + d], preferred_element_type=F32).astype(BF16)
    o += d
    gs_ref[...] = jnp.dot(h, w_ref[:, o:o + d], preferred_element_type=F32).astype(BF16)


def _inproj(xf, mod3, ln1_g, w_in_b, q_gain, k_gain, seq):
    n, d = xf.shape
    tm = MOBA_BLOCK
    tiles_per_seq = seq // tm
    aw, sw = ATTN_WIDTH, SSM_WIDTH
    head_of_lane = np.arange(aw) // HEAD_DIM
    seg = jnp.asarray((head_of_lane[:, None] == head_of_lane[None, :]) / HEAD_DIM, BF16)
    row = lambda i: (i, 0)
    const = lambda i: (0, 0)
    return pl.pallas_call(
        _inproj_kernel,
        out_shape=(jax.ShapeDtypeStruct((n, aw), F32),
                   jax.ShapeDtypeStruct((n, aw), BF16),
                   jax.ShapeDtypeStruct((n, aw), BF16),
                   jax.ShapeDtypeStruct((sw // LANES, n // SSM_CHUNK, SSM_CHUNK * LANES), BF16),
                   jax.ShapeDtypeStruct((n, d), BF16),
                   jax.ShapeDtypeStruct((n, d), BF16),
                   jax.ShapeDtypeStruct((n // tm, 1, aw), F32)),
        grid=(n // tm,),
        in_specs=[pl.BlockSpec((tm, d), row),
                  pl.BlockSpec((1, 6, d), lambda i: (i // tiles_per_seq, 0, 0)),
                  pl.BlockSpec((1, d), const),
                  pl.BlockSpec(w_in_b.shape, const),
                  pl.BlockSpec((aw, aw), const),
                  pl.BlockSpec((1, aw), const),
                  pl.BlockSpec((1, aw), const)],
        out_specs=(pl.BlockSpec((tm, aw), row), pl.BlockSpec((tm, aw), row), pl.BlockSpec((tm, aw), row),
                   pl.BlockSpec((sw // LANES, tm // SSM_CHUNK, SSM_CHUNK * LANES), lambda i: (0, i, 0)),
                   pl.BlockSpec((tm, d), row), pl.BlockSpec((tm, d), row),
                   pl.BlockSpec((1, 1, aw), lambda i: (i, 0, 0))),
        scratch_shapes=[pltpu.VMEM((sw // LANES, tm, LANES), F32)],
        compiler_params=_cparams(1),
        name="inproj",
    )(xf, mod3, ln1_g.reshape(1, d), w_in_b, seg, q_gain, k_gain)


def _t5_bucket(rel):
    n = jnp.maximum(rel, 0)
    max_exact = NUM_BUCKETS // 2
    nf = jnp.maximum(n, 1).astype(F32)
    large = max_exact + (jnp.log(nf / max_exact) / math.log(MAX_DISTANCE / max_exact)
                         * (NUM_BUCKETS - max_exact)).astype(jnp.int32)
    large = jnp.minimum(large, NUM_BUCKETS - 1)
    return jnp.where(n < max_exact, n, large)


def _bias_tables(rel_bias):
    blk = MOBA_BLOCK
    assert blk + 1 >= MAX_DISTANCE
    rel = jnp.arange(blk)[None, :] - jnp.arange(blk)[:, None]
    table = rel_bias.astype(F32)
    table = table - table[NUM_BUCKETS - 1][None, :]

    def lookup(r):
        onehot = jax.nn.one_hot(_t5_bucket(r), NUM_BUCKETS, dtype=F32)
        return jnp.einsum('kqn,nh->hkq', onehot, table, precision=lax.Precision.HIGHEST)

    return lookup(rel), lookup(rel + blk)


def _select_blocks(gate_t, n_past):
    nb, tq = gate_t.shape
    blk = lax.broadcasted_iota(jnp.int32, (nb, tq), 0)
    beaten = jnp.zeros((nb, tq), jnp.int32)
    for m in range(nb):
        gm = gate_t[m:m + 1, :]
        wins = (gm > gate_t) | ((gm == gate_t) & (m < blk))
        beaten = beaten + jnp.where(wins & (m < n_past), 1, 0)
    return jnp.where((blk < n_past) & (beaten < MOBA_TOPK), 1.0, 0.0)


def _attn_kernel(q_ref, k_ref, vt_ref, km_ref, bias_ref, o_ref, sel_ref, s_ref):
    qi = pl.program_id(2)
    tq = q_ref.shape[0]
    blk = MOBA_BLOCK
    hd = HEAD_DIM
    heads = range(HEADS_PER_STEP)
    q = q_ref[...]
    lane = lax.broadcasted_iota(jnp.int32, (tq, LANES), 1)
    kpos = lax.broadcasted_iota(jnp.int32, (blk, tq), 0)
    qpos = lax.broadcasted_iota(jnp.int32, (blk, tq), 1)
    scale = hd ** -0.5
    n_far = jnp.maximum(qi - 1, 0)
    n_pairs = (n_far + 1) // 2
    jp = jnp.maximum(qi - 1, 0)
    k_own = k_ref[pl.ds(pl.multiple_of(qi * blk, blk), blk), :]
    k_prev = k_ref[pl.ds(pl.multiple_of(jp * blk, blk), blk), :]

    qbs = []
    for h in heads:
        in_head = (lane >= h * hd) & (lane < (h + 1) * hd)
        qm = jnp.where(in_head, q, 0.0)
        gate_t = lax.dot_general(km_ref[0], qm, (((1,), (1,)), ((), ())),
                                 precision=lax.Precision.HIGHEST, preferred_element_type=F32)
        sel_ref[h] = _select_blocks(gate_t, qi)
        qbs.append((qm * scale).astype(BF16))

    def pair_scores(j):
        kb = k_ref[pl.ds(pl.multiple_of(j * blk, blk), 2 * blk), :]
        return [_bdot_nt(kb, qbs[h]) for h in heads]

    def attend(h, p, blocks):
        acc = None
        for i, j in enumerate(blocks):
            part = jnp.dot(vt_ref[0, j, h * hd:(h + 1) * hd, :], p[i * blk:(i + 1) * blk, :],
                           preferred_element_type=F32)
            acc = part if acc is None else acc + part
        return acc

    for h, s in enumerate(pair_scores(0)):
        s_ref[h] = s

    carries = []
    for h in heads:
        s_prev = _bdot_nt(k_prev, qbs[h]) + bias_ref[h, 0:blk, :]
        s_prev = jnp.where(sel_ref[h, pl.ds(jp, 1), :] > 0.5, s_prev, MASK_VALUE)
        s_own = _bdot_nt(k_own, qbs[h]) + bias_ref[h, blk:2 * blk, :]
        s_own = jnp.where(kpos <= qpos, s_own, MASK_VALUE)
        s = jnp.concatenate([s_prev, s_own], axis=0)
        m = jnp.max(s, axis=0, keepdims=True)
        p = jnp.exp(s - m)
        l = jnp.sum(p, axis=0, keepdims=True)
        carries.append((m, l, attend(h, p.astype(BF16), (jp, qi))))

    def far_pair(pi, carries):
        j = 2 * pi
        s_cur = [s_ref[h] for h in heads]
        for h, s in enumerate(pair_scores(2 * jnp.minimum(pi + 1, n_pairs - 1))):
            s_ref[h] = s
        second_is_far = j + 1 < n_far
        out = []
        for h in heads:
            m, l, acc = carries[h]
            c0 = sel_ref[h, pl.ds(j, 1), :] > 0.5
            c1 = (sel_ref[h, pl.ds(j + 1, 1), :] > 0.5) & second_is_far
            chosen = jnp.concatenate([jnp.broadcast_to(c0, (blk, tq)), jnp.broadcast_to(c1, (blk, tq))], axis=0)
            s = jnp.where(chosen, s_cur[h], MASK_VALUE)
            m_new = jnp.maximum(m, jnp.max(s, axis=0, keepdims=True))
            alpha = jnp.exp(m - m_new)
            p = jnp.exp(s - m_new)
            l = alpha * l + jnp.sum(p, axis=0, keepdims=True)
            acc = alpha * acc + attend(h, p.astype(BF16), (j, j + 1))
            out.append((m_new, l, acc))
        return tuple(out)

    carries = lax.fori_loop(0, n_pairs, far_pair, tuple(carries))
    out_t = jnp.concatenate([acc / l for _, l, acc in carries], axis=0)
    o_ref[...] = out_t.T.astype(o_ref.dtype)


def _moba_attention(q, k, v, kmean, rel_bias, bsz, seq):
    n, aw = q.shape
    blk = MOBA_BLOCK
    nb = seq // blk
    assert nb >= 2
    own, prev = _bias_tables(rel_bias)
    bias = jnp.concatenate([prev, own], axis=1)
    hps = HEADS_PER_STEP
    npair = aw // LANES
    vt = v.reshape(bsz, nb, blk, aw).transpose(0, 1, 3, 2)
    return pl.pallas_call(
        _attn_kernel,
        out_shape=jax.ShapeDtypeStruct((n, aw), BF16),
        grid=(bsz, npair, nb),
        in_specs=[pl.BlockSpec((blk, LANES), lambda b, hp, qi: (b * nb + qi, hp)),
                  pl.BlockSpec((seq, LANES), lambda b, hp, qi: (b, hp)),
                  pl.BlockSpec((1, nb, LANES, blk), lambda b, hp, qi: (b, 0, hp, 0)),
                  pl.BlockSpec((1, nb, LANES), lambda b, hp, qi: (b, 0, hp)),
                  pl.BlockSpec((hps, 2 * blk, blk), lambda b, hp, qi: (hp, 0, 0))],
        out_specs=pl.BlockSpec((blk, LANES), lambda b, hp, qi: (b * nb + qi, hp)),
        scratch_shapes=[pltpu.VMEM((hps, nb, blk), F32), pltpu.VMEM((hps, 2 * blk, blk), F32)],
        compiler_params=_cparams(3),
        name="moba_attention",
    )(q, k, vt, kmean.reshape(bsz, nb, aw), bias)


def _s5_operators(lambda_re, lambda_im, log_dt, b_re, b_im, c_re, c_im, d_skip, n_chunks):
    hi = lax.Precision.HIGHEST
    L, G, P, C = SSM_CHUNK, SSM_GROUPS, SSM_STATE, SSM_GROUP
    lam_re = jnp.minimum(lambda_re.astype(F32), -1e-4)
    lam_im = lambda_im.astype(F32)
    dt = jnp.exp(log_dt.astype(F32))[:, None]
    z_re, z_im = lam_re * dt, lam_im * dt

    def a_pow(nvec):
        nv = jnp.asarray(nvec, F32)[:, None, None]
        mag = jnp.exp(nv * z_re)
        return mag * jnp.cos(nv * z_im), mag * jnp.sin(nv * z_im)

    a_re, a_im = a_pow([1.0])
    a_re, a_im = a_re[0], a_im[0]
    den = lam_re * lam_re + lam_im * lam_im
    nr = a_re - 1.0
    coef_re = (nr * lam_re + a_im * lam_im) / den
    coef_im = (a_im * lam_re - nr * lam_im) / den
    br, bi = b_re.astype(F32), b_im.astype(F32)
    bbar_re = coef_re[..., None] * br - coef_im[..., None] * bi
    bbar_im = coef_re[..., None] * bi + coef_im[..., None] * br
    cr, ci = c_re.astype(F32), c_im.astype(F32)

    pw_re, pw_im = a_pow(np.arange(L + 1))
    cb_re = cr[None] * pw_re[:, :, None, :] - ci[None] * pw_im[:, :, None, :]
    cb_im = cr[None] * pw_im[:, :, None, :] + ci[None] * pw_re[:, :, None, :]
    kern = (jnp.einsum('jgop,gpi->jgoi', cb_re[:L], bbar_re, precision=hi)
            - jnp.einsum('jgop,gpi->jgoi', cb_im[:L], bbar_im, precision=hi))
    sig = np.arange(L)[:, None]
    tau = np.arange(L)[None, :]
    lag = np.clip(tau - sig, 0, L - 1)
    causal = jnp.asarray((tau >= sig), F32)
    t_op = kern[lag] * causal[:, :, None, None, None]
    t_op = t_op.transpose(2, 0, 4, 1, 3).reshape(G, L * C, L * C)
    d_g = d_skip.astype(F32).reshape(G, C)
    t_op = t_op + jnp.eye(L * C, dtype=F32)[None] * jnp.tile(d_g, (1, L))[:, None, :]

    rp_re, rp_im = pw_re[L - 1 - np.arange(L)], pw_im[L - 1 - np.arange(L)]
    p_re = rp_re[..., None] * bbar_re[None] - rp_im[..., None] * bbar_im[None]
    p_im = rp_re[..., None] * bbar_im[None] + rp_im[..., None] * bbar_re[None]
    p_op = jnp.concatenate([p_re, p_im], axis=2)
    p_op = p_op.transpose(1, 0, 3, 2).reshape(G, L * C, 2 * P)

    q_re = cb_re[1:].transpose(1, 3, 0, 2)
    q_im = -cb_im[1:].transpose(1, 3, 0, 2)
    q_op = jnp.concatenate([q_re, q_im], axis=1).reshape(G, 2 * P, L * C)

    n_steps = max(1, int(math.ceil(math.log2(n_chunks))))
    dk_re, dk_im = a_pow([float(L * 2 ** k) for k in range(n_steps)])
    GB = LANES // C
    NB = G // GB
    eye = jnp.eye(GB, dtype=F32)
    t_big = (t_op.reshape(NB, GB, L, C, 1, L, 1, C) * eye.reshape(1, GB, 1, 1, GB, 1, 1, 1)
             ).transpose(0, 2, 1, 3, 5, 4, 6, 7)[..., 0, :].reshape(NB, L * LANES, L * LANES)
    p_big = (p_op.reshape(NB, GB, L, C, 2, 1, P) * eye.reshape(1, GB, 1, 1, 1, GB, 1)
             ).transpose(0, 2, 1, 3, 4, 5, 6).reshape(NB, L * LANES, 2 * GB * P)
    q_big = (q_op.reshape(NB, GB, 2, P, L, 1, C) * eye.reshape(1, GB, 1, 1, 1, GB, 1)
             ).transpose(0, 2, 1, 3, 4, 5, 6).reshape(NB, 2 * GB * P, L * LANES)
    dk_re = dk_re.reshape(-1, NB, GB * P)
    dk_im = dk_im.reshape(-1, NB, GB * P)
    a1 = jnp.concatenate([dk_re, dk_re], axis=-1).transpose(1, 0, 2)
    a2 = jnp.concatenate([-dk_im, dk_im], axis=-1).transpose(1, 0, 2)
    return t_big.astype(BF16), p_big.astype(BF16), q_big.astype(BF16), a1, a2


def _s5_kernel(x_ref, t_ref, p_ref, q_ref, a1_ref, a2_ref, y_ref):
    x = x_ref[0]
    s = jnp.dot(x, p_ref[0], preferred_element_type=F32)
    n_chunks, width = s.shape
    chunk = lax.broadcasted_iota(jnp.int32, (n_chunks, width), 0)
    a1 = a1_ref[0]
    a2 = a2_ref[0]
    h = jnp.where(chunk >= 1, pltpu.roll(s, 1, axis=0), 0.0)
    for kk in range(a1.shape[0]):
        dist = 2 ** kk
        if dist >= n_chunks:
            break
        hs = jnp.where(chunk >= dist, pltpu.roll(h, dist, axis=0), 0.0)
        h = h + a1[kk:kk + 1, :] * hs + a2[kk:kk + 1, :] * pltpu.roll(hs, width // 2, axis=1)
    y = jnp.dot(x, t_ref[0], preferred_element_type=F32)
    y_ref[0] = y + jnp.dot(h.astype(BF16), q_ref[0], preferred_element_type=F32)


def _s5_scan(x_chunks, ops, bsz, seq):
    t_big, p_big, q_big, a1, a2 = ops
    nblk, rows, w = x_chunks.shape
    nc = seq // SSM_CHUNK
    sw = p_big.shape[2]
    col = lambda cb, b: (cb, 0, 0)
    return pl.pallas_call(
        _s5_kernel,
        out_shape=jax.ShapeDtypeStruct((nblk, rows, w), F32),
        grid=(nblk, bsz),
        in_specs=[pl.BlockSpec((1, nc, w), lambda cb, b: (cb, b, 0)),
                  pl.BlockSpec((1, w, w), col),
                  pl.BlockSpec((1, w, sw), col),
                  pl.BlockSpec((1, sw, w), col),
                  pl.BlockSpec((1,) + a1.shape[1:], col),
                  pl.BlockSpec((1,) + a2.shape[1:], col)],
        out_specs=pl.BlockSpec((1, nc, w), lambda cb, b: (cb, b, 0)),
        compiler_params=_cparams(2),
        name="s5_scan",
    )(x_chunks, t_big, p_big, q_big, a1, a2)


def _gelu_tanh(x):
    return 0.5 * x * (1.0 + jnp.tanh(math.sqrt(2.0 / math.pi) * (x + 0.044715 * (x * x * x))))


def _mix_kernel(x_ref, attn_ref, yssm_ref, ga_ref, gs_ref, mod_ref, ln_ref,
                wglu_ref, bglu_ref, wua_ref, wus_ref, wout_ref, wrt_ref, wsgu_ref, wsd_ref,
                h2_ref, base_ref, score_ref, y_scr):
    n_chunk = yssm_ref.shape[1]
    for cb in range(yssm_ref.shape[0]):
        for tau in range(SSM_CHUNK):
            y_scr[cb, pl.ds(tau, n_chunk, stride=SSM_CHUNK), :] = yssm_ref[cb, :, tau * LANES:(tau + 1) * LANES]
    g = _gelu_tanh(jnp.concatenate([y_scr[cb] for cb in range(yssm_ref.shape[0])], axis=1))
    glu = g * _sigmoid(_bdot(g, wglu_ref[...]) + bglu_ref[...])
    y_attn = jnp.dot(attn_ref[...], wua_ref[...], preferred_element_type=F32)
    y_ssm = _bdot(glu, wus_ref[...])
    mixed = _sigmoid(ga_ref[...].astype(F32)) * y_attn + _sigmoid(gs_ref[...].astype(F32)) * y_ssm
    gate1 = mod_ref[0, 2:3, :]
    x1 = x_ref[...] + gate1 * _bdot(mixed, wout_ref[...])
    h2 = _modulated_norm(x1, ln_ref[...], mod_ref[0, 3:4, :], mod_ref[0, 4:5, :])
    _store_rows_as_tiles(h2_ref, h2)
    h2b = h2.astype(BF16)
    score_ref[...] = _sigmoid(_bdot_nt(wrt_ref[...], h2b))
    gu = jnp.dot(h2b, wsgu_ref[...], preferred_element_type=F32)
    sd = wsd_ref.shape[0]
    shared = _bdot(_silu(gu[:, :sd]) * gu[:, sd:], wsd_ref[...])
    base_ref[...] = x1 + mod_ref[0, 5:6, :] * shared


def _mix(xf, attn, yssm, ga, gs, mod3, ln2_g, w, seq):
    n, d = xf.shape
    tm = TOKEN_TILE
    tiles_per_seq = seq // tm
    row = lambda i: (i, 0)
    const = lambda i: (0, 0)
    nt = d // LANES
    weights = [w["glu"], w["b_glu"], w["up_attn"], w["up_ssm"], w["out"], w["router_t"], w["sh_gu"], w["sh_down"]]
    return pl.pallas_call(
        _mix_kernel,
        out_shape=(jax.ShapeDtypeStruct((n * nt, LANES), F32),
                   jax.ShapeDtypeStruct((n, d), F32),
                   jax.ShapeDtypeStruct((N_EXPERTS, n), F32)),
        grid=(n // tm,),
        in_specs=[pl.BlockSpec((tm, d), row),
                  pl.BlockSpec((tm, attn.shape[1]), row),
                  pl.BlockSpec((yssm.shape[0], tm // SSM_CHUNK, yssm.shape[2]), lambda i: (0, i, 0)),
                  pl.BlockSpec((tm, d), row),
                  pl.BlockSpec((tm, d), row),
                  pl.BlockSpec((1, 6, d), lambda i: (i // tiles_per_seq, 0, 0)),
                  pl.BlockSpec((1, d), const)] + [pl.BlockSpec(a.shape, const) for a in weights],
        out_specs=(pl.BlockSpec((tm * nt, LANES), row), pl.BlockSpec((tm, d), row),
                   pl.BlockSpec((N_EXPERTS, tm), lambda i: (0, i))),
        scratch_shapes=[pltpu.VMEM((yssm.shape[0], tm, LANES), F32)],
        compiler_params=_cparams(1),
        name="mix",
    )(xf, attn, yssm, ga, gs, mod3, ln2_g.reshape(1, d), *weights)


def _route_kernel(score_ref, bias_ref, tri_ref, idx_ref, w_ref, rank_ref, cnt_ref, carry_ref):
    @pl.when(pl.program_id(0) == 0)
    def _():
        carry_ref[...] = jnp.zeros_like(carry_ref)

    scores = score_ref[...]
    ne, tn = scores.shape
    biased = scores + bias_ref[...]
    gsz = GROUP_SIZE
    sub = lax.broadcasted_iota(jnp.int32, (gsz, tn), 0)
    group_score = []
    for g in range(N_EXPERT_GROUPS):
        sg = biased[g * gsz:(g + 1) * gsz, :]
        m1 = jnp.max(sg, axis=0, keepdims=True)
        first = jnp.min(jnp.where(sg == m1, sub, gsz), axis=0, keepdims=True)
        m2 = jnp.max(jnp.where(sub == first, -jnp.inf, sg), axis=0, keepdims=True)
        group_score.append(m1 + m2)
    group_rows = []
    for g in range(N_EXPERT_GROUPS):
        beaten = jnp.zeros((1, tn), jnp.int32)
        for o in range(N_EXPERT_GROUPS):
            if o == g:
                continue
            wins = (group_score[o] > group_score[g])
            if o < g:
                wins = wins | (group_score[o] == group_score[g])
            beaten = beaten + jnp.where(wins, 1, 0)
        group_rows.append(jnp.broadcast_to(beaten < TOPK_GROUPS, (gsz, tn)))
    allowed = jnp.concatenate(group_rows, axis=0)
    cur = jnp.where(allowed, biased, MASK_VALUE)
    eio = lax.broadcasted_iota(jnp.int32, (ne, tn), 0)
    idx_rows, w_rows, hits = [], [], []
    for _ in range(TOP_K):
        vmax = jnp.max(cur, axis=0, keepdims=True)
        eidx = jnp.min(jnp.where(cur == vmax, eio, ne), axis=0, keepdims=True)
        hit = eio == eidx
        w_rows.append(jnp.sum(jnp.where(hit, scores, 0.0), axis=0, keepdims=True))
        idx_rows.append(eidx)
        hits.append(hit)
        cur = jnp.where(hit, -jnp.inf, cur)
    wts = jnp.concatenate(w_rows, axis=0)
    idx_ref[...] = jnp.concatenate(idx_rows, axis=0)
    w_ref[...] = wts / jnp.sum(wts, axis=0, keepdims=True) * ROUTED_SCALE

    onehot = jnp.zeros((ne, tn), F32)
    for hit in hits:
        onehot = onehot + jnp.where(hit, 1.0, 0.0)
    earlier = carry_ref[...] + _bdot(onehot, tri_ref[...])
    rank_rows = [jnp.sum(jnp.where(hit, earlier, 0.0), axis=0, keepdims=True) for hit in hits]
    rank_ref[...] = jnp.concatenate(rank_rows, axis=0).astype(jnp.int32)
    carry_ref[...] = carry_ref[...] + jnp.sum(onehot, axis=1, keepdims=True)
    cnt_ref[...] = carry_ref[...]


def _route(scores_t, router_bias):
    ne, n = scores_t.shape
    tn = ROUTE_TILE
    tri = jnp.asarray(np.arange(tn)[:, None] < np.arange(tn)[None, :], BF16)
    tok = lambda i: (0, i)
    const = lambda i: (0, 0)
    return pl.pallas_call(
        _route_kernel,
        out_shape=(jax.ShapeDtypeStruct((TOP_K, n), jnp.int32), jax.ShapeDtypeStruct((TOP_K, n), F32),
                   jax.ShapeDtypeStruct((TOP_K, n), jnp.int32), jax.ShapeDtypeStruct((ne, 1), F32)),
        grid=(n // tn,),
        in_specs=[pl.BlockSpec((ne, tn), tok), pl.BlockSpec((ne, 1), const), pl.BlockSpec((tn, tn), const)],
        out_specs=(pl.BlockSpec((TOP_K, tn), tok), pl.BlockSpec((TOP_K, tn), tok),
                   pl.BlockSpec((TOP_K, tn), tok), pl.BlockSpec((ne, 1), const)),
        scratch_shapes=[pltpu.VMEM((ne, 1), F32)],
        compiler_params=_cparams(1),
        name="route",
    )(scores_t, router_bias.reshape(ne, 1).astype(F32), tri)


def _dest_kernel(idx_ref, rank_ref, poffs_ref, dest_ref):
    idx = idx_ref[...]
    ne = poffs_ref.shape[0]
    eio = lax.broadcasted_iota(jnp.int32, (ne, idx.shape[1]), 0)
    poffs = poffs_ref[...]
    rows = [jnp.sum(jnp.where(eio == idx[kk:kk + 1, :], poffs, 0.0), axis=0, keepdims=True)
            for kk in range(idx.shape[0])]
    dest_ref[...] = rank_ref[...] + jnp.concatenate(rows, axis=0).astype(jnp.int32)


def _dest_slots(idx_t, rank_t, poffs):
    k, n = idx_t.shape
    tn = ROUTE_TILE
    ne = poffs.shape[0]
    tok = lambda i: (0, i)
    return pl.pallas_call(
        _dest_kernel,
        out_shape=jax.ShapeDtypeStruct((k, n), jnp.int32),
        grid=(n // tn,),
        in_specs=[pl.BlockSpec((k, tn), tok), pl.BlockSpec((k, tn), tok), pl.BlockSpec((ne, 1), lambda i: (0, 0))],
        out_specs=pl.BlockSpec((k, tn), tok),
        compiler_params=_cparams(1),
        name="dest_slots",
    )(idx_t, rank_t, poffs.astype(F32).reshape(ne, 1))


def _dispatch_tables(idx_t, counts, n_blocks):
    k, n = idx_t.shape
    n_assign = k * n
    rb = ROW_BLOCK
    ne = counts.shape[0]
    counts = counts.reshape(ne).astype(jnp.int32)
    offs = jnp.cumsum(counts) - counts
    nblk = (counts + rb - 1) // rb
    bend = jnp.cumsum(nblk)
    bstart = bend - nblk
    blocks = jnp.arange(n_blocks, dtype=jnp.int32)
    blk_e = jnp.minimum(jnp.sum((bend[None, :] <= blocks[:, None]).astype(jnp.int32), axis=1), ne - 1)
    mine = blk_e[:, None] == jnp.arange(ne, dtype=jnp.int32)[None, :]
    src_start = jnp.sum(jnp.where(mine, (offs - bstart * rb)[None, :], 0), axis=1) + blocks * rb
    tok = lax.broadcasted_iota(jnp.int32, (k, n), 1)
    kk = lax.broadcasted_iota(jnp.int32, (k, n), 0)
    assert ne * n_assign < 2 ** 31
    keys = jnp.sort((idx_t * n_assign + tok * k + kk).reshape(n_assign))
    tok_sorted = (keys % n_assign) // k
    return tok_sorted, blk_e, src_start, bend[ne - 1:ne], bstart * rb


def _ffn_kernel(tok_ref, blk_e_ref, start_ref, used_ref, h2_hbm, wg_ref, wu_ref, wd_ref, y_ref, xbuf, sems):
    b = pl.program_id(0)
    rb = ROW_BLOCK
    d = wg_ref.shape[1]
    nt = d // LANES
    n_assign = tok_ref.shape[0]
    n_used = used_ref[0]
    n_blocks = pl.num_programs(0)

    def start_rows(blk, s, inline):
        start = start_ref[blk]

        def one(r):
            tok = tok_ref[jnp.minimum(start + r, n_assign - 1)]
            pltpu.make_async_copy(h2_hbm.at[pl.ds(pl.multiple_of(tok * nt, nt), nt)],
                                  xbuf.at[s, pl.ds(pl.multiple_of(r * nt, nt), nt)], sems.at[s]).start()

        if inline:
            for r in range(rb):
                one(r)
        else:
            lax.fori_loop(0, rb, lambda r, c: (one(r), c)[1], 0, unroll=ISSUE_UNROLL)

    def wait_rows(s):
        pltpu.make_async_copy(h2_hbm.at[pl.ds(0, rb * nt)], xbuf.at[s], sems.at[s]).wait()

    nbuf = xbuf.shape[0]
    ahead = nbuf - 1

    @pl.when(b == 0)
    def _():
        for a in range(ahead):
            start_rows(jnp.minimum(a, n_blocks - 1), a, False)

    for s in range(nbuf):
        @pl.when((b < n_used) & (b % nbuf == s))
        def _(s=s):
            wait_rows(s)
            xb = _load_rows_from_tiles(xbuf.at[s], rb, d).astype(BF16)
            start_rows(jnp.minimum(b + ahead, n_blocks - 1), (s + ahead) % nbuf, True)
            hg = jnp.dot(xb, wg_ref[0].astype(BF16), preferred_element_type=F32)
            hu = jnp.dot(xb, wu_ref[0].astype(BF16), preferred_element_type=F32)
            _store_rows_as_tiles(y_ref, _bdot(_silu(hg) * hu, wd_ref[0]))

    @pl.when(b == n_used - 1)
    def _():
        for a in range(1, nbuf):
            wait_rows((b + a) % nbuf)

    @pl.when(b >= n_used)
    def _():
        y_ref[...] = jnp.zeros_like(y_ref)


def _expert_ffn(h2_tiles, tok_sorted, blk_e, src_start, n_used, w_gate, w_up, w_down):
    n_blocks = blk_e.shape[0]
    rb = ROW_BLOCK
    _, d, ed = w_gate.shape
    nt = d // LANES
    expert = lambda i, tok, be, st, nu: (be[i], 0, 0)
    grid_spec = pltpu.PrefetchScalarGridSpec(
        num_scalar_prefetch=4,
        grid=(n_blocks,),
        in_specs=[pl.BlockSpec(memory_space=pl.ANY),
                  pl.BlockSpec((1, d, ed), expert),
                  pl.BlockSpec((1, d, ed), expert),
                  pl.BlockSpec((1, ed, d), expert)],
        out_specs=pl.BlockSpec((rb * nt, LANES), lambda i, tok, be, st, nu: (i, 0)),
        scratch_shapes=[pltpu.VMEM((FFN_ROW_BUFFERS, rb * nt, LANES), F32),
                        pltpu.SemaphoreType.DMA((FFN_ROW_BUFFERS,))],
    )
    return pl.pallas_call(
        _ffn_kernel,
        out_shape=jax.ShapeDtypeStruct((n_blocks * rb * nt, LANES), F32),
        grid_spec=grid_spec,
        compiler_params=_cparams(1),
        name="expert_ffn",
    )(tok_sorted, blk_e, src_start, n_used, h2_tiles, w_gate, w_up, w_down)


def _combine_kernel(dest_ref, y_hbm, base_ref, wt_ref, mod_ref, o_ref, buf, sems):
    i = pl.program_id(0)
    n_steps = pl.num_programs(0)
    tm, d = base_ref.shape
    k = wt_ref.shape[1]
    nt = d // LANES
    n_rows = k * tm
    slot = i % 2

    def issue(step, s):
        first = step * n_rows

        def body(r, _):
            src = dest_ref[first + r]
            pltpu.make_async_copy(y_hbm.at[pl.ds(pl.multiple_of(src * nt, nt), nt)],
                                  buf.at[s, pl.ds(pl.multiple_of(r * nt, nt), nt)], sems.at[s]).start()
            return 0

        lax.fori_loop(0, n_rows, body, 0, unroll=ISSUE_UNROLL)

    @pl.when(i == 0)
    def _():
        issue(0, 0)

    @pl.when(i + 1 < n_steps)
    def _():
        issue(i + 1, 1 - slot)

    pltpu.make_async_copy(y_hbm.at[pl.ds(0, n_rows * nt)], buf.at[slot], sems.at[slot]).wait()
    rows = buf.at[slot]
    wts = wt_ref[...]
    gate2 = mod_ref[0, 5:6, :]
    for c in range(nt):
        cols = slice(c * LANES, (c + 1) * LANES)
        routed = jnp.zeros((tm, LANES), F32)
        for kk in range(k):
            routed = routed + wts[:, kk:kk + 1] * rows[pl.ds(kk * tm * nt + c, tm, stride=nt), :]
        o_ref[:, cols] = base_ref[:, cols] + gate2[:, cols] * routed


def _combine(y_tiles, dest_t, w_t, base, mod3, seq):
    n, d = base.shape
    tm = COMBINE_TILE
    k = dest_t.shape[0]
    nt = d // LANES
    tiles_per_seq = seq // tm
    dest_tiles = dest_t.reshape(k, n // tm, tm).transpose(1, 0, 2).reshape(n * k)
    grid_spec = pltpu.PrefetchScalarGridSpec(
        num_scalar_prefetch=1,
        grid=(n // tm,),
        in_specs=[pl.BlockSpec(memory_space=pl.ANY),
                  pl.BlockSpec((tm, d), lambda i, dst: (i, 0)),
                  pl.BlockSpec((tm, k), lambda i, dst: (i, 0)),
                  pl.BlockSpec((1, 6, d), lambda i, dst: (i // tiles_per_seq, 0, 0))],
        out_specs=pl.BlockSpec((tm, d), lambda i, dst: (i, 0)),
        scratch_shapes=[pltpu.VMEM((2, k * tm * nt, LANES), F32), pltpu.SemaphoreType.DMA((2,))],
    )
    return pl.pallas_call(
        _combine_kernel,
        out_shape=jax.ShapeDtypeStruct((n, d), F32),
        grid_spec=grid_spec,
        compiler_params=_cparams(1),
        name="combine",
    )(dest_tiles, y_tiles, base, w_t.T, mod3)


def _hybrid_layer(x, cond, rel_bias, w_ada, b_ada, ln1_g, w_in, q_norm_g, k_norm_g,
                  ssm_lambda_re, ssm_lambda_im, ssm_log_dt, ssm_b_re, ssm_b_im, ssm_c_re, ssm_c_im,
                  ssm_d, ssm_w_glu, ssm_b_glu, w_up_attn, w_up_ssm, w_out, ln2_g,
                  w_router, router_bias, w_exp_gate, w_exp_up, w_exp_down,
                  w_sh_gate, w_sh_up, w_sh_down):
    bsz, seq, d = x.shape
    n = bsz * seq
    xf = x.reshape(n, d)
    mod3 = _adaln(cond, w_ada, b_ada).reshape(bsz, 6, d)

    q_gain = jnp.tile(q_norm_g.astype(F32), ATTN_HEADS).reshape(1, ATTN_WIDTH)
    k_gain = jnp.tile(k_norm_g.astype(F32), ATTN_HEADS).reshape(1, ATTN_WIDTH)
    q, k, v, u, ga, gs, kmean = _inproj(xf, mod3, ln1_g, w_in.astype(BF16), q_gain, k_gain, seq)

    attn = _moba_attention(q, k, v, kmean, rel_bias, bsz, seq)
    ops = _s5_operators(ssm_lambda_re, ssm_lambda_im, ssm_log_dt, ssm_b_re, ssm_b_im,
                        ssm_c_re, ssm_c_im, ssm_d, seq // SSM_CHUNK)
    yssm = _s5_scan(u, ops, bsz, seq)

    weights = {
        "glu": ssm_w_glu.astype(BF16), "b_glu": ssm_b_glu.astype(F32).reshape(1, -1),
        "up_attn": w_up_attn.astype(BF16), "up_ssm": w_up_ssm.astype(BF16), "out": w_out.astype(BF16),
        "router_t": w_router.T.astype(BF16),
        "sh_gu": jnp.concatenate([w_sh_gate, w_sh_up], axis=1).astype(BF16),
        "sh_down": w_sh_down.astype(BF16),
    }
    h2, base, scores_t = _mix(xf, attn, yssm, ga, gs, mod3, ln2_g, weights, seq)

    idx_t, w_t, rank_t, counts = _route(scores_t, router_bias)
    n_blocks = -(-(n * TOP_K) // ROW_BLOCK) + N_EXPERTS
    tok_sorted, blk_e, src_start, n_used, poffs = _dispatch_tables(idx_t, counts, n_blocks)
    dest_t = _dest_slots(idx_t, rank_t, poffs)
    y_tiles = _expert_ffn(h2, tok_sorted, blk_e, src_start, n_used, w_exp_gate, w_exp_up, w_exp_down)
    out = _combine(y_tiles, dest_t, w_t, base, mod3, seq)
    return out.reshape(bsz, seq, d)


def kernel(x, c, rel_bias, w_ada, b_ada, ln1_g, w_in, q_norm_g, k_norm_g, ssm_lambda_re, ssm_lambda_im, ssm_log_dt, ssm_b_re, ssm_b_im, ssm_c_re, ssm_c_im, ssm_d, ssm_w_glu, ssm_b_glu, w_up_attn, w_up_ssm, w_out, ln2_g, w_router, router_bias, w_exp_gate, w_exp_up, w_exp_down, w_sh_gate, w_sh_up, w_sh_down):
    for l in range(w_ada.shape[0]):
        x = _hybrid_layer(x, c, rel_bias, w_ada[l], b_ada[l], ln1_g[l], w_in[l], q_norm_g[l], k_norm_g[l],
                          ssm_lambda_re[l], ssm_lambda_im[l], ssm_log_dt[l], ssm_b_re[l], ssm_b_im[l],
                          ssm_c_re[l], ssm_c_im[l], ssm_d[l], ssm_w_glu[l], ssm_b_glu[l],
                          w_up_attn[l], w_up_ssm[l], w_out[l], ln2_g[l], w_router[l], router_bias[l],
                          w_exp_gate[l], w_exp_up[l], w_exp_down[l], w_sh_gate[l], w_sh_up[l], w_sh_down[l])
    return x
```

```python
import functools
import math

import numpy as np
import jax
import jax.numpy as jnp
from jax import lax
from jax.experimental import pallas as pl
from jax.experimental.pallas import tpu as pltpu

F32 = jnp.float32
BF16 = jnp.bfloat16

ATTN_HEADS = 8
HEAD_DIM = 64
ATTN_WIDTH = ATTN_HEADS * HEAD_DIM
MOBA_BLOCK = 256
MOBA_TOPK = 3
NUM_BUCKETS = 32
MAX_DISTANCE = 128
SSM_WIDTH = 512
SSM_GROUP = 16
SSM_GROUPS = SSM_WIDTH // SSM_GROUP
SSM_STATE = 64
N_EXPERTS = 256
TOP_K = 8
N_EXPERT_GROUPS = 8
TOPK_GROUPS = 4
GROUP_SIZE = N_EXPERTS // N_EXPERT_GROUPS
EXPERT_DIM = 256
ROUTED_SCALE = 2.5
EPS = 1e-6
MASK_VALUE = -1e30

LANES = 128
HEADS_PER_STEP = LANES // HEAD_DIM
SSM_CHUNK = 16
SUBLANES = 8
ROW_BLOCK = 256
TOKEN_TILE = 256
ROUTE_TILE = 256
COMBINE_TILE = 128
ISSUE_UNROLL = 8
FFN_ROW_BUFFERS = 3
VMEM_LIMIT = 56 * 1024 * 1024


def _store_rows_as_tiles(ref, val):
    rows, d = val.shape
    nt = d // LANES
    for c in range(nt):
        ref[pl.ds(c, rows, stride=nt), :] = val[:, c * LANES:(c + 1) * LANES]


def _load_rows_from_tiles(ref, rows, d):
    nt = d // LANES
    return jnp.concatenate([ref[pl.ds(c, rows, stride=nt), :] for c in range(nt)], axis=1)


def _cparams(n_axes, vmem=VMEM_LIMIT):
    return pltpu.CompilerParams(dimension_semantics=("arbitrary",) * n_axes, vmem_limit_bytes=vmem)


def _sigmoid(x):
    return 1.0 / (1.0 + jnp.exp(-x))


def _silu(x):
    return x * _sigmoid(x)


def _bdot(a, b):
    return jnp.dot(a.astype(BF16), b.astype(BF16), preferred_element_type=F32)


def _bdot_nt(a, b):
    return lax.dot_general(a.astype(BF16), b.astype(BF16), (((1,), (1,)), ((), ())),
                           preferred_element_type=F32)


def _adaln_kernel(c_ref, w_ref, b_ref, o_ref):
    o_ref[...] = _bdot(_silu(c_ref[...]), w_ref[...]) + b_ref[...]


def _adaln(c, w_ada, b_ada):
    bsz, d = c.shape
    n_out = w_ada.shape[1]
    return pl.pallas_call(
        _adaln_kernel,
        out_shape=jax.ShapeDtypeStruct((bsz, n_out), F32),
        grid=(n_out // d,),
        in_specs=[pl.BlockSpec((bsz, d), lambda j: (0, 0)),
                  pl.BlockSpec((d, d), lambda j: (0, j)),
                  pl.BlockSpec((1, d), lambda j: (0, j))],
        out_specs=pl.BlockSpec((bsz, d), lambda j: (0, j)),
        compiler_params=_cparams(1),
        name="adaln",
    )(c, w_ada, b_ada.reshape(1, n_out))


def _modulated_norm(x, gain, shift, scale):
    y = x * lax.rsqrt(jnp.mean(x * x, axis=-1, keepdims=True) + EPS) * gain
    return y * (1.0 + scale) + shift


def _head_norm(t, seg, gain):
    ms = _bdot(t * t, seg)
    return t * lax.rsqrt(ms + EPS) * gain


def _inproj_kernel(x_ref, mod_ref, ln_ref, w_ref, seg_ref, qg_ref, kg_ref,
                   q_ref, k_ref, v_ref, u_ref, ga_ref, gs_ref, km_ref, u_scr):
    aw, sw, d = ATTN_WIDTH, SSM_WIDTH, x_ref.shape[1]
    h = _modulated_norm(x_ref[...], ln_ref[...], mod_ref[0, 0:1, :], mod_ref[0, 1:2, :]).astype(BF16)
    seg = seg_ref[...]
    q = jnp.dot(h, w_ref[:, 0:aw], preferred_element_type=F32)
    q_ref[...] = _head_norm(q, seg, qg_ref[...])
    k = jnp.dot(h, w_ref[:, aw:2 * aw], preferred_element_type=F32)
    kn = _head_norm(k, seg, kg_ref[...])
    k_ref[...] = kn.astype(BF16)
    km_ref[0] = jnp.mean(kn, axis=0, keepdims=True)
    v_ref[...] = jnp.dot(h, w_ref[:, 2 * aw:3 * aw], preferred_element_type=F32).astype(BF16)
    o = 3 * aw
    u = jnp.dot(h, w_ref[:, o:o + sw], preferred_element_type=F32)
    n_chunk = u_scr.shape[1] // SSM_CHUNK
    for cb in range(sw // LANES):
        u_scr[cb] = u[:, cb * LANES:(cb + 1) * LANES]
        for sg in range(SSM_CHUNK):
            u_ref[cb, :, sg * LANES:(sg + 1) * LANES] = (
                u_scr[cb, pl.ds(sg, n_chunk, stride=SSM_CHUNK), :].astype(BF16))
    o += sw
    ga_ref[...] = jnp.dot(h, w_ref[:, o:o + d], preferred_element_type=F32).astype(BF16)
    o += d
    gs_ref[...] = jnp.dot(h, w_ref[:, o:o + d], preferred_element_type=F32).astype(BF16)


def _inproj(xf, mod3, ln1_g, w_in_b, q_gain, k_gain, seq):
    n, d = xf.shape
    tm = MOBA_BLOCK
    tiles_per_seq = seq // tm
    aw, sw = ATTN_WIDTH, SSM_WIDTH
    head_of_lane = np.arange(aw) // HEAD_DIM
    seg = jnp.asarray((head_of_lane[:, None] == head_of_lane[None, :]) / HEAD_DIM, BF16)
    row = lambda i: (i, 0)
    const = lambda i: (0, 0)
    return pl.pallas_call(
        _inproj_kernel,
        out_shape=(jax.ShapeDtypeStruct((n, aw), F32),
                   jax.ShapeDtypeStruct((n, aw), BF16),
                   jax.ShapeDtypeStruct((n, aw), BF16),
                   jax.ShapeDtypeStruct((sw // LANES, n // SSM_CHUNK, SSM_CHUNK * LANES), BF16),
                   jax.ShapeDtypeStruct((n, d), BF16),
                   jax.ShapeDtypeStruct((n, d), BF16),
                   jax.ShapeDtypeStruct((n // tm, 1, aw), F32)),
        grid=(n // tm,),
        in_specs=[pl.BlockSpec((tm, d), row),
                  pl.BlockSpec((1, 6, d), lambda i: (i // tiles_per_seq, 0, 0)),
                  pl.BlockSpec((1, d), const),
                  pl.BlockSpec(w_in_b.shape, const),
                  pl.BlockSpec((aw, aw), const),
                  pl.BlockSpec((1, aw), const),
                  pl.BlockSpec((1, aw), const)],
        out_specs=(pl.BlockSpec((tm, aw), row), pl.BlockSpec((tm, aw), row), pl.BlockSpec((tm, aw), row),
                   pl.BlockSpec((sw // LANES, tm // SSM_CHUNK, SSM_CHUNK * LANES), lambda i: (0, i, 0)),
                   pl.BlockSpec((tm, d), row), pl.BlockSpec((tm, d), row),
                   pl.BlockSpec((1, 1, aw), lambda i: (i, 0, 0))),
        scratch_shapes=[pltpu.VMEM((sw // LANES, tm, LANES), F32)],
        compiler_params=_cparams(1),
        name="inproj",
    )(xf, mod3, ln1_g.reshape(1, d), w_in_b, seg, q_gain, k_gain)


def _t5_bucket(rel):
    n = jnp.maximum(rel, 0)
    max_exact = NUM_BUCKETS // 2
    nf = jnp.maximum(n, 1).astype(F32)
    large = max_exact + (jnp.log(nf / max_exact) / math.log(MAX_DISTANCE / max_exact)
                         * (NUM_BUCKETS - max_exact)).astype(jnp.int32)
    large = jnp.minimum(large, NUM_BUCKETS - 1)
    return jnp.where(n < max_exact, n, large)


def _bias_tables(rel_bias):
    blk = MOBA_BLOCK
    assert blk + 1 >= MAX_DISTANCE
    rel = jnp.arange(blk)[None, :] - jnp.arange(blk)[:, None]
    table = rel_bias.astype(F32)
    table = table - table[NUM_BUCKETS - 1][None, :]

    def lookup(r):
        onehot = jax.nn.one_hot(_t5_bucket(r), NUM_BUCKETS, dtype=F32)
        return jnp.einsum('kqn,nh->hkq', onehot, table, precision=lax.Precision.HIGHEST)

    return lookup(rel), lookup(rel + blk)


def _select_blocks(gate_t, n_past):
    nb, tq = gate_t.shape
    blk = lax.broadcasted_iota(jnp.int32, (nb, tq), 0)
    beaten = jnp.zeros((nb, tq), jnp.int32)
    for m in range(nb):
        gm = gate_t[m:m + 1, :]
        wins = (gm > gate_t) | ((gm == gate_t) & (m < blk))
        beaten = beaten + jnp.where(wins & (m < n_past), 1, 0)
    return jnp.where((blk < n_past) & (beaten < MOBA_TOPK), 1.0, 0.0)


def _attn_kernel(q_ref, k_ref, vt_ref, km_ref, bias_ref, o_ref, sel_ref, s_ref):
    qi = pl.program_id(2)
    tq = q_ref.shape[0]
    blk = MOBA_BLOCK
    hd = HEAD_DIM
    heads = range(HEADS_PER_STEP)
    q = q_ref[...]
    lane = lax.broadcasted_iota(jnp.int32, (tq, LANES), 1)
    kpos = lax.broadcasted_iota(jnp.int32, (blk, tq), 0)
    qpos = lax.broadcasted_iota(jnp.int32, (blk, tq), 1)
    scale = hd ** -0.5
    n_far = jnp.maximum(qi - 1, 0)
    n_pairs = (n_far + 1) // 2
    jp = jnp.maximum(qi - 1, 0)
    k_own = k_ref[pl.ds(pl.multiple_of(qi * blk, blk), blk), :]
    k_prev = k_ref[pl.ds(pl.multiple_of(jp * blk, blk), blk), :]

    qbs = []
    for h in heads:
        in_head = (lane >= h * hd) & (lane < (h + 1) * hd)
        qm = jnp.where(in_head, q, 0.0)
        gate_t = lax.dot_general(km_ref[0], qm, (((1,), (1,)), ((), ())),
                                 precision=lax.Precision.HIGHEST, preferred_element_type=F32)
        sel_ref[h] = _select_blocks(gate_t, qi)
        qbs.append((qm * scale).astype(BF16))

    def pair_scores(j):
        kb = k_ref[pl.ds(pl.multiple_of(j * blk, blk), 2 * blk), :]
        return [_bdot_nt(kb, qbs[h]) for h in heads]

    def attend(h, p, blocks):
        acc = None
        for i, j in enumerate(blocks):
            part = jnp.dot(vt_ref[0, j, h * hd:(h + 1) * hd, :], p[i * blk:(i + 1) * blk, :],
                           preferred_element_type=F32)
            acc = part if acc is None else acc + part
        return acc

    for h, s in enumerate(pair_scores(0)):
        s_ref[h] = s

    carries = []
    for h in heads:
        s_prev = _bdot_nt(k_prev, qbs[h]) + bias_ref[h, 0:blk, :]
        s_prev = jnp.where(sel_ref[h, pl.ds(jp, 1), :] > 0.5, s_prev, MASK_VALUE)
        s_own = _bdot_nt(k_own, qbs[h]) + bias_ref[h, blk:2 * blk, :]
        s_own = jnp.where(kpos <= qpos, s_own, MASK_VALUE)
        s = jnp.concatenate([s_prev, s_own], axis=0)
        m = jnp.max(s, axis=0, keepdims=True)
        p = jnp.exp(s - m)
        l = jnp.sum(p, axis=0, keepdims=True)
        carries.append((m, l, attend(h, p.astype(BF16), (jp, qi))))

    def far_pair(pi, carries):
        j = 2 * pi
        s_cur = [s_ref[h] for h in heads]
        for h, s in enumerate(pair_scores(2 * jnp.minimum(pi + 1, n_pairs - 1))):
            s_ref[h] = s
        second_is_far = j + 1 < n_far
        out = []
        for h in heads:
            m, l, acc = carries[h]
            c0 = sel_ref[h, pl.ds(j, 1), :] > 0.5
            c1 = (sel_ref[h, pl.ds(j + 1, 1), :] > 0.5) & second_is_far
            chosen = jnp.concatenate([jnp.broadcast_to(c0, (blk, tq)), jnp.broadcast_to(c1, (blk, tq))], axis=0)
            s = jnp.where(chosen, s_cur[h], MASK_VALUE)
            m_new = jnp.maximum(m, jnp.max(s, axis=0, keepdims=True))
            alpha = jnp.exp(m - m_new)
            p = jnp.exp(s - m_new)
            l = alpha * l + jnp.sum(p, axis=0, keepdims=True)
            acc = alpha * acc + attend(h, p.astype(BF16), (j, j + 1))
            out.append((m_new, l, acc))
        return tuple(out)

    carries = lax.fori_loop(0, n_pairs, far_pair, tuple(carries))
    out_t = jnp.concatenate([acc / l for _, l, acc in carries], axis=0)
    o_ref[...] = out_t.T.astype(o_ref.dtype)


def _moba_attention(q, k, v, kmean, rel_bias, bsz, seq):
    n, aw = q.shape
    blk = MOBA_BLOCK
    nb = seq // blk
    assert nb >= 2
    own, prev = _bias_tables(rel_bias)
    bias = jnp.concatenate([prev, own], axis=1)
    hps = HEADS_PER_STEP
    npair = aw // LANES
    vt = v.reshape(bsz, nb, blk, aw).transpose(0, 1, 3, 2)
    return pl.pallas_call(
        _attn_kernel,
        out_shape=jax.ShapeDtypeStruct((n, aw), BF16),
        grid=(bsz, npair, nb),
        in_specs=[pl.BlockSpec((blk, LANES), lambda b, hp, qi: (b * nb + qi, hp)),
                  pl.BlockSpec((seq, LANES), lambda b, hp, qi: (b, hp)),
                  pl.BlockSpec((1, nb, LANES, blk), lambda b, hp, qi: (b, 0, hp, 0)),
                  pl.BlockSpec((1, nb, LANES), lambda b, hp, qi: (b, 0, hp)),
                  pl.BlockSpec((hps, 2 * blk, blk), lambda b, hp, qi: (hp, 0, 0))],
        out_specs=pl.BlockSpec((blk, LANES), lambda b, hp, qi: (b * nb + qi, hp)),
        scratch_shapes=[pltpu.VMEM((hps, nb, blk), F32), pltpu.VMEM((hps, 2 * blk, blk), F32)],
        compiler_params=_cparams(3),
        name="moba_attention",
    )(q, k, vt, kmean.reshape(bsz, nb, aw), bias)


def _s5_operators(lambda_re, lambda_im, log_dt, b_re, b_im, c_re, c_im, d_skip, n_chunks):
    hi = lax.Precision.HIGHEST
    L, G, P, C = SSM_CHUNK, SSM_GROUPS, SSM_STATE, SSM_GROUP
    lam_re = jnp.minimum(lambda_re.astype(F32), -1e-4)
    lam_im = lambda_im.astype(F32)
    dt = jnp.exp(log_dt.astype(F32))[:, None]
    z_re, z_im = lam_re * dt, lam_im * dt

    def a_pow(nvec):
        nv = jnp.asarray(nvec, F32)[:, None, None]
        mag = jnp.exp(nv * z_re)
        return mag * jnp.cos(nv * z_im), mag * jnp.sin(nv * z_im)

    a_re, a_im = a_pow([1.0])
    a_re, a_im = a_re[0], a_im[0]
    den = lam_re * lam_re + lam_im * lam_im
    nr = a_re - 1.0
    coef_re = (nr * lam_re + a_im * lam_im) / den
    coef_im = (a_im * lam_re - nr * lam_im) / den
    br, bi = b_re.astype(F32), b_im.astype(F32)
    bbar_re = coef_re[..., None] * br - coef_im[..., None] * bi
    bbar_im = coef_re[..., None] * bi + coef_im[..., None] * br
    cr, ci = c_re.astype(F32), c_im.astype(F32)

    pw_re, pw_im = a_pow(np.arange(L + 1))
    cb_re = cr[None] * pw_re[:, :, None, :] - ci[None] * pw_im[:, :, None, :]
    cb_im = cr[None] * pw_im[:, :, None, :] + ci[None] * pw_re[:, :, None, :]
    kern = (jnp.einsum('jgop,gpi->jgoi', cb_re[:L], bbar_re, precision=hi)
            - jnp.einsum('jgop,gpi->jgoi', cb_im[:L], bbar_im, precision=hi))
    sig = np.arange(L)[:, None]
    tau = np.arange(L)[None, :]
    lag = np.clip(tau - sig, 0, L - 1)
    causal = jnp.asarray((tau >= sig), F32)
    t_op = kern[lag] * causal[:, :, None, None, None]
    t_op = t_op.transpose(2, 0, 4, 1, 3).reshape(G, L * C, L * C)
    d_g = d_skip.astype(F32).reshape(G, C)
    t_op = t_op + jnp.eye(L * C, dtype=F32)[None] * jnp.tile(d_g, (1, L))[:, None, :]

    rp_re, rp_im = pw_re[L - 1 - np.arange(L)], pw_im[L - 1 - np.arange(L)]
    p_re = rp_re[..., None] * bbar_re[None] - rp_im[..., None] * bbar_im[None]
    p_im = rp_re[..., None] * bbar_im[None] + rp_im[..., None] * bbar_re[None]
    p_op = jnp.concatenate([p_re, p_im], axis=2)
    p_op = p_op.transpose(1, 0, 3, 2).reshape(G, L * C, 2 * P)

    q_re = cb_re[1:].transpose(1, 3, 0, 2)
    q_im = -cb_im[1:].transpose(1, 3, 0, 2)
    q_op = jnp.concatenate([q_re, q_im], axis=1).reshape(G, 2 * P, L * C)

    n_steps = max(1, int(math.ceil(math.log2(n_chunks))))
    dk_re, dk_im = a_pow([float(L * 2 ** k) for k in range(n_steps)])
    GB = LANES // C
    NB = G // GB
    lc = np.arange(L * C)
    wide = np.arange(L * LANES)
    expand_tc = jnp.asarray((lc[:, None] // C == wide[None, :] // LANES) & (lc[:, None] % C == wide[None, :] % C), BF16)
    st = np.arange(2 * P)
    wide_st = np.arange(2 * GB * P)
    expand_st = jnp.asarray((st[:, None] // P == wide_st[None, :] // (GB * P))
                            & (st[:, None] % P == wide_st[None, :] % P), BF16)
    g_of_wide = (jnp.arange(L * LANES) // C) % GB
    g_of_state = (jnp.arange(2 * GB * P) // P) % GB

    def widen(rows, expand, g_row, g_col):
        full = jnp.einsum('brk,kc->brc', rows.astype(BF16), expand, preferred_element_type=F32)
        return jnp.where(g_row[:, None] == g_col[None, :], full, 0.0).astype(BF16)

    t_rows = t_op.reshape(NB, GB, L, C, L * C).transpose(0, 2, 1, 3, 4).reshape(NB, L * LANES, L * C)
    p_rows = p_op.reshape(NB, GB, L, C, 2 * P).transpose(0, 2, 1, 3, 4).reshape(NB, L * LANES, 2 * P)
    q_rows = q_op.reshape(NB, GB, 2, P, L * C).transpose(0, 2, 1, 3, 4).reshape(NB, 2 * GB * P, L * C)
    t_big = widen(t_rows, expand_tc, g_of_wide, g_of_wide)
    p_big = widen(p_rows, expand_st, g_of_wide, g_of_state)
    q_big = widen(q_rows, expand_tc, g_of_state, g_of_wide)
    dk_re = dk_re.reshape(-1, NB, GB * P)
    dk_im = dk_im.reshape(-1, NB, GB * P)
    a1 = jnp.concatenate([dk_re, dk_re], axis=-1).transpose(1, 0, 2)
    a2 = jnp.concatenate([-dk_im, dk_im], axis=-1).transpose(1, 0, 2)
    return t_big.astype(BF16), p_big.astype(BF16), q_big.astype(BF16), a1, a2


def _s5_kernel(x_ref, t_ref, p_ref, q_ref, a1_ref, a2_ref, y_ref):
    x = x_ref[0]
    s = jnp.dot(x, p_ref[0], preferred_element_type=F32)
    n_chunks, width = s.shape
    chunk = lax.broadcasted_iota(jnp.int32, (n_chunks, width), 0)
    a1 = a1_ref[0]
    a2 = a2_ref[0]
    h = jnp.where(chunk >= 1, pltpu.roll(s, 1, axis=0), 0.0)
    for kk in range(a1.shape[0]):
        dist = 2 ** kk
        if dist >= n_chunks:
            break
        hs = jnp.where(chunk >= dist, pltpu.roll(h, dist, axis=0), 0.0)
        h = h + a1[kk:kk + 1, :] * hs + a2[kk:kk + 1, :] * pltpu.roll(hs, width // 2, axis=1)
    hb = h.astype(BF16)
    step = 2 * LANES
    for t in range(x.shape[1] // step):
        hi = (t + 1) * step
        y_ref[0, :, t * step:hi] = (jnp.dot(x[:, :hi], t_ref[0, :hi, t * step:hi], preferred_element_type=F32)
                                    + jnp.dot(hb, q_ref[0, :, t * step:hi], preferred_element_type=F32))


def _s5_scan(x_chunks, ops, bsz, seq):
    t_big, p_big, q_big, a1, a2 = ops
    nblk, rows, w = x_chunks.shape
    nc = seq // SSM_CHUNK
    sw = p_big.shape[2]
    col = lambda cb, b: (cb, 0, 0)
    return pl.pallas_call(
        _s5_kernel,
        out_shape=jax.ShapeDtypeStruct((nblk, rows, w), F32),
        grid=(nblk, bsz),
        in_specs=[pl.BlockSpec((1, nc, w), lambda cb, b: (cb, b, 0)),
                  pl.BlockSpec((1, w, w), col),
                  pl.BlockSpec((1, w, sw), col),
                  pl.BlockSpec((1, sw, w), col),
                  pl.BlockSpec((1,) + a1.shape[1:], col),
                  pl.BlockSpec((1,) + a2.shape[1:], col)],
        out_specs=pl.BlockSpec((1, nc, w), lambda cb, b: (cb, b, 0)),
        compiler_params=_cparams(2),
        name="s5_scan",
    )(x_chunks, t_big, p_big, q_big, a1, a2)


def _gelu_tanh(x):
    return 0.5 * x * (1.0 + jnp.tanh(math.sqrt(2.0 / math.pi) * (x + 0.044715 * (x * x * x))))


def _mix_kernel(x_ref, attn_ref, yssm_ref, ga_ref, gs_ref, mod_ref, ln_ref,
                wglu_ref, bglu_ref, wua_ref, wus_ref, wout_ref, wrt_ref, wsgu_ref, wsd_ref,
                h2_ref, base_ref, score_ref, y_scr):
    n_chunk = yssm_ref.shape[1]
    for cb in range(yssm_ref.shape[0]):
        for tau in range(SSM_CHUNK):
            y_scr[cb, pl.ds(tau, n_chunk, stride=SSM_CHUNK), :] = yssm_ref[cb, :, tau * LANES:(tau + 1) * LANES]
    g = _gelu_tanh(jnp.concatenate([y_scr[cb] for cb in range(yssm_ref.shape[0])], axis=1))
    glu = g * _sigmoid(_bdot(g, wglu_ref[...]) + bglu_ref[...])
    y_attn = jnp.dot(attn_ref[...], wua_ref[...], preferred_element_type=F32)
    y_ssm = _bdot(glu, wus_ref[...])
    mixed = _sigmoid(ga_ref[...].astype(F32)) * y_attn + _sigmoid(gs_ref[...].astype(F32)) * y_ssm
    gate1 = mod_ref[0, 2:3, :]
    x1 = x_ref[...] + gate1 * _bdot(mixed, wout_ref[...])
    h2 = _modulated_norm(x1, ln_ref[...], mod_ref[0, 3:4, :], mod_ref[0, 4:5, :])
    _store_rows_as_tiles(h2_ref, h2)
    h2b = h2.astype(BF16)
    score_ref[...] = _sigmoid(_bdot_nt(wrt_ref[...], h2b))
    gu = jnp.dot(h2b, wsgu_ref[...], preferred_element_type=F32)
    sd = wsd_ref.shape[0]
    shared = _bdot(_silu(gu[:, :sd]) * gu[:, sd:], wsd_ref[...])
    base_ref[...] = x1 + mod_ref[0, 5:6, :] * shared


def _mix(xf, attn, yssm, ga, gs, mod3, ln2_g, w, seq):
    n, d = xf.shape
    tm = TOKEN_TILE
    tiles_per_seq = seq // tm
    row = lambda i: (i, 0)
    const = lambda i: (0, 0)
    nt = d // LANES
    weights = [w["glu"], w["b_glu"], w["up_attn"], w["up_ssm"], w["out"], w["router_t"], w["sh_gu"], w["sh_down"]]
    return pl.pallas_call(
        _mix_kernel,
        out_shape=(jax.ShapeDtypeStruct((n * nt, LANES), F32),
                   jax.ShapeDtypeStruct((n, d), F32),
                   jax.ShapeDtypeStruct((N_EXPERTS, n), F32)),
        grid=(n // tm,),
        in_specs=[pl.BlockSpec((tm, d), row),
                  pl.BlockSpec((tm, attn.shape[1]), row),
                  pl.BlockSpec((yssm.shape[0], tm // SSM_CHUNK, yssm.shape[2]), lambda i: (0, i, 0)),
                  pl.BlockSpec((tm, d), row),
                  pl.BlockSpec((tm, d), row),
                  pl.BlockSpec((1, 6, d), lambda i: (i // tiles_per_seq, 0, 0)),
                  pl.BlockSpec((1, d), const)] + [pl.BlockSpec(a.shape, const) for a in weights],
        out_specs=(pl.BlockSpec((tm * nt, LANES), row), pl.BlockSpec((tm, d), row),
                   pl.BlockSpec((N_EXPERTS, tm), lambda i: (0, i))),
        scratch_shapes=[pltpu.VMEM((yssm.shape[0], tm, LANES), F32)],
        compiler_params=_cparams(1),
        name="mix",
    )(xf, attn, yssm, ga, gs, mod3, ln2_g.reshape(1, d), *weights)


def _route_kernel(score_ref, bias_ref, tri_ref, idx_ref, w_ref, rank_ref, cnt_ref, carry_ref):
    @pl.when(pl.program_id(0) == 0)
    def _():
        carry_ref[...] = jnp.zeros_like(carry_ref)

    scores = score_ref[...]
    ne, tn = scores.shape
    biased = scores + bias_ref[...]
    gsz = GROUP_SIZE
    sub = lax.broadcasted_iota(jnp.int32, (gsz, tn), 0)
    group_score = []
    for g in range(N_EXPERT_GROUPS):
        sg = biased[g * gsz:(g + 1) * gsz, :]
        m1 = jnp.max(sg, axis=0, keepdims=True)
        first = jnp.min(jnp.where(sg == m1, sub, gsz), axis=0, keepdims=True)
        m2 = jnp.max(jnp.where(sub == first, -jnp.inf, sg), axis=0, keepdims=True)
        group_score.append(m1 + m2)
    group_rows = []
    for g in range(N_EXPERT_GROUPS):
        beaten = jnp.zeros((1, tn), jnp.int32)
        for o in range(N_EXPERT_GROUPS):
            if o == g:
                continue
            wins = (group_score[o] > group_score[g])
            if o < g:
                wins = wins | (group_score[o] == group_score[g])
            beaten = beaten + jnp.where(wins, 1, 0)
        group_rows.append(jnp.broadcast_to(beaten < TOPK_GROUPS, (gsz, tn)))
    allowed = jnp.concatenate(group_rows, axis=0)
    cur = jnp.where(allowed, biased, MASK_VALUE)
    eio = lax.broadcasted_iota(jnp.int32, (ne, tn), 0)
    idx_rows, w_rows, hits = [], [], []
    for _ in range(TOP_K):
        vmax = jnp.max(cur, axis=0, keepdims=True)
        eidx = jnp.min(jnp.where(cur == vmax, eio, ne), axis=0, keepdims=True)
        hit = eio == eidx
        w_rows.append(jnp.sum(jnp.where(hit, scores, 0.0), axis=0, keepdims=True))
        idx_rows.append(eidx)
        hits.append(hit)
        cur = jnp.where(hit, -jnp.inf, cur)
    wts = jnp.concatenate(w_rows, axis=0)
    idx_ref[...] = jnp.concatenate(idx_rows, axis=0)
    w_ref[...] = wts / jnp.sum(wts, axis=0, keepdims=True) * ROUTED_SCALE

    onehot = jnp.zeros((ne, tn), F32)
    for hit in hits:
        onehot = onehot + jnp.where(hit, 1.0, 0.0)
    earlier = carry_ref[...] + _bdot(onehot, tri_ref[...])
    rank_rows = [jnp.sum(jnp.where(hit, earlier, 0.0), axis=0, keepdims=True) for hit in hits]
    rank_ref[...] = jnp.concatenate(rank_rows, axis=0).astype(jnp.int32)
    carry_ref[...] = carry_ref[...] + jnp.sum(onehot, axis=1, keepdims=True)
    cnt_ref[...] = carry_ref[...]


def _route(scores_t, router_bias):
    ne, n = scores_t.shape
    tn = ROUTE_TILE
    tri = jnp.asarray(np.arange(tn)[:, None] < np.arange(tn)[None, :], BF16)
    tok = lambda i: (0, i)
    const = lambda i: (0, 0)
    return pl.pallas_call(
        _route_kernel,
        out_shape=(jax.ShapeDtypeStruct((TOP_K, n), jnp.int32), jax.ShapeDtypeStruct((TOP_K, n), F32),
                   jax.ShapeDtypeStruct((TOP_K, n), jnp.int32), jax.ShapeDtypeStruct((ne, 1), F32)),
        grid=(n // tn,),
        in_specs=[pl.BlockSpec((ne, tn), tok), pl.BlockSpec((ne, 1), const), pl.BlockSpec((tn, tn), const)],
        out_specs=(pl.BlockSpec((TOP_K, tn), tok), pl.BlockSpec((TOP_K, tn), tok),
                   pl.BlockSpec((TOP_K, tn), tok), pl.BlockSpec((ne, 1), const)),
        scratch_shapes=[pltpu.VMEM((ne, 1), F32)],
        compiler_params=_cparams(1),
        name="route",
    )(scores_t, router_bias.reshape(ne, 1).astype(F32), tri)


def _dest_kernel(idx_ref, rank_ref, poffs_ref, dest_ref):
    idx = idx_ref[...]
    ne = poffs_ref.shape[0]
    eio = lax.broadcasted_iota(jnp.int32, (ne, idx.shape[1]), 0)
    poffs = poffs_ref[...]
    rows = [jnp.sum(jnp.where(eio == idx[kk:kk + 1, :], poffs, 0.0), axis=0, keepdims=True)
            for kk in range(idx.shape[0])]
    dest_ref[...] = rank_ref[...] + jnp.concatenate(rows, axis=0).astype(jnp.int32)


def _dest_slots(idx_t, rank_t, poffs):
    k, n = idx_t.shape
    tn = ROUTE_TILE
    ne = poffs.shape[0]
    tok = lambda i: (0, i)
    return pl.pallas_call(
        _dest_kernel,
        out_shape=jax.ShapeDtypeStruct((k, n), jnp.int32),
        grid=(n // tn,),
        in_specs=[pl.BlockSpec((k, tn), tok), pl.BlockSpec((k, tn), tok), pl.BlockSpec((ne, 1), lambda i: (0, 0))],
        out_specs=pl.BlockSpec((k, tn), tok),
        compiler_params=_cparams(1),
        name="dest_slots",
    )(idx_t, rank_t, poffs.astype(F32).reshape(ne, 1))


def _dispatch_tables(idx_t, counts, n_blocks):
    k, n = idx_t.shape
    n_assign = k * n
    rb = ROW_BLOCK
    ne = counts.shape[0]
    counts = counts.reshape(ne).astype(jnp.int32)
    offs = jnp.cumsum(counts) - counts
    nblk = (counts + rb - 1) // rb
    bend = jnp.cumsum(nblk)
    bstart = bend - nblk
    blocks = jnp.arange(n_blocks, dtype=jnp.int32)
    blk_e = jnp.minimum(jnp.sum((bend[None, :] <= blocks[:, None]).astype(jnp.int32), axis=1), ne - 1)
    mine = blk_e[:, None] == jnp.arange(ne, dtype=jnp.int32)[None, :]
    src_start = jnp.sum(jnp.where(mine, (offs - bstart * rb)[None, :], 0), axis=1) + blocks * rb
    tok = lax.broadcasted_iota(jnp.int32, (k, n), 1)
    kk = lax.broadcasted_iota(jnp.int32, (k, n), 0)
    assert ne * n_assign < 2 ** 31
    keys = jnp.sort((idx_t * n_assign + tok * k + kk).reshape(n_assign))
    tok_sorted = (keys % n_assign) // k
    return tok_sorted, blk_e, src_start, bend[ne - 1:ne], bstart * rb


def _ffn_kernel(tok_ref, blk_e_ref, start_ref, used_ref, h2_hbm, wg_ref, wu_ref, wd_ref, y_ref, xbuf, sems):
    b = pl.program_id(0)
    rb = ROW_BLOCK
    d = wg_ref.shape[1]
    nt = d // LANES
    n_assign = tok_ref.shape[0]
    n_used = used_ref[0]
    n_blocks = pl.num_programs(0)

    def start_rows(blk, s, inline):
        start = start_ref[blk]

        def one(r):
            tok = tok_ref[jnp.minimum(start + r, n_assign - 1)]
            pltpu.make_async_copy(h2_hbm.at[pl.ds(pl.multiple_of(tok * nt, nt), nt)],
                                  xbuf.at[s, pl.ds(pl.multiple_of(r * nt, nt), nt)], sems.at[s]).start()

        if inline:
            for r in range(rb):
                one(r)
        else:
            lax.fori_loop(0, rb, lambda r, c: (one(r), c)[1], 0, unroll=ISSUE_UNROLL)

    def wait_rows(s):
        pltpu.make_async_copy(h2_hbm.at[pl.ds(0, rb * nt)], xbuf.at[s], sems.at[s]).wait()

    nbuf = xbuf.shape[0]
    ahead = nbuf - 1

    @pl.when(b == 0)
    def _():
        for a in range(ahead):
            start_rows(jnp.minimum(a, n_blocks - 1), a, False)

    for s in range(nbuf):
        @pl.when((b < n_used) & (b % nbuf == s))
        def _(s=s):
            wait_rows(s)
            xb = _load_rows_from_tiles(xbuf.at[s], rb, d).astype(BF16)
            start_rows(jnp.minimum(b + ahead, n_blocks - 1), (s + ahead) % nbuf, True)
            hg = jnp.dot(xb, wg_ref[0].astype(BF16), preferred_element_type=F32)
            hu = jnp.dot(xb, wu_ref[0].astype(BF16), preferred_element_type=F32)
            _store_rows_as_tiles(y_ref, _bdot(_silu(hg) * hu, wd_ref[0]))

    @pl.when(b == n_used - 1)
    def _():
        for a in range(1, nbuf):
            wait_rows((b + a) % nbuf)

    @pl.when(b >= n_used)
    def _():
        y_ref[...] = jnp.zeros_like(y_ref)


def _expert_ffn(h2_tiles, tok_sorted, blk_e, src_start, n_used, w_gate, w_up, w_down):
    n_blocks = blk_e.shape[0]
    rb = ROW_BLOCK
    _, d, ed = w_gate.shape
    nt = d // LANES
    expert = lambda i, tok, be, st, nu: (be[i], 0, 0)
    grid_spec = pltpu.PrefetchScalarGridSpec(
        num_scalar_prefetch=4,
        grid=(n_blocks,),
        in_specs=[pl.BlockSpec(memory_space=pl.ANY),
                  pl.BlockSpec((1, d, ed), expert),
                  pl.BlockSpec((1, d, ed), expert),
                  pl.BlockSpec((1, ed, d), expert)],
        out_specs=pl.BlockSpec((rb * nt, LANES), lambda i, tok, be, st, nu: (i, 0)),
        scratch_shapes=[pltpu.VMEM((FFN_ROW_BUFFERS, rb * nt, LANES), F32),
                        pltpu.SemaphoreType.DMA((FFN_ROW_BUFFERS,))],
    )
    return pl.pallas_call(
        _ffn_kernel,
        out_shape=jax.ShapeDtypeStruct((n_blocks * rb * nt, LANES), F32),
        grid_spec=grid_spec,
        compiler_params=_cparams(1),
        name="expert_ffn",
    )(tok_sorted, blk_e, src_start, n_used, h2_tiles, w_gate, w_up, w_down)


def _combine_kernel(dest_ref, y_hbm, base_ref, wt_ref, mod_ref, o_ref, buf, sems):
    i = pl.program_id(0)
    n_steps = pl.num_programs(0)
    tm, d = base_ref.shape
    k = wt_ref.shape[1]
    nt = d // LANES
    n_rows = k * tm
    slot = i % 2

    def issue(step, s):
        first = step * n_rows

        def body(r, _):
            src = dest_ref[first + r]
            pltpu.make_async_copy(y_hbm.at[pl.ds(pl.multiple_of(src * nt, nt), nt)],
                                  buf.at[s, pl.ds(pl.multiple_of(r * nt, nt), nt)], sems.at[s]).start()
            return 0

        lax.fori_loop(0, n_rows, body, 0, unroll=ISSUE_UNROLL)

    @pl.when(i == 0)
    def _():
        issue(0, 0)

    @pl.when(i + 1 < n_steps)
    def _():
        issue(i + 1, 1 - slot)

    pltpu.make_async_copy(y_hbm.at[pl.ds(0, n_rows * nt)], buf.at[slot], sems.at[slot]).wait()
    rows = buf.at[slot]
    wts = wt_ref[...]
    gate2 = mod_ref[0, 5:6, :]
    for c in range(nt):
        cols = slice(c * LANES, (c + 1) * LANES)
        routed = jnp.zeros((tm, LANES), F32)
        for kk in range(k):
            routed = routed + wts[:, kk:kk + 1] * rows[pl.ds(kk * tm * nt + c, tm, stride=nt), :]
        o_ref[:, cols] = base_ref[:, cols] + gate2[:, cols] * routed


def _combine(y_tiles, dest_t, w_t, base, mod3, seq):
    n, d = base.shape
    tm = COMBINE_TILE
    k = dest_t.shape[0]
    nt = d // LANES
    tiles_per_seq = seq // tm
    dest_tiles = dest_t.reshape(k, n // tm, tm).transpose(1, 0, 2).reshape(n * k)
    grid_spec = pltpu.PrefetchScalarGridSpec(
        num_scalar_prefetch=1,
        grid=(n // tm,),
        in_specs=[pl.BlockSpec(memory_space=pl.ANY),
                  pl.BlockSpec((tm, d), lambda i, dst: (i, 0)),
                  pl.BlockSpec((tm, k), lambda i, dst: (i, 0)),
                  pl.BlockSpec((1, 6, d), lambda i, dst: (i // tiles_per_seq, 0, 0))],
        out_specs=pl.BlockSpec((tm, d), lambda i, dst: (i, 0)),
        scratch_shapes=[pltpu.VMEM((2, k * tm * nt, LANES), F32), pltpu.SemaphoreType.DMA((2,))],
    )
    return pl.pallas_call(
        _combine_kernel,
        out_shape=jax.ShapeDtypeStruct((n, d), F32),
        grid_spec=grid_spec,
        compiler_params=_cparams(1),
        name="combine",
    )(dest_tiles, y_tiles, base, w_t.T, mod3)


def _hybrid_layer(x, cond, rel_bias, w_ada, b_ada, ln1_g, w_in, q_norm_g, k_norm_g,
                  ssm_lambda_re, ssm_lambda_im, ssm_log_dt, ssm_b_re, ssm_b_im, ssm_c_re, ssm_c_im,
                  ssm_d, ssm_w_glu, ssm_b_glu, w_up_attn, w_up_ssm, w_out, ln2_g,
                  w_router, router_bias, w_exp_gate, w_exp_up, w_exp_down,
                  w_sh_gate, w_sh_up, w_sh_down):
    bsz, seq, d = x.shape
    n = bsz * seq
    xf = x.reshape(n, d)
    mod3 = _adaln(cond, w_ada, b_ada).reshape(bsz, 6, d)

    q_gain = jnp.tile(q_norm_g.astype(F32), ATTN_HEADS).reshape(1, ATTN_WIDTH)
    k_gain = jnp.tile(k_norm_g.astype(F32), ATTN_HEADS).reshape(1, ATTN_WIDTH)
    q, k, v, u, ga, gs, kmean = _inproj(xf, mod3, ln1_g, w_in.astype(BF16), q_gain, k_gain, seq)

    attn = _moba_attention(q, k, v, kmean, rel_bias, bsz, seq)
    ops = _s5_operators(ssm_lambda_re, ssm_lambda_im, ssm_log_dt, ssm_b_re, ssm_b_im,
                        ssm_c_re, ssm_c_im, ssm_d, seq // SSM_CHUNK)
    yssm = _s5_scan(u, ops, bsz, seq)

    weights = {
        "glu": ssm_w_glu.astype(BF16), "b_glu": ssm_b_glu.astype(F32).reshape(1, -1),
        "up_attn": w_up_attn.astype(BF16), "up_ssm": w_up_ssm.astype(BF16), "out": w_out.astype(BF16),
        "router_t": w_router.T.astype(BF16),
        "sh_gu": jnp.concatenate([w_sh_gate, w_sh_up], axis=1).astype(BF16),
        "sh_down": w_sh_down.astype(BF16),
    }
    h2, base, scores_t = _mix(xf, attn, yssm, ga, gs, mod3, ln2_g, weights, seq)

    idx_t, w_t, rank_t, counts = _route(scores_t, router_bias)
    n_blocks = -(-(n * TOP_K) // ROW_BLOCK) + N_EXPERTS
    tok_sorted, blk_e, src_start, n_used, poffs = _dispatch_tables(idx_t, counts, n_blocks)
    dest_t = _dest_slots(idx_t, rank_t, poffs)
    y_tiles = _expert_ffn(h2, tok_sorted, blk_e, src_start, n_used, w_exp_gate, w_exp_up, w_exp_down)
    out = _combine(y_tiles, dest_t, w_t, base, mod3, seq)
    return out.reshape(bsz, seq, d)


def kernel(x, c, rel_bias, w_ada, b_ada, ln1_g, w_in, q_norm_g, k_norm_g, ssm_lambda_re, ssm_lambda_im, ssm_log_dt, ssm_b_re, ssm_b_im, ssm_c_re, ssm_c_im, ssm_d, ssm_w_glu, ssm_b_glu, w_up_attn, w_up_ssm, w_out, ln2_g, w_router, router_bias, w_exp_gate, w_exp_up, w_exp_down, w_sh_gate, w_sh_up, w_sh_down):
    for l in range(w_ada.shape[0]):
        x = _hybrid_layer(x, c, rel_bias, w_ada[l], b_ada[l], ln1_g[l], w_in[l], q_norm_g[l], k_norm_g[l],
                          ssm_lambda_re[l], ssm_lambda_im[l], ssm_log_dt[l], ssm_b_re[l], ssm_b_im[l],
                          ssm_c_re[l], ssm_c_im[l], ssm_d[l], ssm_w_glu[l], ssm_b_glu[l],
                          w_up_attn[l], w_up_ssm[l], w_out[l], ln2_g[l], w_router[l], router_bias[l],
                          w_exp_gate[l], w_exp_up[l], w_exp_down[l], w_sh_gate[l], w_sh_up[l], w_sh_down[l])
    return x
```

```python
import functools
import math

import numpy as np
import jax
import jax.numpy as jnp
from jax import lax
from jax.experimental import pallas as pl
from jax.experimental.pallas import tpu as pltpu

F32 = jnp.float32
BF16 = jnp.bfloat16

ATTN_HEADS = 8
HEAD_DIM = 64
ATTN_WIDTH = ATTN_HEADS * HEAD_DIM
MOBA_BLOCK = 256
MOBA_TOPK = 3
NUM_BUCKETS = 32
MAX_DISTANCE = 128
SSM_WIDTH = 512
SSM_GROUP = 16
SSM_GROUPS = SSM_WIDTH // SSM_GROUP
SSM_STATE = 64
N_EXPERTS = 256
TOP_K = 8
N_EXPERT_GROUPS = 8
TOPK_GROUPS = 4
GROUP_SIZE = N_EXPERTS // N_EXPERT_GROUPS
EXPERT_DIM = 256
ROUTED_SCALE = 2.5
EPS = 1e-6
MASK_VALUE = -1e30

LANES = 128
HEADS_PER_STEP = LANES // HEAD_DIM
SSM_CHUNK = 16
SUBLANES = 8
ROW_BLOCK = 256
TOKEN_TILE = 256
ROUTE_TILE = 256
COMBINE_TILE = 128
ISSUE_UNROLL = 8
FFN_ROW_BUFFERS = 3
VMEM_LIMIT = 56 * 1024 * 1024


def _store_rows_as_tiles(ref, val):
    rows, d = val.shape
    nt = d // LANES
    for c in range(nt):
        ref[pl.ds(c, rows, stride=nt), :] = val[:, c * LANES:(c + 1) * LANES]


def _load_rows_from_tiles(ref, rows, d):
    nt = d // LANES
    return jnp.concatenate([ref[pl.ds(c, rows, stride=nt), :] for c in range(nt)], axis=1)


def _cparams(n_axes, vmem=VMEM_LIMIT):
    return pltpu.CompilerParams(dimension_semantics=("arbitrary",) * n_axes, vmem_limit_bytes=vmem)


def _sigmoid(x):
    return 1.0 / (1.0 + jnp.exp(-x))


def _silu(x):
    return x * _sigmoid(x)


def _bdot(a, b):
    return jnp.dot(a.astype(BF16), b.astype(BF16), preferred_element_type=F32)


def _bdot_nt(a, b):
    return lax.dot_general(a.astype(BF16), b.astype(BF16), (((1,), (1,)), ((), ())),
                           preferred_element_type=F32)


def _adaln_kernel(c_ref, w_ref, b_ref, o_ref):
    o_ref[...] = _bdot(_silu(c_ref[...]), w_ref[...]) + b_ref[...]


def _adaln(c, w_ada, b_ada):
    bsz, d = c.shape
    n_out = w_ada.shape[1]
    return pl.pallas_call(
        _adaln_kernel,
        out_shape=jax.ShapeDtypeStruct((bsz, n_out), F32),
        grid=(n_out // d,),
        in_specs=[pl.BlockSpec((bsz, d), lambda j: (0, 0)),
                  pl.BlockSpec((d, d), lambda j: (0, j)),
                  pl.BlockSpec((1, d), lambda j: (0, j))],
        out_specs=pl.BlockSpec((bsz, d), lambda j: (0, j)),
        compiler_params=_cparams(1),
        name="adaln",
    )(c, w_ada, b_ada.reshape(1, n_out))


def _modulated_norm(x, gain, shift, scale):
    y = x * lax.rsqrt(jnp.mean(x * x, axis=-1, keepdims=True) + EPS) * gain
    return y * (1.0 + scale) + shift


def _head_norm(t, seg, gain):
    ms = _bdot(t * t, seg)
    return t * lax.rsqrt(ms + EPS) * gain


def _inproj_kernel(x_ref, mod_ref, ln_ref, w_ref, seg_ref, qg_ref, kg_ref,
                   q_ref, k_ref, v_ref, u_ref, ga_ref, gs_ref, km_ref, u_scr):
    aw, sw, d = ATTN_WIDTH, SSM_WIDTH, x_ref.shape[1]
    h = _modulated_norm(x_ref[...], ln_ref[...], mod_ref[0, 0:1, :], mod_ref[0, 1:2, :]).astype(BF16)
    seg = seg_ref[...]
    q = jnp.dot(h, w_ref[:, 0:aw], preferred_element_type=F32)
    q_ref[...] = _head_norm(q, seg, qg_ref[...])
    k = jnp.dot(h, w_ref[:, aw:2 * aw], preferred_element_type=F32)
    kn = _head_norm(k, seg, kg_ref[...])
    k_ref[...] = kn.astype(BF16)
    km_ref[0] = jnp.mean(kn, axis=0, keepdims=True)
    v_ref[...] = jnp.dot(h, w_ref[:, 2 * aw:3 * aw], preferred_element_type=F32).astype(BF16)
    o = 3 * aw
    u = jnp.dot(h, w_ref[:, o:o + sw], preferred_element_type=F32)
    n_chunk = u_scr.shape[1] // SSM_CHUNK
    for cb in range(sw // LANES):
        u_scr[cb] = u[:, cb * LANES:(cb + 1) * LANES]
        for sg in range(SSM_CHUNK):
            u_ref[cb, :, sg * LANES:(sg + 1) * LANES] = (
                u_scr[cb, pl.ds(sg, n_chunk, stride=SSM_CHUNK), :].astype(BF16))
    o += sw
    ga_ref[...] = jnp.dot(h, w_ref[:, o:o + d], preferred_element_type=F32).astype(BF16)
    o += d
    gs_ref[...] = jnp.dot(h, w_ref[:, o:o + d], preferred_element_type=F32).astype(BF16)


def _inproj(xf, mod3, ln1_g, w_in_b, q_gain, k_gain, seq):
    n, d = xf.shape
    tm = MOBA_BLOCK
    tiles_per_seq = seq // tm
    aw, sw = ATTN_WIDTH, SSM_WIDTH
    head_of_lane = np.arange(aw) // HEAD_DIM
    seg = jnp.asarray((head_of_lane[:, None] == head_of_lane[None, :]) / HEAD_DIM, BF16)
    row = lambda i: (i, 0)
    const = lambda i: (0, 0)
    return pl.pallas_call(
        _inproj_kernel,
        out_shape=(jax.ShapeDtypeStruct((n, aw), F32),
                   jax.ShapeDtypeStruct((n, aw), BF16),
                   jax.ShapeDtypeStruct((n, aw), BF16),
                   jax.ShapeDtypeStruct((sw // LANES, n // SSM_CHUNK, SSM_CHUNK * LANES), BF16),
                   jax.ShapeDtypeStruct((n, d), BF16),
                   jax.ShapeDtypeStruct((n, d), BF16),
                   jax.ShapeDtypeStruct((n // tm, 1, aw), F32)),
        grid=(n // tm,),
        in_specs=[pl.BlockSpec((tm, d), row),
                  pl.BlockSpec((1, 6, d), lambda i: (i // tiles_per_seq, 0, 0)),
                  pl.BlockSpec((1, d), const),
                  pl.BlockSpec(w_in_b.shape, const),
                  pl.BlockSpec((aw, aw), const),
                  pl.BlockSpec((1, aw), const),
                  pl.BlockSpec((1, aw), const)],
        out_specs=(pl.BlockSpec((tm, aw), row), pl.BlockSpec((tm, aw), row), pl.BlockSpec((tm, aw), row),
                   pl.BlockSpec((sw // LANES, tm // SSM_CHUNK, SSM_CHUNK * LANES), lambda i: (0, i, 0)),
                   pl.BlockSpec((tm, d), row), pl.BlockSpec((tm, d), row),
                   pl.BlockSpec((1, 1, aw), lambda i: (i, 0, 0))),
        scratch_shapes=[pltpu.VMEM((sw // LANES, tm, LANES), F32)],
        compiler_params=_cparams(1),
        name="inproj",
    )(xf, mod3, ln1_g.reshape(1, d), w_in_b, seg, q_gain, k_gain)


def _t5_bucket(rel):
    n = jnp.maximum(rel, 0)
    max_exact = NUM_BUCKETS // 2
    nf = jnp.maximum(n, 1).astype(F32)
    large = max_exact + (jnp.log(nf / max_exact) / math.log(MAX_DISTANCE / max_exact)
                         * (NUM_BUCKETS - max_exact)).astype(jnp.int32)
    large = jnp.minimum(large, NUM_BUCKETS - 1)
    return jnp.where(n < max_exact, n, large)


def _bias_tables(rel_bias):
    blk = MOBA_BLOCK
    assert blk + 1 >= MAX_DISTANCE
    rel = jnp.arange(blk)[None, :] - jnp.arange(blk)[:, None]
    table = rel_bias.astype(F32)
    table = table - table[NUM_BUCKETS - 1][None, :]

    def lookup(r):
        onehot = jax.nn.one_hot(_t5_bucket(r), NUM_BUCKETS, dtype=F32)
        return jnp.einsum('kqn,nh->hkq', onehot, table, precision=lax.Precision.HIGHEST)

    return lookup(rel), lookup(rel + blk)


def _select_blocks(gate_t, n_past):
    nb, tq = gate_t.shape
    blk = lax.broadcasted_iota(jnp.int32, (nb, tq), 0)
    beaten = jnp.zeros((nb, tq), jnp.int32)
    for m in range(nb):
        gm = gate_t[m:m + 1, :]
        wins = (gm > gate_t) | ((gm == gate_t) & (m < blk))
        beaten = beaten + jnp.where(wins & (m < n_past), 1, 0)
    return jnp.where((blk < n_past) & (beaten < MOBA_TOPK), 1.0, 0.0)


def _attn_kernel(q_ref, k_ref, vt_ref, km_ref, bias_ref, o_ref, sel_ref, s_ref):
    qi = pl.program_id(2)
    tq = q_ref.shape[0]
    blk = MOBA_BLOCK
    hd = HEAD_DIM
    heads = range(HEADS_PER_STEP)
    q = q_ref[...]
    lane = lax.broadcasted_iota(jnp.int32, (tq, LANES), 1)
    kpos = lax.broadcasted_iota(jnp.int32, (blk, tq), 0)
    qpos = lax.broadcasted_iota(jnp.int32, (blk, tq), 1)
    scale = hd ** -0.5
    n_far = jnp.maximum(qi - 1, 0)
    n_pairs = (n_far + 1) // 2
    jp = jnp.maximum(qi - 1, 0)
    k_own = k_ref[pl.ds(pl.multiple_of(qi * blk, blk), blk), :]
    k_prev = k_ref[pl.ds(pl.multiple_of(jp * blk, blk), blk), :]

    qbs = []
    for h in heads:
        in_head = (lane >= h * hd) & (lane < (h + 1) * hd)
        qm = jnp.where(in_head, q, 0.0)
        gate_t = lax.dot_general(km_ref[0], qm, (((1,), (1,)), ((), ())),
                                 precision=lax.Precision.HIGHEST, preferred_element_type=F32)
        sel_ref[h] = _select_blocks(gate_t, qi)
        qbs.append((qm * scale).astype(BF16))

    def pair_scores(j):
        kb = k_ref[pl.ds(pl.multiple_of(j * blk, blk), 2 * blk), :]
        return [_bdot_nt(kb, qbs[h]) for h in heads]

    def attend(h, p, blocks):
        acc = None
        for i, j in enumerate(blocks):
            part = jnp.dot(vt_ref[0, j, h * hd:(h + 1) * hd, :], p[i * blk:(i + 1) * blk, :],
                           preferred_element_type=F32)
            acc = part if acc is None else acc + part
        return acc

    for h, s in enumerate(pair_scores(0)):
        s_ref[h] = s

    carries = []
    for h in heads:
        s_prev = _bdot_nt(k_prev, qbs[h]) + bias_ref[h, 0:blk, :]
        s_prev = jnp.where(sel_ref[h, pl.ds(jp, 1), :] > 0.5, s_prev, MASK_VALUE)
        s_own = _bdot_nt(k_own, qbs[h]) + bias_ref[h, blk:2 * blk, :]
        s_own = jnp.where(kpos <= qpos, s_own, MASK_VALUE)
        s = jnp.concatenate([s_prev, s_own], axis=0)
        m = jnp.max(s, axis=0, keepdims=True)
        p = jnp.exp(s - m)
        l = jnp.sum(p, axis=0, keepdims=True)
        carries.append((m, l, attend(h, p.astype(BF16), (jp, qi))))

    def far_pair(pi, carries):
        j = 2 * pi
        s_cur = [s_ref[h] for h in heads]
        for h, s in enumerate(pair_scores(2 * jnp.minimum(pi + 1, n_pairs - 1))):
            s_ref[h] = s
        second_is_far = j + 1 < n_far
        out = []
        for h in heads:
            m, l, acc = carries[h]
            c0 = sel_ref[h, pl.ds(j, 1), :] > 0.5
            c1 = (sel_ref[h, pl.ds(j + 1, 1), :] > 0.5) & second_is_far
            chosen = jnp.concatenate([jnp.broadcast_to(c0, (blk, tq)), jnp.broadcast_to(c1, (blk, tq))], axis=0)
            s = jnp.where(chosen, s_cur[h], MASK_VALUE)
            m_new = jnp.maximum(m, jnp.max(s, axis=0, keepdims=True))
            alpha = jnp.exp(m - m_new)
            p = jnp.exp(s - m_new)
            l = alpha * l + jnp.sum(p, axis=0, keepdims=True)
            acc = alpha * acc + attend(h, p.astype(BF16), (j, j + 1))
            out.append((m_new, l, acc))
        return tuple(out)

    carries = lax.fori_loop(0, n_pairs, far_pair, tuple(carries))
    out_t = jnp.concatenate([acc / l for _, l, acc in carries], axis=0)
    o_ref[...] = out_t.T.astype(o_ref.dtype)


def _moba_attention(q, k, v, kmean, rel_bias, bsz, seq):
    n, aw = q.shape
    blk = MOBA_BLOCK
    nb = seq // blk
    assert nb >= 2
    own, prev = _bias_tables(rel_bias)
    bias = jnp.concatenate([prev, own], axis=1)
    hps = HEADS_PER_STEP
    npair = aw // LANES
    vt = v.reshape(bsz, nb, blk, aw).transpose(0, 1, 3, 2)
    return pl.pallas_call(
        _attn_kernel,
        out_shape=jax.ShapeDtypeStruct((n, aw), BF16),
        grid=(bsz, npair, nb),
        in_specs=[pl.BlockSpec((blk, LANES), lambda b, hp, qi: (b * nb + qi, hp)),
                  pl.BlockSpec((seq, LANES), lambda b, hp, qi: (b, hp)),
                  pl.BlockSpec((1, nb, LANES, blk), lambda b, hp, qi: (b, 0, hp, 0)),
                  pl.BlockSpec((1, nb, LANES), lambda b, hp, qi: (b, 0, hp)),
                  pl.BlockSpec((hps, 2 * blk, blk), lambda b, hp, qi: (hp, 0, 0))],
        out_specs=pl.BlockSpec((blk, LANES), lambda b, hp, qi: (b * nb + qi, hp)),
        scratch_shapes=[pltpu.VMEM((hps, nb, blk), F32), pltpu.VMEM((hps, 2 * blk, blk), F32)],
        compiler_params=_cparams(3),
        name="moba_attention",
    )(q, k, vt, kmean.reshape(bsz, nb, aw), bias)


def _s5_operators(lambda_re, lambda_im, log_dt, b_re, b_im, c_re, c_im, d_skip, n_chunks):
    hi = lax.Precision.HIGHEST
    L, G, P, C = SSM_CHUNK, SSM_GROUPS, SSM_STATE, SSM_GROUP
    lam_re = jnp.minimum(lambda_re.astype(F32), -1e-4)
    lam_im = lambda_im.astype(F32)
    dt = jnp.exp(log_dt.astype(F32))[:, None]
    z_re, z_im = lam_re * dt, lam_im * dt

    def a_pow(nvec):
        nv = jnp.asarray(nvec, F32)[:, None, None]
        mag = jnp.exp(nv * z_re)
        return mag * jnp.cos(nv * z_im), mag * jnp.sin(nv * z_im)

    a_re, a_im = a_pow([1.0])
    a_re, a_im = a_re[0], a_im[0]
    den = lam_re * lam_re + lam_im * lam_im
    nr = a_re - 1.0
    coef_re = (nr * lam_re + a_im * lam_im) / den
    coef_im = (a_im * lam_re - nr * lam_im) / den
    br, bi = b_re.astype(F32), b_im.astype(F32)
    bbar_re = coef_re[..., None] * br - coef_im[..., None] * bi
    bbar_im = coef_re[..., None] * bi + coef_im[..., None] * br
    cr, ci = c_re.astype(F32), c_im.astype(F32)

    pw_re, pw_im = a_pow(np.arange(L + 1))
    cb_re = cr[None] * pw_re[:, :, None, :] - ci[None] * pw_im[:, :, None, :]
    cb_im = cr[None] * pw_im[:, :, None, :] + ci[None] * pw_re[:, :, None, :]
    kern = (jnp.einsum('jgop,gpi->gijo', cb_re[:L], bbar_re, precision=hi)
            - jnp.einsum('jgop,gpi->gijo', cb_im[:L], bbar_im, precision=hi)).reshape(G, C, L * C)
    t_op = jnp.stack([jnp.pad(kern[:, :, :(L - s) * C], ((0, 0), (0, 0), (s * C, 0))) for s in range(L)], axis=1)
    d_g = d_skip.astype(F32).reshape(G, 1, C, 1)
    on_diag = (lax.broadcasted_iota(jnp.int32, (1, L, C, L * C), 3)
               == lax.broadcasted_iota(jnp.int32, (1, L, C, L * C), 1) * C
               + lax.broadcasted_iota(jnp.int32, (1, L, C, L * C), 2))
    t_op = (t_op + jnp.where(on_diag, d_g, 0.0)).reshape(G, L * C, L * C)

    rp_re, rp_im = jnp.flip(pw_re[:L], 0), jnp.flip(pw_im[:L], 0)
    p_re = rp_re[..., None] * bbar_re[None] - rp_im[..., None] * bbar_im[None]
    p_im = rp_re[..., None] * bbar_im[None] + rp_im[..., None] * bbar_re[None]
    p_op = jnp.concatenate([p_re, p_im], axis=2)
    p_op = p_op.transpose(1, 0, 3, 2).reshape(G, L * C, 2 * P)

    q_re = cb_re[1:].transpose(1, 3, 0, 2)
    q_im = -cb_im[1:].transpose(1, 3, 0, 2)
    q_op = jnp.concatenate([q_re, q_im], axis=1).reshape(G, 2 * P, L * C)

    n_steps = max(1, int(math.ceil(math.log2(n_chunks))))
    dk_re, dk_im = a_pow([float(L * 2 ** k) for k in range(n_steps)])
    GB = LANES // C
    NB = G // GB
    lc = np.arange(L * C)
    wide = np.arange(L * LANES)
    expand_tc = jnp.asarray((lc[:, None] // C == wide[None, :] // LANES) & (lc[:, None] % C == wide[None, :] % C), BF16)
    st = np.arange(2 * P)
    wide_st = np.arange(2 * GB * P)
    expand_st = jnp.asarray((st[:, None] // P == wide_st[None, :] // (GB * P))
                            & (st[:, None] % P == wide_st[None, :] % P), BF16)
    g_of_wide = (jnp.arange(L * LANES) // C) % GB
    g_of_state = (jnp.arange(2 * GB * P) // P) % GB

    def widen(rows, expand, g_row, g_col):
        full = jnp.einsum('brk,kc->brc', rows.astype(BF16), expand, preferred_element_type=F32)
        return jnp.where(g_row[:, None] == g_col[None, :], full, 0.0).astype(BF16)

    t_rows = t_op.reshape(NB, GB, L, C, L * C).transpose(0, 2, 1, 3, 4).reshape(NB, L * LANES, L * C)
    p_rows = p_op.reshape(NB, GB, L, C, 2 * P).transpose(0, 2, 1, 3, 4).reshape(NB, L * LANES, 2 * P)
    q_rows = q_op.reshape(NB, GB, 2, P, L * C).transpose(0, 2, 1, 3, 4).reshape(NB, 2 * GB * P, L * C)
    t_big = widen(t_rows, expand_tc, g_of_wide, g_of_wide)
    p_big = widen(p_rows, expand_st, g_of_wide, g_of_state)
    q_big = widen(q_rows, expand_tc, g_of_state, g_of_wide)
    dk_re = dk_re.reshape(-1, NB, GB * P)
    dk_im = dk_im.reshape(-1, NB, GB * P)
    a1 = jnp.concatenate([dk_re, dk_re], axis=-1).transpose(1, 0, 2)
    a2 = jnp.concatenate([-dk_im, dk_im], axis=-1).transpose(1, 0, 2)
    return t_big.astype(BF16), p_big.astype(BF16), q_big.astype(BF16), a1, a2


def _s5_kernel(x_ref, t_ref, p_ref, q_ref, a1_ref, a2_ref, y_ref):
    x = x_ref[0]
    s = jnp.dot(x, p_ref[0], preferred_element_type=F32)
    n_chunks, width = s.shape
    chunk = lax.broadcasted_iota(jnp.int32, (n_chunks, width), 0)
    a1 = a1_ref[0]
    a2 = a2_ref[0]
    h = jnp.where(chunk >= 1, pltpu.roll(s, 1, axis=0), 0.0)
    for kk in range(a1.shape[0]):
        dist = 2 ** kk
        if dist >= n_chunks:
            break
        hs = jnp.where(chunk >= dist, pltpu.roll(h, dist, axis=0), 0.0)
        h = h + a1[kk:kk + 1, :] * hs + a2[kk:kk + 1, :] * pltpu.roll(hs, width // 2, axis=1)
    hb = h.astype(BF16)
    step = 2 * LANES
    for t in range(x.shape[1] // step):
        hi = (t + 1) * step
        y_ref[0, :, t * step:hi] = (jnp.dot(x[:, :hi], t_ref[0, :hi, t * step:hi], preferred_element_type=F32)
                                    + jnp.dot(hb, q_ref[0, :, t * step:hi], preferred_element_type=F32))


def _s5_scan(x_chunks, ops, bsz, seq):
    t_big, p_big, q_big, a1, a2 = ops
    nblk, rows, w = x_chunks.shape
    nc = seq // SSM_CHUNK
    sw = p_big.shape[2]
    col = lambda cb, b: (cb, 0, 0)
    return pl.pallas_call(
        _s5_kernel,
        out_shape=jax.ShapeDtypeStruct((nblk, rows, w), F32),
        grid=(nblk, bsz),
        in_specs=[pl.BlockSpec((1, nc, w), lambda cb, b: (cb, b, 0)),
                  pl.BlockSpec((1, w, w), col),
                  pl.BlockSpec((1, w, sw), col),
                  pl.BlockSpec((1, sw, w), col),
                  pl.BlockSpec((1,) + a1.shape[1:], col),
                  pl.BlockSpec((1,) + a2.shape[1:], col)],
        out_specs=pl.BlockSpec((1, nc, w), lambda cb, b: (cb, b, 0)),
        compiler_params=_cparams(2),
        name="s5_scan",
    )(x_chunks, t_big, p_big, q_big, a1, a2)


def _gelu_tanh(x):
    return 0.5 * x * (1.0 + jnp.tanh(math.sqrt(2.0 / math.pi) * (x + 0.044715 * (x * x * x))))


def _mix_kernel(x_ref, attn_ref, yssm_ref, ga_ref, gs_ref, mod_ref, ln_ref,
                wglu_ref, bglu_ref, wua_ref, wus_ref, wout_ref, wrt_ref, wsgu_ref, wsd_ref,
                h2_ref, base_ref, score_ref, y_scr):
    n_chunk = yssm_ref.shape[1]
    for cb in range(yssm_ref.shape[0]):
        for tau in range(SSM_CHUNK):
            y_scr[cb, pl.ds(tau, n_chunk, stride=SSM_CHUNK), :] = yssm_ref[cb, :, tau * LANES:(tau + 1) * LANES]
    g = _gelu_tanh(jnp.concatenate([y_scr[cb] for cb in range(yssm_ref.shape[0])], axis=1))
    glu = g * _sigmoid(_bdot(g, wglu_ref[...]) + bglu_ref[...])
    y_attn = jnp.dot(attn_ref[...], wua_ref[...], preferred_element_type=F32)
    y_ssm = _bdot(glu, wus_ref[...])
    mixed = _sigmoid(ga_ref[...].astype(F32)) * y_attn + _sigmoid(gs_ref[...].astype(F32)) * y_ssm
    gate1 = mod_ref[0, 2:3, :]
    x1 = x_ref[...] + gate1 * _bdot(mixed, wout_ref[...])
    h2 = _modulated_norm(x1, ln_ref[...], mod_ref[0, 3:4, :], mod_ref[0, 4:5, :])
    _store_rows_as_tiles(h2_ref, h2)
    h2b = h2.astype(BF16)
    score_ref[...] = _sigmoid(_bdot_nt(wrt_ref[...], h2b))
    gu = jnp.dot(h2b, wsgu_ref[...], preferred_element_type=F32)
    sd = wsd_ref.shape[0]
    shared = _bdot(_silu(gu[:, :sd]) * gu[:, sd:], wsd_ref[...])
    base_ref[...] = x1 + mod_ref[0, 5:6, :] * shared


def _mix(xf, attn, yssm, ga, gs, mod3, ln2_g, w, seq):
    n, d = xf.shape
    tm = TOKEN_TILE
    tiles_per_seq = seq // tm
    row = lambda i: (i, 0)
    const = lambda i: (0, 0)
    nt = d // LANES
    weights = [w["glu"], w["b_glu"], w["up_attn"], w["up_ssm"], w["out"], w["router_t"], w["sh_gu"], w["sh_down"]]
    return pl.pallas_call(
        _mix_kernel,
        out_shape=(jax.ShapeDtypeStruct((n * nt, LANES), F32),
                   jax.ShapeDtypeStruct((n, d), F32),
                   jax.ShapeDtypeStruct((N_EXPERTS, n), F32)),
        grid=(n // tm,),
        in_specs=[pl.BlockSpec((tm, d), row),
                  pl.BlockSpec((tm, attn.shape[1]), row),
                  pl.BlockSpec((yssm.shape[0], tm // SSM_CHUNK, yssm.shape[2]), lambda i: (0, i, 0)),
                  pl.BlockSpec((tm, d), row),
                  pl.BlockSpec((tm, d), row),
                  pl.BlockSpec((1, 6, d), lambda i: (i // tiles_per_seq, 0, 0)),
                  pl.BlockSpec((1, d), const)] + [pl.BlockSpec(a.shape, const) for a in weights],
        out_specs=(pl.BlockSpec((tm * nt, LANES), row), pl.BlockSpec((tm, d), row),
                   pl.BlockSpec((N_EXPERTS, tm), lambda i: (0, i))),
        scratch_shapes=[pltpu.VMEM((yssm.shape[0], tm, LANES), F32)],
        compiler_params=_cparams(1),
        name="mix",
    )(xf, attn, yssm, ga, gs, mod3, ln2_g.reshape(1, d), *weights)


def _route_kernel(score_ref, bias_ref, tri_ref, idx_ref, w_ref, rank_ref, cnt_ref, carry_ref):
    @pl.when(pl.program_id(0) == 0)
    def _():
        carry_ref[...] = jnp.zeros_like(carry_ref)

    scores = score_ref[...]
    ne, tn = scores.shape
    biased = scores + bias_ref[...]
    gsz = GROUP_SIZE
    sub = lax.broadcasted_iota(jnp.int32, (gsz, tn), 0)
    group_score = []
    for g in range(N_EXPERT_GROUPS):
        sg = biased[g * gsz:(g + 1) * gsz, :]
        m1 = jnp.max(sg, axis=0, keepdims=True)
        first = jnp.min(jnp.where(sg == m1, sub, gsz), axis=0, keepdims=True)
        m2 = jnp.max(jnp.where(sub == first, -jnp.inf, sg), axis=0, keepdims=True)
        group_score.append(m1 + m2)
    group_rows = []
    for g in range(N_EXPERT_GROUPS):
        beaten = jnp.zeros((1, tn), jnp.int32)
        for o in range(N_EXPERT_GROUPS):
            if o == g:
                continue
            wins = (group_score[o] > group_score[g])
            if o < g:
                wins = wins | (group_score[o] == group_score[g])
            beaten = beaten + jnp.where(wins, 1, 0)
        group_rows.append(jnp.broadcast_to(beaten < TOPK_GROUPS, (gsz, tn)))
    allowed = jnp.concatenate(group_rows, axis=0)
    cur = jnp.where(allowed, biased, MASK_VALUE)
    eio = lax.broadcasted_iota(jnp.int32, (ne, tn), 0)
    idx_rows, w_rows, hits = [], [], []
    for _ in range(TOP_K):
        vmax = jnp.max(cur, axis=0, keepdims=True)
        eidx = jnp.min(jnp.where(cur == vmax, eio, ne), axis=0, keepdims=True)
        hit = eio == eidx
        w_rows.append(jnp.sum(jnp.where(hit, scores, 0.0), axis=0, keepdims=True))
        idx_rows.append(eidx)
        hits.append(hit)
        cur = jnp.where(hit, -jnp.inf, cur)
    wts = jnp.concatenate(w_rows, axis=0)
    idx_ref[...] = jnp.concatenate(idx_rows, axis=0)
    w_ref[...] = wts / jnp.sum(wts, axis=0, keepdims=True) * ROUTED_SCALE

    onehot = jnp.zeros((ne, tn), F32)
    for hit in hits:
        onehot = onehot + jnp.where(hit, 1.0, 0.0)
    earlier = carry_ref[...] + _bdot(onehot, tri_ref[...])
    rank_rows = [jnp.sum(jnp.where(hit, earlier, 0.0), axis=0, keepdims=True) for hit in hits]
    rank_ref[...] = jnp.concatenate(rank_rows, axis=0).astype(jnp.int32)
    carry_ref[...] = carry_ref[...] + jnp.sum(onehot, axis=1, keepdims=True)
    cnt_ref[...] = carry_ref[...]


def _route(scores_t, router_bias):
    ne, n = scores_t.shape
    tn = ROUTE_TILE
    tri = jnp.asarray(np.arange(tn)[:, None] < np.arange(tn)[None, :], BF16)
    tok = lambda i: (0, i)
    const = lambda i: (0, 0)
    return pl.pallas_call(
        _route_kernel,
        out_shape=(jax.ShapeDtypeStruct((TOP_K, n), jnp.int32), jax.ShapeDtypeStruct((TOP_K, n), F32),
                   jax.ShapeDtypeStruct((TOP_K, n), jnp.int32), jax.ShapeDtypeStruct((ne, 1), F32)),
        grid=(n // tn,),
        in_specs=[pl.BlockSpec((ne, tn), tok), pl.BlockSpec((ne, 1), const), pl.BlockSpec((tn, tn), const)],
        out_specs=(pl.BlockSpec((TOP_K, tn), tok), pl.BlockSpec((TOP_K, tn), tok),
                   pl.BlockSpec((TOP_K, tn), tok), pl.BlockSpec((ne, 1), const)),
        scratch_shapes=[pltpu.VMEM((ne, 1), F32)],
        compiler_params=_cparams(1),
        name="route",
    )(scores_t, router_bias.reshape(ne, 1).astype(F32), tri)


def _dest_kernel(idx_ref, rank_ref, poffs_ref, dest_ref):
    idx = idx_ref[...]
    ne = poffs_ref.shape[0]
    eio = lax.broadcasted_iota(jnp.int32, (ne, idx.shape[1]), 0)
    poffs = poffs_ref[...]
    rows = [jnp.sum(jnp.where(eio == idx[kk:kk + 1, :], poffs, 0.0), axis=0, keepdims=True)
            for kk in range(idx.shape[0])]
    dest_ref[...] = rank_ref[...] + jnp.concatenate(rows, axis=0).astype(jnp.int32)


def _dest_slots(idx_t, rank_t, poffs):
    k, n = idx_t.shape
    tn = ROUTE_TILE
    ne = poffs.shape[0]
    tok = lambda i: (0, i)
    return pl.pallas_call(
        _dest_kernel,
        out_shape=jax.ShapeDtypeStruct((k, n), jnp.int32),
        grid=(n // tn,),
        in_specs=[pl.BlockSpec((k, tn), tok), pl.BlockSpec((k, tn), tok), pl.BlockSpec((ne, 1), lambda i: (0, 0))],
        out_specs=pl.BlockSpec((k, tn), tok),
        compiler_params=_cparams(1),
        name="dest_slots",
    )(idx_t, rank_t, poffs.astype(F32).reshape(ne, 1))


def _dispatch_tables(idx_t, counts, n_blocks):
    k, n = idx_t.shape
    n_assign = k * n
    rb = ROW_BLOCK
    ne = counts.shape[0]
    counts = counts.reshape(ne).astype(jnp.int32)
    offs = jnp.cumsum(counts) - counts
    nblk = (counts + rb - 1) // rb
    bend = jnp.cumsum(nblk)
    bstart = bend - nblk
    blocks = jnp.arange(n_blocks, dtype=jnp.int32)
    blk_e = jnp.minimum(jnp.sum((bend[None, :] <= blocks[:, None]).astype(jnp.int32), axis=1), ne - 1)
    mine = blk_e[:, None] == jnp.arange(ne, dtype=jnp.int32)[None, :]
    src_start = jnp.sum(jnp.where(mine, (offs - bstart * rb)[None, :], 0), axis=1) + blocks * rb
    tok = lax.broadcasted_iota(jnp.int32, (k, n), 1)
    kk = lax.broadcasted_iota(jnp.int32, (k, n), 0)
    assert ne * n_assign < 2 ** 31
    keys = jnp.sort((idx_t * n_assign + tok * k + kk).reshape(n_assign))
    tok_sorted = (keys % n_assign) // k
    return tok_sorted, blk_e, src_start, bend[ne - 1:ne], bstart * rb


def _ffn_kernel(tok_ref, blk_e_ref, start_ref, used_ref, h2_hbm, wg_ref, wu_ref, wd_ref, y_ref, xbuf, sems):
    b = pl.program_id(0)
    rb = ROW_BLOCK
    d = wg_ref.shape[1]
    nt = d // LANES
    n_assign = tok_ref.shape[0]
    n_used = used_ref[0]
    n_blocks = pl.num_programs(0)

    def start_rows(blk, s, inline):
        start = start_ref[blk]

        def one(r):
            tok = tok_ref[jnp.minimum(start + r, n_assign - 1)]
            pltpu.make_async_copy(h2_hbm.at[pl.ds(pl.multiple_of(tok * nt, nt), nt)],
                                  xbuf.at[s, pl.ds(pl.multiple_of(r * nt, nt), nt)], sems.at[s]).start()

        if inline:
            for r in range(rb):
                one(r)
        else:
            lax.fori_loop(0, rb, lambda r, c: (one(r), c)[1], 0, unroll=ISSUE_UNROLL)

    def wait_rows(s):
        pltpu.make_async_copy(h2_hbm.at[pl.ds(0, rb * nt)], xbuf.at[s], sems.at[s]).wait()

    nbuf = xbuf.shape[0]
    ahead = nbuf - 1

    @pl.when(b == 0)
    def _():
        for a in range(ahead):
            start_rows(jnp.minimum(a, n_blocks - 1), a, False)

    for s in range(nbuf):
        @pl.when((b < n_used) & (b % nbuf == s))
        def _(s=s):
            wait_rows(s)
            xb = _load_rows_from_tiles(xbuf.at[s], rb, d).astype(BF16)
            start_rows(jnp.minimum(b + ahead, n_blocks - 1), (s + ahead) % nbuf, True)
            hg = jnp.dot(xb, wg_ref[0].astype(BF16), preferred_element_type=F32)
            hu = jnp.dot(xb, wu_ref[0].astype(BF16), preferred_element_type=F32)
            _store_rows_as_tiles(y_ref, _bdot(_silu(hg) * hu, wd_ref[0]))

    @pl.when(b == n_used - 1)
    def _():
        for a in range(1, nbuf):
            wait_rows((b + a) % nbuf)

    @pl.when(b >= n_used)
    def _():
        y_ref[...] = jnp.zeros_like(y_ref)


def _expert_ffn(h2_tiles, tok_sorted, blk_e, src_start, n_used, w_gate, w_up, w_down):
    n_blocks = blk_e.shape[0]
    rb = ROW_BLOCK
    _, d, ed = w_gate.shape
    nt = d // LANES
    expert = lambda i, tok, be, st, nu: (be[i], 0, 0)
    grid_spec = pltpu.PrefetchScalarGridSpec(
        num_scalar_prefetch=4,
        grid=(n_blocks,),
        in_specs=[pl.BlockSpec(memory_space=pl.ANY),
                  pl.BlockSpec((1, d, ed), expert),
                  pl.BlockSpec((1, d, ed), expert),
                  pl.BlockSpec((1, ed, d), expert)],
        out_specs=pl.BlockSpec((rb * nt, LANES), lambda i, tok, be, st, nu: (i, 0)),
        scratch_shapes=[pltpu.VMEM((FFN_ROW_BUFFERS, rb * nt, LANES), F32),
                        pltpu.SemaphoreType.DMA((FFN_ROW_BUFFERS,))],
    )
    return pl.pallas_call(
        _ffn_kernel,
        out_shape=jax.ShapeDtypeStruct((n_blocks * rb * nt, LANES), F32),
        grid_spec=grid_spec,
        compiler_params=_cparams(1),
        name="expert_ffn",
    )(tok_sorted, blk_e, src_start, n_used, h2_tiles, w_gate, w_up, w_down)


def _combine_kernel(dest_ref, y_hbm, base_ref, wt_ref, mod_ref, o_ref, buf, sems):
    i = pl.program_id(0)
    n_steps = pl.num_programs(0)
    tm, d = base_ref.shape
    k = wt_ref.shape[1]
    nt = d // LANES
    n_rows = k * tm
    slot = i % 2

    def issue(step, s):
        first = step * n_rows

        def body(r, _):
            src = dest_ref[first + r]
            pltpu.make_async_copy(y_hbm.at[pl.ds(pl.multiple_of(src * nt, nt), nt)],
                                  buf.at[s, pl.ds(pl.multiple_of(r * nt, nt), nt)], sems.at[s]).start()
            return 0

        lax.fori_loop(0, n_rows, body, 0, unroll=ISSUE_UNROLL)

    @pl.when(i == 0)
    def _():
        issue(0, 0)

    @pl.when(i + 1 < n_steps)
    def _():
        issue(i + 1, 1 - slot)

    pltpu.make_async_copy(y_hbm.at[pl.ds(0, n_rows * nt)], buf.at[slot], sems.at[slot]).wait()
    rows = buf.at[slot]
    wts = wt_ref[...]
    gate2 = mod_ref[0, 5:6, :]
    for c in range(nt):
        cols = slice(c * LANES, (c + 1) * LANES)
        routed = jnp.zeros((tm, LANES), F32)
        for kk in range(k):
            routed = routed + wts[:, kk:kk + 1] * rows[pl.ds(kk * tm * nt + c, tm, stride=nt), :]
        o_ref[:, cols] = base_ref[:, cols] + gate2[:, cols] * routed


def _combine(y_tiles, dest_t, w_t, base, mod3, seq):
    n, d = base.shape
    tm = COMBINE_TILE
    k = dest_t.shape[0]
    nt = d // LANES
    tiles_per_seq = seq // tm
    dest_tiles = dest_t.reshape(k, n // tm, tm).transpose(1, 0, 2).reshape(n * k)
    grid_spec = pltpu.PrefetchScalarGridSpec(
        num_scalar_prefetch=1,
        grid=(n // tm,),
        in_specs=[pl.BlockSpec(memory_space=pl.ANY),
                  pl.BlockSpec((tm, d), lambda i, dst: (i, 0)),
                  pl.BlockSpec((tm, k), lambda i, dst: (i, 0)),
                  pl.BlockSpec((1, 6, d), lambda i, dst: (i // tiles_per_seq, 0, 0))],
        out_specs=pl.BlockSpec((tm, d), lambda i, dst: (i, 0)),
        scratch_shapes=[pltpu.VMEM((2, k * tm * nt, LANES), F32), pltpu.SemaphoreType.DMA((2,))],
    )
    return pl.pallas_call(
        _combine_kernel,
        out_shape=jax.ShapeDtypeStruct((n, d), F32),
        grid_spec=grid_spec,
        compiler_params=_cparams(1),
        name="combine",
    )(dest_tiles, y_tiles, base, w_t.T, mod3)


def _hybrid_layer(x, cond, rel_bias, w_ada, b_ada, ln1_g, w_in, q_norm_g, k_norm_g,
                  ssm_lambda_re, ssm_lambda_im, ssm_log_dt, ssm_b_re, ssm_b_im, ssm_c_re, ssm_c_im,
                  ssm_d, ssm_w_glu, ssm_b_glu, w_up_attn, w_up_ssm, w_out, ln2_g,
                  w_router, router_bias, w_exp_gate, w_exp_up, w_exp_down,
                  w_sh_gate, w_sh_up, w_sh_down):
    bsz, seq, d = x.shape
    n = bsz * seq
    xf = x.reshape(n, d)
    mod3 = _adaln(cond, w_ada, b_ada).reshape(bsz, 6, d)

    q_gain = jnp.tile(q_norm_g.astype(F32), ATTN_HEADS).reshape(1, ATTN_WIDTH)
    k_gain = jnp.tile(k_norm_g.astype(F32), ATTN_HEADS).reshape(1, ATTN_WIDTH)
    q, k, v, u, ga, gs, kmean = _inproj(xf, mod3, ln1_g, w_in.astype(BF16), q_gain, k_gain, seq)

    attn = _moba_attention(q, k, v, kmean, rel_bias, bsz, seq)
    ops = _s5_operators(ssm_lambda_re, ssm_lambda_im, ssm_log_dt, ssm_b_re, ssm_b_im,
                        ssm_c_re, ssm_c_im, ssm_d, seq // SSM_CHUNK)
    yssm = _s5_scan(u, ops, bsz, seq)

    weights = {
        "glu": ssm_w_glu.astype(BF16), "b_glu": ssm_b_glu.astype(F32).reshape(1, -1),
        "up_attn": w_up_attn.astype(BF16), "up_ssm": w_up_ssm.astype(BF16), "out": w_out.astype(BF16),
        "router_t": w_router.T.astype(BF16),
        "sh_gu": jnp.concatenate([w_sh_gate, w_sh_up], axis=1).astype(BF16),
        "sh_down": w_sh_down.astype(BF16),
    }
    h2, base, scores_t = _mix(xf, attn, yssm, ga, gs, mod3, ln2_g, weights, seq)

    idx_t, w_t, rank_t, counts = _route(scores_t, router_bias)
    n_blocks = -(-(n * TOP_K) // ROW_BLOCK) + N_EXPERTS
    tok_sorted, blk_e, src_start, n_used, poffs = _dispatch_tables(idx_t, counts, n_blocks)
    dest_t = _dest_slots(idx_t, rank_t, poffs)
    y_tiles = _expert_ffn(h2, tok_sorted, blk_e, src_start, n_used, w_exp_gate, w_exp_up, w_exp_down)
    out = _combine(y_tiles, dest_t, w_t, base, mod3, seq)
    return out.reshape(bsz, seq, d)


def kernel(x, c, rel_bias, w_ada, b_ada, ln1_g, w_in, q_norm_g, k_norm_g, ssm_lambda_re, ssm_lambda_im, ssm_log_dt, ssm_b_re, ssm_b_im, ssm_c_re, ssm_c_im, ssm_d, ssm_w_glu, ssm_b_glu, w_up_attn, w_up_ssm, w_out, ln2_g, w_router, router_bias, w_exp_gate, w_exp_up, w_exp_down, w_sh_gate, w_sh_up, w_sh_down):
    for l in range(w_ada.shape[0]):
        x = _hybrid_layer(x, c, rel_bias, w_ada[l], b_ada[l], ln1_g[l], w_in[l], q_norm_g[l], k_norm_g[l],
                          ssm_lambda_re[l], ssm_lambda_im[l], ssm_log_dt[l], ssm_b_re[l], ssm_b_im[l],
                          ssm_c_re[l], ssm_c_im[l], ssm_d[l], ssm_w_glu[l], ssm_b_glu[l],
                          w_up_attn[l], w_up_ssm[l], w_out[l], ln2_g[l], w_router[l], router_bias[l],
                          w_exp_gate[l], w_exp_up[l], w_exp_down[l], w_sh_gate[l], w_sh_up[l], w_sh_down[l])
    return x
```

```python
import functools
import math

import numpy as np
import jax
import jax.numpy as jnp
from jax import lax
from jax.experimental import pallas as pl
from jax.experimental.pallas import tpu as pltpu

F32 = jnp.float32
BF16 = jnp.bfloat16

ATTN_HEADS = 8
HEAD_DIM = 64
ATTN_WIDTH = ATTN_HEADS * HEAD_DIM
MOBA_BLOCK = 256
MOBA_TOPK = 3
NUM_BUCKETS = 32
MAX_DISTANCE = 128
SSM_WIDTH = 512
SSM_GROUP = 16
SSM_GROUPS = SSM_WIDTH // SSM_GROUP
SSM_STATE = 64
N_EXPERTS = 256
TOP_K = 8
N_EXPERT_GROUPS = 8
TOPK_GROUPS = 4
GROUP_SIZE = N_EXPERTS // N_EXPERT_GROUPS
EXPERT_DIM = 256
ROUTED_SCALE = 2.5
EPS = 1e-6
MASK_VALUE = -1e30

LANES = 128
HEADS_PER_STEP = LANES // HEAD_DIM
SSM_CHUNK = 16
SUBLANES = 8
BF16_ROWS = 16
ROW_BLOCK = 256
TOKEN_TILE = 512
ROUTE_TILE = 256
COMBINE_TILE = 128
ISSUE_UNROLL = 8
FFN_ROW_BUFFERS = 3
VMEM_LIMIT = 56 * 1024 * 1024


def _store_rows_as_tiles(ref, val):
    rows, d = val.shape
    nt = d // LANES
    for c in range(nt):
        ref[pl.ds(c, rows, stride=nt), :] = val[:, c * LANES:(c + 1) * LANES]


def _load_rows_from_tiles(ref, rows, d):
    nt = d // LANES
    return jnp.concatenate([ref[pl.ds(c, rows, stride=nt), :] for c in range(nt)], axis=1)


def _cparams(n_axes, vmem=VMEM_LIMIT):
    return pltpu.CompilerParams(dimension_semantics=("arbitrary",) * n_axes, vmem_limit_bytes=vmem)


def _sigmoid(x):
    return 1.0 / (1.0 + jnp.exp(-x))


def _silu(x):
    return x * _sigmoid(x)


def _bdot(a, b):
    return jnp.dot(a.astype(BF16), b.astype(BF16), preferred_element_type=F32)


def _bdot_nt(a, b):
    return lax.dot_general(a.astype(BF16), b.astype(BF16), (((1,), (1,)), ((), ())),
                           preferred_element_type=F32)


def _adaln_kernel(c_ref, w_ref, b_ref, o_ref):
    o_ref[...] = _bdot(_silu(c_ref[...]), w_ref[...]) + b_ref[...]


def _adaln(c, w_ada, b_ada):
    bsz, d = c.shape
    n_out = w_ada.shape[1]
    return pl.pallas_call(
        _adaln_kernel,
        out_shape=jax.ShapeDtypeStruct((bsz, n_out), F32),
        grid=(n_out // d,),
        in_specs=[pl.BlockSpec((bsz, d), lambda j: (0, 0)),
                  pl.BlockSpec((d, d), lambda j: (0, j)),
                  pl.BlockSpec((1, d), lambda j: (0, j))],
        out_specs=pl.BlockSpec((bsz, d), lambda j: (0, j)),
        compiler_params=_cparams(1),
        name="adaln",
    )(c, w_ada, b_ada.reshape(1, n_out))


def _modulated_norm(x, gain, shift, scale):
    y = x * lax.rsqrt(jnp.mean(x * x, axis=-1, keepdims=True) + EPS) * gain
    return y * (1.0 + scale) + shift


def _head_norm(t, seg, gain):
    ms = _bdot(t * t, seg)
    return t * lax.rsqrt(ms + EPS) * gain


def _inproj_kernel(x_ref, mod_ref, ln_ref, w_ref, seg_ref, qg_ref, kg_ref,
                   q_ref, k_ref, v_ref, u_ref, ga_ref, gs_ref, km_ref, u_scr):
    aw, sw, d = ATTN_WIDTH, SSM_WIDTH, x_ref.shape[1]
    h = _modulated_norm(x_ref[...], ln_ref[...], mod_ref[0, 0:1, :], mod_ref[0, 1:2, :]).astype(BF16)
    seg = seg_ref[...]
    q = jnp.dot(h, w_ref[:, 0:aw], preferred_element_type=F32)
    q_ref[...] = _head_norm(q, seg, qg_ref[...])
    k = jnp.dot(h, w_ref[:, aw:2 * aw], preferred_element_type=F32)
    kn = _head_norm(k, seg, kg_ref[...])
    k_ref[...] = kn.astype(BF16)
    for blk in range(km_ref.shape[0]):
        km_ref[blk] = jnp.mean(kn[blk * MOBA_BLOCK:(blk + 1) * MOBA_BLOCK, :], axis=0, keepdims=True)
    v_ref[...] = jnp.dot(h, w_ref[:, 2 * aw:3 * aw], preferred_element_type=F32).astype(BF16)
    o = 3 * aw
    u = jnp.dot(h, w_ref[:, o:o + sw], preferred_element_type=F32)
    n_chunk = u_scr.shape[1] // SSM_CHUNK
    for cb in range(sw // LANES):
        u_scr[cb] = u[:, cb * LANES:(cb + 1) * LANES]
        for sg in range(SSM_CHUNK):
            u_ref[cb, :, sg * LANES:(sg + 1) * LANES] = (
                u_scr[cb, pl.ds(sg, n_chunk, stride=SSM_CHUNK), :].astype(BF16))
    o += sw
    ga_ref[...] = jnp.dot(h, w_ref[:, o:o + d], preferred_element_type=F32).astype(BF16)
    o += d
    gs_ref[...] = jnp.dot(h, w_ref[:, o:o + d], preferred_element_type=F32).astype(BF16)


def _inproj(xf, mod3, ln1_g, w_in_b, q_gain, k_gain, seq):
    n, d = xf.shape
    tm = TOKEN_TILE
    assert tm % MOBA_BLOCK == 0 and seq % tm == 0
    blocks_per_tile = tm // MOBA_BLOCK
    tiles_per_seq = seq // tm
    aw, sw = ATTN_WIDTH, SSM_WIDTH
    head_of_lane = np.arange(aw) // HEAD_DIM
    seg = jnp.asarray((head_of_lane[:, None] == head_of_lane[None, :]) / HEAD_DIM, BF16)
    row = lambda i: (i, 0)
    const = lambda i: (0, 0)
    return pl.pallas_call(
        _inproj_kernel,
        out_shape=(jax.ShapeDtypeStruct((n, aw), F32),
                   jax.ShapeDtypeStruct((n, aw), BF16),
                   jax.ShapeDtypeStruct((n, aw), BF16),
                   jax.ShapeDtypeStruct((sw // LANES, n // SSM_CHUNK, SSM_CHUNK * LANES), BF16),
                   jax.ShapeDtypeStruct((n, d), BF16),
                   jax.ShapeDtypeStruct((n, d), BF16),
                   jax.ShapeDtypeStruct((n // MOBA_BLOCK, 1, aw), F32)),
        grid=(n // tm,),
        in_specs=[pl.BlockSpec((tm, d), row),
                  pl.BlockSpec((1, 6, d), lambda i: (i // tiles_per_seq, 0, 0)),
                  pl.BlockSpec((1, d), const),
                  pl.BlockSpec(w_in_b.shape, const),
                  pl.BlockSpec((aw, aw), const),
                  pl.BlockSpec((1, aw), const),
                  pl.BlockSpec((1, aw), const)],
        out_specs=(pl.BlockSpec((tm, aw), row), pl.BlockSpec((tm, aw), row), pl.BlockSpec((tm, aw), row),
                   pl.BlockSpec((sw // LANES, tm // SSM_CHUNK, SSM_CHUNK * LANES), lambda i: (0, i, 0)),
                   pl.BlockSpec((tm, d), row), pl.BlockSpec((tm, d), row),
                   pl.BlockSpec((blocks_per_tile, 1, aw), lambda i: (i, 0, 0))),
        scratch_shapes=[pltpu.VMEM((sw // LANES, tm, LANES), F32)],
        compiler_params=_cparams(1),
        name="inproj",
    )(xf, mod3, ln1_g.reshape(1, d), w_in_b, seg, q_gain, k_gain)


def _t5_bucket(rel):
    n = jnp.maximum(rel, 0)
    max_exact = NUM_BUCKETS // 2
    nf = jnp.maximum(n, 1).astype(F32)
    large = max_exact + (jnp.log(nf / max_exact) / math.log(MAX_DISTANCE / max_exact)
                         * (NUM_BUCKETS - max_exact)).astype(jnp.int32)
    large = jnp.minimum(large, NUM_BUCKETS - 1)
    return jnp.where(n < max_exact, n, large)


def _bias_tables(rel_bias):
    blk = MOBA_BLOCK
    assert blk + 1 >= MAX_DISTANCE
    rel = jnp.arange(blk)[None, :] - jnp.arange(blk)[:, None]
    table = rel_bias.astype(F32)
    table = table - table[NUM_BUCKETS - 1][None, :]

    def lookup(r):
        onehot = jax.nn.one_hot(_t5_bucket(r), NUM_BUCKETS, dtype=F32)
        return jnp.einsum('kqn,nh->hkq', onehot, table, precision=lax.Precision.HIGHEST)

    return lookup(rel), lookup(rel + blk)


def _select_blocks(gate_t, n_past):
    nb, tq = gate_t.shape
    blk = lax.broadcasted_iota(jnp.int32, (nb, tq), 0)
    beaten = jnp.zeros((nb, tq), jnp.int32)
    for m in range(nb):
        gm = gate_t[m:m + 1, :]
        wins = (gm > gate_t) | ((gm == gate_t) & (m < blk))
        beaten = beaten + jnp.where(wins & (m < n_past), 1, 0)
    return jnp.where((blk < n_past) & (beaten < MOBA_TOPK), 1.0, 0.0)


def _attn_kernel(q_ref, k_ref, vt_ref, km_ref, bias_ref, o_ref, sel_ref, s_ref):
    qi = pl.program_id(2)
    tq = q_ref.shape[0]
    blk = MOBA_BLOCK
    hd = HEAD_DIM
    heads = range(HEADS_PER_STEP)
    q = q_ref[...]
    lane = lax.broadcasted_iota(jnp.int32, (tq, LANES), 1)
    kpos = lax.broadcasted_iota(jnp.int32, (blk, tq), 0)
    qpos = lax.broadcasted_iota(jnp.int32, (blk, tq), 1)
    scale = hd ** -0.5
    n_far = jnp.maximum(qi - 1, 0)
    n_pairs = (n_far + 1) // 2
    jp = jnp.maximum(qi - 1, 0)
    k_own = k_ref[pl.ds(pl.multiple_of(qi * blk, blk), blk), :]
    k_prev = k_ref[pl.ds(pl.multiple_of(jp * blk, blk), blk), :]

    qbs = []
    for h in heads:
        in_head = (lane >= h * hd) & (lane < (h + 1) * hd)
        qm = jnp.where(in_head, q, 0.0)
        gate_t = lax.dot_general(km_ref[0], qm, (((1,), (1,)), ((), ())),
                                 precision=lax.Precision.HIGHEST, preferred_element_type=F32)
        sel_ref[h] = _select_blocks(gate_t, qi)
        qbs.append((qm * scale).astype(BF16))

    def pair_scores(j):
        kb = k_ref[pl.ds(pl.multiple_of(j * blk, blk), 2 * blk), :]
        return [_bdot_nt(kb, qbs[h]) for h in heads]

    ones_rows = jnp.ones((BF16_ROWS, blk), BF16)

    def probs(s, m):
        return jnp.exp((s - m).astype(BF16))

    def attend(h, p, blocks):
        acc = None
        for i, j in enumerate(blocks):
            lhs = jnp.concatenate([vt_ref[0, j, h * hd:(h + 1) * hd, :], ones_rows], axis=0)
            part = jnp.dot(lhs, p[i * blk:(i + 1) * blk, :], preferred_element_type=F32)
            acc = part if acc is None else acc + part
        return acc

    for h, s in enumerate(pair_scores(0)):
        s_ref[h] = s

    carries = []
    for h in heads:
        s_prev = _bdot_nt(k_prev, qbs[h]) + bias_ref[h, 0:blk, :]
        s_prev = jnp.where(sel_ref[h, pl.ds(jp, 1), :] > 0.5, s_prev, MASK_VALUE)
        s_own = _bdot_nt(k_own, qbs[h]) + bias_ref[h, blk:2 * blk, :]
        s_own = jnp.where(kpos <= qpos, s_own, MASK_VALUE)
        s = jnp.concatenate([s_prev, s_own], axis=0)
        m = jnp.max(s, axis=0, keepdims=True)
        carries.append((m, attend(h, probs(s, m), (jp, qi))))

    def far_pair(pi, carries):
        j = 2 * pi
        s_cur = [s_ref[h] for h in heads]
        for h, s in enumerate(pair_scores(2 * jnp.minimum(pi + 1, n_pairs - 1))):
            s_ref[h] = s
        second_is_far = j + 1 < n_far
        out = []
        for h in heads:
            m, acc = carries[h]
            c0 = sel_ref[h, pl.ds(j, 1), :] > 0.5
            c1 = (sel_ref[h, pl.ds(j + 1, 1), :] > 0.5) & second_is_far
            chosen = jnp.concatenate([jnp.broadcast_to(c0, (blk, tq)), jnp.broadcast_to(c1, (blk, tq))], axis=0)
            s = jnp.where(chosen, s_cur[h], MASK_VALUE)
            m_new = jnp.maximum(m, jnp.max(s, axis=0, keepdims=True))
            acc = jnp.exp(m - m_new) * acc + attend(h, probs(s, m_new), (j, j + 1))
            out.append((m_new, acc))
        return tuple(out)

    carries = lax.fori_loop(0, n_pairs, far_pair, tuple(carries))
    out_t = jnp.concatenate([acc[:hd] / acc[hd:hd + 1] for _, acc in carries], axis=0)
    o_ref[...] = out_t.T.astype(o_ref.dtype)


def _moba_attention(q, k, v, kmean, rel_bias, bsz, seq):
    n, aw = q.shape
    blk = MOBA_BLOCK
    nb = seq // blk
    assert nb >= 2
    own, prev = _bias_tables(rel_bias)
    bias = jnp.concatenate([prev, own], axis=1)
    hps = HEADS_PER_STEP
    npair = aw // LANES
    vt = v.reshape(bsz, nb, blk, aw).transpose(0, 1, 3, 2)
    return pl.pallas_call(
        _attn_kernel,
        out_shape=jax.ShapeDtypeStruct((n, aw), BF16),
        grid=(bsz, npair, nb),
        in_specs=[pl.BlockSpec((blk, LANES), lambda b, hp, qi: (b * nb + qi, hp)),
                  pl.BlockSpec((seq, LANES), lambda b, hp, qi: (b, hp)),
                  pl.BlockSpec((1, nb, LANES, blk), lambda b, hp, qi: (b, 0, hp, 0)),
                  pl.BlockSpec((1, nb, LANES), lambda b, hp, qi: (b, 0, hp)),
                  pl.BlockSpec((hps, 2 * blk, blk), lambda b, hp, qi: (hp, 0, 0))],
        out_specs=pl.BlockSpec((blk, LANES), lambda b, hp, qi: (b * nb + qi, hp)),
        scratch_shapes=[pltpu.VMEM((hps, nb, blk), F32), pltpu.VMEM((hps, 2 * blk, blk), F32)],
        compiler_params=_cparams(3),
        name="moba_attention",
    )(q, k, vt, kmean.reshape(bsz, nb, aw), bias)


def _s5_operators(lambda_re, lambda_im, log_dt, b_re, b_im, c_re, c_im, d_skip, n_chunks):
    hi = lax.Precision.HIGHEST
    L, G, P, C = SSM_CHUNK, SSM_GROUPS, SSM_STATE, SSM_GROUP
    lam_re = jnp.minimum(lambda_re.astype(F32), -1e-4)
    lam_im = lambda_im.astype(F32)
    dt = jnp.exp(log_dt.astype(F32))[:, None]
    z_re, z_im = lam_re * dt, lam_im * dt

    def a_pow(nvec):
        nv = jnp.asarray(nvec, F32)[:, None, None]
        mag = jnp.exp(nv * z_re)
        return mag * jnp.cos(nv * z_im), mag * jnp.sin(nv * z_im)

    a_re, a_im = a_pow([1.0])
    a_re, a_im = a_re[0], a_im[0]
    den = lam_re * lam_re + lam_im * lam_im
    nr = a_re - 1.0
    coef_re = (nr * lam_re + a_im * lam_im) / den
    coef_im = (a_im * lam_re - nr * lam_im) / den
    br, bi = b_re.astype(F32), b_im.astype(F32)
    bbar_re = coef_re[..., None] * br - coef_im[..., None] * bi
    bbar_im = coef_re[..., None] * bi + coef_im[..., None] * br
    cr, ci = c_re.astype(F32), c_im.astype(F32)

    pw_re, pw_im = a_pow(np.arange(L + 1))
    cb_re = cr[None] * pw_re[:, :, None, :] - ci[None] * pw_im[:, :, None, :]
    cb_im = cr[None] * pw_im[:, :, None, :] + ci[None] * pw_re[:, :, None, :]
    kern = (jnp.einsum('jgop,gpi->gijo', cb_re[:L], bbar_re, precision=hi)
            - jnp.einsum('jgop,gpi->gijo', cb_im[:L], bbar_im, precision=hi)).reshape(G, C, L * C)
    t_op = jnp.stack([jnp.pad(kern[:, :, :(L - s) * C], ((0, 0), (0, 0), (s * C, 0))) for s in range(L)], axis=1)
    d_g = d_skip.astype(F32).reshape(G, 1, C, 1)
    on_diag = (lax.broadcasted_iota(jnp.int32, (1, L, C, L * C), 3)
               == lax.broadcasted_iota(jnp.int32, (1, L, C, L * C), 1) * C
               + lax.broadcasted_iota(jnp.int32, (1, L, C, L * C), 2))
    t_op = (t_op + jnp.where(on_diag, d_g, 0.0)).reshape(G, L * C, L * C)

    rp_re, rp_im = jnp.flip(pw_re[:L], 0), jnp.flip(pw_im[:L], 0)
    p_re = rp_re[..., None] * bbar_re[None] - rp_im[..., None] * bbar_im[None]
    p_im = rp_re[..., None] * bbar_im[None] + rp_im[..., None] * bbar_re[None]
    p_op = jnp.concatenate([p_re, p_im], axis=2)
    p_op = p_op.transpose(1, 0, 3, 2).reshape(G, L * C, 2 * P)

    q_re = cb_re[1:].transpose(1, 3, 0, 2)
    q_im = -cb_im[1:].transpose(1, 3, 0, 2)
    q_op = jnp.concatenate([q_re, q_im], axis=1).reshape(G, 2 * P, L * C)

    n_steps = max(1, int(math.ceil(math.log2(n_chunks))))
    dk_re, dk_im = a_pow([float(L * 2 ** k) for k in range(n_steps)])
    GB = LANES // C
    NB = G // GB
    lc = np.arange(L * C)
    wide = np.arange(L * LANES)
    expand_tc = jnp.asarray((lc[:, None] // C == wide[None, :] // LANES) & (lc[:, None] % C == wide[None, :] % C), BF16)
    st = np.arange(2 * P)
    wide_st = np.arange(2 * GB * P)
    expand_st = jnp.asarray((st[:, None] // P == wide_st[None, :] // (GB * P))
                            & (st[:, None] % P == wide_st[None, :] % P), BF16)
    g_of_wide = (jnp.arange(L * LANES) // C) % GB
    g_of_state = (jnp.arange(2 * GB * P) // P) % GB

    def widen(rows, expand, g_row, g_col):
        full = jnp.einsum('brk,kc->brc', rows.astype(BF16), expand, preferred_element_type=F32)
        return jnp.where(g_row[:, None] == g_col[None, :], full, 0.0).astype(BF16)

    t_rows = t_op.reshape(NB, GB, L, C, L * C).transpose(0, 2, 1, 3, 4).reshape(NB, L * LANES, L * C)
    p_rows = p_op.reshape(NB, GB, L, C, 2 * P).transpose(0, 2, 1, 3, 4).reshape(NB, L * LANES, 2 * P)
    q_rows = q_op.reshape(NB, GB, 2, P, L * C).transpose(0, 2, 1, 3, 4).reshape(NB, 2 * GB * P, L * C)
    t_big = widen(t_rows, expand_tc, g_of_wide, g_of_wide)
    p_big = widen(p_rows, expand_st, g_of_wide, g_of_state)
    q_big = widen(q_rows, expand_tc, g_of_state, g_of_wide)
    dk_re = dk_re.reshape(-1, NB, GB * P)
    dk_im = dk_im.reshape(-1, NB, GB * P)
    a1 = jnp.concatenate([dk_re, dk_re], axis=-1).transpose(1, 0, 2)
    a2 = jnp.concatenate([-dk_im, dk_im], axis=-1).transpose(1, 0, 2)
    return t_big.astype(BF16), p_big.astype(BF16), q_big.astype(BF16), a1, a2


def _s5_kernel(x_ref, t_ref, p_ref, q_ref, a1_ref, a2_ref, y_ref):
    x = x_ref[0]
    s = jnp.dot(x, p_ref[0], preferred_element_type=F32)
    n_chunks, width = s.shape
    chunk = lax.broadcasted_iota(jnp.int32, (n_chunks, width), 0)
    a1 = a1_ref[0]
    a2 = a2_ref[0]
    h = jnp.where(chunk >= 1, pltpu.roll(s, 1, axis=0), 0.0)
    for kk in range(a1.shape[0]):
        dist = 2 ** kk
        if dist >= n_chunks:
            break
        hs = jnp.where(chunk >= dist, pltpu.roll(h, dist, axis=0), 0.0)
        h = h + a1[kk:kk + 1, :] * hs + a2[kk:kk + 1, :] * pltpu.roll(hs, width // 2, axis=1)
    hb = h.astype(BF16)
    step = 2 * LANES
    for t in range(x.shape[1] // step):
        hi = (t + 1) * step
        y_ref[0, :, t * step:hi] = (jnp.dot(x[:, :hi], t_ref[0, :hi, t * step:hi], preferred_element_type=F32)
                                    + jnp.dot(hb, q_ref[0, :, t * step:hi], preferred_element_type=F32))


def _s5_scan(x_chunks, ops, bsz, seq):
    t_big, p_big, q_big, a1, a2 = ops
    nblk, rows, w = x_chunks.shape
    nc = seq // SSM_CHUNK
    sw = p_big.shape[2]
    col = lambda cb, b: (cb, 0, 0)
    return pl.pallas_call(
        _s5_kernel,
        out_shape=jax.ShapeDtypeStruct((nblk, rows, w), F32),
        grid=(nblk, bsz),
        in_specs=[pl.BlockSpec((1, nc, w), lambda cb, b: (cb, b, 0)),
                  pl.BlockSpec((1, w, w), col),
                  pl.BlockSpec((1, w, sw), col),
                  pl.BlockSpec((1, sw, w), col),
                  pl.BlockSpec((1,) + a1.shape[1:], col),
                  pl.BlockSpec((1,) + a2.shape[1:], col)],
        out_specs=pl.BlockSpec((1, nc, w), lambda cb, b: (cb, b, 0)),
        compiler_params=_cparams(2),
        name="s5_scan",
    )(x_chunks, t_big, p_big, q_big, a1, a2)


def _gelu_tanh(x):
    return 0.5 * x * (1.0 + jnp.tanh(math.sqrt(2.0 / math.pi) * (x + 0.044715 * (x * x * x))))


def _mix_kernel(x_ref, attn_ref, yssm_ref, ga_ref, gs_ref, mod_ref, ln_ref,
                wglu_ref, bglu_ref, wua_ref, wus_ref, wout_ref, wrt_ref, wsgu_ref, wsd_ref,
                h2_ref, base_ref, score_ref, y_scr):
    n_chunk = yssm_ref.shape[1]
    for cb in range(yssm_ref.shape[0]):
        for tau in range(SSM_CHUNK):
            y_scr[cb, pl.ds(tau, n_chunk, stride=SSM_CHUNK), :] = yssm_ref[cb, :, tau * LANES:(tau + 1) * LANES]
    g = _gelu_tanh(jnp.concatenate([y_scr[cb] for cb in range(yssm_ref.shape[0])], axis=1))
    glu = g * _sigmoid(_bdot(g, wglu_ref[...]) + bglu_ref[...])
    y_attn = jnp.dot(attn_ref[...], wua_ref[...], preferred_element_type=F32)
    y_ssm = _bdot(glu, wus_ref[...])
    mixed = _sigmoid(ga_ref[...].astype(F32)) * y_attn + _sigmoid(gs_ref[...].astype(F32)) * y_ssm
    gate1 = mod_ref[0, 2:3, :]
    x1 = x_ref[...] + gate1 * _bdot(mixed, wout_ref[...])
    h2 = _modulated_norm(x1, ln_ref[...], mod_ref[0, 3:4, :], mod_ref[0, 4:5, :])
    _store_rows_as_tiles(h2_ref, h2)
    h2b = h2.astype(BF16)
    score_ref[...] = _sigmoid(_bdot_nt(wrt_ref[...], h2b))
    gu = jnp.dot(h2b, wsgu_ref[...], preferred_element_type=F32)
    sd = wsd_ref.shape[0]
    shared = _bdot(_silu(gu[:, :sd]) * gu[:, sd:], wsd_ref[...])
    base_ref[...] = x1 + mod_ref[0, 5:6, :] * shared


def _mix(xf, attn, yssm, ga, gs, mod3, ln2_g, w, seq):
    n, d = xf.shape
    tm = TOKEN_TILE
    tiles_per_seq = seq // tm
    row = lambda i: (i, 0)
    const = lambda i: (0, 0)
    nt = d // LANES
    weights = [w["glu"], w["b_glu"], w["up_attn"], w["up_ssm"], w["out"], w["router_t"], w["sh_gu"], w["sh_down"]]
    return pl.pallas_call(
        _mix_kernel,
        out_shape=(jax.ShapeDtypeStruct((n * nt, LANES), F32),
                   jax.ShapeDtypeStruct((n, d), F32),
                   jax.ShapeDtypeStruct((N_EXPERTS, n), F32)),
        grid=(n // tm,),
        in_specs=[pl.BlockSpec((tm, d), row),
                  pl.BlockSpec((tm, attn.shape[1]), row),
                  pl.BlockSpec((yssm.shape[0], tm // SSM_CHUNK, yssm.shape[2]), lambda i: (0, i, 0)),
                  pl.BlockSpec((tm, d), row),
                  pl.BlockSpec((tm, d), row),
                  pl.BlockSpec((1, 6, d), lambda i: (i // tiles_per_seq, 0, 0)),
                  pl.BlockSpec((1, d), const)] + [pl.BlockSpec(a.shape, const) for a in weights],
        out_specs=(pl.BlockSpec((tm * nt, LANES), row), pl.BlockSpec((tm, d), row),
                   pl.BlockSpec((N_EXPERTS, tm), lambda i: (0, i))),
        scratch_shapes=[pltpu.VMEM((yssm.shape[0], tm, LANES), F32)],
        compiler_params=_cparams(1),
        name="mix",
    )(xf, attn, yssm, ga, gs, mod3, ln2_g.reshape(1, d), *weights)


def _route_kernel(score_ref, bias_ref, tri_ref, idx_ref, w_ref, rank_ref, cnt_ref, carry_ref):
    @pl.when(pl.program_id(0) == 0)
    def _():
        carry_ref[...] = jnp.zeros_like(carry_ref)

    scores = score_ref[...]
    ne, tn = scores.shape
    biased = scores + bias_ref[...]
    gsz = GROUP_SIZE
    sub = lax.broadcasted_iota(jnp.int32, (gsz, tn), 0)
    group_score = []
    for g in range(N_EXPERT_GROUPS):
        sg = biased[g * gsz:(g + 1) * gsz, :]
        m1 = jnp.max(sg, axis=0, keepdims=True)
        first = jnp.min(jnp.where(sg == m1, sub, gsz), axis=0, keepdims=True)
        m2 = jnp.max(jnp.where(sub == first, -jnp.inf, sg), axis=0, keepdims=True)
        group_score.append(m1 + m2)
    group_rows = []
    for g in range(N_EXPERT_GROUPS):
        beaten = jnp.zeros((1, tn), jnp.int32)
        for o in range(N_EXPERT_GROUPS):
            if o == g:
                continue
            wins = (group_score[o] > group_score[g])
            if o < g:
                wins = wins | (group_score[o] == group_score[g])
            beaten = beaten + jnp.where(wins, 1, 0)
        group_rows.append(jnp.broadcast_to(beaten < TOPK_GROUPS, (gsz, tn)))
    allowed = jnp.concatenate(group_rows, axis=0)
    cur = jnp.where(allowed, biased, MASK_VALUE)
    eio = lax.broadcasted_iota(jnp.int32, (ne, tn), 0)
    idx_rows, w_rows, hits = [], [], []
    for _ in range(TOP_K):
        vmax = jnp.max(cur, axis=0, keepdims=True)
        eidx = jnp.min(jnp.where(cur == vmax, eio, ne), axis=0, keepdims=True)
        hit = eio == eidx
        w_rows.append(jnp.sum(jnp.where(hit, scores, 0.0), axis=0, keepdims=True))
        idx_rows.append(eidx)
        hits.append(hit)
        cur = jnp.where(hit, -jnp.inf, cur)
    wts = jnp.concatenate(w_rows, axis=0)
    idx_ref[...] = jnp.concatenate(idx_rows, axis=0)
    w_ref[...] = wts / jnp.sum(wts, axis=0, keepdims=True) * ROUTED_SCALE

    onehot = jnp.zeros((ne, tn), F32)
    for hit in hits:
        onehot = onehot + jnp.where(hit, 1.0, 0.0)
    earlier = carry_ref[...] + _bdot(onehot, tri_ref[...])
    rank_rows = [jnp.sum(jnp.where(hit, earlier, 0.0), axis=0, keepdims=True) for hit in hits]
    rank_ref[...] = jnp.concatenate(rank_rows, axis=0).astype(jnp.int32)
    carry_ref[...] = carry_ref[...] + jnp.sum(onehot, axis=1, keepdims=True)
    cnt_ref[...] = carry_ref[...]


def _route(scores_t, router_bias):
    ne, n = scores_t.shape
    tn = ROUTE_TILE
    tri = jnp.asarray(np.arange(tn)[:, None] < np.arange(tn)[None, :], BF16)
    tok = lambda i: (0, i)
    const = lambda i: (0, 0)
    return pl.pallas_call(
        _route_kernel,
        out_shape=(jax.ShapeDtypeStruct((TOP_K, n), jnp.int32), jax.ShapeDtypeStruct((TOP_K, n), F32),
                   jax.ShapeDtypeStruct((TOP_K, n), jnp.int32), jax.ShapeDtypeStruct((ne, 1), F32)),
        grid=(n // tn,),
        in_specs=[pl.BlockSpec((ne, tn), tok), pl.BlockSpec((ne, 1), const), pl.BlockSpec((tn, tn), const)],
        out_specs=(pl.BlockSpec((TOP_K, tn), tok), pl.BlockSpec((TOP_K, tn), tok),
                   pl.BlockSpec((TOP_K, tn), tok), pl.BlockSpec((ne, 1), const)),
        scratch_shapes=[pltpu.VMEM((ne, 1), F32)],
        compiler_params=_cparams(1),
        name="route",
    )(scores_t, router_bias.reshape(ne, 1).astype(F32), tri)


def _dest_kernel(idx_ref, rank_ref, poffs_ref, dest_ref):
    idx = idx_ref[...]
    ne = poffs_ref.shape[0]
    eio = lax.broadcasted_iota(jnp.int32, (ne, idx.shape[1]), 0)
    poffs = poffs_ref[...]
    rows = [jnp.sum(jnp.where(eio == idx[kk:kk + 1, :], poffs, 0.0), axis=0, keepdims=True)
            for kk in range(idx.shape[0])]
    dest_ref[...] = rank_ref[...] + jnp.concatenate(rows, axis=0).astype(jnp.int32)


def _dest_slots(idx_t, rank_t, poffs):
    k, n = idx_t.shape
    tn = ROUTE_TILE
    ne = poffs.shape[0]
    tok = lambda i: (0, i)
    return pl.pallas_call(
        _dest_kernel,
        out_shape=jax.ShapeDtypeStruct((k, n), jnp.int32),
        grid=(n // tn,),
        in_specs=[pl.BlockSpec((k, tn), tok), pl.BlockSpec((k, tn), tok), pl.BlockSpec((ne, 1), lambda i: (0, 0))],
        out_specs=pl.BlockSpec((k, tn), tok),
        compiler_params=_cparams(1),
        name="dest_slots",
    )(idx_t, rank_t, poffs.astype(F32).reshape(ne, 1))


def _dispatch_tables(idx_t, counts, n_blocks):
    k, n = idx_t.shape
    n_assign = k * n
    rb = ROW_BLOCK
    ne = counts.shape[0]
    counts = counts.reshape(ne).astype(jnp.int32)
    offs = jnp.cumsum(counts) - counts
    nblk = (counts + rb - 1) // rb
    bend = jnp.cumsum(nblk)
    bstart = bend - nblk
    blocks = jnp.arange(n_blocks, dtype=jnp.int32)
    blk_e = jnp.minimum(jnp.sum((bend[None, :] <= blocks[:, None]).astype(jnp.int32), axis=1), ne - 1)
    mine = blk_e[:, None] == jnp.arange(ne, dtype=jnp.int32)[None, :]
    src_start = jnp.sum(jnp.where(mine, (offs - bstart * rb)[None, :], 0), axis=1) + blocks * rb
    tok = lax.broadcasted_iota(jnp.int32, (k, n), 1)
    kk = lax.broadcasted_iota(jnp.int32, (k, n), 0)
    assert ne * n_assign < 2 ** 31
    keys = jnp.sort((idx_t * n_assign + tok * k + kk).reshape(n_assign))
    tok_sorted = (keys % n_assign) // k
    return tok_sorted, blk_e, src_start, bend[ne - 1:ne], bstart * rb


def _ffn_kernel(tok_ref, blk_e_ref, start_ref, used_ref, h2_hbm, wg_ref, wu_ref, wd_ref, y_ref, xbuf, sems):
    b = pl.program_id(0)
    rb = ROW_BLOCK
    d = wg_ref.shape[1]
    nt = d // LANES
    n_assign = tok_ref.shape[0]
    n_used = used_ref[0]
    n_blocks = pl.num_programs(0)

    def start_rows(blk, s, inline):
        start = start_ref[blk]

        def one(r):
            tok = tok_ref[jnp.minimum(start + r, n_assign - 1)]
            pltpu.make_async_copy(h2_hbm.at[pl.ds(pl.multiple_of(tok * nt, nt), nt)],
                                  xbuf.at[s, pl.ds(pl.multiple_of(r * nt, nt), nt)], sems.at[s]).start()

        if inline:
            for r in range(rb):
                one(r)
        else:
            lax.fori_loop(0, rb, lambda r, c: (one(r), c)[1], 0, unroll=ISSUE_UNROLL)

    def wait_rows(s):
        pltpu.make_async_copy(h2_hbm.at[pl.ds(0, rb * nt)], xbuf.at[s], sems.at[s]).wait()

    nbuf = xbuf.shape[0]
    ahead = nbuf - 1

    @pl.when(b == 0)
    def _():
        for a in range(ahead):
            start_rows(jnp.minimum(a, n_blocks - 1), a, False)

    for s in range(nbuf):
        @pl.when((b < n_used) & (b % nbuf == s))
        def _(s=s):
            wait_rows(s)
            xb = _load_rows_from_tiles(xbuf.at[s], rb, d).astype(BF16)
            start_rows(jnp.minimum(b + ahead, n_blocks - 1), (s + ahead) % nbuf, True)
            hg = jnp.dot(xb, wg_ref[0].astype(BF16), preferred_element_type=F32)
            hu = jnp.dot(xb, wu_ref[0].astype(BF16), preferred_element_type=F32)
            _store_rows_as_tiles(y_ref, _bdot(_silu(hg) * hu, wd_ref[0]))

    @pl.when(b == n_used - 1)
    def _():
        for a in range(1, nbuf):
            wait_rows((b + a) % nbuf)

    @pl.when(b >= n_used)
    def _():
        y_ref[...] = jnp.zeros_like(y_ref)


def _expert_ffn(h2_tiles, tok_sorted, blk_e, src_start, n_used, w_gate, w_up, w_down):
    n_blocks = blk_e.shape[0]
    rb = ROW_BLOCK
    _, d, ed = w_gate.shape
    nt = d // LANES
    expert = lambda i, tok, be, st, nu: (be[i], 0, 0)
    grid_spec = pltpu.PrefetchScalarGridSpec(
        num_scalar_prefetch=4,
        grid=(n_blocks,),
        in_specs=[pl.BlockSpec(memory_space=pl.ANY),
                  pl.BlockSpec((1, d, ed), expert),
                  pl.BlockSpec((1, d, ed), expert),
                  pl.BlockSpec((1, ed, d), expert)],
        out_specs=pl.BlockSpec((rb * nt, LANES), lambda i, tok, be, st, nu: (i, 0)),
        scratch_shapes=[pltpu.VMEM((FFN_ROW_BUFFERS, rb * nt, LANES), F32),
                        pltpu.SemaphoreType.DMA((FFN_ROW_BUFFERS,))],
    )
    return pl.pallas_call(
        _ffn_kernel,
        out_shape=jax.ShapeDtypeStruct((n_blocks * rb * nt, LANES), F32),
        grid_spec=grid_spec,
        compiler_params=_cparams(1),
        name="expert_ffn",
    )(tok_sorted, blk_e, src_start, n_used, h2_tiles, w_gate, w_up, w_down)


def _combine_kernel(dest_ref, y_hbm, base_ref, wt_ref, mod_ref, o_ref, buf, sems):
    i = pl.program_id(0)
    n_steps = pl.num_programs(0)
    tm, d = base_ref.shape
    k = wt_ref.shape[1]
    nt = d // LANES
    n_rows = k * tm
    slot = i % 2

    def issue(step, s):
        first = step * n_rows

        def body(r, _):
            src = dest_ref[first + r]
            pltpu.make_async_copy(y_hbm.at[pl.ds(pl.multiple_of(src * nt, nt), nt)],
                                  buf.at[s, pl.ds(pl.multiple_of(r * nt, nt), nt)], sems.at[s]).start()
            return 0

        lax.fori_loop(0, n_rows, body, 0, unroll=ISSUE_UNROLL)

    @pl.when(i == 0)
    def _():
        issue(0, 0)

    @pl.when(i + 1 < n_steps)
    def _():
        issue(i + 1, 1 - slot)

    pltpu.make_async_copy(y_hbm.at[pl.ds(0, n_rows * nt)], buf.at[slot], sems.at[slot]).wait()
    rows = buf.at[slot]
    wts = wt_ref[...]
    gate2 = mod_ref[0, 5:6, :]
    for c in range(nt):
        cols = slice(c * LANES, (c + 1) * LANES)
        routed = jnp.zeros((tm, LANES), F32)
        for kk in range(k):
            routed = routed + wts[:, kk:kk + 1] * rows[pl.ds(kk * tm * nt + c, tm, stride=nt), :]
        o_ref[:, cols] = base_ref[:, cols] + gate2[:, cols] * routed


def _combine(y_tiles, dest_t, w_t, base, mod3, seq):
    n, d = base.shape
    tm = COMBINE_TILE
    k = dest_t.shape[0]
    nt = d // LANES
    tiles_per_seq = seq // tm
    dest_tiles = dest_t.reshape(k, n // tm, tm).transpose(1, 0, 2).reshape(n * k)
    grid_spec = pltpu.PrefetchScalarGridSpec(
        num_scalar_prefetch=1,
        grid=(n // tm,),
        in_specs=[pl.BlockSpec(memory_space=pl.ANY),
                  pl.BlockSpec((tm, d), lambda i, dst: (i, 0)),
                  pl.BlockSpec((tm, k), lambda i, dst: (i, 0)),
                  pl.BlockSpec((1, 6, d), lambda i, dst: (i // tiles_per_seq, 0, 0))],
        out_specs=pl.BlockSpec((tm, d), lambda i, dst: (i, 0)),
        scratch_shapes=[pltpu.VMEM((2, k * tm * nt, LANES), F32), pltpu.SemaphoreType.DMA((2,))],
    )
    return pl.pallas_call(
        _combine_kernel,
        out_shape=jax.ShapeDtypeStruct((n, d), F32),
        grid_spec=grid_spec,
        compiler_params=_cparams(1),
        name="combine",
    )(dest_tiles, y_tiles, base, w_t.T, mod3)


def _hybrid_layer(x, cond, rel_bias, w_ada, b_ada, ln1_g, w_in, q_norm_g, k_norm_g,
                  ssm_lambda_re, ssm_lambda_im, ssm_log_dt, ssm_b_re, ssm_b_im, ssm_c_re, ssm_c_im,
                  ssm_d, ssm_w_glu, ssm_b_glu, w_up_attn, w_up_ssm, w_out, ln2_g,
                  w_router, router_bias, w_exp_gate, w_exp_up, w_exp_down,
                  w_sh_gate, w_sh_up, w_sh_down):
    bsz, seq, d = x.shape
    n = bsz * seq
    xf = x.reshape(n, d)
    mod3 = _adaln(cond, w_ada, b_ada).reshape(bsz, 6, d)

    q_gain = jnp.tile(q_norm_g.astype(F32), ATTN_HEADS).reshape(1, ATTN_WIDTH)
    k_gain = jnp.tile(k_norm_g.astype(F32), ATTN_HEADS).reshape(1, ATTN_WIDTH)
    q, k, v, u, ga, gs, kmean = _inproj(xf, mod3, ln1_g, w_in.astype(BF16), q_gain, k_gain, seq)

    attn = _moba_attention(q, k, v, kmean, rel_bias, bsz, seq)
    ops = _s5_operators(ssm_lambda_re, ssm_lambda_im, ssm_log_dt, ssm_b_re, ssm_b_im,
                        ssm_c_re, ssm_c_im, ssm_d, seq // SSM_CHUNK)
    yssm = _s5_scan(u, ops, bsz, seq)

    weights = {
        "glu": ssm_w_glu.astype(BF16), "b_glu": ssm_b_glu.astype(F32).reshape(1, -1),
        "up_attn": w_up_attn.astype(BF16), "up_ssm": w_up_ssm.astype(BF16), "out": w_out.astype(BF16),
        "router_t": w_router.T.astype(BF16),
        "sh_gu": jnp.concatenate([w_sh_gate, w_sh_up], axis=1).astype(BF16),
        "sh_down": w_sh_down.astype(BF16),
    }
    h2, base, scores_t = _mix(xf, attn, yssm, ga, gs, mod3, ln2_g, weights, seq)

    idx_t, w_t, rank_t, counts = _route(scores_t, router_bias)
    n_blocks = -(-(n * TOP_K) // ROW_BLOCK) + N_EXPERTS
    tok_sorted, blk_e, src_start, n_used, poffs = _dispatch_tables(idx_t, counts, n_blocks)
    dest_t = _dest_slots(idx_t, rank_t, poffs)
    y_tiles = _expert_ffn(h2, tok_sorted, blk_e, src_start, n_used, w_exp_gate, w_exp_up, w_exp_down)
    out = _combine(y_tiles, dest_t, w_t, base, mod3, seq)
    return out.reshape(bsz, seq, d)


def kernel(x, c, rel_bias, w_ada, b_ada, ln1_g, w_in, q_norm_g, k_norm_g, ssm_lambda_re, ssm_lambda_im, ssm_log_dt, ssm_b_re, ssm_b_im, ssm_c_re, ssm_c_im, ssm_d, ssm_w_glu, ssm_b_glu, w_up_attn, w_up_ssm, w_out, ln2_g, w_router, router_bias, w_exp_gate, w_exp_up, w_exp_down, w_sh_gate, w_sh_up, w_sh_down):
    for l in range(w_ada.shape[0]):
        x = _hybrid_layer(x, c, rel_bias, w_ada[l], b_ada[l], ln1_g[l], w_in[l], q_norm_g[l], k_norm_g[l],
                          ssm_lambda_re[l], ssm_lambda_im[l], ssm_log_dt[l], ssm_b_re[l], ssm_b_im[l],
                          ssm_c_re[l], ssm_c_im[l], ssm_d[l], ssm_w_glu[l], ssm_b_glu[l],
                          w_up_attn[l], w_up_ssm[l], w_out[l], ln2_g[l], w_router[l], router_bias[l],
                          w_exp_gate[l], w_exp_up[l], w_exp_down[l], w_sh_gate[l], w_sh_up[l], w_sh_down[l])
    return x
```

```python
import functools
import math

import numpy as np
import jax
import jax.numpy as jnp
from jax import lax
from jax.experimental import pallas as pl
from jax.experimental.pallas import tpu as pltpu

F32 = jnp.float32
BF16 = jnp.bfloat16

ATTN_HEADS = 8
HEAD_DIM = 64
ATTN_WIDTH = ATTN_HEADS * HEAD_DIM
MOBA_BLOCK = 256
MOBA_TOPK = 3
NUM_BUCKETS = 32
MAX_DISTANCE = 128
SSM_WIDTH = 512
SSM_GROUP = 16
SSM_GROUPS = SSM_WIDTH // SSM_GROUP
SSM_STATE = 64
N_EXPERTS = 256
TOP_K = 8
N_EXPERT_GROUPS = 8
TOPK_GROUPS = 4
GROUP_SIZE = N_EXPERTS // N_EXPERT_GROUPS
EXPERT_DIM = 256
ROUTED_SCALE = 2.5
EPS = 1e-6
MASK_VALUE = -1e30

LANES = 128
HEADS_PER_STEP = LANES // HEAD_DIM
SSM_CHUNK = 16
SUBLANES = 8
BF16_ROWS = 16
ROW_BLOCK = 256
TOKEN_TILE = 512
ROUTE_TILE = 256
COMBINE_TILE = 128
ISSUE_UNROLL = 8
ROW_COPY_PRIORITY = 1
FFN_ROW_BUFFERS = 3
VMEM_LIMIT = 56 * 1024 * 1024


def _store_rows_as_tiles(ref, val):
    rows, d = val.shape
    nt = d // LANES
    for c in range(nt):
        ref[pl.ds(c, rows, stride=nt), :] = val[:, c * LANES:(c + 1) * LANES]


def _load_rows_from_tiles(ref, rows, d):
    nt = d // LANES
    return jnp.concatenate([ref[pl.ds(c, rows, stride=nt), :] for c in range(nt)], axis=1)


def _cparams(n_axes, vmem=VMEM_LIMIT):
    return pltpu.CompilerParams(dimension_semantics=("arbitrary",) * n_axes, vmem_limit_bytes=vmem)


def _sigmoid(x):
    return 1.0 / (1.0 + jnp.exp(-x))


def _silu(x):
    return x * _sigmoid(x)


def _bdot(a, b):
    return jnp.dot(a.astype(BF16), b.astype(BF16), preferred_element_type=F32)


def _bdot_nt(a, b):
    return lax.dot_general(a.astype(BF16), b.astype(BF16), (((1,), (1,)), ((), ())),
                           preferred_element_type=F32)


def _adaln_kernel(c_ref, w_ref, b_ref, o_ref):
    o_ref[...] = _bdot(_silu(c_ref[...]), w_ref[...]) + b_ref[...]


def _adaln(c, w_ada, b_ada):
    bsz, d = c.shape
    n_out = w_ada.shape[1]
    return pl.pallas_call(
        _adaln_kernel,
        out_shape=jax.ShapeDtypeStruct((bsz, n_out), F32),
        grid=(n_out // d,),
        in_specs=[pl.BlockSpec((bsz, d), lambda j: (0, 0)),
                  pl.BlockSpec((d, d), lambda j: (0, j)),
                  pl.BlockSpec((1, d), lambda j: (0, j))],
        out_specs=pl.BlockSpec((bsz, d), lambda j: (0, j)),
        compiler_params=_cparams(1),
        name="adaln",
    )(c, w_ada, b_ada.reshape(1, n_out))


def _modulated_norm(x, gain, shift, scale):
    y = x * lax.rsqrt(jnp.mean(x * x, axis=-1, keepdims=True) + EPS) * gain
    return y * (1.0 + scale) + shift


def _head_norm(t, seg, gain):
    ms = _bdot(t * t, seg)
    return t * lax.rsqrt(ms + EPS) * gain


def _inproj_kernel(x_ref, mod_ref, ln_ref, w_ref, seg_ref, qg_ref, kg_ref,
                   q_ref, k_ref, v_ref, u_ref, ga_ref, gs_ref, km_ref, u_scr):
    aw, sw, d = ATTN_WIDTH, SSM_WIDTH, x_ref.shape[1]
    h = _modulated_norm(x_ref[...], ln_ref[...], mod_ref[0, 0:1, :], mod_ref[0, 1:2, :]).astype(BF16)
    seg = seg_ref[...]
    q = jnp.dot(h, w_ref[:, 0:aw], preferred_element_type=F32)
    q_ref[...] = _head_norm(q, seg, qg_ref[...])
    k = jnp.dot(h, w_ref[:, aw:2 * aw], preferred_element_type=F32)
    kn = _head_norm(k, seg, kg_ref[...])
    k_ref[...] = kn.astype(BF16)
    for blk in range(km_ref.shape[0]):
        km_ref[blk] = jnp.mean(kn[blk * MOBA_BLOCK:(blk + 1) * MOBA_BLOCK, :], axis=0, keepdims=True)
    v_ref[...] = jnp.dot(h, w_ref[:, 2 * aw:3 * aw], preferred_element_type=F32).astype(BF16)
    o = 3 * aw
    u = jnp.dot(h, w_ref[:, o:o + sw], preferred_element_type=F32)
    n_chunk = u_scr.shape[1] // SSM_CHUNK
    for cb in range(sw // LANES):
        u_scr[cb] = u[:, cb * LANES:(cb + 1) * LANES]
        for sg in range(SSM_CHUNK):
            u_ref[cb, :, sg * LANES:(sg + 1) * LANES] = (
                u_scr[cb, pl.ds(sg, n_chunk, stride=SSM_CHUNK), :].astype(BF16))
    o += sw
    ga_ref[...] = jnp.dot(h, w_ref[:, o:o + d], preferred_element_type=F32).astype(BF16)
    o += d
    gs_ref[...] = jnp.dot(h, w_ref[:, o:o + d], preferred_element_type=F32).astype(BF16)


def _inproj(xf, mod3, ln1_g, w_in_b, q_gain, k_gain, seq):
    n, d = xf.shape
    tm = TOKEN_TILE
    assert tm % MOBA_BLOCK == 0 and seq % tm == 0
    blocks_per_tile = tm // MOBA_BLOCK
    tiles_per_seq = seq // tm
    aw, sw = ATTN_WIDTH, SSM_WIDTH
    head_of_lane = np.arange(aw) // HEAD_DIM
    seg = jnp.asarray((head_of_lane[:, None] == head_of_lane[None, :]) / HEAD_DIM, BF16)
    row = lambda i: (i, 0)
    const = lambda i: (0, 0)
    return pl.pallas_call(
        _inproj_kernel,
        out_shape=(jax.ShapeDtypeStruct((n, aw), F32),
                   jax.ShapeDtypeStruct((n, aw), BF16),
                   jax.ShapeDtypeStruct((n, aw), BF16),
                   jax.ShapeDtypeStruct((sw // LANES, n // SSM_CHUNK, SSM_CHUNK * LANES), BF16),
                   jax.ShapeDtypeStruct((n, d), BF16),
                   jax.ShapeDtypeStruct((n, d), BF16),
                   jax.ShapeDtypeStruct((n // MOBA_BLOCK, 1, aw), F32)),
        grid=(n // tm,),
        in_specs=[pl.BlockSpec((tm, d), row),
                  pl.BlockSpec((1, 6, d), lambda i: (i // tiles_per_seq, 0, 0)),
                  pl.BlockSpec((1, d), const),
                  pl.BlockSpec(w_in_b.shape, const),
                  pl.BlockSpec((aw, aw), const),
                  pl.BlockSpec((1, aw), const),
                  pl.BlockSpec((1, aw), const)],
        out_specs=(pl.BlockSpec((tm, aw), row), pl.BlockSpec((tm, aw), row), pl.BlockSpec((tm, aw), row),
                   pl.BlockSpec((sw // LANES, tm // SSM_CHUNK, SSM_CHUNK * LANES), lambda i: (0, i, 0)),
                   pl.BlockSpec((tm, d), row), pl.BlockSpec((tm, d), row),
                   pl.BlockSpec((blocks_per_tile, 1, aw), lambda i: (i, 0, 0))),
        scratch_shapes=[pltpu.VMEM((sw // LANES, tm, LANES), F32)],
        compiler_params=_cparams(1),
        name="inproj",
    )(xf, mod3, ln1_g.reshape(1, d), w_in_b, seg, q_gain, k_gain)


def _t5_bucket(rel):
    n = jnp.maximum(rel, 0)
    max_exact = NUM_BUCKETS // 2
    nf = jnp.maximum(n, 1).astype(F32)
    large = max_exact + (jnp.log(nf / max_exact) / math.log(MAX_DISTANCE / max_exact)
                         * (NUM_BUCKETS - max_exact)).astype(jnp.int32)
    large = jnp.minimum(large, NUM_BUCKETS - 1)
    return jnp.where(n < max_exact, n, large)


def _bias_tables(rel_bias):
    blk = MOBA_BLOCK
    assert blk + 1 >= MAX_DISTANCE
    rel = jnp.arange(blk)[None, :] - jnp.arange(blk)[:, None]
    table = rel_bias.astype(F32)
    table = table - table[NUM_BUCKETS - 1][None, :]

    def lookup(r):
        onehot = jax.nn.one_hot(_t5_bucket(r), NUM_BUCKETS, dtype=F32)
        return jnp.einsum('kqn,nh->hkq', onehot, table, precision=lax.Precision.HIGHEST)

    return lookup(rel), lookup(rel + blk)


def _select_blocks(gate_t, n_past):
    nb, tq = gate_t.shape
    blk = lax.broadcasted_iota(jnp.int32, (nb, tq), 0)
    beaten = jnp.zeros((nb, tq), jnp.int32)
    for m in range(nb):
        gm = gate_t[m:m + 1, :]
        wins = (gm > gate_t) | ((gm == gate_t) & (m < blk))
        beaten = beaten + jnp.where(wins & (m < n_past), 1, 0)
    return jnp.where((blk < n_past) & (beaten < MOBA_TOPK), 1.0, 0.0)


def _attn_kernel(q_ref, k_ref, vt_ref, km_ref, bias_ref, o_ref, sel_ref, s_ref):
    qi = pl.program_id(2)
    tq = q_ref.shape[0]
    blk = MOBA_BLOCK
    hd = HEAD_DIM
    heads = range(HEADS_PER_STEP)
    q = q_ref[...]
    lane = lax.broadcasted_iota(jnp.int32, (tq, LANES), 1)
    kpos = lax.broadcasted_iota(jnp.int32, (blk, tq), 0)
    qpos = lax.broadcasted_iota(jnp.int32, (blk, tq), 1)
    scale = hd ** -0.5
    n_far = jnp.maximum(qi - 1, 0)
    n_pairs = (n_far + 1) // 2
    jp = jnp.maximum(qi - 1, 0)
    k_own = k_ref[pl.ds(pl.multiple_of(qi * blk, blk), blk), :]
    k_prev = k_ref[pl.ds(pl.multiple_of(jp * blk, blk), blk), :]

    qbs = []
    for h in heads:
        in_head = (lane >= h * hd) & (lane < (h + 1) * hd)
        qm = jnp.where(in_head, q, 0.0)
        gate_t = lax.dot_general(km_ref[0], qm, (((1,), (1,)), ((), ())),
                                 precision=lax.Precision.HIGHEST, preferred_element_type=F32)
        sel_ref[h] = _select_blocks(gate_t, qi)
        qbs.append((qm * scale).astype(BF16))

    def pair_scores(j):
        kb = k_ref[pl.ds(pl.multiple_of(j * blk, blk), 2 * blk), :]
        return [_bdot_nt(kb, qbs[h]) for h in heads]

    ones_rows = jnp.ones((BF16_ROWS, blk), BF16)

    def probs(s, m):
        return jnp.exp((s - m).astype(BF16))

    def attend(h, p, blocks):
        acc = None
        for i, j in enumerate(blocks):
            lhs = jnp.concatenate([vt_ref[0, j, h * hd:(h + 1) * hd, :], ones_rows], axis=0)
            part = jnp.dot(lhs, p[i * blk:(i + 1) * blk, :], preferred_element_type=F32)
            acc = part if acc is None else acc + part
        return acc

    for h, s in enumerate(pair_scores(0)):
        s_ref[h] = s

    carries = []
    for h in heads:
        s_prev = _bdot_nt(k_prev, qbs[h]) + bias_ref[h, 0:blk, :]
        s_prev = jnp.where(sel_ref[h, pl.ds(jp, 1), :] > 0.5, s_prev, MASK_VALUE)
        s_own = _bdot_nt(k_own, qbs[h]) + bias_ref[h, blk:2 * blk, :]
        s_own = jnp.where(kpos <= qpos, s_own, MASK_VALUE)
        s = jnp.concatenate([s_prev, s_own], axis=0)
        m = jnp.max(s, axis=0, keepdims=True)
        carries.append((m, attend(h, probs(s, m), (jp, qi))))

    def far_pair(pi, carries):
        j = 2 * pi
        s_cur = [s_ref[h] for h in heads]
        for h, s in enumerate(pair_scores(2 * jnp.minimum(pi + 1, n_pairs - 1))):
            s_ref[h] = s
        second_is_far = j + 1 < n_far
        out = []
        for h in heads:
            m, acc = carries[h]
            c0 = sel_ref[h, pl.ds(j, 1), :] > 0.5
            c1 = (sel_ref[h, pl.ds(j + 1, 1), :] > 0.5) & second_is_far
            chosen = jnp.concatenate([jnp.broadcast_to(c0, (blk, tq)), jnp.broadcast_to(c1, (blk, tq))], axis=0)
            s = jnp.where(chosen, s_cur[h], MASK_VALUE)
            m_new = jnp.maximum(m, jnp.max(s, axis=0, keepdims=True))
            acc = jnp.exp(m - m_new) * acc + attend(h, probs(s, m_new), (j, j + 1))
            out.append((m_new, acc))
        return tuple(out)

    carries = lax.fori_loop(0, n_pairs, far_pair, tuple(carries))
    out_t = jnp.concatenate([acc[:hd] / acc[hd:hd + 1] for _, acc in carries], axis=0)
    o_ref[...] = out_t.T.astype(o_ref.dtype)


def _moba_attention(q, k, v, kmean, rel_bias, bsz, seq):
    n, aw = q.shape
    blk = MOBA_BLOCK
    nb = seq // blk
    assert nb >= 2
    own, prev = _bias_tables(rel_bias)
    bias = jnp.concatenate([prev, own], axis=1)
    hps = HEADS_PER_STEP
    npair = aw // LANES
    vt = v.reshape(bsz, nb, blk, aw).transpose(0, 1, 3, 2)
    return pl.pallas_call(
        _attn_kernel,
        out_shape=jax.ShapeDtypeStruct((n, aw), BF16),
        grid=(bsz, npair, nb),
        in_specs=[pl.BlockSpec((blk, LANES), lambda b, hp, qi: (b * nb + qi, hp)),
                  pl.BlockSpec((seq, LANES), lambda b, hp, qi: (b, hp)),
                  pl.BlockSpec((1, nb, LANES, blk), lambda b, hp, qi: (b, 0, hp, 0)),
                  pl.BlockSpec((1, nb, LANES), lambda b, hp, qi: (b, 0, hp)),
                  pl.BlockSpec((hps, 2 * blk, blk), lambda b, hp, qi: (hp, 0, 0))],
        out_specs=pl.BlockSpec((blk, LANES), lambda b, hp, qi: (b * nb + qi, hp)),
        scratch_shapes=[pltpu.VMEM((hps, nb, blk), F32), pltpu.VMEM((hps, 2 * blk, blk), F32)],
        compiler_params=_cparams(3),
        name="moba_attention",
    )(q, k, vt, kmean.reshape(bsz, nb, aw), bias)


def _s5_operators(lambda_re, lambda_im, log_dt, b_re, b_im, c_re, c_im, d_skip, n_chunks):
    hi = lax.Precision.HIGHEST
    L, G, P, C = SSM_CHUNK, SSM_GROUPS, SSM_STATE, SSM_GROUP
    lam_re = jnp.minimum(lambda_re.astype(F32), -1e-4)
    lam_im = lambda_im.astype(F32)
    dt = jnp.exp(log_dt.astype(F32))[:, None]
    z_re, z_im = lam_re * dt, lam_im * dt

    def a_pow(nvec):
        nv = jnp.asarray(nvec, F32)[:, None, None]
        mag = jnp.exp(nv * z_re)
        return mag * jnp.cos(nv * z_im), mag * jnp.sin(nv * z_im)

    a_re, a_im = a_pow([1.0])
    a_re, a_im = a_re[0], a_im[0]
    den = lam_re * lam_re + lam_im * lam_im
    nr = a_re - 1.0
    coef_re = (nr * lam_re + a_im * lam_im) / den
    coef_im = (a_im * lam_re - nr * lam_im) / den
    br, bi = b_re.astype(F32), b_im.astype(F32)
    bbar_re = coef_re[..., None] * br - coef_im[..., None] * bi
    bbar_im = coef_re[..., None] * bi + coef_im[..., None] * br
    cr, ci = c_re.astype(F32), c_im.astype(F32)

    pw_re, pw_im = a_pow(np.arange(L + 1))
    cb_re = cr[None] * pw_re[:, :, None, :] - ci[None] * pw_im[:, :, None, :]
    cb_im = cr[None] * pw_im[:, :, None, :] + ci[None] * pw_re[:, :, None, :]
    kern = (jnp.einsum('jgop,gpi->gijo', cb_re[:L], bbar_re, precision=hi)
            - jnp.einsum('jgop,gpi->gijo', cb_im[:L], bbar_im, precision=hi)).reshape(G, C, L * C)
    t_op = jnp.stack([jnp.pad(kern[:, :, :(L - s) * C], ((0, 0), (0, 0), (s * C, 0))) for s in range(L)], axis=1)
    d_g = d_skip.astype(F32).reshape(G, 1, C, 1)
    on_diag = (lax.broadcasted_iota(jnp.int32, (1, L, C, L * C), 3)
               == lax.broadcasted_iota(jnp.int32, (1, L, C, L * C), 1) * C
               + lax.broadcasted_iota(jnp.int32, (1, L, C, L * C), 2))
    t_op = (t_op + jnp.where(on_diag, d_g, 0.0)).reshape(G, L * C, L * C)

    rp_re, rp_im = jnp.flip(pw_re[:L], 0), jnp.flip(pw_im[:L], 0)
    p_re = rp_re[..., None] * bbar_re[None] - rp_im[..., None] * bbar_im[None]
    p_im = rp_re[..., None] * bbar_im[None] + rp_im[..., None] * bbar_re[None]
    p_op = jnp.concatenate([p_re, p_im], axis=2)
    p_op = p_op.transpose(1, 0, 3, 2).reshape(G, L * C, 2 * P)

    q_re = cb_re[1:].transpose(1, 3, 0, 2)
    q_im = -cb_im[1:].transpose(1, 3, 0, 2)
    q_op = jnp.concatenate([q_re, q_im], axis=1).reshape(G, 2 * P, L * C)

    n_steps = max(1, int(math.ceil(math.log2(n_chunks))))
    dk_re, dk_im = a_pow([float(L * 2 ** k) for k in range(n_steps)])
    GB = LANES // C
    NB = G // GB
    lc = np.arange(L * C)
    wide = np.arange(L * LANES)
    expand_tc = jnp.asarray((lc[:, None] // C == wide[None, :] // LANES) & (lc[:, None] % C == wide[None, :] % C), BF16)
    st = np.arange(2 * P)
    wide_st = np.arange(2 * GB * P)
    expand_st = jnp.asarray((st[:, None] // P == wide_st[None, :] // (GB * P))
                            & (st[:, None] % P == wide_st[None, :] % P), BF16)
    g_of_wide = (jnp.arange(L * LANES) // C) % GB
    g_of_state = (jnp.arange(2 * GB * P) // P) % GB

    def widen(rows, expand, g_row, g_col):
        full = jnp.einsum('brk,kc->brc', rows.astype(BF16), expand, preferred_element_type=F32)
        return jnp.where(g_row[:, None] == g_col[None, :], full, 0.0).astype(BF16)

    t_rows = t_op.reshape(NB, GB, L, C, L * C).transpose(0, 2, 1, 3, 4).reshape(NB, L * LANES, L * C)
    p_rows = p_op.reshape(NB, GB, L, C, 2 * P).transpose(0, 2, 1, 3, 4).reshape(NB, L * LANES, 2 * P)
    q_rows = q_op.reshape(NB, GB, 2, P, L * C).transpose(0, 2, 1, 3, 4).reshape(NB, 2 * GB * P, L * C)
    t_big = widen(t_rows, expand_tc, g_of_wide, g_of_wide)
    p_big = widen(p_rows, expand_st, g_of_wide, g_of_state)
    q_big = widen(q_rows, expand_tc, g_of_state, g_of_wide)
    dk_re = dk_re.reshape(-1, NB, GB * P)
    dk_im = dk_im.reshape(-1, NB, GB * P)
    a1 = jnp.concatenate([dk_re, dk_re], axis=-1).transpose(1, 0, 2)
    a2 = jnp.concatenate([-dk_im, dk_im], axis=-1).transpose(1, 0, 2)
    return t_big.astype(BF16), p_big.astype(BF16), q_big.astype(BF16), a1, a2


def _s5_kernel(x_ref, t_ref, p_ref, q_ref, a1_ref, a2_ref, y_ref):
    x = x_ref[0]
    s = jnp.dot(x, p_ref[0], preferred_element_type=F32)
    n_chunks, width = s.shape
    chunk = lax.broadcasted_iota(jnp.int32, (n_chunks, width), 0)
    a1 = a1_ref[0]
    a2 = a2_ref[0]
    h = jnp.where(chunk >= 1, pltpu.roll(s, 1, axis=0), 0.0)
    for kk in range(a1.shape[0]):
        dist = 2 ** kk
        if dist >= n_chunks:
            break
        hs = jnp.where(chunk >= dist, pltpu.roll(h, dist, axis=0), 0.0)
        h = h + a1[kk:kk + 1, :] * hs + a2[kk:kk + 1, :] * pltpu.roll(hs, width // 2, axis=1)
    hb = h.astype(BF16)
    step = 2 * LANES
    for t in range(x.shape[1] // step):
        hi = (t + 1) * step
        y_ref[0, :, t * step:hi] = (jnp.dot(x[:, :hi], t_ref[0, :hi, t * step:hi], preferred_element_type=F32)
                                    + jnp.dot(hb, q_ref[0, :, t * step:hi], preferred_element_type=F32))


def _s5_scan(x_chunks, ops, bsz, seq):
    t_big, p_big, q_big, a1, a2 = ops
    nblk, rows, w = x_chunks.shape
    nc = seq // SSM_CHUNK
    sw = p_big.shape[2]
    col = lambda cb, b: (cb, 0, 0)
    return pl.pallas_call(
        _s5_kernel,
        out_shape=jax.ShapeDtypeStruct((nblk, rows, w), F32),
        grid=(nblk, bsz),
        in_specs=[pl.BlockSpec((1, nc, w), lambda cb, b: (cb, b, 0)),
                  pl.BlockSpec((1, w, w), col),
                  pl.BlockSpec((1, w, sw), col),
                  pl.BlockSpec((1, sw, w), col),
                  pl.BlockSpec((1,) + a1.shape[1:], col),
                  pl.BlockSpec((1,) + a2.shape[1:], col)],
        out_specs=pl.BlockSpec((1, nc, w), lambda cb, b: (cb, b, 0)),
        compiler_params=_cparams(2),
        name="s5_scan",
    )(x_chunks, t_big, p_big, q_big, a1, a2)


def _gelu_tanh(x):
    return 0.5 * x * (1.0 + jnp.tanh(math.sqrt(2.0 / math.pi) * (x + 0.044715 * (x * x * x))))


def _mix_kernel(x_ref, attn_ref, yssm_ref, ga_ref, gs_ref, mod_ref, ln_ref,
                wglu_ref, bglu_ref, wua_ref, wus_ref, wout_ref, wrt_ref, wsgu_ref, wsd_ref,
                h2_ref, base_ref, score_ref, y_scr):
    n_chunk = yssm_ref.shape[1]
    for cb in range(yssm_ref.shape[0]):
        for tau in range(SSM_CHUNK):
            y_scr[cb, pl.ds(tau, n_chunk, stride=SSM_CHUNK), :] = yssm_ref[cb, :, tau * LANES:(tau + 1) * LANES]
    g = _gelu_tanh(jnp.concatenate([y_scr[cb] for cb in range(yssm_ref.shape[0])], axis=1))
    glu = g * _sigmoid(_bdot(g, wglu_ref[...]) + bglu_ref[...])
    y_attn = jnp.dot(attn_ref[...], wua_ref[...], preferred_element_type=F32)
    y_ssm = _bdot(glu, wus_ref[...])
    mixed = _sigmoid(ga_ref[...].astype(F32)) * y_attn + _sigmoid(gs_ref[...].astype(F32)) * y_ssm
    gate1 = mod_ref[0, 2:3, :]
    x1 = x_ref[...] + gate1 * _bdot(mixed, wout_ref[...])
    h2 = _modulated_norm(x1, ln_ref[...], mod_ref[0, 3:4, :], mod_ref[0, 4:5, :])
    _store_rows_as_tiles(h2_ref, h2)
    h2b = h2.astype(BF16)
    score_ref[...] = _sigmoid(_bdot_nt(wrt_ref[...], h2b))
    gu = jnp.dot(h2b, wsgu_ref[...], preferred_element_type=F32)
    sd = wsd_ref.shape[0]
    shared = _bdot(_silu(gu[:, :sd]) * gu[:, sd:], wsd_ref[...])
    base_ref[...] = x1 + mod_ref[0, 5:6, :] * shared


def _mix(xf, attn, yssm, ga, gs, mod3, ln2_g, w, seq):
    n, d = xf.shape
    tm = TOKEN_TILE
    tiles_per_seq = seq // tm
    row = lambda i: (i, 0)
    const = lambda i: (0, 0)
    nt = d // LANES
    weights = [w["glu"], w["b_glu"], w["up_attn"], w["up_ssm"], w["out"], w["router_t"], w["sh_gu"], w["sh_down"]]
    return pl.pallas_call(
        _mix_kernel,
        out_shape=(jax.ShapeDtypeStruct((n * nt, LANES), F32),
                   jax.ShapeDtypeStruct((n, d), F32),
                   jax.ShapeDtypeStruct((N_EXPERTS, n), F32)),
        grid=(n // tm,),
        in_specs=[pl.BlockSpec((tm, d), row),
                  pl.BlockSpec((tm, attn.shape[1]), row),
                  pl.BlockSpec((yssm.shape[0], tm // SSM_CHUNK, yssm.shape[2]), lambda i: (0, i, 0)),
                  pl.BlockSpec((tm, d), row),
                  pl.BlockSpec((tm, d), row),
                  pl.BlockSpec((1, 6, d), lambda i: (i // tiles_per_seq, 0, 0)),
                  pl.BlockSpec((1, d), const)] + [pl.BlockSpec(a.shape, const) for a in weights],
        out_specs=(pl.BlockSpec((tm * nt, LANES), row), pl.BlockSpec((tm, d), row),
                   pl.BlockSpec((N_EXPERTS, tm), lambda i: (0, i))),
        scratch_shapes=[pltpu.VMEM((yssm.shape[0], tm, LANES), F32)],
        compiler_params=_cparams(1),
        name="mix",
    )(xf, attn, yssm, ga, gs, mod3, ln2_g.reshape(1, d), *weights)


def _route_kernel(score_ref, bias_ref, tri_ref, idx_ref, w_ref, rank_ref, cnt_ref, carry_ref):
    @pl.when(pl.program_id(0) == 0)
    def _():
        carry_ref[...] = jnp.zeros_like(carry_ref)

    scores = score_ref[...]
    ne, tn = scores.shape
    biased = scores + bias_ref[...]
    gsz = GROUP_SIZE
    sub = lax.broadcasted_iota(jnp.int32, (gsz, tn), 0)
    group_score = []
    for g in range(N_EXPERT_GROUPS):
        sg = biased[g * gsz:(g + 1) * gsz, :]
        m1 = jnp.max(sg, axis=0, keepdims=True)
        first = jnp.min(jnp.where(sg == m1, sub, gsz), axis=0, keepdims=True)
        m2 = jnp.max(jnp.where(sub == first, -jnp.inf, sg), axis=0, keepdims=True)
        group_score.append(m1 + m2)
    group_rows = []
    for g in range(N_EXPERT_GROUPS):
        beaten = jnp.zeros((1, tn), jnp.int32)
        for o in range(N_EXPERT_GROUPS):
            if o == g:
                continue
            wins = (group_score[o] > group_score[g])
            if o < g:
                wins = wins | (group_score[o] == group_score[g])
            beaten = beaten + jnp.where(wins, 1, 0)
        group_rows.append(jnp.broadcast_to(beaten < TOPK_GROUPS, (gsz, tn)))
    allowed = jnp.concatenate(group_rows, axis=0)
    cur = jnp.where(allowed, biased, MASK_VALUE)
    eio = lax.broadcasted_iota(jnp.int32, (ne, tn), 0)
    idx_rows, w_rows, hits = [], [], []
    for _ in range(TOP_K):
        vmax = jnp.max(cur, axis=0, keepdims=True)
        eidx = jnp.min(jnp.where(cur == vmax, eio, ne), axis=0, keepdims=True)
        hit = eio == eidx
        w_rows.append(jnp.sum(jnp.where(hit, scores, 0.0), axis=0, keepdims=True))
        idx_rows.append(eidx)
        hits.append(hit)
        cur = jnp.where(hit, -jnp.inf, cur)
    wts = jnp.concatenate(w_rows, axis=0)
    idx_ref[...] = jnp.concatenate(idx_rows, axis=0)
    w_ref[...] = wts / jnp.sum(wts, axis=0, keepdims=True) * ROUTED_SCALE

    onehot = jnp.zeros((ne, tn), F32)
    for hit in hits:
        onehot = onehot + jnp.where(hit, 1.0, 0.0)
    earlier = carry_ref[...] + _bdot(onehot, tri_ref[...])
    rank_rows = [jnp.sum(jnp.where(hit, earlier, 0.0), axis=0, keepdims=True) for hit in hits]
    rank_ref[...] = jnp.concatenate(rank_rows, axis=0).astype(jnp.int32)
    carry_ref[...] = carry_ref[...] + jnp.sum(onehot, axis=1, keepdims=True)
    cnt_ref[...] = carry_ref[...]


def _route(scores_t, router_bias):
    ne, n = scores_t.shape
    tn = ROUTE_TILE
    tri = jnp.asarray(np.arange(tn)[:, None] < np.arange(tn)[None, :], BF16)
    tok = lambda i: (0, i)
    const = lambda i: (0, 0)
    return pl.pallas_call(
        _route_kernel,
        out_shape=(jax.ShapeDtypeStruct((TOP_K, n), jnp.int32), jax.ShapeDtypeStruct((TOP_K, n), F32),
                   jax.ShapeDtypeStruct((TOP_K, n), jnp.int32), jax.ShapeDtypeStruct((ne, 1), F32)),
        grid=(n // tn,),
        in_specs=[pl.BlockSpec((ne, tn), tok), pl.BlockSpec((ne, 1), const), pl.BlockSpec((tn, tn), const)],
        out_specs=(pl.BlockSpec((TOP_K, tn), tok), pl.BlockSpec((TOP_K, tn), tok),
                   pl.BlockSpec((TOP_K, tn), tok), pl.BlockSpec((ne, 1), const)),
        scratch_shapes=[pltpu.VMEM((ne, 1), F32)],
        compiler_params=_cparams(1),
        name="route",
    )(scores_t, router_bias.reshape(ne, 1).astype(F32), tri)


def _dest_kernel(idx_ref, rank_ref, poffs_ref, dest_ref):
    idx = idx_ref[...]
    ne = poffs_ref.shape[0]
    eio = lax.broadcasted_iota(jnp.int32, (ne, idx.shape[1]), 0)
    poffs = poffs_ref[...]
    rows = [jnp.sum(jnp.where(eio == idx[kk:kk + 1, :], poffs, 0.0), axis=0, keepdims=True)
            for kk in range(idx.shape[0])]
    dest_ref[...] = rank_ref[...] + jnp.concatenate(rows, axis=0).astype(jnp.int32)


def _dest_slots(idx_t, rank_t, poffs):
    k, n = idx_t.shape
    tn = ROUTE_TILE
    ne = poffs.shape[0]
    tok = lambda i: (0, i)
    return pl.pallas_call(
        _dest_kernel,
        out_shape=jax.ShapeDtypeStruct((k, n), jnp.int32),
        grid=(n // tn,),
        in_specs=[pl.BlockSpec((k, tn), tok), pl.BlockSpec((k, tn), tok), pl.BlockSpec((ne, 1), lambda i: (0, 0))],
        out_specs=pl.BlockSpec((k, tn), tok),
        compiler_params=_cparams(1),
        name="dest_slots",
    )(idx_t, rank_t, poffs.astype(F32).reshape(ne, 1))


def _dispatch_tables(idx_t, counts, n_blocks):
    k, n = idx_t.shape
    n_assign = k * n
    rb = ROW_BLOCK
    ne = counts.shape[0]
    counts = counts.reshape(ne).astype(jnp.int32)
    offs = jnp.cumsum(counts) - counts
    nblk = (counts + rb - 1) // rb
    bend = jnp.cumsum(nblk)
    bstart = bend - nblk
    blocks = jnp.arange(n_blocks, dtype=jnp.int32)
    blk_e = jnp.minimum(jnp.sum((bend[None, :] <= blocks[:, None]).astype(jnp.int32), axis=1), ne - 1)
    mine = blk_e[:, None] == jnp.arange(ne, dtype=jnp.int32)[None, :]
    src_start = jnp.sum(jnp.where(mine, (offs - bstart * rb)[None, :], 0), axis=1) + blocks * rb
    tok = lax.broadcasted_iota(jnp.int32, (k, n), 1)
    kk = lax.broadcasted_iota(jnp.int32, (k, n), 0)
    assert ne * n_assign < 2 ** 31
    keys = jnp.sort((idx_t * n_assign + tok * k + kk).reshape(n_assign))
    tok_sorted = (keys % n_assign) // k
    return tok_sorted, blk_e, src_start, bend[ne - 1:ne], bstart * rb


def _ffn_kernel(tok_ref, blk_e_ref, start_ref, used_ref, h2_hbm, wg_ref, wu_ref, wd_ref, y_ref, xbuf, sems):
    b = pl.program_id(0)
    rb = ROW_BLOCK
    d = wg_ref.shape[1]
    nt = d // LANES
    n_assign = tok_ref.shape[0]
    n_used = used_ref[0]
    n_blocks = pl.num_programs(0)

    def start_rows(blk, s, inline):
        start = start_ref[blk]

        def one(r):
            tok = tok_ref[jnp.minimum(start + r, n_assign - 1)]
            pltpu.make_async_copy(h2_hbm.at[pl.ds(pl.multiple_of(tok * nt, nt), nt)],
                                  xbuf.at[s, pl.ds(pl.multiple_of(r * nt, nt), nt)], sems.at[s]
                                  ).start(priority=ROW_COPY_PRIORITY)

        if inline:
            for r in range(rb):
                one(r)
        else:
            lax.fori_loop(0, rb, lambda r, c: (one(r), c)[1], 0, unroll=ISSUE_UNROLL)

    def wait_rows(s):
        pltpu.make_async_copy(h2_hbm.at[pl.ds(0, rb * nt)], xbuf.at[s], sems.at[s]).wait()

    nbuf = xbuf.shape[0]
    ahead = nbuf - 1

    @pl.when(b == 0)
    def _():
        for a in range(ahead):
            start_rows(jnp.minimum(a, n_blocks - 1), a, False)

    for s in range(nbuf):
        @pl.when((b < n_used) & (b % nbuf == s))
        def _(s=s):
            wait_rows(s)
            xb = _load_rows_from_tiles(xbuf.at[s], rb, d).astype(BF16)
            start_rows(jnp.minimum(b + ahead, n_blocks - 1), (s + ahead) % nbuf, True)
            hg = jnp.dot(xb, wg_ref[0].astype(BF16), preferred_element_type=F32)
            hu = jnp.dot(xb, wu_ref[0].astype(BF16), preferred_element_type=F32)
            _store_rows_as_tiles(y_ref, _bdot(_silu(hg) * hu, wd_ref[0]))

    @pl.when(b == n_used - 1)
    def _():
        for a in range(1, nbuf):
            wait_rows((b + a) % nbuf)

    @pl.when(b >= n_used)
    def _():
        y_ref[...] = jnp.zeros_like(y_ref)


def _expert_ffn(h2_tiles, tok_sorted, blk_e, src_start, n_used, w_gate, w_up, w_down):
    n_blocks = blk_e.shape[0]
    rb = ROW_BLOCK
    _, d, ed = w_gate.shape
    nt = d // LANES
    expert = lambda i, tok, be, st, nu: (be[i], 0, 0)
    grid_spec = pltpu.PrefetchScalarGridSpec(
        num_scalar_prefetch=4,
        grid=(n_blocks,),
        in_specs=[pl.BlockSpec(memory_space=pl.ANY),
                  pl.BlockSpec((1, d, ed), expert),
                  pl.BlockSpec((1, d, ed), expert),
                  pl.BlockSpec((1, ed, d), expert)],
        out_specs=pl.BlockSpec((rb * nt, LANES), lambda i, tok, be, st, nu: (i, 0)),
        scratch_shapes=[pltpu.VMEM((FFN_ROW_BUFFERS, rb * nt, LANES), F32),
                        pltpu.SemaphoreType.DMA((FFN_ROW_BUFFERS,))],
    )
    return pl.pallas_call(
        _ffn_kernel,
        out_shape=jax.ShapeDtypeStruct((n_blocks * rb * nt, LANES), F32),
        grid_spec=grid_spec,
        compiler_params=_cparams(1),
        name="expert_ffn",
    )(tok_sorted, blk_e, src_start, n_used, h2_tiles, w_gate, w_up, w_down)


def _combine_kernel(dest_ref, y_hbm, base_ref, wt_ref, mod_ref, o_ref, buf, sems):
    i = pl.program_id(0)
    n_steps = pl.num_programs(0)
    tm, d = base_ref.shape
    k = wt_ref.shape[1]
    nt = d // LANES
    n_rows = k * tm
    slot = i % 2

    def issue(step, s):
        first = step * n_rows

        def body(r, _):
            src = dest_ref[first + r]
            pltpu.make_async_copy(y_hbm.at[pl.ds(pl.multiple_of(src * nt, nt), nt)],
                                  buf.at[s, pl.ds(pl.multiple_of(r * nt, nt), nt)], sems.at[s]).start()
            return 0

        lax.fori_loop(0, n_rows, body, 0, unroll=ISSUE_UNROLL)

    @pl.when(i == 0)
    def _():
        issue(0, 0)

    @pl.when(i + 1 < n_steps)
    def _():
        issue(i + 1, 1 - slot)

    pltpu.make_async_copy(y_hbm.at[pl.ds(0, n_rows * nt)], buf.at[slot], sems.at[slot]).wait()
    rows = buf.at[slot]
    wts = wt_ref[...]
    gate2 = mod_ref[0, 5:6, :]
    for c in range(nt):
        cols = slice(c * LANES, (c + 1) * LANES)
        routed = jnp.zeros((tm, LANES), F32)
        for kk in range(k):
            routed = routed + wts[:, kk:kk + 1] * rows[pl.ds(kk * tm * nt + c, tm, stride=nt), :]
        o_ref[:, cols] = base_ref[:, cols] + gate2[:, cols] * routed


def _combine(y_tiles, dest_t, w_t, base, mod3, seq):
    n, d = base.shape
    tm = COMBINE_TILE
    k = dest_t.shape[0]
    nt = d // LANES
    tiles_per_seq = seq // tm
    dest_tiles = dest_t.reshape(k, n // tm, tm).transpose(1, 0, 2).reshape(n * k)
    grid_spec = pltpu.PrefetchScalarGridSpec(
        num_scalar_prefetch=1,
        grid=(n // tm,),
        in_specs=[pl.BlockSpec(memory_space=pl.ANY),
                  pl.BlockSpec((tm, d), lambda i, dst: (i, 0)),
                  pl.BlockSpec((tm, k), lambda i, dst: (i, 0)),
                  pl.BlockSpec((1, 6, d), lambda i, dst: (i // tiles_per_seq, 0, 0))],
        out_specs=pl.BlockSpec((tm, d), lambda i, dst: (i, 0)),
        scratch_shapes=[pltpu.VMEM((2, k * tm * nt, LANES), F32), pltpu.SemaphoreType.DMA((2,))],
    )
    return pl.pallas_call(
        _combine_kernel,
        out_shape=jax.ShapeDtypeStruct((n, d), F32),
        grid_spec=grid_spec,
        compiler_params=_cparams(1),
        name="combine",
    )(dest_tiles, y_tiles, base, w_t.T, mod3)


def _hybrid_layer(x, cond, rel_bias, w_ada, b_ada, ln1_g, w_in, q_norm_g, k_norm_g,
                  ssm_lambda_re, ssm_lambda_im, ssm_log_dt, ssm_b_re, ssm_b_im, ssm_c_re, ssm_c_im,
                  ssm_d, ssm_w_glu, ssm_b_glu, w_up_attn, w_up_ssm, w_out, ln2_g,
                  w_router, router_bias, w_exp_gate, w_exp_up, w_exp_down,
                  w_sh_gate, w_sh_up, w_sh_down):
    bsz, seq, d = x.shape
    n = bsz * seq
    xf = x.reshape(n, d)
    mod3 = _adaln(cond, w_ada, b_ada).reshape(bsz, 6, d)

    q_gain = jnp.tile(q_norm_g.astype(F32), ATTN_HEADS).reshape(1, ATTN_WIDTH)
    k_gain = jnp.tile(k_norm_g.astype(F32), ATTN_HEADS).reshape(1, ATTN_WIDTH)
    q, k, v, u, ga, gs, kmean = _inproj(xf, mod3, ln1_g, w_in.astype(BF16), q_gain, k_gain, seq)

    attn = _moba_attention(q, k, v, kmean, rel_bias, bsz, seq)
    ops = _s5_operators(ssm_lambda_re, ssm_lambda_im, ssm_log_dt, ssm_b_re, ssm_b_im,
                        ssm_c_re, ssm_c_im, ssm_d, seq // SSM_CHUNK)
    yssm = _s5_scan(u, ops, bsz, seq)

    weights = {
        "glu": ssm_w_glu.astype(BF16), "b_glu": ssm_b_glu.astype(F32).reshape(1, -1),
        "up_attn": w_up_attn.astype(BF16), "up_ssm": w_up_ssm.astype(BF16), "out": w_out.astype(BF16),
        "router_t": w_router.T.astype(BF16),
        "sh_gu": jnp.concatenate([w_sh_gate, w_sh_up], axis=1).astype(BF16),
        "sh_down": w_sh_down.astype(BF16),
    }
    h2, base, scores_t = _mix(xf, attn, yssm, ga, gs, mod3, ln2_g, weights, seq)

    idx_t, w_t, rank_t, counts = _route(scores_t, router_bias)
    n_blocks = -(-(n * TOP_K) // ROW_BLOCK) + N_EXPERTS
    tok_sorted, blk_e, src_start, n_used, poffs = _dispatch_tables(idx_t, counts, n_blocks)
    dest_t = _dest_slots(idx_t, rank_t, poffs)
    y_tiles = _expert_ffn(h2, tok_sorted, blk_e, src_start, n_used, w_exp_gate, w_exp_up, w_exp_down)
    out = _combine(y_tiles, dest_t, w_t, base, mod3, seq)
    return out.reshape(bsz, seq, d)


def kernel(x, c, rel_bias, w_ada, b_ada, ln1_g, w_in, q_norm_g, k_norm_g, ssm_lambda_re, ssm_lambda_im, ssm_log_dt, ssm_b_re, ssm_b_im, ssm_c_re, ssm_c_im, ssm_d, ssm_w_glu, ssm_b_glu, w_up_attn, w_up_ssm, w_out, ln2_g, w_router, router_bias, w_exp_gate, w_exp_up, w_exp_down, w_sh_gate, w_sh_up, w_sh_down):
    for l in range(w_ada.shape[0]):
        x = _hybrid_layer(x, c, rel_bias, w_ada[l], b_ada[l], ln1_g[l], w_in[l], q_norm_g[l], k_norm_g[l],
                          ssm_lambda_re[l], ssm_lambda_im[l], ssm_log_dt[l], ssm_b_re[l], ssm_b_im[l],
                          ssm_c_re[l], ssm_c_im[l], ssm_d[l], ssm_w_glu[l], ssm_b_glu[l],
                          w_up_attn[l], w_up_ssm[l], w_out[l], ln2_g[l], w_router[l], router_bias[l],
                          w_exp_gate[l], w_exp_up[l], w_exp_down[l], w_sh_gate[l], w_sh_up[l], w_sh_down[l])
    return x
```

```python
import functools
import math

import numpy as np
import jax
import jax.numpy as jnp
from jax import lax
from jax.experimental import pallas as pl
from jax.experimental.pallas import tpu as pltpu

F32 = jnp.float32
BF16 = jnp.bfloat16

ATTN_HEADS = 8
HEAD_DIM = 64
ATTN_WIDTH = ATTN_HEADS * HEAD_DIM
MOBA_BLOCK = 256
MOBA_TOPK = 3
NUM_BUCKETS = 32
MAX_DISTANCE = 128
SSM_WIDTH = 512
SSM_GROUP = 16
SSM_GROUPS = SSM_WIDTH // SSM_GROUP
SSM_STATE = 64
N_EXPERTS = 256
TOP_K = 8
N_EXPERT_GROUPS = 8
TOPK_GROUPS = 4
GROUP_SIZE = N_EXPERTS // N_EXPERT_GROUPS
EXPERT_DIM = 256
ROUTED_SCALE = 2.5
EPS = 1e-6
MASK_VALUE = -1e30

LANES = 128
HEADS_PER_STEP = LANES // HEAD_DIM
SSM_CHUNK = 16
SUBLANES = 8
BF16_ROWS = 16
ROW_BLOCK = 256
TOKEN_TILE = 512
ROUTE_TILE = 256
COMBINE_TILE = 128
ISSUE_UNROLL = 8
ROW_COPY_PRIORITY = 1
FFN_ROW_BUFFERS = 3
VMEM_LIMIT = 56 * 1024 * 1024


def _store_rows_as_tiles(ref, val):
    rows, d = val.shape
    nt = d // LANES
    for c in range(nt):
        ref[pl.ds(c, rows, stride=nt), :] = val[:, c * LANES:(c + 1) * LANES]


def _load_rows_from_tiles(ref, rows, d):
    nt = d // LANES
    return jnp.concatenate([ref[pl.ds(c, rows, stride=nt), :] for c in range(nt)], axis=1)


def _cparams(n_axes, vmem=VMEM_LIMIT):
    return pltpu.CompilerParams(dimension_semantics=("arbitrary",) * n_axes, vmem_limit_bytes=vmem)


def _sigmoid(x):
    return 1.0 / (1.0 + jnp.exp(-x))


def _silu(x):
    return x * _sigmoid(x)


def _bdot(a, b):
    return jnp.dot(a.astype(BF16), b.astype(BF16), preferred_element_type=F32)


def _bdot_nt(a, b):
    return lax.dot_general(a.astype(BF16), b.astype(BF16), (((1,), (1,)), ((), ())),
                           preferred_element_type=F32)


def _adaln_kernel(c_ref, w_ref, b_ref, o_ref):
    o_ref[...] = _bdot(_silu(c_ref[...]), w_ref[...]) + b_ref[...]


def _adaln(c, w_ada, b_ada):
    bsz, d = c.shape
    n_out = w_ada.shape[1]
    return pl.pallas_call(
        _adaln_kernel,
        out_shape=jax.ShapeDtypeStruct((bsz, n_out), F32),
        grid=(n_out // d,),
        in_specs=[pl.BlockSpec((bsz, d), lambda j: (0, 0)),
                  pl.BlockSpec((d, d), lambda j: (0, j)),
                  pl.BlockSpec((1, d), lambda j: (0, j))],
        out_specs=pl.BlockSpec((bsz, d), lambda j: (0, j)),
        compiler_params=_cparams(1),
        name="adaln",
    )(c, w_ada, b_ada.reshape(1, n_out))


def _modulated_norm(x, gain, shift, scale):
    y = x * lax.rsqrt(jnp.mean(x * x, axis=-1, keepdims=True) + EPS) * gain
    return y * (1.0 + scale) + shift


def _head_norm(t, seg, gain):
    ms = _bdot(t * t, seg)
    return t * lax.rsqrt(ms + EPS) * gain


def _inproj_kernel(x_ref, mod_ref, ln_ref, w_ref, seg_ref, qg_ref, kg_ref,
                   q_ref, k_ref, v_ref, u_ref, ga_ref, gs_ref, km_ref, u_scr):
    aw, sw, d = ATTN_WIDTH, SSM_WIDTH, x_ref.shape[1]
    h = _modulated_norm(x_ref[...], ln_ref[...], mod_ref[0, 0:1, :], mod_ref[0, 1:2, :]).astype(BF16)
    seg = seg_ref[...]
    q = jnp.dot(h, w_ref[:, 0:aw], preferred_element_type=F32)
    q_ref[...] = _head_norm(q, seg, qg_ref[...])
    k = jnp.dot(h, w_ref[:, aw:2 * aw], preferred_element_type=F32)
    kn = _head_norm(k, seg, kg_ref[...])
    k_ref[...] = kn.astype(BF16)
    for blk in range(km_ref.shape[0]):
        km_ref[blk] = jnp.mean(kn[blk * MOBA_BLOCK:(blk + 1) * MOBA_BLOCK, :], axis=0, keepdims=True)
    v_ref[...] = jnp.dot(h, w_ref[:, 2 * aw:3 * aw], preferred_element_type=F32).astype(BF16)
    o = 3 * aw
    u = jnp.dot(h, w_ref[:, o:o + sw], preferred_element_type=F32)
    n_chunk = u_scr.shape[1] // SSM_CHUNK
    for cb in range(sw // LANES):
        u_scr[cb] = u[:, cb * LANES:(cb + 1) * LANES]
        for sg in range(SSM_CHUNK):
            u_ref[cb, :, sg * LANES:(sg + 1) * LANES] = (
                u_scr[cb, pl.ds(sg, n_chunk, stride=SSM_CHUNK), :].astype(BF16))
    o += sw
    ga_ref[...] = jnp.dot(h, w_ref[:, o:o + d], preferred_element_type=F32).astype(BF16)
    o += d
    gs_ref[...] = jnp.dot(h, w_ref[:, o:o + d], preferred_element_type=F32).astype(BF16)


def _inproj(xf, mod3, ln1_g, w_in_b, q_gain, k_gain, seq):
    n, d = xf.shape
    tm = TOKEN_TILE
    assert tm % MOBA_BLOCK == 0 and seq % tm == 0
    blocks_per_tile = tm // MOBA_BLOCK
    tiles_per_seq = seq // tm
    aw, sw = ATTN_WIDTH, SSM_WIDTH
    head_of_lane = np.arange(aw) // HEAD_DIM
    seg = jnp.asarray((head_of_lane[:, None] == head_of_lane[None, :]) / HEAD_DIM, BF16)
    row = lambda i: (i, 0)
    const = lambda i: (0, 0)
    return pl.pallas_call(
        _inproj_kernel,
        out_shape=(jax.ShapeDtypeStruct((n, aw), F32),
                   jax.ShapeDtypeStruct((n, aw), BF16),
                   jax.ShapeDtypeStruct((n, aw), BF16),
                   jax.ShapeDtypeStruct((sw // LANES, n // SSM_CHUNK, SSM_CHUNK * LANES), BF16),
                   jax.ShapeDtypeStruct((n, d), BF16),
                   jax.ShapeDtypeStruct((n, d), BF16),
                   jax.ShapeDtypeStruct((n // MOBA_BLOCK, 1, aw), F32)),
        grid=(n // tm,),
        in_specs=[pl.BlockSpec((tm, d), row),
                  pl.BlockSpec((1, 6, d), lambda i: (i // tiles_per_seq, 0, 0)),
                  pl.BlockSpec((1, d), const),
                  pl.BlockSpec(w_in_b.shape, const),
                  pl.BlockSpec((aw, aw), const),
                  pl.BlockSpec((1, aw), const),
                  pl.BlockSpec((1, aw), const)],
        out_specs=(pl.BlockSpec((tm, aw), row), pl.BlockSpec((tm, aw), row), pl.BlockSpec((tm, aw), row),
                   pl.BlockSpec((sw // LANES, tm // SSM_CHUNK, SSM_CHUNK * LANES), lambda i: (0, i, 0)),
                   pl.BlockSpec((tm, d), row), pl.BlockSpec((tm, d), row),
                   pl.BlockSpec((blocks_per_tile, 1, aw), lambda i: (i, 0, 0))),
        scratch_shapes=[pltpu.VMEM((sw // LANES, tm, LANES), F32)],
        compiler_params=_cparams(1),
        name="inproj",
    )(xf, mod3, ln1_g.reshape(1, d), w_in_b, seg, q_gain, k_gain)


def _t5_bucket(rel):
    n = jnp.maximum(rel, 0)
    max_exact = NUM_BUCKETS // 2
    nf = jnp.maximum(n, 1).astype(F32)
    large = max_exact + (jnp.log(nf / max_exact) / math.log(MAX_DISTANCE / max_exact)
                         * (NUM_BUCKETS - max_exact)).astype(jnp.int32)
    large = jnp.minimum(large, NUM_BUCKETS - 1)
    return jnp.where(n < max_exact, n, large)


def _bias_tables(rel_bias):
    blk = MOBA_BLOCK
    assert blk + 1 >= MAX_DISTANCE
    rel = jnp.arange(blk)[None, :] - jnp.arange(blk)[:, None]
    table = rel_bias.astype(F32)
    table = table - table[NUM_BUCKETS - 1][None, :]

    def lookup(r):
        onehot = jax.nn.one_hot(_t5_bucket(r), NUM_BUCKETS, dtype=F32)
        return jnp.einsum('kqn,nh->hkq', onehot, table, precision=lax.Precision.HIGHEST)

    return lookup(rel), lookup(rel + blk)


def _select_blocks(gate_t, n_past):
    nb, tq = gate_t.shape
    blk = lax.broadcasted_iota(jnp.int32, (nb, tq), 0)
    beaten = jnp.zeros((nb, tq), jnp.int32)
    for m in range(nb):
        gm = gate_t[m:m + 1, :]
        wins = (gm > gate_t) | ((gm == gate_t) & (m < blk))
        beaten = beaten + jnp.where(wins & (m < n_past), 1, 0)
    return jnp.where((blk < n_past) & (beaten < MOBA_TOPK), 1.0, 0.0)


def _attn_kernel(q_ref, k_ref, vt_ref, km_ref, bias_ref, o_ref, sel_ref, s_ref):
    qi = pl.program_id(2)
    tq = q_ref.shape[0]
    blk = MOBA_BLOCK
    hd = HEAD_DIM
    heads = range(HEADS_PER_STEP)
    q = q_ref[...]
    lane = lax.broadcasted_iota(jnp.int32, (tq, LANES), 1)
    kpos = lax.broadcasted_iota(jnp.int32, (blk, tq), 0)
    qpos = lax.broadcasted_iota(jnp.int32, (blk, tq), 1)
    scale = hd ** -0.5
    n_far = jnp.maximum(qi - 1, 0)
    n_pairs = (n_far + 1) // 2
    jp = jnp.maximum(qi - 1, 0)
    k_own = k_ref[pl.ds(pl.multiple_of(qi * blk, blk), blk), :]
    k_prev = k_ref[pl.ds(pl.multiple_of(jp * blk, blk), blk), :]

    qbs = []
    for h in heads:
        in_head = (lane >= h * hd) & (lane < (h + 1) * hd)
        qm = jnp.where(in_head, q, 0.0)
        gate_t = lax.dot_general(km_ref[0], qm, (((1,), (1,)), ((), ())),
                                 precision=lax.Precision.HIGHEST, preferred_element_type=F32)
        sel_ref[h] = _select_blocks(gate_t, qi)
        qbs.append((qm * scale).astype(BF16))

    def pair_scores(j):
        kb = k_ref[pl.ds(pl.multiple_of(j * blk, blk), 2 * blk), :]
        return [_bdot_nt(kb, qbs[h]) for h in heads]

    ones_rows = jnp.ones((BF16_ROWS, blk), BF16)

    def probs(s, m):
        return jnp.exp((s - m).astype(BF16))

    def attend(h, p, blocks):
        acc = None
        for i, j in enumerate(blocks):
            lhs = jnp.concatenate([vt_ref[0, j, h * hd:(h + 1) * hd, :], ones_rows], axis=0)
            part = jnp.dot(lhs, p[i * blk:(i + 1) * blk, :], preferred_element_type=F32)
            acc = part if acc is None else acc + part
        return acc

    for h, s in enumerate(pair_scores(0)):
        s_ref[h] = s

    carries = []
    for h in heads:
        s_prev = _bdot_nt(k_prev, qbs[h]) + bias_ref[h, 0:blk, :]
        s_prev = jnp.where(sel_ref[h, pl.ds(jp, 1), :] > 0.5, s_prev, MASK_VALUE)
        s_own = _bdot_nt(k_own, qbs[h]) + bias_ref[h, blk:2 * blk, :]
        s_own = jnp.where(kpos <= qpos, s_own, MASK_VALUE)
        s = jnp.concatenate([s_prev, s_own], axis=0)
        m = jnp.max(s, axis=0, keepdims=True)
        carries.append((m, attend(h, probs(s, m), (jp, qi))))

    def far_pair(pi, carries):
        j = 2 * pi
        s_cur = [s_ref[h] for h in heads]
        for h, s in enumerate(pair_scores(2 * jnp.minimum(pi + 1, n_pairs - 1))):
            s_ref[h] = s
        second_is_far = j + 1 < n_far
        out = []
        for h in heads:
            m, acc = carries[h]
            c0 = sel_ref[h, pl.ds(j, 1), :] > 0.5
            c1 = (sel_ref[h, pl.ds(j + 1, 1), :] > 0.5) & second_is_far
            chosen = jnp.concatenate([jnp.broadcast_to(c0, (blk, tq)), jnp.broadcast_to(c1, (blk, tq))], axis=0)
            s = jnp.where(chosen, s_cur[h], MASK_VALUE)
            m_new = jnp.maximum(m, jnp.max(s, axis=0, keepdims=True))
            acc = jnp.exp(m - m_new) * acc + attend(h, probs(s, m_new), (j, j + 1))
            out.append((m_new, acc))
        return tuple(out)

    carries = lax.fori_loop(0, n_pairs, far_pair, tuple(carries))
    out_t = jnp.concatenate([acc[:hd] / acc[hd:hd + 1] for _, acc in carries], axis=0)
    o_ref[...] = out_t.T.astype(o_ref.dtype)


def _moba_attention(q, k, v, kmean, rel_bias, bsz, seq):
    n, aw = q.shape
    blk = MOBA_BLOCK
    nb = seq // blk
    assert nb >= 2
    own, prev = _bias_tables(rel_bias)
    bias = jnp.concatenate([prev, own], axis=1)
    hps = HEADS_PER_STEP
    npair = aw // LANES
    vt = v.reshape(bsz, nb, blk, aw).transpose(0, 1, 3, 2)
    return pl.pallas_call(
        _attn_kernel,
        out_shape=jax.ShapeDtypeStruct((n, aw), BF16),
        grid=(bsz, npair, nb),
        in_specs=[pl.BlockSpec((blk, LANES), lambda b, hp, qi: (b * nb + qi, hp)),
                  pl.BlockSpec((seq, LANES), lambda b, hp, qi: (b, hp)),
                  pl.BlockSpec((1, nb, LANES, blk), lambda b, hp, qi: (b, 0, hp, 0)),
                  pl.BlockSpec((1, nb, LANES), lambda b, hp, qi: (b, 0, hp)),
                  pl.BlockSpec((hps, 2 * blk, blk), lambda b, hp, qi: (hp, 0, 0))],
        out_specs=pl.BlockSpec((blk, LANES), lambda b, hp, qi: (b * nb + qi, hp)),
        scratch_shapes=[pltpu.VMEM((hps, nb, blk), F32), pltpu.VMEM((hps, 2 * blk, blk), F32)],
        compiler_params=_cparams(3),
        name="moba_attention",
    )(q, k, vt, kmean.reshape(bsz, nb, aw), bias)


def _s5_operators(lambda_re, lambda_im, log_dt, b_re, b_im, c_re, c_im, d_skip, n_chunks):
    hi = lax.Precision.HIGHEST
    L, G, P, C = SSM_CHUNK, SSM_GROUPS, SSM_STATE, SSM_GROUP
    lam_re = jnp.minimum(lambda_re.astype(F32), -1e-4)
    lam_im = lambda_im.astype(F32)
    dt = jnp.exp(log_dt.astype(F32))[:, None]
    z_re, z_im = lam_re * dt, lam_im * dt

    def a_pow(nvec):
        nv = jnp.asarray(nvec, F32)[:, None, None]
        mag = jnp.exp(nv * z_re)
        return mag * jnp.cos(nv * z_im), mag * jnp.sin(nv * z_im)

    a_re, a_im = a_pow([1.0])
    a_re, a_im = a_re[0], a_im[0]
    den = lam_re * lam_re + lam_im * lam_im
    nr = a_re - 1.0
    coef_re = (nr * lam_re + a_im * lam_im) / den
    coef_im = (a_im * lam_re - nr * lam_im) / den
    br, bi = b_re.astype(F32), b_im.astype(F32)
    bbar_re = coef_re[..., None] * br - coef_im[..., None] * bi
    bbar_im = coef_re[..., None] * bi + coef_im[..., None] * br
    cr, ci = c_re.astype(F32), c_im.astype(F32)

    pw_re, pw_im = a_pow(np.arange(L + 1))
    cb_re = cr[None] * pw_re[:, :, None, :] - ci[None] * pw_im[:, :, None, :]
    cb_im = cr[None] * pw_im[:, :, None, :] + ci[None] * pw_re[:, :, None, :]
    kern = (jnp.einsum('jgop,gpi->gijo', cb_re[:L], bbar_re, precision=hi)
            - jnp.einsum('jgop,gpi->gijo', cb_im[:L], bbar_im, precision=hi)).reshape(G, C, L * C)
    t_op = jnp.stack([jnp.pad(kern[:, :, :(L - s) * C], ((0, 0), (0, 0), (s * C, 0))) for s in range(L)], axis=1)
    d_g = d_skip.astype(F32).reshape(G, 1, C, 1)
    on_diag = (lax.broadcasted_iota(jnp.int32, (1, L, C, L * C), 3)
               == lax.broadcasted_iota(jnp.int32, (1, L, C, L * C), 1) * C
               + lax.broadcasted_iota(jnp.int32, (1, L, C, L * C), 2))
    t_op = (t_op + jnp.where(on_diag, d_g, 0.0)).reshape(G, L * C, L * C)

    rp_re, rp_im = jnp.flip(pw_re[:L], 0), jnp.flip(pw_im[:L], 0)
    p_re = rp_re[..., None] * bbar_re[None] - rp_im[..., None] * bbar_im[None]
    p_im = rp_re[..., None] * bbar_im[None] + rp_im[..., None] * bbar_re[None]
    p_op = jnp.concatenate([p_re, p_im], axis=2)
    p_op = p_op.transpose(1, 0, 3, 2).reshape(G, L * C, 2 * P)

    q_re = cb_re[1:].transpose(1, 3, 0, 2)
    q_im = -cb_im[1:].transpose(1, 3, 0, 2)
    q_op = jnp.concatenate([q_re, q_im], axis=1).reshape(G, 2 * P, L * C)

    n_steps = max(1, int(math.ceil(math.log2(n_chunks))))
    dk_re, dk_im = a_pow([float(L * 2 ** k) for k in range(n_steps)])
    GB = LANES // C
    NB = G // GB
    lc = np.arange(L * C)
    wide = np.arange(L * LANES)
    expand_tc = jnp.asarray((lc[:, None] // C == wide[None, :] // LANES) & (lc[:, None] % C == wide[None, :] % C), BF16)
    st = np.arange(2 * P)
    wide_st = np.arange(2 * GB * P)
    expand_st = jnp.asarray((st[:, None] // P == wide_st[None, :] // (GB * P))
                            & (st[:, None] % P == wide_st[None, :] % P), BF16)
    g_of_wide = (jnp.arange(L * LANES) // C) % GB
    g_of_state = (jnp.arange(2 * GB * P) // P) % GB

    def widen(rows, expand, g_row, g_col):
        full = jnp.einsum('brk,kc->brc', rows.astype(BF16), expand, preferred_element_type=F32)
        return jnp.where(g_row[:, None] == g_col[None, :], full, 0.0).astype(BF16)

    t_rows = t_op.reshape(NB, GB, L, C, L * C).transpose(0, 2, 1, 3, 4).reshape(NB, L * LANES, L * C)
    p_rows = p_op.reshape(NB, GB, L, C, 2 * P).transpose(0, 2, 1, 3, 4).reshape(NB, L * LANES, 2 * P)
    q_rows = q_op.reshape(NB, GB, 2, P, L * C).transpose(0, 2, 1, 3, 4).reshape(NB, 2 * GB * P, L * C)
    t_big = widen(t_rows, expand_tc, g_of_wide, g_of_wide)
    p_big = widen(p_rows, expand_st, g_of_wide, g_of_state)
    q_big = widen(q_rows, expand_tc, g_of_state, g_of_wide)
    dk_re = dk_re.reshape(-1, NB, GB * P)
    dk_im = dk_im.reshape(-1, NB, GB * P)
    a1 = jnp.concatenate([dk_re, dk_re], axis=-1).transpose(1, 0, 2)
    a2 = jnp.concatenate([-dk_im, dk_im], axis=-1).transpose(1, 0, 2)
    return t_big.astype(BF16), p_big.astype(BF16), q_big.astype(BF16), a1, a2


def _s5_kernel(x_ref, t_ref, p_ref, q_ref, a1_ref, a2_ref, y_ref):
    x = x_ref[0]
    s = jnp.dot(x, p_ref[0], preferred_element_type=F32)
    n_chunks, width = s.shape
    chunk = lax.broadcasted_iota(jnp.int32, (n_chunks, width), 0)
    a1 = a1_ref[0]
    a2 = a2_ref[0]
    h = jnp.where(chunk >= 1, pltpu.roll(s, 1, axis=0), 0.0)
    for kk in range(a1.shape[0]):
        dist = 2 ** kk
        if dist >= n_chunks:
            break
        hs = jnp.where(chunk >= dist, pltpu.roll(h, dist, axis=0), 0.0)
        h = h + a1[kk:kk + 1, :] * hs + a2[kk:kk + 1, :] * pltpu.roll(hs, width // 2, axis=1)
    hb = h.astype(BF16)
    step = 2 * LANES
    for t in range(x.shape[1] // step):
        hi = (t + 1) * step
        y_ref[0, :, t * step:hi] = (jnp.dot(x[:, :hi], t_ref[0, :hi, t * step:hi], preferred_element_type=F32)
                                    + jnp.dot(hb, q_ref[0, :, t * step:hi], preferred_element_type=F32))


def _s5_scan(x_chunks, ops, bsz, seq):
    t_big, p_big, q_big, a1, a2 = ops
    nblk, rows, w = x_chunks.shape
    nc = seq // SSM_CHUNK
    sw = p_big.shape[2]
    col = lambda cb, b: (cb, 0, 0)
    return pl.pallas_call(
        _s5_kernel,
        out_shape=jax.ShapeDtypeStruct((nblk, rows, w), F32),
        grid=(nblk, bsz),
        in_specs=[pl.BlockSpec((1, nc, w), lambda cb, b: (cb, b, 0)),
                  pl.BlockSpec((1, w, w), col),
                  pl.BlockSpec((1, w, sw), col),
                  pl.BlockSpec((1, sw, w), col),
                  pl.BlockSpec((1,) + a1.shape[1:], col),
                  pl.BlockSpec((1,) + a2.shape[1:], col)],
        out_specs=pl.BlockSpec((1, nc, w), lambda cb, b: (cb, b, 0)),
        compiler_params=_cparams(2),
        name="s5_scan",
    )(x_chunks, t_big, p_big, q_big, a1, a2)


def _gelu_tanh(x):
    return 0.5 * x * (1.0 + jnp.tanh(math.sqrt(2.0 / math.pi) * (x + 0.044715 * (x * x * x))))


def _mix_kernel(x_ref, attn_ref, yssm_ref, ga_ref, gs_ref, mod_ref, ln_ref,
                wglu_ref, bglu_ref, wua_ref, wus_ref, wout_ref, wrt_ref, wsgu_ref, wsd_ref,
                h2_ref, base_ref, score_ref, y_scr):
    n_chunk = yssm_ref.shape[1]
    for cb in range(yssm_ref.shape[0]):
        for tau in range(SSM_CHUNK):
            y_scr[cb, pl.ds(tau, n_chunk, stride=SSM_CHUNK), :] = yssm_ref[cb, :, tau * LANES:(tau + 1) * LANES]
    g = _gelu_tanh(jnp.concatenate([y_scr[cb] for cb in range(yssm_ref.shape[0])], axis=1))
    glu = g * _sigmoid(_bdot(g, wglu_ref[...]) + bglu_ref[...])
    y_attn = jnp.dot(attn_ref[...], wua_ref[...], preferred_element_type=F32)
    y_ssm = _bdot(glu, wus_ref[...])
    mixed = _sigmoid(ga_ref[...].astype(F32)) * y_attn + _sigmoid(gs_ref[...].astype(F32)) * y_ssm
    gate1 = mod_ref[0, 2:3, :]
    x1 = x_ref[...] + gate1 * _bdot(mixed, wout_ref[...])
    h2 = _modulated_norm(x1, ln_ref[...], mod_ref[0, 3:4, :], mod_ref[0, 4:5, :])
    _store_rows_as_tiles(h2_ref, h2)
    h2b = h2.astype(BF16)
    score_ref[...] = _sigmoid(_bdot_nt(wrt_ref[...], h2b))
    gu = jnp.dot(h2b, wsgu_ref[...], preferred_element_type=F32)
    sd = wsd_ref.shape[0]
    shared = _bdot(_silu(gu[:, :sd]) * gu[:, sd:], wsd_ref[...])
    base_ref[...] = x1 + mod_ref[0, 5:6, :] * shared


def _mix(xf, attn, yssm, ga, gs, mod3, ln2_g, w, seq):
    n, d = xf.shape
    tm = TOKEN_TILE
    tiles_per_seq = seq // tm
    row = lambda i: (i, 0)
    const = lambda i: (0, 0)
    nt = d // LANES
    weights = [w["glu"], w["b_glu"], w["up_attn"], w["up_ssm"], w["out"], w["router_t"], w["sh_gu"], w["sh_down"]]
    return pl.pallas_call(
        _mix_kernel,
        out_shape=(jax.ShapeDtypeStruct((n * nt, LANES), F32),
                   jax.ShapeDtypeStruct((n, d), F32),
                   jax.ShapeDtypeStruct((N_EXPERTS, n), F32)),
        grid=(n // tm,),
        in_specs=[pl.BlockSpec((tm, d), row),
                  pl.BlockSpec((tm, attn.shape[1]), row),
                  pl.BlockSpec((yssm.shape[0], tm // SSM_CHUNK, yssm.shape[2]), lambda i: (0, i, 0)),
                  pl.BlockSpec((tm, d), row),
                  pl.BlockSpec((tm, d), row),
                  pl.BlockSpec((1, 6, d), lambda i: (i // tiles_per_seq, 0, 0)),
                  pl.BlockSpec((1, d), const)] + [pl.BlockSpec(a.shape, const) for a in weights],
        out_specs=(pl.BlockSpec((tm * nt, LANES), row), pl.BlockSpec((tm, d), row),
                   pl.BlockSpec((N_EXPERTS, tm), lambda i: (0, i))),
        scratch_shapes=[pltpu.VMEM((yssm.shape[0], tm, LANES), F32)],
        compiler_params=_cparams(1),
        name="mix",
    )(xf, attn, yssm, ga, gs, mod3, ln2_g.reshape(1, d), *weights)


def _route_kernel(score_ref, bias_ref, tri_ref, idx_ref, w_ref, rank_ref, cnt_ref, carry_ref):
    @pl.when(pl.program_id(0) == 0)
    def _():
        carry_ref[...] = jnp.zeros_like(carry_ref)

    scores = score_ref[...]
    ne, tn = scores.shape
    biased = scores + bias_ref[...]
    gsz = GROUP_SIZE
    sub = lax.broadcasted_iota(jnp.int32, (gsz, tn), 0)
    group_score = []
    for g in range(N_EXPERT_GROUPS):
        sg = biased[g * gsz:(g + 1) * gsz, :]
        m1 = jnp.max(sg, axis=0, keepdims=True)
        first = jnp.min(jnp.where(sg == m1, sub, gsz), axis=0, keepdims=True)
        m2 = jnp.max(jnp.where(sub == first, -jnp.inf, sg), axis=0, keepdims=True)
        group_score.append(m1 + m2)
    group_rows = []
    for g in range(N_EXPERT_GROUPS):
        beaten = jnp.zeros((1, tn), jnp.int32)
        for o in range(N_EXPERT_GROUPS):
            if o == g:
                continue
            wins = (group_score[o] > group_score[g])
            if o < g:
                wins = wins | (group_score[o] == group_score[g])
            beaten = beaten + jnp.where(wins, 1, 0)
        group_rows.append(jnp.broadcast_to(beaten < TOPK_GROUPS, (gsz, tn)))
    allowed = jnp.concatenate(group_rows, axis=0)
    cur = jnp.where(allowed, biased, MASK_VALUE)
    eio = lax.broadcasted_iota(jnp.int32, (ne, tn), 0)
    idx_rows, w_rows, hits = [], [], []
    for _ in range(TOP_K):
        vmax = jnp.max(cur, axis=0, keepdims=True)
        eidx = jnp.min(jnp.where(cur == vmax, eio, ne), axis=0, keepdims=True)
        hit = eio == eidx
        w_rows.append(jnp.sum(jnp.where(hit, scores, 0.0), axis=0, keepdims=True))
        idx_rows.append(eidx)
        hits.append(hit)
        cur = jnp.where(hit, -jnp.inf, cur)
    wts = jnp.concatenate(w_rows, axis=0)
    idx_ref[...] = jnp.concatenate(idx_rows, axis=0)
    w_ref[...] = wts / jnp.sum(wts, axis=0, keepdims=True) * ROUTED_SCALE

    onehot = jnp.zeros((ne, tn), F32)
    for hit in hits:
        onehot = onehot + jnp.where(hit, 1.0, 0.0)
    earlier = carry_ref[...] + _bdot(onehot, tri_ref[...])
    rank_rows = [jnp.sum(jnp.where(hit, earlier, 0.0), axis=0, keepdims=True) for hit in hits]
    rank_ref[...] = jnp.concatenate(rank_rows, axis=0).astype(jnp.int32)
    carry_ref[...] = carry_ref[...] + jnp.sum(onehot, axis=1, keepdims=True)
    cnt_ref[...] = carry_ref[...]


def _route(scores_t, router_bias):
    ne, n = scores_t.shape
    tn = ROUTE_TILE
    tri = jnp.asarray(np.arange(tn)[:, None] < np.arange(tn)[None, :], BF16)
    tok = lambda i: (0, i)
    const = lambda i: (0, 0)
    return pl.pallas_call(
        _route_kernel,
        out_shape=(jax.ShapeDtypeStruct((TOP_K, n), jnp.int32), jax.ShapeDtypeStruct((TOP_K, n), F32),
                   jax.ShapeDtypeStruct((TOP_K, n), jnp.int32), jax.ShapeDtypeStruct((ne, 1), F32)),
        grid=(n // tn,),
        in_specs=[pl.BlockSpec((ne, tn), tok), pl.BlockSpec((ne, 1), const), pl.BlockSpec((tn, tn), const)],
        out_specs=(pl.BlockSpec((TOP_K, tn), tok), pl.BlockSpec((TOP_K, tn), tok),
                   pl.BlockSpec((TOP_K, tn), tok), pl.BlockSpec((ne, 1), const)),
        scratch_shapes=[pltpu.VMEM((ne, 1), F32)],
        compiler_params=_cparams(1),
        name="route",
    )(scores_t, router_bias.reshape(ne, 1).astype(F32), tri)


def _dest_kernel(idx_ref, rank_ref, poffs_ref, dest_ref):
    idx = idx_ref[...]
    ne = poffs_ref.shape[0]
    eio = lax.broadcasted_iota(jnp.int32, (ne, idx.shape[1]), 0)
    poffs = poffs_ref[...]
    rows = [jnp.sum(jnp.where(eio == idx[kk:kk + 1, :], poffs, 0.0), axis=0, keepdims=True)
            for kk in range(idx.shape[0])]
    dest_ref[...] = rank_ref[...] + jnp.concatenate(rows, axis=0).astype(jnp.int32)


def _dest_slots(idx_t, rank_t, poffs):
    k, n = idx_t.shape
    tn = ROUTE_TILE
    ne = poffs.shape[0]
    tok = lambda i: (0, i)
    return pl.pallas_call(
        _dest_kernel,
        out_shape=jax.ShapeDtypeStruct((k, n), jnp.int32),
        grid=(n // tn,),
        in_specs=[pl.BlockSpec((k, tn), tok), pl.BlockSpec((k, tn), tok), pl.BlockSpec((ne, 1), lambda i: (0, 0))],
        out_specs=pl.BlockSpec((k, tn), tok),
        compiler_params=_cparams(1),
        name="dest_slots",
    )(idx_t, rank_t, poffs.astype(F32).reshape(ne, 1))


def _dispatch_tables(idx_t, counts, n_blocks):
    k, n = idx_t.shape
    n_assign = k * n
    rb = ROW_BLOCK
    ne = counts.shape[0]
    counts = counts.reshape(ne).astype(jnp.int32)
    offs = jnp.cumsum(counts) - counts
    nblk = (counts + rb - 1) // rb
    bend = jnp.cumsum(nblk)
    bstart = bend - nblk
    blocks = jnp.arange(n_blocks, dtype=jnp.int32)
    blk_e = jnp.minimum(jnp.sum((bend[None, :] <= blocks[:, None]).astype(jnp.int32), axis=1), ne - 1)
    mine = blk_e[:, None] == jnp.arange(ne, dtype=jnp.int32)[None, :]
    src_start = jnp.sum(jnp.where(mine, (offs - bstart * rb)[None, :], 0), axis=1) + blocks * rb
    tok = lax.broadcasted_iota(jnp.int32, (k, n), 1)
    kk = lax.broadcasted_iota(jnp.int32, (k, n), 0)
    assert ne * n_assign < 2 ** 31
    keys = jnp.sort((idx_t * n_assign + tok * k + kk).reshape(n_assign))
    tok_sorted = (keys % n_assign) // k
    experts = jnp.arange(ne, dtype=jnp.int32)
    nonempty = nblk > 0
    run = jnp.cumsum(nonempty.astype(jnp.int32)) - 1
    later = (experts[None, :] > experts[:, None]) & nonempty[None, :]
    next_e = jnp.min(jnp.where(later, experts[None, :], ne), axis=1)
    next_e = jnp.where(next_e == ne, -1, next_e)
    per_expert = (run % 2) * 2 + (next_e + 1) * 4
    winfo = (jnp.sum(jnp.where(mine, per_expert[None, :], 0), axis=1)
             + (blocks == jnp.sum(jnp.where(mine, bstart[None, :], 0), axis=1)).astype(jnp.int32))
    return tok_sorted, blk_e, src_start, bend[ne - 1:ne], winfo.astype(jnp.int32), bstart * rb


def _ffn_kernel(tok_ref, blk_e_ref, start_ref, used_ref, winfo_ref, h2_hbm, wg_hbm, wu_hbm, wd_hbm, y_ref,
                xbuf, wg_buf, wu_buf, wd_buf, sems, wsems):
    b = pl.program_id(0)
    rb = ROW_BLOCK
    d = wg_hbm.shape[1]
    nt = d // LANES
    n_assign = tok_ref.shape[0]
    n_used = used_ref[0]
    n_blocks = pl.num_programs(0)

    info = winfo_ref[b]
    first_of_expert = (info & 1) == 1
    wslot = (info >> 1) & 1
    next_expert = (info >> 2) - 1

    def weight_copies(e, s):
        return [pltpu.make_async_copy(src.at[e], dst.at[s], wsems.at[s])
                for src, dst in ((wg_hbm, wg_buf), (wu_hbm, wu_buf), (wd_hbm, wd_buf))]

    @pl.when(b == 0)
    def _():
        for c in weight_copies(blk_e_ref[0], wslot):
            c.start()

    @pl.when((b < n_used) & first_of_expert)
    def _():
        for c in weight_copies(blk_e_ref[b], wslot):
            c.wait()

        @pl.when(next_expert >= 0)
        def _():
            for c in weight_copies(next_expert, 1 - wslot):
                c.start()

    def start_rows(blk, s, inline):
        start = start_ref[blk]

        def one(r):
            tok = tok_ref[jnp.minimum(start + r, n_assign - 1)]
            pltpu.make_async_copy(h2_hbm.at[pl.ds(pl.multiple_of(tok * nt, nt), nt)],
                                  xbuf.at[s, pl.ds(pl.multiple_of(r * nt, nt), nt)], sems.at[s]
                                  ).start(priority=ROW_COPY_PRIORITY)

        if inline:
            for r in range(rb):
                one(r)
        else:
            lax.fori_loop(0, rb, lambda r, c: (one(r), c)[1], 0, unroll=ISSUE_UNROLL)

    def wait_rows(s):
        pltpu.make_async_copy(h2_hbm.at[pl.ds(0, rb * nt)], xbuf.at[s], sems.at[s]).wait()

    nbuf = xbuf.shape[0]
    ahead = nbuf - 1

    @pl.when(b == 0)
    def _():
        for a in range(ahead):
            start_rows(jnp.minimum(a, n_blocks - 1), a, False)

    for s in range(nbuf):
        @pl.when((b < n_used) & (b % nbuf == s))
        def _(s=s):
            wait_rows(s)
            xb = _load_rows_from_tiles(xbuf.at[s], rb, d).astype(BF16)
            start_rows(jnp.minimum(b + ahead, n_blocks - 1), (s + ahead) % nbuf, True)
            hg = jnp.dot(xb, wg_buf[wslot].astype(BF16), preferred_element_type=F32)
            hu = jnp.dot(xb, wu_buf[wslot].astype(BF16), preferred_element_type=F32)
            _store_rows_as_tiles(y_ref, _bdot(_silu(hg) * hu, wd_buf[wslot]))

    @pl.when(b == n_used - 1)
    def _():
        for a in range(1, nbuf):
            wait_rows((b + a) % nbuf)

    @pl.when(b >= n_used)
    def _():
        y_ref[...] = jnp.zeros_like(y_ref)


def _expert_ffn(h2_tiles, tok_sorted, blk_e, src_start, n_used, winfo, w_gate, w_up, w_down):
    n_blocks = blk_e.shape[0]
    rb = ROW_BLOCK
    _, d, ed = w_gate.shape
    nt = d // LANES
    hbm = pl.BlockSpec(memory_space=pl.ANY)
    grid_spec = pltpu.PrefetchScalarGridSpec(
        num_scalar_prefetch=5,
        grid=(n_blocks,),
        in_specs=[hbm, hbm, hbm, hbm],
        out_specs=pl.BlockSpec((rb * nt, LANES), lambda i, tok, be, st, nu, wi: (i, 0)),
        scratch_shapes=[pltpu.VMEM((FFN_ROW_BUFFERS, rb * nt, LANES), F32),
                        pltpu.VMEM((2, d, ed), w_gate.dtype), pltpu.VMEM((2, d, ed), w_up.dtype),
                        pltpu.VMEM((2, ed, d), w_down.dtype),
                        pltpu.SemaphoreType.DMA((FFN_ROW_BUFFERS,)), pltpu.SemaphoreType.DMA((2,))],
    )
    return pl.pallas_call(
        _ffn_kernel,
        out_shape=jax.ShapeDtypeStruct((n_blocks * rb * nt, LANES), F32),
        grid_spec=grid_spec,
        compiler_params=_cparams(1),
        name="expert_ffn",
    )(tok_sorted, blk_e, src_start, n_used, winfo, h2_tiles, w_gate, w_up, w_down)


def _combine_kernel(dest_ref, y_hbm, base_ref, wt_ref, mod_ref, o_ref, buf, sems):
    i = pl.program_id(0)
    n_steps = pl.num_programs(0)
    tm, d = base_ref.shape
    k = wt_ref.shape[1]
    nt = d // LANES
    n_rows = k * tm
    slot = i % 2

    def issue(step, s):
        first = step * n_rows

        def body(r, _):
            src = dest_ref[first + r]
            pltpu.make_async_copy(y_hbm.at[pl.ds(pl.multiple_of(src * nt, nt), nt)],
                                  buf.at[s, pl.ds(pl.multiple_of(r * nt, nt), nt)], sems.at[s]).start()
            return 0

        lax.fori_loop(0, n_rows, body, 0, unroll=ISSUE_UNROLL)

    @pl.when(i == 0)
    def _():
        issue(0, 0)

    @pl.when(i + 1 < n_steps)
    def _():
        issue(i + 1, 1 - slot)

    pltpu.make_async_copy(y_hbm.at[pl.ds(0, n_rows * nt)], buf.at[slot], sems.at[slot]).wait()
    rows = buf.at[slot]
    wts = wt_ref[...]
    gate2 = mod_ref[0, 5:6, :]
    for c in range(nt):
        cols = slice(c * LANES, (c + 1) * LANES)
        routed = jnp.zeros((tm, LANES), F32)
        for kk in range(k):
            routed = routed + wts[:, kk:kk + 1] * rows[pl.ds(kk * tm * nt + c, tm, stride=nt), :]
        o_ref[:, cols] = base_ref[:, cols] + gate2[:, cols] * routed


def _combine(y_tiles, dest_t, w_t, base, mod3, seq):
    n, d = base.shape
    tm = COMBINE_TILE
    k = dest_t.shape[0]
    nt = d // LANES
    tiles_per_seq = seq // tm
    dest_tiles = dest_t.reshape(k, n // tm, tm).transpose(1, 0, 2).reshape(n * k)
    grid_spec = pltpu.PrefetchScalarGridSpec(
        num_scalar_prefetch=1,
        grid=(n // tm,),
        in_specs=[pl.BlockSpec(memory_space=pl.ANY),
                  pl.BlockSpec((tm, d), lambda i, dst: (i, 0)),
                  pl.BlockSpec((tm, k), lambda i, dst: (i, 0)),
                  pl.BlockSpec((1, 6, d), lambda i, dst: (i // tiles_per_seq, 0, 0))],
        out_specs=pl.BlockSpec((tm, d), lambda i, dst: (i, 0)),
        scratch_shapes=[pltpu.VMEM((2, k * tm * nt, LANES), F32), pltpu.SemaphoreType.DMA((2,))],
    )
    return pl.pallas_call(
        _combine_kernel,
        out_shape=jax.ShapeDtypeStruct((n, d), F32),
        grid_spec=grid_spec,
        compiler_params=_cparams(1),
        name="combine",
    )(dest_tiles, y_tiles, base, w_t.T, mod3)


def _hybrid_layer(x, cond, rel_bias, w_ada, b_ada, ln1_g, w_in, q_norm_g, k_norm_g,
                  ssm_lambda_re, ssm_lambda_im, ssm_log_dt, ssm_b_re, ssm_b_im, ssm_c_re, ssm_c_im,
                  ssm_d, ssm_w_glu, ssm_b_glu, w_up_attn, w_up_ssm, w_out, ln2_g,
                  w_router, router_bias, w_exp_gate, w_exp_up, w_exp_down,
                  w_sh_gate, w_sh_up, w_sh_down):
    bsz, seq, d = x.shape
    n = bsz * seq
    xf = x.reshape(n, d)
    mod3 = _adaln(cond, w_ada, b_ada).reshape(bsz, 6, d)

    q_gain = jnp.tile(q_norm_g.astype(F32), ATTN_HEADS).reshape(1, ATTN_WIDTH)
    k_gain = jnp.tile(k_norm_g.astype(F32), ATTN_HEADS).reshape(1, ATTN_WIDTH)
    q, k, v, u, ga, gs, kmean = _inproj(xf, mod3, ln1_g, w_in.astype(BF16), q_gain, k_gain, seq)

    attn = _moba_attention(q, k, v, kmean, rel_bias, bsz, seq)
    ops = _s5_operators(ssm_lambda_re, ssm_lambda_im, ssm_log_dt, ssm_b_re, ssm_b_im,
                        ssm_c_re, ssm_c_im, ssm_d, seq // SSM_CHUNK)
    yssm = _s5_scan(u, ops, bsz, seq)

    weights = {
        "glu": ssm_w_glu.astype(BF16), "b_glu": ssm_b_glu.astype(F32).reshape(1, -1),
        "up_attn": w_up_attn.astype(BF16), "up_ssm": w_up_ssm.astype(BF16), "out": w_out.astype(BF16),
        "router_t": w_router.T.astype(BF16),
        "sh_gu": jnp.concatenate([w_sh_gate, w_sh_up], axis=1).astype(BF16),
        "sh_down": w_sh_down.astype(BF16),
    }
    h2, base, scores_t = _mix(xf, attn, yssm, ga, gs, mod3, ln2_g, weights, seq)

    idx_t, w_t, rank_t, counts = _route(scores_t, router_bias)
    n_blocks = -(-(n * TOP_K) // ROW_BLOCK) + N_EXPERTS
    tok_sorted, blk_e, src_start, n_used, winfo, poffs = _dispatch_tables(idx_t, counts, n_blocks)
    dest_t = _dest_slots(idx_t, rank_t, poffs)
    y_tiles = _expert_ffn(h2, tok_sorted, blk_e, src_start, n_used, winfo, w_exp_gate, w_exp_up, w_exp_down)
    out = _combine(y_tiles, dest_t, w_t, base, mod3, seq)
    return out.reshape(bsz, seq, d)


def kernel(x, c, rel_bias, w_ada, b_ada, ln1_g, w_in, q_norm_g, k_norm_g, ssm_lambda_re, ssm_lambda_im, ssm_log_dt, ssm_b_re, ssm_b_im, ssm_c_re, ssm_c_im, ssm_d, ssm_w_glu, ssm_b_glu, w_up_attn, w_up_ssm, w_out, ln2_g, w_router, router_bias, w_exp_gate, w_exp_up, w_exp_down, w_sh_gate, w_sh_up, w_sh_down):
    for l in range(w_ada.shape[0]):
        x = _hybrid_layer(x, c, rel_bias, w_ada[l], b_ada[l], ln1_g[l], w_in[l], q_norm_g[l], k_norm_g[l],
                          ssm_lambda_re[l], ssm_lambda_im[l], ssm_log_dt[l], ssm_b_re[l], ssm_b_im[l],
                          ssm_c_re[l], ssm_c_im[l], ssm_d[l], ssm_w_glu[l], ssm_b_glu[l],
                          w_up_attn[l], w_up_ssm[l], w_out[l], ln2_g[l], w_router[l], router_bias[l],
                          w_exp_gate[l], w_exp_up[l], w_exp_down[l], w_sh_gate[l], w_sh_up[l], w_sh_down[l])
    return x
```

```python
import functools
import math

import numpy as np
import jax
import jax.numpy as jnp
from jax import lax
from jax.experimental import pallas as pl
from jax.experimental.pallas import tpu as pltpu

F32 = jnp.float32
BF16 = jnp.bfloat16

ATTN_HEADS = 8
HEAD_DIM = 64
ATTN_WIDTH = ATTN_HEADS * HEAD_DIM
MOBA_BLOCK = 256
MOBA_TOPK = 3
NUM_BUCKETS = 32
MAX_DISTANCE = 128
SSM_WIDTH = 512
SSM_GROUP = 16
SSM_GROUPS = SSM_WIDTH // SSM_GROUP
SSM_STATE = 64
N_EXPERTS = 256
TOP_K = 8
N_EXPERT_GROUPS = 8
TOPK_GROUPS = 4
GROUP_SIZE = N_EXPERTS // N_EXPERT_GROUPS
EXPERT_DIM = 256
ROUTED_SCALE = 2.5
EPS = 1e-6
MASK_VALUE = -1e30

LANES = 128
HEADS_PER_STEP = LANES // HEAD_DIM
SSM_CHUNK = 16
SUBLANES = 8
BF16_ROWS = 16
ROW_BLOCK = 256
TOKEN_TILE = 512
ROUTE_TILE = 256
COMBINE_TILE = 128
ISSUE_UNROLL = 8
DMA_QUEUES = 2
FFN_ROW_BUFFERS = 3
VMEM_LIMIT = 56 * 1024 * 1024


def _store_rows_as_tiles(ref, val):
    rows, d = val.shape
    nt = d // LANES
    for c in range(nt):
        ref[pl.ds(c, rows, stride=nt), :] = val[:, c * LANES:(c + 1) * LANES]


def _load_rows_from_tiles(ref, rows, d):
    nt = d // LANES
    return jnp.concatenate([ref[pl.ds(c, rows, stride=nt), :] for c in range(nt)], axis=1)


def _cparams(n_axes, vmem=VMEM_LIMIT):
    return pltpu.CompilerParams(dimension_semantics=("arbitrary",) * n_axes, vmem_limit_bytes=vmem)


def _sigmoid(x):
    return 1.0 / (1.0 + jnp.exp(-x))


def _silu(x):
    return x * _sigmoid(x)


def _bdot(a, b):
    return jnp.dot(a.astype(BF16), b.astype(BF16), preferred_element_type=F32)


def _bdot_nt(a, b):
    return lax.dot_general(a.astype(BF16), b.astype(BF16), (((1,), (1,)), ((), ())),
                           preferred_element_type=F32)


def _adaln_kernel(c_ref, w_ref, b_ref, o_ref):
    o_ref[...] = _bdot(_silu(c_ref[...]), w_ref[...]) + b_ref[...]


def _adaln(c, w_ada, b_ada):
    bsz, d = c.shape
    n_out = w_ada.shape[1]
    return pl.pallas_call(
        _adaln_kernel,
        out_shape=jax.ShapeDtypeStruct((bsz, n_out), F32),
        grid=(n_out // d,),
        in_specs=[pl.BlockSpec((bsz, d), lambda j: (0, 0)),
                  pl.BlockSpec((d, d), lambda j: (0, j)),
                  pl.BlockSpec((1, d), lambda j: (0, j))],
        out_specs=pl.BlockSpec((bsz, d), lambda j: (0, j)),
        compiler_params=_cparams(1),
        name="adaln",
    )(c, w_ada, b_ada.reshape(1, n_out))


def _modulated_norm(x, gain, shift, scale):
    y = x * lax.rsqrt(jnp.mean(x * x, axis=-1, keepdims=True) + EPS) * gain
    return y * (1.0 + scale) + shift


def _head_norm(t, seg, gain):
    ms = _bdot(t * t, seg)
    return t * lax.rsqrt(ms + EPS) * gain


def _inproj_kernel(x_ref, mod_ref, ln_ref, w_ref, seg_ref, qg_ref, kg_ref,
                   q_ref, k_ref, v_ref, u_ref, ga_ref, gs_ref, km_ref, u_scr):
    aw, sw, d = ATTN_WIDTH, SSM_WIDTH, x_ref.shape[1]
    h = _modulated_norm(x_ref[...], ln_ref[...], mod_ref[0, 0:1, :], mod_ref[0, 1:2, :]).astype(BF16)
    seg = seg_ref[...]
    q = jnp.dot(h, w_ref[:, 0:aw], preferred_element_type=F32)
    q_ref[...] = _head_norm(q, seg, qg_ref[...])
    k = jnp.dot(h, w_ref[:, aw:2 * aw], preferred_element_type=F32)
    kn = _head_norm(k, seg, kg_ref[...])
    k_ref[...] = kn.astype(BF16)
    for blk in range(km_ref.shape[0]):
        km_ref[blk] = jnp.mean(kn[blk * MOBA_BLOCK:(blk + 1) * MOBA_BLOCK, :], axis=0, keepdims=True)
    v_ref[...] = jnp.dot(h, w_ref[:, 2 * aw:3 * aw], preferred_element_type=F32).astype(BF16)
    o = 3 * aw
    u = jnp.dot(h, w_ref[:, o:o + sw], preferred_element_type=F32)
    n_chunk = u_scr.shape[1] // SSM_CHUNK
    for cb in range(sw // LANES):
        u_scr[cb] = u[:, cb * LANES:(cb + 1) * LANES]
        for sg in range(SSM_CHUNK):
            u_ref[cb, :, sg * LANES:(sg + 1) * LANES] = (
                u_scr[cb, pl.ds(sg, n_chunk, stride=SSM_CHUNK), :].astype(BF16))
    o += sw
    ga_ref[...] = jnp.dot(h, w_ref[:, o:o + d], preferred_element_type=F32).astype(BF16)
    o += d
    gs_ref[...] = jnp.dot(h, w_ref[:, o:o + d], preferred_element_type=F32).astype(BF16)


def _inproj(xf, mod3, ln1_g, w_in_b, q_gain, k_gain, seq):
    n, d = xf.shape
    tm = TOKEN_TILE
    assert tm % MOBA_BLOCK == 0 and seq % tm == 0
    blocks_per_tile = tm // MOBA_BLOCK
    tiles_per_seq = seq // tm
    aw, sw = ATTN_WIDTH, SSM_WIDTH
    head_of_lane = np.arange(aw) // HEAD_DIM
    seg = jnp.asarray((head_of_lane[:, None] == head_of_lane[None, :]) / HEAD_DIM, BF16)
    row = lambda i: (i, 0)
    const = lambda i: (0, 0)
    return pl.pallas_call(
        _inproj_kernel,
        out_shape=(jax.ShapeDtypeStruct((n, aw), F32),
                   jax.ShapeDtypeStruct((n, aw), BF16),
                   jax.ShapeDtypeStruct((n, aw), BF16),
                   jax.ShapeDtypeStruct((sw // LANES, n // SSM_CHUNK, SSM_CHUNK * LANES), BF16),
                   jax.ShapeDtypeStruct((n, d), BF16),
                   jax.ShapeDtypeStruct((n, d), BF16),
                   jax.ShapeDtypeStruct((n // MOBA_BLOCK, 1, aw), F32)),
        grid=(n // tm,),
        in_specs=[pl.BlockSpec((tm, d), row),
                  pl.BlockSpec((1, 6, d), lambda i: (i // tiles_per_seq, 0, 0)),
                  pl.BlockSpec((1, d), const),
                  pl.BlockSpec(w_in_b.shape, const),
                  pl.BlockSpec((aw, aw), const),
                  pl.BlockSpec((1, aw), const),
                  pl.BlockSpec((1, aw), const)],
        out_specs=(pl.BlockSpec((tm, aw), row), pl.BlockSpec((tm, aw), row), pl.BlockSpec((tm, aw), row),
                   pl.BlockSpec((sw // LANES, tm // SSM_CHUNK, SSM_CHUNK * LANES), lambda i: (0, i, 0)),
                   pl.BlockSpec((tm, d), row), pl.BlockSpec((tm, d), row),
                   pl.BlockSpec((blocks_per_tile, 1, aw), lambda i: (i, 0, 0))),
        scratch_shapes=[pltpu.VMEM((sw // LANES, tm, LANES), F32)],
        compiler_params=_cparams(1),
        name="inproj",
    )(xf, mod3, ln1_g.reshape(1, d), w_in_b, seg, q_gain, k_gain)


def _t5_bucket(rel):
    n = jnp.maximum(rel, 0)
    max_exact = NUM_BUCKETS // 2
    nf = jnp.maximum(n, 1).astype(F32)
    large = max_exact + (jnp.log(nf / max_exact) / math.log(MAX_DISTANCE / max_exact)
                         * (NUM_BUCKETS - max_exact)).astype(jnp.int32)
    large = jnp.minimum(large, NUM_BUCKETS - 1)
    return jnp.where(n < max_exact, n, large)


def _bias_tables(rel_bias):
    blk = MOBA_BLOCK
    assert blk + 1 >= MAX_DISTANCE
    rel = jnp.arange(blk)[None, :] - jnp.arange(blk)[:, None]
    table = rel_bias.astype(F32)
    table = table - table[NUM_BUCKETS - 1][None, :]

    def lookup(r):
        onehot = jax.nn.one_hot(_t5_bucket(r), NUM_BUCKETS, dtype=F32)
        return jnp.einsum('kqn,nh->hkq', onehot, table, precision=lax.Precision.HIGHEST)

    return lookup(rel), lookup(rel + blk)


def _select_blocks(gate_t, n_past):
    nb, tq = gate_t.shape
    blk = lax.broadcasted_iota(jnp.int32, (nb, tq), 0)
    beaten = jnp.zeros((nb, tq), jnp.int32)
    for m in range(nb):
        gm = gate_t[m:m + 1, :]
        wins = (gm > gate_t) | ((gm == gate_t) & (m < blk))
        beaten = beaten + jnp.where(wins & (m < n_past), 1, 0)
    return jnp.where((blk < n_past) & (beaten < MOBA_TOPK), 1.0, 0.0)


def _attn_kernel(q_ref, k_ref, vt_ref, km_ref, bias_ref, o_ref, sel_ref, s_ref):
    qi = pl.program_id(2)
    tq = q_ref.shape[0]
    blk = MOBA_BLOCK
    hd = HEAD_DIM
    heads = range(HEADS_PER_STEP)
    q = q_ref[...]
    lane = lax.broadcasted_iota(jnp.int32, (tq, LANES), 1)
    kpos = lax.broadcasted_iota(jnp.int32, (blk, tq), 0)
    qpos = lax.broadcasted_iota(jnp.int32, (blk, tq), 1)
    scale = hd ** -0.5
    n_far = jnp.maximum(qi - 1, 0)
    n_pairs = (n_far + 1) // 2
    jp = jnp.maximum(qi - 1, 0)
    k_own = k_ref[pl.ds(pl.multiple_of(qi * blk, blk), blk), :]
    k_prev = k_ref[pl.ds(pl.multiple_of(jp * blk, blk), blk), :]

    qbs = []
    for h in heads:
        in_head = (lane >= h * hd) & (lane < (h + 1) * hd)
        qm = jnp.where(in_head, q, 0.0)
        gate_t = lax.dot_general(km_ref[0], qm, (((1,), (1,)), ((), ())),
                                 precision=lax.Precision.HIGHEST, preferred_element_type=F32)
        sel_ref[h] = _select_blocks(gate_t, qi)
        qbs.append((qm * scale).astype(BF16))

    def pair_scores(j):
        kb = k_ref[pl.ds(pl.multiple_of(j * blk, blk), 2 * blk), :]
        return [_bdot_nt(kb, qbs[h]) for h in heads]

    ones_rows = jnp.ones((BF16_ROWS, blk), BF16)

    def probs(s, m):
        return jnp.exp((s - m).astype(BF16))

    def attend(h, p, blocks):
        acc = None
        for i, j in enumerate(blocks):
            lhs = jnp.concatenate([vt_ref[0, j, h * hd:(h + 1) * hd, :], ones_rows], axis=0)
            part = jnp.dot(lhs, p[i * blk:(i + 1) * blk, :], preferred_element_type=F32)
            acc = part if acc is None else acc + part
        return acc

    for h, s in enumerate(pair_scores(0)):
        s_ref[h] = s

    carries = []
    for h in heads:
        s_prev = _bdot_nt(k_prev, qbs[h]) + bias_ref[h, 0:blk, :]
        s_prev = jnp.where(sel_ref[h, pl.ds(jp, 1), :] > 0.5, s_prev, MASK_VALUE)
        s_own = _bdot_nt(k_own, qbs[h]) + bias_ref[h, blk:2 * blk, :]
        s_own = jnp.where(kpos <= qpos, s_own, MASK_VALUE)
        s = jnp.concatenate([s_prev, s_own], axis=0)
        m = jnp.max(s, axis=0, keepdims=True)
        carries.append((m, attend(h, probs(s, m), (jp, qi))))

    def far_pair(pi, carries):
        j = 2 * pi
        s_cur = [s_ref[h] for h in heads]
        for h, s in enumerate(pair_scores(2 * jnp.minimum(pi + 1, n_pairs - 1))):
            s_ref[h] = s
        second_is_far = j + 1 < n_far
        out = []
        for h in heads:
            m, acc = carries[h]
            c0 = sel_ref[h, pl.ds(j, 1), :] > 0.5
            c1 = (sel_ref[h, pl.ds(j + 1, 1), :] > 0.5) & second_is_far
            chosen = jnp.concatenate([jnp.broadcast_to(c0, (blk, tq)), jnp.broadcast_to(c1, (blk, tq))], axis=0)
            s = jnp.where(chosen, s_cur[h], MASK_VALUE)
            m_new = jnp.maximum(m, jnp.max(s, axis=0, keepdims=True))
            acc = jnp.exp(m - m_new) * acc + attend(h, probs(s, m_new), (j, j + 1))
            out.append((m_new, acc))
        return tuple(out)

    carries = lax.fori_loop(0, n_pairs, far_pair, tuple(carries))
    out_t = jnp.concatenate([acc[:hd] / acc[hd:hd + 1] for _, acc in carries], axis=0)
    o_ref[...] = out_t.T.astype(o_ref.dtype)


def _moba_attention(q, k, v, kmean, rel_bias, bsz, seq):
    n, aw = q.shape
    blk = MOBA_BLOCK
    nb = seq // blk
    assert nb >= 2
    own, prev = _bias_tables(rel_bias)
    bias = jnp.concatenate([prev, own], axis=1)
    hps = HEADS_PER_STEP
    npair = aw // LANES
    vt = v.reshape(bsz, nb, blk, aw).transpose(0, 1, 3, 2)
    return pl.pallas_call(
        _attn_kernel,
        out_shape=jax.ShapeDtypeStruct((n, aw), BF16),
        grid=(bsz, npair, nb),
        in_specs=[pl.BlockSpec((blk, LANES), lambda b, hp, qi: (b * nb + qi, hp)),
                  pl.BlockSpec((seq, LANES), lambda b, hp, qi: (b, hp)),
                  pl.BlockSpec((1, nb, LANES, blk), lambda b, hp, qi: (b, 0, hp, 0)),
                  pl.BlockSpec((1, nb, LANES), lambda b, hp, qi: (b, 0, hp)),
                  pl.BlockSpec((hps, 2 * blk, blk), lambda b, hp, qi: (hp, 0, 0))],
        out_specs=pl.BlockSpec((blk, LANES), lambda b, hp, qi: (b * nb + qi, hp)),
        scratch_shapes=[pltpu.VMEM((hps, nb, blk), F32), pltpu.VMEM((hps, 2 * blk, blk), F32)],
        compiler_params=_cparams(3),
        name="moba_attention",
    )(q, k, vt, kmean.reshape(bsz, nb, aw), bias)


def _s5_operators(lambda_re, lambda_im, log_dt, b_re, b_im, c_re, c_im, d_skip, n_chunks):
    hi = lax.Precision.HIGHEST
    L, G, P, C = SSM_CHUNK, SSM_GROUPS, SSM_STATE, SSM_GROUP
    lam_re = jnp.minimum(lambda_re.astype(F32), -1e-4)
    lam_im = lambda_im.astype(F32)
    dt = jnp.exp(log_dt.astype(F32))[:, None]
    z_re, z_im = lam_re * dt, lam_im * dt

    def a_pow(nvec):
        nv = jnp.asarray(nvec, F32)[:, None, None]
        mag = jnp.exp(nv * z_re)
        return mag * jnp.cos(nv * z_im), mag * jnp.sin(nv * z_im)

    a_re, a_im = a_pow([1.0])
    a_re, a_im = a_re[0], a_im[0]
    den = lam_re * lam_re + lam_im * lam_im
    nr = a_re - 1.0
    coef_re = (nr * lam_re + a_im * lam_im) / den
    coef_im = (a_im * lam_re - nr * lam_im) / den
    br, bi = b_re.astype(F32), b_im.astype(F32)
    bbar_re = coef_re[..., None] * br - coef_im[..., None] * bi
    bbar_im = coef_re[..., None] * bi + coef_im[..., None] * br
    cr, ci = c_re.astype(F32), c_im.astype(F32)

    pw_re, pw_im = a_pow(np.arange(L + 1))
    cb_re = cr[None] * pw_re[:, :, None, :] - ci[None] * pw_im[:, :, None, :]
    cb_im = cr[None] * pw_im[:, :, None, :] + ci[None] * pw_re[:, :, None, :]
    kern = (jnp.einsum('jgop,gpi->gijo', cb_re[:L], bbar_re, precision=hi)
            - jnp.einsum('jgop,gpi->gijo', cb_im[:L], bbar_im, precision=hi)).reshape(G, C, L * C)
    t_op = jnp.stack([jnp.pad(kern[:, :, :(L - s) * C], ((0, 0), (0, 0), (s * C, 0))) for s in range(L)], axis=1)
    d_g = d_skip.astype(F32).reshape(G, 1, C, 1)
    on_diag = (lax.broadcasted_iota(jnp.int32, (1, L, C, L * C), 3)
               == lax.broadcasted_iota(jnp.int32, (1, L, C, L * C), 1) * C
               + lax.broadcasted_iota(jnp.int32, (1, L, C, L * C), 2))
    t_op = (t_op + jnp.where(on_diag, d_g, 0.0)).reshape(G, L * C, L * C)

    rp_re, rp_im = jnp.flip(pw_re[:L], 0), jnp.flip(pw_im[:L], 0)
    p_re = rp_re[..., None] * bbar_re[None] - rp_im[..., None] * bbar_im[None]
    p_im = rp_re[..., None] * bbar_im[None] + rp_im[..., None] * bbar_re[None]
    p_op = jnp.concatenate([p_re, p_im], axis=2)
    p_op = p_op.transpose(1, 0, 3, 2).reshape(G, L * C, 2 * P)

    q_re = cb_re[1:].transpose(1, 3, 0, 2)
    q_im = -cb_im[1:].transpose(1, 3, 0, 2)
    q_op = jnp.concatenate([q_re, q_im], axis=1).reshape(G, 2 * P, L * C)

    n_steps = max(1, int(math.ceil(math.log2(n_chunks))))
    dk_re, dk_im = a_pow([float(L * 2 ** k) for k in range(n_steps)])
    GB = LANES // C
    NB = G // GB
    lc = np.arange(L * C)
    wide = np.arange(L * LANES)
    expand_tc = jnp.asarray((lc[:, None] // C == wide[None, :] // LANES) & (lc[:, None] % C == wide[None, :] % C), BF16)
    st = np.arange(2 * P)
    wide_st = np.arange(2 * GB * P)
    expand_st = jnp.asarray((st[:, None] // P == wide_st[None, :] // (GB * P))
                            & (st[:, None] % P == wide_st[None, :] % P), BF16)
    g_of_wide = (jnp.arange(L * LANES) // C) % GB
    g_of_state = (jnp.arange(2 * GB * P) // P) % GB

    def widen(rows, expand, g_row, g_col):
        full = jnp.einsum('brk,kc->brc', rows.astype(BF16), expand, preferred_element_type=F32)
        return jnp.where(g_row[:, None] == g_col[None, :], full, 0.0).astype(BF16)

    t_rows = t_op.reshape(NB, GB, L, C, L * C).transpose(0, 2, 1, 3, 4).reshape(NB, L * LANES, L * C)
    p_rows = p_op.reshape(NB, GB, L, C, 2 * P).transpose(0, 2, 1, 3, 4).reshape(NB, L * LANES, 2 * P)
    q_rows = q_op.reshape(NB, GB, 2, P, L * C).transpose(0, 2, 1, 3, 4).reshape(NB, 2 * GB * P, L * C)
    t_big = widen(t_rows, expand_tc, g_of_wide, g_of_wide)
    p_big = widen(p_rows, expand_st, g_of_wide, g_of_state)
    q_big = widen(q_rows, expand_tc, g_of_state, g_of_wide)
    dk_re = dk_re.reshape(-1, NB, GB * P)
    dk_im = dk_im.reshape(-1, NB, GB * P)
    a1 = jnp.concatenate([dk_re, dk_re], axis=-1).transpose(1, 0, 2)
    a2 = jnp.concatenate([-dk_im, dk_im], axis=-1).transpose(1, 0, 2)
    return t_big.astype(BF16), p_big.astype(BF16), q_big.astype(BF16), a1, a2


def _s5_kernel(x_ref, t_ref, p_ref, q_ref, a1_ref, a2_ref, y_ref):
    x = x_ref[0]
    s = jnp.dot(x, p_ref[0], preferred_element_type=F32)
    n_chunks, width = s.shape
    chunk = lax.broadcasted_iota(jnp.int32, (n_chunks, width), 0)
    a1 = a1_ref[0]
    a2 = a2_ref[0]
    h = jnp.where(chunk >= 1, pltpu.roll(s, 1, axis=0), 0.0)
    for kk in range(a1.shape[0]):
        dist = 2 ** kk
        if dist >= n_chunks:
            break
        hs = jnp.where(chunk >= dist, pltpu.roll(h, dist, axis=0), 0.0)
        h = h + a1[kk:kk + 1, :] * hs + a2[kk:kk + 1, :] * pltpu.roll(hs, width // 2, axis=1)
    hb = h.astype(BF16)
    step = 2 * LANES
    for t in range(x.shape[1] // step):
        hi = (t + 1) * step
        y_ref[0, :, t * step:hi] = (jnp.dot(x[:, :hi], t_ref[0, :hi, t * step:hi], preferred_element_type=F32)
                                    + jnp.dot(hb, q_ref[0, :, t * step:hi], preferred_element_type=F32))


def _s5_scan(x_chunks, ops, bsz, seq):
    t_big, p_big, q_big, a1, a2 = ops
    nblk, rows, w = x_chunks.shape
    nc = seq // SSM_CHUNK
    sw = p_big.shape[2]
    col = lambda cb, b: (cb, 0, 0)
    return pl.pallas_call(
        _s5_kernel,
        out_shape=jax.ShapeDtypeStruct((nblk, rows, w), F32),
        grid=(nblk, bsz),
        in_specs=[pl.BlockSpec((1, nc, w), lambda cb, b: (cb, b, 0)),
                  pl.BlockSpec((1, w, w), col),
                  pl.BlockSpec((1, w, sw), col),
                  pl.BlockSpec((1, sw, w), col),
                  pl.BlockSpec((1,) + a1.shape[1:], col),
                  pl.BlockSpec((1,) + a2.shape[1:], col)],
        out_specs=pl.BlockSpec((1, nc, w), lambda cb, b: (cb, b, 0)),
        compiler_params=_cparams(2),
        name="s5_scan",
    )(x_chunks, t_big, p_big, q_big, a1, a2)


def _gelu_tanh(x):
    return 0.5 * x * (1.0 + jnp.tanh(math.sqrt(2.0 / math.pi) * (x + 0.044715 * (x * x * x))))


def _mix_kernel(x_ref, attn_ref, yssm_ref, ga_ref, gs_ref, mod_ref, ln_ref,
                wglu_ref, bglu_ref, wua_ref, wus_ref, wout_ref, wrt_ref, wsgu_ref, wsd_ref,
                h2_ref, base_ref, score_ref, y_scr):
    n_chunk = yssm_ref.shape[1]
    for cb in range(yssm_ref.shape[0]):
        for tau in range(SSM_CHUNK):
            y_scr[cb, pl.ds(tau, n_chunk, stride=SSM_CHUNK), :] = yssm_ref[cb, :, tau * LANES:(tau + 1) * LANES]
    g = _gelu_tanh(jnp.concatenate([y_scr[cb] for cb in range(yssm_ref.shape[0])], axis=1))
    glu = g * _sigmoid(_bdot(g, wglu_ref[...]) + bglu_ref[...])
    y_attn = jnp.dot(attn_ref[...], wua_ref[...], preferred_element_type=F32)
    y_ssm = _bdot(glu, wus_ref[...])
    mixed = _sigmoid(ga_ref[...].astype(F32)) * y_attn + _sigmoid(gs_ref[...].astype(F32)) * y_ssm
    gate1 = mod_ref[0, 2:3, :]
    x1 = x_ref[...] + gate1 * _bdot(mixed, wout_ref[...])
    h2 = _modulated_norm(x1, ln_ref[...], mod_ref[0, 3:4, :], mod_ref[0, 4:5, :])
    _store_rows_as_tiles(h2_ref, h2)
    h2b = h2.astype(BF16)
    score_ref[...] = _sigmoid(_bdot_nt(wrt_ref[...], h2b))
    gu = jnp.dot(h2b, wsgu_ref[...], preferred_element_type=F32)
    sd = wsd_ref.shape[0]
    shared = _bdot(_silu(gu[:, :sd]) * gu[:, sd:], wsd_ref[...])
    base_ref[...] = x1 + mod_ref[0, 5:6, :] * shared


def _mix(xf, attn, yssm, ga, gs, mod3, ln2_g, w, seq):
    n, d = xf.shape
    tm = TOKEN_TILE
    tiles_per_seq = seq // tm
    row = lambda i: (i, 0)
    const = lambda i: (0, 0)
    nt = d // LANES
    weights = [w["glu"], w["b_glu"], w["up_attn"], w["up_ssm"], w["out"], w["router_t"], w["sh_gu"], w["sh_down"]]
    return pl.pallas_call(
        _mix_kernel,
        out_shape=(jax.ShapeDtypeStruct((n * nt, LANES), F32),
                   jax.ShapeDtypeStruct((n, d), F32),
                   jax.ShapeDtypeStruct((N_EXPERTS, n), F32)),
        grid=(n // tm,),
        in_specs=[pl.BlockSpec((tm, d), row),
                  pl.BlockSpec((tm, attn.shape[1]), row),
                  pl.BlockSpec((yssm.shape[0], tm // SSM_CHUNK, yssm.shape[2]), lambda i: (0, i, 0)),
                  pl.BlockSpec((tm, d), row),
                  pl.BlockSpec((tm, d), row),
                  pl.BlockSpec((1, 6, d), lambda i: (i // tiles_per_seq, 0, 0)),
                  pl.BlockSpec((1, d), const)] + [pl.BlockSpec(a.shape, const) for a in weights],
        out_specs=(pl.BlockSpec((tm * nt, LANES), row), pl.BlockSpec((tm, d), row),
                   pl.BlockSpec((N_EXPERTS, tm), lambda i: (0, i))),
        scratch_shapes=[pltpu.VMEM((yssm.shape[0], tm, LANES), F32)],
        compiler_params=_cparams(1),
        name="mix",
    )(xf, attn, yssm, ga, gs, mod3, ln2_g.reshape(1, d), *weights)


def _route_kernel(score_ref, bias_ref, tri_ref, idx_ref, w_ref, rank_ref, cnt_ref, carry_ref):
    @pl.when(pl.program_id(0) == 0)
    def _():
        carry_ref[...] = jnp.zeros_like(carry_ref)

    scores = score_ref[...]
    ne, tn = scores.shape
    biased = scores + bias_ref[...]
    gsz = GROUP_SIZE
    sub = lax.broadcasted_iota(jnp.int32, (gsz, tn), 0)
    group_score = []
    for g in range(N_EXPERT_GROUPS):
        sg = biased[g * gsz:(g + 1) * gsz, :]
        m1 = jnp.max(sg, axis=0, keepdims=True)
        first = jnp.min(jnp.where(sg == m1, sub, gsz), axis=0, keepdims=True)
        m2 = jnp.max(jnp.where(sub == first, -jnp.inf, sg), axis=0, keepdims=True)
        group_score.append(m1 + m2)
    group_rows = []
    for g in range(N_EXPERT_GROUPS):
        beaten = jnp.zeros((1, tn), jnp.int32)
        for o in range(N_EXPERT_GROUPS):
            if o == g:
                continue
            wins = (group_score[o] > group_score[g])
            if o < g:
                wins = wins | (group_score[o] == group_score[g])
            beaten = beaten + jnp.where(wins, 1, 0)
        group_rows.append(jnp.broadcast_to(beaten < TOPK_GROUPS, (gsz, tn)))
    allowed = jnp.concatenate(group_rows, axis=0)
    cur = jnp.where(allowed, biased, MASK_VALUE)
    eio = lax.broadcasted_iota(jnp.int32, (ne, tn), 0)
    idx_rows, w_rows, hits = [], [], []
    for _ in range(TOP_K):
        vmax = jnp.max(cur, axis=0, keepdims=True)
        eidx = jnp.min(jnp.where(cur == vmax, eio, ne), axis=0, keepdims=True)
        hit = eio == eidx
        w_rows.append(jnp.sum(jnp.where(hit, scores, 0.0), axis=0, keepdims=True))
        idx_rows.append(eidx)
        hits.append(hit)
        cur = jnp.where(hit, -jnp.inf, cur)
    wts = jnp.concatenate(w_rows, axis=0)
    idx_ref[...] = jnp.concatenate(idx_rows, axis=0)
    w_ref[...] = wts / jnp.sum(wts, axis=0, keepdims=True) * ROUTED_SCALE

    onehot = jnp.zeros((ne, tn), F32)
    for hit in hits:
        onehot = onehot + jnp.where(hit, 1.0, 0.0)
    earlier = carry_ref[...] + _bdot(onehot, tri_ref[...])
    rank_rows = [jnp.sum(jnp.where(hit, earlier, 0.0), axis=0, keepdims=True) for hit in hits]
    rank_ref[...] = jnp.concatenate(rank_rows, axis=0).astype(jnp.int32)
    carry_ref[...] = carry_ref[...] + jnp.sum(onehot, axis=1, keepdims=True)
    cnt_ref[...] = carry_ref[...]


def _route(scores_t, router_bias):
    ne, n = scores_t.shape
    tn = ROUTE_TILE
    tri = jnp.asarray(np.arange(tn)[:, None] < np.arange(tn)[None, :], BF16)
    tok = lambda i: (0, i)
    const = lambda i: (0, 0)
    return pl.pallas_call(
        _route_kernel,
        out_shape=(jax.ShapeDtypeStruct((TOP_K, n), jnp.int32), jax.ShapeDtypeStruct((TOP_K, n), F32),
                   jax.ShapeDtypeStruct((TOP_K, n), jnp.int32), jax.ShapeDtypeStruct((ne, 1), F32)),
        grid=(n // tn,),
        in_specs=[pl.BlockSpec((ne, tn), tok), pl.BlockSpec((ne, 1), const), pl.BlockSpec((tn, tn), const)],
        out_specs=(pl.BlockSpec((TOP_K, tn), tok), pl.BlockSpec((TOP_K, tn), tok),
                   pl.BlockSpec((TOP_K, tn), tok), pl.BlockSpec((ne, 1), const)),
        scratch_shapes=[pltpu.VMEM((ne, 1), F32)],
        compiler_params=_cparams(1),
        name="route",
    )(scores_t, router_bias.reshape(ne, 1).astype(F32), tri)


def _dest_kernel(idx_ref, rank_ref, poffs_ref, dest_ref):
    idx = idx_ref[...]
    ne = poffs_ref.shape[0]
    eio = lax.broadcasted_iota(jnp.int32, (ne, idx.shape[1]), 0)
    poffs = poffs_ref[...]
    rows = [jnp.sum(jnp.where(eio == idx[kk:kk + 1, :], poffs, 0.0), axis=0, keepdims=True)
            for kk in range(idx.shape[0])]
    dest_ref[...] = rank_ref[...] + jnp.concatenate(rows, axis=0).astype(jnp.int32)


def _dest_slots(idx_t, rank_t, poffs):
    k, n = idx_t.shape
    tn = ROUTE_TILE
    ne = poffs.shape[0]
    tok = lambda i: (0, i)
    return pl.pallas_call(
        _dest_kernel,
        out_shape=jax.ShapeDtypeStruct((k, n), jnp.int32),
        grid=(n // tn,),
        in_specs=[pl.BlockSpec((k, tn), tok), pl.BlockSpec((k, tn), tok), pl.BlockSpec((ne, 1), lambda i: (0, 0))],
        out_specs=pl.BlockSpec((k, tn), tok),
        compiler_params=_cparams(1),
        name="dest_slots",
    )(idx_t, rank_t, poffs.astype(F32).reshape(ne, 1))


def _dispatch_tables(idx_t, counts, n_blocks):
    k, n = idx_t.shape
    n_assign = k * n
    rb = ROW_BLOCK
    ne = counts.shape[0]
    counts = counts.reshape(ne).astype(jnp.int32)
    offs = jnp.cumsum(counts) - counts
    nblk = (counts + rb - 1) // rb
    bend = jnp.cumsum(nblk)
    bstart = bend - nblk
    blocks = jnp.arange(n_blocks, dtype=jnp.int32)
    blk_e = jnp.minimum(jnp.sum((bend[None, :] <= blocks[:, None]).astype(jnp.int32), axis=1), ne - 1)
    mine = blk_e[:, None] == jnp.arange(ne, dtype=jnp.int32)[None, :]
    src_start = jnp.sum(jnp.where(mine, (offs - bstart * rb)[None, :], 0), axis=1) + blocks * rb
    tok = lax.broadcasted_iota(jnp.int32, (k, n), 1)
    kk = lax.broadcasted_iota(jnp.int32, (k, n), 0)
    assert ne * n_assign < 2 ** 31
    keys = jnp.sort((idx_t * n_assign + tok * k + kk).reshape(n_assign))
    tok_sorted = (keys % n_assign) // k
    experts = jnp.arange(ne, dtype=jnp.int32)
    nonempty = nblk > 0
    run = jnp.cumsum(nonempty.astype(jnp.int32)) - 1
    later = (experts[None, :] > experts[:, None]) & nonempty[None, :]
    next_e = jnp.min(jnp.where(later, experts[None, :], ne), axis=1)
    next_e = jnp.where(next_e == ne, -1, next_e)
    per_expert = (run % 2) * 2 + (next_e + 1) * 4
    winfo = (jnp.sum(jnp.where(mine, per_expert[None, :], 0), axis=1)
             + (blocks == jnp.sum(jnp.where(mine, bstart[None, :], 0), axis=1)).astype(jnp.int32))
    return tok_sorted, blk_e, src_start, bend[ne - 1:ne], winfo.astype(jnp.int32), bstart * rb


def _ffn_kernel(tok_ref, blk_e_ref, start_ref, used_ref, winfo_ref, h2_hbm, wg_hbm, wu_hbm, wd_hbm, y_ref,
                xbuf, wg_buf, wu_buf, wd_buf, sems, wsems):
    b = pl.program_id(0)
    rb = ROW_BLOCK
    d = wg_hbm.shape[1]
    nt = d // LANES
    n_assign = tok_ref.shape[0]
    n_used = used_ref[0]
    n_blocks = pl.num_programs(0)

    info = winfo_ref[b]
    first_of_expert = (info & 1) == 1
    wslot = (info >> 1) & 1
    next_expert = (info >> 2) - 1

    def weight_copies(e, s):
        return [pltpu.make_async_copy(src.at[e], dst.at[s], wsems.at[s])
                for src, dst in ((wg_hbm, wg_buf), (wu_hbm, wu_buf), (wd_hbm, wd_buf))]

    @pl.when(b == 0)
    def _():
        for c in weight_copies(blk_e_ref[0], wslot):
            c.start()

    @pl.when((b < n_used) & first_of_expert)
    def _():
        for c in weight_copies(blk_e_ref[b], wslot):
            c.wait()

        @pl.when(next_expert >= 0)
        def _():
            for c in weight_copies(next_expert, 1 - wslot):
                c.start()

    def start_rows(blk, s, inline):
        start = start_ref[blk]

        def one(r, queue):
            tok = tok_ref[jnp.minimum(start + r, n_assign - 1)]
            pltpu.make_async_copy(h2_hbm.at[pl.ds(pl.multiple_of(tok * nt, nt), nt)],
                                  xbuf.at[s, pl.ds(pl.multiple_of(r * nt, nt), nt)], sems.at[s]
                                  ).start(priority=queue)

        if inline:
            for r in range(rb):
                one(r, r % DMA_QUEUES)
        else:
            def group(g, c):
                for i in range(ISSUE_UNROLL):
                    one(g * ISSUE_UNROLL + i, i % DMA_QUEUES)
                return c

            lax.fori_loop(0, rb // ISSUE_UNROLL, group, 0)

    def wait_rows(s):
        pltpu.make_async_copy(h2_hbm.at[pl.ds(0, rb * nt)], xbuf.at[s], sems.at[s]).wait()

    nbuf = xbuf.shape[0]
    ahead = nbuf - 1

    @pl.when(b == 0)
    def _():
        for a in range(ahead):
            start_rows(jnp.minimum(a, n_blocks - 1), a, False)

    for s in range(nbuf):
        @pl.when((b < n_used) & (b % nbuf == s))
        def _(s=s):
            wait_rows(s)
            xb = _load_rows_from_tiles(xbuf.at[s], rb, d).astype(BF16)
            start_rows(jnp.minimum(b + ahead, n_blocks - 1), (s + ahead) % nbuf, True)
            hg = jnp.dot(xb, wg_buf[wslot].astype(BF16), preferred_element_type=F32)
            hu = jnp.dot(xb, wu_buf[wslot].astype(BF16), preferred_element_type=F32)
            _store_rows_as_tiles(y_ref, _bdot(_silu(hg) * hu, wd_buf[wslot]))

    @pl.when(b == n_used - 1)
    def _():
        for a in range(1, nbuf):
            wait_rows((b + a) % nbuf)

    @pl.when(b >= n_used)
    def _():
        y_ref[...] = jnp.zeros_like(y_ref)


def _expert_ffn(h2_tiles, tok_sorted, blk_e, src_start, n_used, winfo, w_gate, w_up, w_down):
    n_blocks = blk_e.shape[0]
    rb = ROW_BLOCK
    _, d, ed = w_gate.shape
    nt = d // LANES
    hbm = pl.BlockSpec(memory_space=pl.ANY)
    grid_spec = pltpu.PrefetchScalarGridSpec(
        num_scalar_prefetch=5,
        grid=(n_blocks,),
        in_specs=[hbm, hbm, hbm, hbm],
        out_specs=pl.BlockSpec((rb * nt, LANES), lambda i, tok, be, st, nu, wi: (i, 0)),
        scratch_shapes=[pltpu.VMEM((FFN_ROW_BUFFERS, rb * nt, LANES), F32),
                        pltpu.VMEM((2, d, ed), w_gate.dtype), pltpu.VMEM((2, d, ed), w_up.dtype),
                        pltpu.VMEM((2, ed, d), w_down.dtype),
                        pltpu.SemaphoreType.DMA((FFN_ROW_BUFFERS,)), pltpu.SemaphoreType.DMA((2,))],
    )
    return pl.pallas_call(
        _ffn_kernel,
        out_shape=jax.ShapeDtypeStruct((n_blocks * rb * nt, LANES), F32),
        grid_spec=grid_spec,
        compiler_params=_cparams(1),
        name="expert_ffn",
    )(tok_sorted, blk_e, src_start, n_used, winfo, h2_tiles, w_gate, w_up, w_down)


def _combine_kernel(dest_ref, y_hbm, base_ref, wt_ref, mod_ref, o_ref, buf, sems):
    i = pl.program_id(0)
    n_steps = pl.num_programs(0)
    tm, d = base_ref.shape
    k = wt_ref.shape[1]
    nt = d // LANES
    n_rows = k * tm
    slot = i % 2

    def issue(step, s):
        first = step * n_rows

        def group(g, c):
            for i in range(ISSUE_UNROLL):
                r = g * ISSUE_UNROLL + i
                src = dest_ref[first + r]
                pltpu.make_async_copy(y_hbm.at[pl.ds(pl.multiple_of(src * nt, nt), nt)],
                                      buf.at[s, pl.ds(pl.multiple_of(r * nt, nt), nt)], sems.at[s]
                                      ).start(priority=i % DMA_QUEUES)
            return c

        lax.fori_loop(0, n_rows // ISSUE_UNROLL, group, 0)

    @pl.when(i == 0)
    def _():
        issue(0, 0)

    @pl.when(i + 1 < n_steps)
    def _():
        issue(i + 1, 1 - slot)

    pltpu.make_async_copy(y_hbm.at[pl.ds(0, n_rows * nt)], buf.at[slot], sems.at[slot]).wait()
    rows = buf.at[slot]
    wts = wt_ref[...]
    gate2 = mod_ref[0, 5:6, :]
    for c in range(nt):
        cols = slice(c * LANES, (c + 1) * LANES)
        routed = jnp.zeros((tm, LANES), F32)
        for kk in range(k):
            routed = routed + wts[:, kk:kk + 1] * rows[pl.ds(kk * tm * nt + c, tm, stride=nt), :]
        o_ref[:, cols] = base_ref[:, cols] + gate2[:, cols] * routed


def _combine(y_tiles, dest_t, w_t, base, mod3, seq):
    n, d = base.shape
    tm = COMBINE_TILE
    k = dest_t.shape[0]
    nt = d // LANES
    tiles_per_seq = seq // tm
    dest_tiles = dest_t.reshape(k, n // tm, tm).transpose(1, 0, 2).reshape(n * k)
    grid_spec = pltpu.PrefetchScalarGridSpec(
        num_scalar_prefetch=1,
        grid=(n // tm,),
        in_specs=[pl.BlockSpec(memory_space=pl.ANY),
                  pl.BlockSpec((tm, d), lambda i, dst: (i, 0)),
                  pl.BlockSpec((tm, k), lambda i, dst: (i, 0)),
                  pl.BlockSpec((1, 6, d), lambda i, dst: (i // tiles_per_seq, 0, 0))],
        out_specs=pl.BlockSpec((tm, d), lambda i, dst: (i, 0)),
        scratch_shapes=[pltpu.VMEM((2, k * tm * nt, LANES), F32), pltpu.SemaphoreType.DMA((2,))],
    )
    return pl.pallas_call(
        _combine_kernel,
        out_shape=jax.ShapeDtypeStruct((n, d), F32),
        grid_spec=grid_spec,
        compiler_params=_cparams(1),
        name="combine",
    )(dest_tiles, y_tiles, base, w_t.T, mod3)


def _hybrid_layer(x, cond, rel_bias, w_ada, b_ada, ln1_g, w_in, q_norm_g, k_norm_g,
                  ssm_lambda_re, ssm_lambda_im, ssm_log_dt, ssm_b_re, ssm_b_im, ssm_c_re, ssm_c_im,
                  ssm_d, ssm_w_glu, ssm_b_glu, w_up_attn, w_up_ssm, w_out, ln2_g,
                  w_router, router_bias, w_exp_gate, w_exp_up, w_exp_down,
                  w_sh_gate, w_sh_up, w_sh_down):
    bsz, seq, d = x.shape
    n = bsz * seq
    xf = x.reshape(n, d)
    mod3 = _adaln(cond, w_ada, b_ada).reshape(bsz, 6, d)

    q_gain = jnp.tile(q_norm_g.astype(F32), ATTN_HEADS).reshape(1, ATTN_WIDTH)
    k_gain = jnp.tile(k_norm_g.astype(F32), ATTN_HEADS).reshape(1, ATTN_WIDTH)
    q, k, v, u, ga, gs, kmean = _inproj(xf, mod3, ln1_g, w_in.astype(BF16), q_gain, k_gain, seq)

    attn = _moba_attention(q, k, v, kmean, rel_bias, bsz, seq)
    ops = _s5_operators(ssm_lambda_re, ssm_lambda_im, ssm_log_dt, ssm_b_re, ssm_b_im,
                        ssm_c_re, ssm_c_im, ssm_d, seq // SSM_CHUNK)
    yssm = _s5_scan(u, ops, bsz, seq)

    weights = {
        "glu": ssm_w_glu.astype(BF16), "b_glu": ssm_b_glu.astype(F32).reshape(1, -1),
        "up_attn": w_up_attn.astype(BF16), "up_ssm": w_up_ssm.astype(BF16), "out": w_out.astype(BF16),
        "router_t": w_router.T.astype(BF16),
        "sh_gu": jnp.concatenate([w_sh_gate, w_sh_up], axis=1).astype(BF16),
        "sh_down": w_sh_down.astype(BF16),
    }
    h2, base, scores_t = _mix(xf, attn, yssm, ga, gs, mod3, ln2_g, weights, seq)

    idx_t, w_t, rank_t, counts = _route(scores_t, router_bias)
    n_blocks = -(-(n * TOP_K) // ROW_BLOCK) + N_EXPERTS
    tok_sorted, blk_e, src_start, n_used, winfo, poffs = _dispatch_tables(idx_t, counts, n_blocks)
    dest_t = _dest_slots(idx_t, rank_t, poffs)
    y_tiles = _expert_ffn(h2, tok_sorted, blk_e, src_start, n_used, winfo, w_exp_gate, w_exp_up, w_exp_down)
    out = _combine(y_tiles, dest_t, w_t, base, mod3, seq)
    return out.reshape(bsz, seq, d)


def kernel(x, c, rel_bias, w_ada, b_ada, ln1_g, w_in, q_norm_g, k_norm_g, ssm_lambda_re, ssm_lambda_im, ssm_log_dt, ssm_b_re, ssm_b_im, ssm_c_re, ssm_c_im, ssm_d, ssm_w_glu, ssm_b_glu, w_up_attn, w_up_ssm, w_out, ln2_g, w_router, router_bias, w_exp_gate, w_exp_up, w_exp_down, w_sh_gate, w_sh_up, w_sh_down):
    for l in range(w_ada.shape[0]):
        x = _hybrid_layer(x, c, rel_bias, w_ada[l], b_ada[l], ln1_g[l], w_in[l], q_norm_g[l], k_norm_g[l],
                          ssm_lambda_re[l], ssm_lambda_im[l], ssm_log_dt[l], ssm_b_re[l], ssm_b_im[l],
                          ssm_c_re[l], ssm_c_im[l], ssm_d[l], ssm_w_glu[l], ssm_b_glu[l],
                          w_up_attn[l], w_up_ssm[l], w_out[l], ln2_g[l], w_router[l], router_bias[l],
                          w_exp_gate[l], w_exp_up[l], w_exp_down[l], w_sh_gate[l], w_sh_up[l], w_sh_down[l])
    return x
```

```python
import functools
import math

import numpy as np
import jax
import jax.numpy as jnp
from jax import lax
from jax.experimental import pallas as pl
from jax.experimental.pallas import tpu as pltpu

F32 = jnp.float32
BF16 = jnp.bfloat16

ATTN_HEADS = 8
HEAD_DIM = 64
ATTN_WIDTH = ATTN_HEADS * HEAD_DIM
MOBA_BLOCK = 256
MOBA_TOPK = 3
NUM_BUCKETS = 32
MAX_DISTANCE = 128
SSM_WIDTH = 512
SSM_GROUP = 16
SSM_GROUPS = SSM_WIDTH // SSM_GROUP
SSM_STATE = 64
N_EXPERTS = 256
TOP_K = 8
N_EXPERT_GROUPS = 8
TOPK_GROUPS = 4
GROUP_SIZE = N_EXPERTS // N_EXPERT_GROUPS
EXPERT_DIM = 256
ROUTED_SCALE = 2.5
EPS = 1e-6
MASK_VALUE = -1e30

LANES = 128
HEADS_PER_STEP = LANES // HEAD_DIM
SSM_CHUNK = 16
SUBLANES = 8
BF16_ROWS = 16
ROW_BLOCK = 256
TOKEN_TILE = 512
ROUTE_TILE = 256
COMBINE_TILE = 128
ISSUE_UNROLL = 8
DMA_QUEUES = 2
COMBINE_ROW_BUFFERS = 3
FFN_ROW_BUFFERS = 3
VMEM_LIMIT = 56 * 1024 * 1024


def _store_rows_as_tiles(ref, val):
    rows, d = val.shape
    nt = d // LANES
    for c in range(nt):
        ref[pl.ds(c, rows, stride=nt), :] = val[:, c * LANES:(c + 1) * LANES]


def _load_rows_from_tiles(ref, rows, d):
    nt = d // LANES
    return jnp.concatenate([ref[pl.ds(c, rows, stride=nt), :] for c in range(nt)], axis=1)


def _cparams(n_axes, vmem=VMEM_LIMIT):
    return pltpu.CompilerParams(dimension_semantics=("arbitrary",) * n_axes, vmem_limit_bytes=vmem)


def _sigmoid(x):
    return 1.0 / (1.0 + jnp.exp(-x))


def _silu(x):
    return x * _sigmoid(x)


def _bdot(a, b):
    return jnp.dot(a.astype(BF16), b.astype(BF16), preferred_element_type=F32)


def _bdot_nt(a, b):
    return lax.dot_general(a.astype(BF16), b.astype(BF16), (((1,), (1,)), ((), ())),
                           preferred_element_type=F32)


def _adaln_kernel(c_ref, w_ref, b_ref, o_ref):
    o_ref[...] = _bdot(_silu(c_ref[...]), w_ref[...]) + b_ref[...]


def _adaln(c, w_ada, b_ada):
    bsz, d = c.shape
    n_out = w_ada.shape[1]
    return pl.pallas_call(
        _adaln_kernel,
        out_shape=jax.ShapeDtypeStruct((bsz, n_out), F32),
        grid=(n_out // d,),
        in_specs=[pl.BlockSpec((bsz, d), lambda j: (0, 0)),
                  pl.BlockSpec((d, d), lambda j: (0, j)),
                  pl.BlockSpec((1, d), lambda j: (0, j))],
        out_specs=pl.BlockSpec((bsz, d), lambda j: (0, j)),
        compiler_params=_cparams(1),
        name="adaln",
    )(c, w_ada, b_ada.reshape(1, n_out))


def _modulated_norm(x, gain, shift, scale):
    y = x * lax.rsqrt(jnp.mean(x * x, axis=-1, keepdims=True) + EPS) * gain
    return y * (1.0 + scale) + shift


def _head_norm(t, seg, gain):
    ms = _bdot(t * t, seg)
    return t * lax.rsqrt(ms + EPS) * gain


def _inproj_kernel(x_ref, mod_ref, ln_ref, w_ref, seg_ref, qg_ref, kg_ref,
                   q_ref, k_ref, v_ref, u_ref, ga_ref, gs_ref, km_ref, u_scr):
    aw, sw, d = ATTN_WIDTH, SSM_WIDTH, x_ref.shape[1]
    h = _modulated_norm(x_ref[...], ln_ref[...], mod_ref[0, 0:1, :], mod_ref[0, 1:2, :]).astype(BF16)
    seg = seg_ref[...]
    q = jnp.dot(h, w_ref[:, 0:aw], preferred_element_type=F32)
    q_ref[...] = _head_norm(q, seg, qg_ref[...])
    k = jnp.dot(h, w_ref[:, aw:2 * aw], preferred_element_type=F32)
    kn = _head_norm(k, seg, kg_ref[...])
    k_ref[...] = kn.astype(BF16)
    for blk in range(km_ref.shape[0]):
        km_ref[blk] = jnp.mean(kn[blk * MOBA_BLOCK:(blk + 1) * MOBA_BLOCK, :], axis=0, keepdims=True)
    v_ref[...] = jnp.dot(h, w_ref[:, 2 * aw:3 * aw], preferred_element_type=F32).astype(BF16)
    o = 3 * aw
    u = jnp.dot(h, w_ref[:, o:o + sw], preferred_element_type=F32)
    n_chunk = u_scr.shape[1] // SSM_CHUNK
    for cb in range(sw // LANES):
        u_scr[cb] = u[:, cb * LANES:(cb + 1) * LANES]
        for sg in range(SSM_CHUNK):
            u_ref[cb, :, sg * LANES:(sg + 1) * LANES] = (
                u_scr[cb, pl.ds(sg, n_chunk, stride=SSM_CHUNK), :].astype(BF16))
    o += sw
    ga_ref[...] = jnp.dot(h, w_ref[:, o:o + d], preferred_element_type=F32).astype(BF16)
    o += d
    gs_ref[...] = jnp.dot(h, w_ref[:, o:o + d], preferred_element_type=F32).astype(BF16)


def _inproj(xf, mod3, ln1_g, w_in_b, q_gain, k_gain, seq):
    n, d = xf.shape
    tm = TOKEN_TILE
    assert tm % MOBA_BLOCK == 0 and seq % tm == 0
    blocks_per_tile = tm // MOBA_BLOCK
    tiles_per_seq = seq // tm
    aw, sw = ATTN_WIDTH, SSM_WIDTH
    head_of_lane = np.arange(aw) // HEAD_DIM
    seg = jnp.asarray((head_of_lane[:, None] == head_of_lane[None, :]) / HEAD_DIM, BF16)
    row = lambda i: (i, 0)
    const = lambda i: (0, 0)
    return pl.pallas_call(
        _inproj_kernel,
        out_shape=(jax.ShapeDtypeStruct((n, aw), F32),
                   jax.ShapeDtypeStruct((n, aw), BF16),
                   jax.ShapeDtypeStruct((n, aw), BF16),
                   jax.ShapeDtypeStruct((sw // LANES, n // SSM_CHUNK, SSM_CHUNK * LANES), BF16),
                   jax.ShapeDtypeStruct((n, d), BF16),
                   jax.ShapeDtypeStruct((n, d), BF16),
                   jax.ShapeDtypeStruct((n // MOBA_BLOCK, 1, aw), F32)),
        grid=(n // tm,),
        in_specs=[pl.BlockSpec((tm, d), row),
                  pl.BlockSpec((1, 6, d), lambda i: (i // tiles_per_seq, 0, 0)),
                  pl.BlockSpec((1, d), const),
                  pl.BlockSpec(w_in_b.shape, const),
                  pl.BlockSpec((aw, aw), const),
                  pl.BlockSpec((1, aw), const),
                  pl.BlockSpec((1, aw), const)],
        out_specs=(pl.BlockSpec((tm, aw), row), pl.BlockSpec((tm, aw), row), pl.BlockSpec((tm, aw), row),
                   pl.BlockSpec((sw // LANES, tm // SSM_CHUNK, SSM_CHUNK * LANES), lambda i: (0, i, 0)),
                   pl.BlockSpec((tm, d), row), pl.BlockSpec((tm, d), row),
                   pl.BlockSpec((blocks_per_tile, 1, aw), lambda i: (i, 0, 0))),
        scratch_shapes=[pltpu.VMEM((sw // LANES, tm, LANES), F32)],
        compiler_params=_cparams(1),
        name="inproj",
    )(xf, mod3, ln1_g.reshape(1, d), w_in_b, seg, q_gain, k_gain)


def _t5_bucket(rel):
    n = jnp.maximum(rel, 0)
    max_exact = NUM_BUCKETS // 2
    nf = jnp.maximum(n, 1).astype(F32)
    large = max_exact + (jnp.log(nf / max_exact) / math.log(MAX_DISTANCE / max_exact)
                         * (NUM_BUCKETS - max_exact)).astype(jnp.int32)
    large = jnp.minimum(large, NUM_BUCKETS - 1)
    return jnp.where(n < max_exact, n, large)


def _bias_tables(rel_bias):
    blk = MOBA_BLOCK
    assert blk + 1 >= MAX_DISTANCE
    rel = jnp.arange(blk)[None, :] - jnp.arange(blk)[:, None]
    table = rel_bias.astype(F32)
    table = table - table[NUM_BUCKETS - 1][None, :]

    def lookup(r):
        onehot = jax.nn.one_hot(_t5_bucket(r), NUM_BUCKETS, dtype=F32)
        return jnp.einsum('kqn,nh->hkq', onehot, table, precision=lax.Precision.HIGHEST)

    return lookup(rel), lookup(rel + blk)


def _select_blocks(gate_t, n_past):
    nb, tq = gate_t.shape
    blk = lax.broadcasted_iota(jnp.int32, (nb, tq), 0)
    beaten = jnp.zeros((nb, tq), jnp.int32)
    for m in range(nb):
        gm = gate_t[m:m + 1, :]
        wins = (gm > gate_t) | ((gm == gate_t) & (m < blk))
        beaten = beaten + jnp.where(wins & (m < n_past), 1, 0)
    return jnp.where((blk < n_past) & (beaten < MOBA_TOPK), 1.0, 0.0)


def _attn_kernel(q_ref, k_ref, vt_ref, km_ref, bias_ref, o_ref, sel_ref, s_ref):
    qi = pl.program_id(2)
    tq = q_ref.shape[0]
    blk = MOBA_BLOCK
    hd = HEAD_DIM
    heads = range(HEADS_PER_STEP)
    q = q_ref[...]
    lane = lax.broadcasted_iota(jnp.int32, (tq, LANES), 1)
    kpos = lax.broadcasted_iota(jnp.int32, (blk, tq), 0)
    qpos = lax.broadcasted_iota(jnp.int32, (blk, tq), 1)
    scale = hd ** -0.5
    n_far = jnp.maximum(qi - 1, 0)
    n_pairs = (n_far + 1) // 2
    jp = jnp.maximum(qi - 1, 0)
    k_own = k_ref[pl.ds(pl.multiple_of(qi * blk, blk), blk), :]
    k_prev = k_ref[pl.ds(pl.multiple_of(jp * blk, blk), blk), :]

    qbs = []
    for h in heads:
        in_head = (lane >= h * hd) & (lane < (h + 1) * hd)
        qm = jnp.where(in_head, q, 0.0)
        gate_t = lax.dot_general(km_ref[0], qm, (((1,), (1,)), ((), ())),
                                 precision=lax.Precision.HIGHEST, preferred_element_type=F32)
        sel_ref[h] = _select_blocks(gate_t, qi)
        qbs.append((qm * scale).astype(BF16))

    def pair_scores(j):
        kb = k_ref[pl.ds(pl.multiple_of(j * blk, blk), 2 * blk), :]
        return [_bdot_nt(kb, qbs[h]) for h in heads]

    ones_rows = jnp.ones((BF16_ROWS, blk), BF16)

    def probs(s, m):
        return jnp.exp((s - m).astype(BF16))

    def attend(h, p, blocks):
        acc = None
        for i, j in enumerate(blocks):
            lhs = jnp.concatenate([vt_ref[0, j, h * hd:(h + 1) * hd, :], ones_rows], axis=0)
            part = jnp.dot(lhs, p[i * blk:(i + 1) * blk, :], preferred_element_type=F32)
            acc = part if acc is None else acc + part
        return acc

    for h, s in enumerate(pair_scores(0)):
        s_ref[h] = s

    carries = []
    for h in heads:
        s_prev = _bdot_nt(k_prev, qbs[h]) + bias_ref[h, 0:blk, :]
        s_prev = jnp.where(sel_ref[h, pl.ds(jp, 1), :] > 0.5, s_prev, MASK_VALUE)
        s_own = _bdot_nt(k_own, qbs[h]) + bias_ref[h, blk:2 * blk, :]
        s_own = jnp.where(kpos <= qpos, s_own, MASK_VALUE)
        s = jnp.concatenate([s_prev, s_own], axis=0)
        m = jnp.max(s, axis=0, keepdims=True)
        carries.append((m, attend(h, probs(s, m), (jp, qi))))

    def far_pair(pi, carries):
        j = 2 * pi
        s_cur = [s_ref[h] for h in heads]
        for h, s in enumerate(pair_scores(2 * jnp.minimum(pi + 1, n_pairs - 1))):
            s_ref[h] = s
        second_is_far = j + 1 < n_far
        out = []
        for h in heads:
            m, acc = carries[h]
            c0 = sel_ref[h, pl.ds(j, 1), :] > 0.5
            c1 = (sel_ref[h, pl.ds(j + 1, 1), :] > 0.5) & second_is_far
            chosen = jnp.concatenate([jnp.broadcast_to(c0, (blk, tq)), jnp.broadcast_to(c1, (blk, tq))], axis=0)
            s = jnp.where(chosen, s_cur[h], MASK_VALUE)
            m_new = jnp.maximum(m, jnp.max(s, axis=0, keepdims=True))
            acc = jnp.exp(m - m_new) * acc + attend(h, probs(s, m_new), (j, j + 1))
            out.append((m_new, acc))
        return tuple(out)

    carries = lax.fori_loop(0, n_pairs, far_pair, tuple(carries))
    out_t = jnp.concatenate([acc[:hd] / acc[hd:hd + 1] for _, acc in carries], axis=0)
    o_ref[...] = out_t.T.astype(o_ref.dtype)


def _moba_attention(q, k, v, kmean, rel_bias, bsz, seq):
    n, aw = q.shape
    blk = MOBA_BLOCK
    nb = seq // blk
    assert nb >= 2
    own, prev = _bias_tables(rel_bias)
    bias = jnp.concatenate([prev, own], axis=1)
    hps = HEADS_PER_STEP
    npair = aw // LANES
    vt = v.reshape(bsz, nb, blk, aw).transpose(0, 1, 3, 2)
    return pl.pallas_call(
        _attn_kernel,
        out_shape=jax.ShapeDtypeStruct((n, aw), BF16),
        grid=(bsz, npair, nb),
        in_specs=[pl.BlockSpec((blk, LANES), lambda b, hp, qi: (b * nb + qi, hp)),
                  pl.BlockSpec((seq, LANES), lambda b, hp, qi: (b, hp)),
                  pl.BlockSpec((1, nb, LANES, blk), lambda b, hp, qi: (b, 0, hp, 0)),
                  pl.BlockSpec((1, nb, LANES), lambda b, hp, qi: (b, 0, hp)),
                  pl.BlockSpec((hps, 2 * blk, blk), lambda b, hp, qi: (hp, 0, 0))],
        out_specs=pl.BlockSpec((blk, LANES), lambda b, hp, qi: (b * nb + qi, hp)),
        scratch_shapes=[pltpu.VMEM((hps, nb, blk), F32), pltpu.VMEM((hps, 2 * blk, blk), F32)],
        compiler_params=_cparams(3),
        name="moba_attention",
    )(q, k, vt, kmean.reshape(bsz, nb, aw), bias)


def _s5_operators(lambda_re, lambda_im, log_dt, b_re, b_im, c_re, c_im, d_skip, n_chunks):
    hi = lax.Precision.HIGHEST
    L, G, P, C = SSM_CHUNK, SSM_GROUPS, SSM_STATE, SSM_GROUP
    lam_re = jnp.minimum(lambda_re.astype(F32), -1e-4)
    lam_im = lambda_im.astype(F32)
    dt = jnp.exp(log_dt.astype(F32))[:, None]
    z_re, z_im = lam_re * dt, lam_im * dt

    def a_pow(nvec):
        nv = jnp.asarray(nvec, F32)[:, None, None]
        mag = jnp.exp(nv * z_re)
        return mag * jnp.cos(nv * z_im), mag * jnp.sin(nv * z_im)

    a_re, a_im = a_pow([1.0])
    a_re, a_im = a_re[0], a_im[0]
    den = lam_re * lam_re + lam_im * lam_im
    nr = a_re - 1.0
    coef_re = (nr * lam_re + a_im * lam_im) / den
    coef_im = (a_im * lam_re - nr * lam_im) / den
    br, bi = b_re.astype(F32), b_im.astype(F32)
    bbar_re = coef_re[..., None] * br - coef_im[..., None] * bi
    bbar_im = coef_re[..., None] * bi + coef_im[..., None] * br
    cr, ci = c_re.astype(F32), c_im.astype(F32)

    pw_re, pw_im = a_pow(np.arange(L + 1))
    cb_re = cr[None] * pw_re[:, :, None, :] - ci[None] * pw_im[:, :, None, :]
    cb_im = cr[None] * pw_im[:, :, None, :] + ci[None] * pw_re[:, :, None, :]
    kern = (jnp.einsum('jgop,gpi->gijo', cb_re[:L], bbar_re, precision=hi)
            - jnp.einsum('jgop,gpi->gijo', cb_im[:L], bbar_im, precision=hi)).reshape(G, C, L * C)
    t_op = jnp.stack([jnp.pad(kern[:, :, :(L - s) * C], ((0, 0), (0, 0), (s * C, 0))) for s in range(L)], axis=1)
    d_g = d_skip.astype(F32).reshape(G, 1, C, 1)
    on_diag = (lax.broadcasted_iota(jnp.int32, (1, L, C, L * C), 3)
               == lax.broadcasted_iota(jnp.int32, (1, L, C, L * C), 1) * C
               + lax.broadcasted_iota(jnp.int32, (1, L, C, L * C), 2))
    t_op = (t_op + jnp.where(on_diag, d_g, 0.0)).reshape(G, L * C, L * C)

    rp_re, rp_im = jnp.flip(pw_re[:L], 0), jnp.flip(pw_im[:L], 0)
    p_re = rp_re[..., None] * bbar_re[None] - rp_im[..., None] * bbar_im[None]
    p_im = rp_re[..., None] * bbar_im[None] + rp_im[..., None] * bbar_re[None]
    p_op = jnp.concatenate([p_re, p_im], axis=2)
    p_op = p_op.transpose(1, 0, 3, 2).reshape(G, L * C, 2 * P)

    q_re = cb_re[1:].transpose(1, 3, 0, 2)
    q_im = -cb_im[1:].transpose(1, 3, 0, 2)
    q_op = jnp.concatenate([q_re, q_im], axis=1).reshape(G, 2 * P, L * C)

    n_steps = max(1, int(math.ceil(math.log2(n_chunks))))
    dk_re, dk_im = a_pow([float(L * 2 ** k) for k in range(n_steps)])
    GB = LANES // C
    NB = G // GB
    lc = np.arange(L * C)
    wide = np.arange(L * LANES)
    expand_tc = jnp.asarray((lc[:, None] // C == wide[None, :] // LANES) & (lc[:, None] % C == wide[None, :] % C), BF16)
    st = np.arange(2 * P)
    wide_st = np.arange(2 * GB * P)
    expand_st = jnp.asarray((st[:, None] // P == wide_st[None, :] // (GB * P))
                            & (st[:, None] % P == wide_st[None, :] % P), BF16)
    g_of_wide = (jnp.arange(L * LANES) // C) % GB
    g_of_state = (jnp.arange(2 * GB * P) // P) % GB

    def widen(rows, expand, g_row, g_col):
        full = jnp.einsum('brk,kc->brc', rows.astype(BF16), expand, preferred_element_type=F32)
        return jnp.where(g_row[:, None] == g_col[None, :], full, 0.0).astype(BF16)

    t_rows = t_op.reshape(NB, GB, L, C, L * C).transpose(0, 2, 1, 3, 4).reshape(NB, L * LANES, L * C)
    p_rows = p_op.reshape(NB, GB, L, C, 2 * P).transpose(0, 2, 1, 3, 4).reshape(NB, L * LANES, 2 * P)
    q_rows = q_op.reshape(NB, GB, 2, P, L * C).transpose(0, 2, 1, 3, 4).reshape(NB, 2 * GB * P, L * C)
    t_big = widen(t_rows, expand_tc, g_of_wide, g_of_wide)
    p_big = widen(p_rows, expand_st, g_of_wide, g_of_state)
    q_big = widen(q_rows, expand_tc, g_of_state, g_of_wide)
    dk_re = dk_re.reshape(-1, NB, GB * P)
    dk_im = dk_im.reshape(-1, NB, GB * P)
    a1 = jnp.concatenate([dk_re, dk_re], axis=-1).transpose(1, 0, 2)
    a2 = jnp.concatenate([-dk_im, dk_im], axis=-1).transpose(1, 0, 2)
    return t_big.astype(BF16), p_big.astype(BF16), q_big.astype(BF16), a1, a2


def _s5_kernel(x_ref, t_ref, p_ref, q_ref, a1_ref, a2_ref, y_ref):
    x = x_ref[0]
    s = jnp.dot(x, p_ref[0], preferred_element_type=F32)
    n_chunks, width = s.shape
    chunk = lax.broadcasted_iota(jnp.int32, (n_chunks, width), 0)
    a1 = a1_ref[0]
    a2 = a2_ref[0]
    h = jnp.where(chunk >= 1, pltpu.roll(s, 1, axis=0), 0.0)
    for kk in range(a1.shape[0]):
        dist = 2 ** kk
        if dist >= n_chunks:
            break
        hs = jnp.where(chunk >= dist, pltpu.roll(h, dist, axis=0), 0.0)
        h = h + a1[kk:kk + 1, :] * hs + a2[kk:kk + 1, :] * pltpu.roll(hs, width // 2, axis=1)
    hb = h.astype(BF16)
    step = 2 * LANES
    for t in range(x.shape[1] // step):
        hi = (t + 1) * step
        y_ref[0, :, t * step:hi] = (jnp.dot(x[:, :hi], t_ref[0, :hi, t * step:hi], preferred_element_type=F32)
                                    + jnp.dot(hb, q_ref[0, :, t * step:hi], preferred_element_type=F32))


def _s5_scan(x_chunks, ops, bsz, seq):
    t_big, p_big, q_big, a1, a2 = ops
    nblk, rows, w = x_chunks.shape
    nc = seq // SSM_CHUNK
    sw = p_big.shape[2]
    col = lambda cb, b: (cb, 0, 0)
    return pl.pallas_call(
        _s5_kernel,
        out_shape=jax.ShapeDtypeStruct((nblk, rows, w), F32),
        grid=(nblk, bsz),
        in_specs=[pl.BlockSpec((1, nc, w), lambda cb, b: (cb, b, 0)),
                  pl.BlockSpec((1, w, w), col),
                  pl.BlockSpec((1, w, sw), col),
                  pl.BlockSpec((1, sw, w), col),
                  pl.BlockSpec((1,) + a1.shape[1:], col),
                  pl.BlockSpec((1,) + a2.shape[1:], col)],
        out_specs=pl.BlockSpec((1, nc, w), lambda cb, b: (cb, b, 0)),
        compiler_params=_cparams(2),
        name="s5_scan",
    )(x_chunks, t_big, p_big, q_big, a1, a2)


def _gelu_tanh(x):
    return 0.5 * x * (1.0 + jnp.tanh(math.sqrt(2.0 / math.pi) * (x + 0.044715 * (x * x * x))))


def _mix_kernel(x_ref, attn_ref, yssm_ref, ga_ref, gs_ref, mod_ref, ln_ref,
                wglu_ref, bglu_ref, wua_ref, wus_ref, wout_ref, wrt_ref, wsgu_ref, wsd_ref,
                h2_ref, base_ref, score_ref, y_scr):
    n_chunk = yssm_ref.shape[1]
    for cb in range(yssm_ref.shape[0]):
        for tau in range(SSM_CHUNK):
            y_scr[cb, pl.ds(tau, n_chunk, stride=SSM_CHUNK), :] = yssm_ref[cb, :, tau * LANES:(tau + 1) * LANES]
    g = _gelu_tanh(jnp.concatenate([y_scr[cb] for cb in range(yssm_ref.shape[0])], axis=1))
    glu = g * _sigmoid(_bdot(g, wglu_ref[...]) + bglu_ref[...])
    y_attn = jnp.dot(attn_ref[...], wua_ref[...], preferred_element_type=F32)
    y_ssm = _bdot(glu, wus_ref[...])
    mixed = _sigmoid(ga_ref[...].astype(F32)) * y_attn + _sigmoid(gs_ref[...].astype(F32)) * y_ssm
    gate1 = mod_ref[0, 2:3, :]
    x1 = x_ref[...] + gate1 * _bdot(mixed, wout_ref[...])
    h2 = _modulated_norm(x1, ln_ref[...], mod_ref[0, 3:4, :], mod_ref[0, 4:5, :])
    _store_rows_as_tiles(h2_ref, h2)
    h2b = h2.astype(BF16)
    score_ref[...] = _sigmoid(_bdot_nt(wrt_ref[...], h2b))
    gu = jnp.dot(h2b, wsgu_ref[...], preferred_element_type=F32)
    sd = wsd_ref.shape[0]
    shared = _bdot(_silu(gu[:, :sd]) * gu[:, sd:], wsd_ref[...])
    base_ref[...] = x1 + mod_ref[0, 5:6, :] * shared


def _mix(xf, attn, yssm, ga, gs, mod3, ln2_g, w, seq):
    n, d = xf.shape
    tm = TOKEN_TILE
    tiles_per_seq = seq // tm
    row = lambda i: (i, 0)
    const = lambda i: (0, 0)
    nt = d // LANES
    weights = [w["glu"], w["b_glu"], w["up_attn"], w["up_ssm"], w["out"], w["router_t"], w["sh_gu"], w["sh_down"]]
    return pl.pallas_call(
        _mix_kernel,
        out_shape=(jax.ShapeDtypeStruct((n * nt, LANES), F32),
                   jax.ShapeDtypeStruct((n, d), F32),
                   jax.ShapeDtypeStruct((N_EXPERTS, n), F32)),
        grid=(n // tm,),
        in_specs=[pl.BlockSpec((tm, d), row),
                  pl.BlockSpec((tm, attn.shape[1]), row),
                  pl.BlockSpec((yssm.shape[0], tm // SSM_CHUNK, yssm.shape[2]), lambda i: (0, i, 0)),
                  pl.BlockSpec((tm, d), row),
                  pl.BlockSpec((tm, d), row),
                  pl.BlockSpec((1, 6, d), lambda i: (i // tiles_per_seq, 0, 0)),
                  pl.BlockSpec((1, d), const)] + [pl.BlockSpec(a.shape, const) for a in weights],
        out_specs=(pl.BlockSpec((tm * nt, LANES), row), pl.BlockSpec((tm, d), row),
                   pl.BlockSpec((N_EXPERTS, tm), lambda i: (0, i))),
        scratch_shapes=[pltpu.VMEM((yssm.shape[0], tm, LANES), F32)],
        compiler_params=_cparams(1),
        name="mix",
    )(xf, attn, yssm, ga, gs, mod3, ln2_g.reshape(1, d), *weights)


def _route_kernel(score_ref, bias_ref, tri_ref, idx_ref, w_ref, rank_ref, cnt_ref, carry_ref):
    @pl.when(pl.program_id(0) == 0)
    def _():
        carry_ref[...] = jnp.zeros_like(carry_ref)

    scores = score_ref[...]
    ne, tn = scores.shape
    biased = scores + bias_ref[...]
    gsz = GROUP_SIZE
    sub = lax.broadcasted_iota(jnp.int32, (gsz, tn), 0)
    group_score = []
    for g in range(N_EXPERT_GROUPS):
        sg = biased[g * gsz:(g + 1) * gsz, :]
        m1 = jnp.max(sg, axis=0, keepdims=True)
        first = jnp.min(jnp.where(sg == m1, sub, gsz), axis=0, keepdims=True)
        m2 = jnp.max(jnp.where(sub == first, -jnp.inf, sg), axis=0, keepdims=True)
        group_score.append(m1 + m2)
    group_rows = []
    for g in range(N_EXPERT_GROUPS):
        beaten = jnp.zeros((1, tn), jnp.int32)
        for o in range(N_EXPERT_GROUPS):
            if o == g:
                continue
            wins = (group_score[o] > group_score[g])
            if o < g:
                wins = wins | (group_score[o] == group_score[g])
            beaten = beaten + jnp.where(wins, 1, 0)
        group_rows.append(jnp.broadcast_to(beaten < TOPK_GROUPS, (gsz, tn)))
    allowed = jnp.concatenate(group_rows, axis=0)
    cur = jnp.where(allowed, biased, MASK_VALUE)
    eio = lax.broadcasted_iota(jnp.int32, (ne, tn), 0)
    idx_rows, w_rows, hits = [], [], []
    for _ in range(TOP_K):
        vmax = jnp.max(cur, axis=0, keepdims=True)
        eidx = jnp.min(jnp.where(cur == vmax, eio, ne), axis=0, keepdims=True)
        hit = eio == eidx
        w_rows.append(jnp.sum(jnp.where(hit, scores, 0.0), axis=0, keepdims=True))
        idx_rows.append(eidx)
        hits.append(hit)
        cur = jnp.where(hit, -jnp.inf, cur)
    wts = jnp.concatenate(w_rows, axis=0)
    idx_ref[...] = jnp.concatenate(idx_rows, axis=0)
    w_ref[...] = wts / jnp.sum(wts, axis=0, keepdims=True) * ROUTED_SCALE

    onehot = jnp.zeros((ne, tn), F32)
    for hit in hits:
        onehot = onehot + jnp.where(hit, 1.0, 0.0)
    earlier = carry_ref[...] + _bdot(onehot, tri_ref[...])
    rank_rows = [jnp.sum(jnp.where(hit, earlier, 0.0), axis=0, keepdims=True) for hit in hits]
    rank_ref[...] = jnp.concatenate(rank_rows, axis=0).astype(jnp.int32)
    carry_ref[...] = carry_ref[...] + jnp.sum(onehot, axis=1, keepdims=True)
    cnt_ref[...] = carry_ref[...]


def _route(scores_t, router_bias):
    ne, n = scores_t.shape
    tn = ROUTE_TILE
    tri = jnp.asarray(np.arange(tn)[:, None] < np.arange(tn)[None, :], BF16)
    tok = lambda i: (0, i)
    const = lambda i: (0, 0)
    return pl.pallas_call(
        _route_kernel,
        out_shape=(jax.ShapeDtypeStruct((TOP_K, n), jnp.int32), jax.ShapeDtypeStruct((TOP_K, n), F32),
                   jax.ShapeDtypeStruct((TOP_K, n), jnp.int32), jax.ShapeDtypeStruct((ne, 1), F32)),
        grid=(n // tn,),
        in_specs=[pl.BlockSpec((ne, tn), tok), pl.BlockSpec((ne, 1), const), pl.BlockSpec((tn, tn), const)],
        out_specs=(pl.BlockSpec((TOP_K, tn), tok), pl.BlockSpec((TOP_K, tn), tok),
                   pl.BlockSpec((TOP_K, tn), tok), pl.BlockSpec((ne, 1), const)),
        scratch_shapes=[pltpu.VMEM((ne, 1), F32)],
        compiler_params=_cparams(1),
        name="route",
    )(scores_t, router_bias.reshape(ne, 1).astype(F32), tri)


def _dest_kernel(idx_ref, rank_ref, poffs_ref, dest_ref):
    idx = idx_ref[...]
    ne = poffs_ref.shape[0]
    eio = lax.broadcasted_iota(jnp.int32, (ne, idx.shape[1]), 0)
    poffs = poffs_ref[...]
    rows = [jnp.sum(jnp.where(eio == idx[kk:kk + 1, :], poffs, 0.0), axis=0, keepdims=True)
            for kk in range(idx.shape[0])]
    dest_ref[...] = rank_ref[...] + jnp.concatenate(rows, axis=0).astype(jnp.int32)


def _dest_slots(idx_t, rank_t, poffs):
    k, n = idx_t.shape
    tn = ROUTE_TILE
    ne = poffs.shape[0]
    tok = lambda i: (0, i)
    return pl.pallas_call(
        _dest_kernel,
        out_shape=jax.ShapeDtypeStruct((k, n), jnp.int32),
        grid=(n // tn,),
        in_specs=[pl.BlockSpec((k, tn), tok), pl.BlockSpec((k, tn), tok), pl.BlockSpec((ne, 1), lambda i: (0, 0))],
        out_specs=pl.BlockSpec((k, tn), tok),
        compiler_params=_cparams(1),
        name="dest_slots",
    )(idx_t, rank_t, poffs.astype(F32).reshape(ne, 1))


def _dispatch_tables(idx_t, counts, n_blocks):
    k, n = idx_t.shape
    n_assign = k * n
    rb = ROW_BLOCK
    ne = counts.shape[0]
    counts = counts.reshape(ne).astype(jnp.int32)
    offs = jnp.cumsum(counts) - counts
    nblk = (counts + rb - 1) // rb
    bend = jnp.cumsum(nblk)
    bstart = bend - nblk
    blocks = jnp.arange(n_blocks, dtype=jnp.int32)
    blk_e = jnp.minimum(jnp.sum((bend[None, :] <= blocks[:, None]).astype(jnp.int32), axis=1), ne - 1)
    mine = blk_e[:, None] == jnp.arange(ne, dtype=jnp.int32)[None, :]
    src_start = jnp.sum(jnp.where(mine, (offs - bstart * rb)[None, :], 0), axis=1) + blocks * rb
    tok = lax.broadcasted_iota(jnp.int32, (k, n), 1)
    kk = lax.broadcasted_iota(jnp.int32, (k, n), 0)
    assert ne * n_assign < 2 ** 31
    keys = jnp.sort((idx_t * n_assign + tok * k + kk).reshape(n_assign))
    tok_sorted = (keys % n_assign) // k
    experts = jnp.arange(ne, dtype=jnp.int32)
    nonempty = nblk > 0
    run = jnp.cumsum(nonempty.astype(jnp.int32)) - 1
    later = (experts[None, :] > experts[:, None]) & nonempty[None, :]
    next_e = jnp.min(jnp.where(later, experts[None, :], ne), axis=1)
    next_e = jnp.where(next_e == ne, -1, next_e)
    per_expert = (run % 2) * 2 + (next_e + 1) * 4
    winfo = (jnp.sum(jnp.where(mine, per_expert[None, :], 0), axis=1)
             + (blocks == jnp.sum(jnp.where(mine, bstart[None, :], 0), axis=1)).astype(jnp.int32))
    return tok_sorted, blk_e, src_start, bend[ne - 1:ne], winfo.astype(jnp.int32), bstart * rb


def _ffn_kernel(tok_ref, blk_e_ref, start_ref, used_ref, winfo_ref, h2_hbm, wg_hbm, wu_hbm, wd_hbm, y_ref,
                xbuf, wg_buf, wu_buf, wd_buf, sems, wsems):
    b = pl.program_id(0)
    rb = ROW_BLOCK
    d = wg_hbm.shape[1]
    nt = d // LANES
    n_assign = tok_ref.shape[0]
    n_used = used_ref[0]
    n_blocks = pl.num_programs(0)

    info = winfo_ref[b]
    first_of_expert = (info & 1) == 1
    wslot = (info >> 1) & 1
    next_expert = (info >> 2) - 1

    def weight_copies(e, s):
        return [pltpu.make_async_copy(src.at[e], dst.at[s], wsems.at[s])
                for src, dst in ((wg_hbm, wg_buf), (wu_hbm, wu_buf), (wd_hbm, wd_buf))]

    @pl.when(b == 0)
    def _():
        for c in weight_copies(blk_e_ref[0], wslot):
            c.start()

    @pl.when((b < n_used) & first_of_expert)
    def _():
        for c in weight_copies(blk_e_ref[b], wslot):
            c.wait()

        @pl.when(next_expert >= 0)
        def _():
            for c in weight_copies(next_expert, 1 - wslot):
                c.start()

    def start_rows(blk, s, inline):
        start = start_ref[blk]

        def one(r, queue):
            tok = tok_ref[jnp.minimum(start + r, n_assign - 1)]
            pltpu.make_async_copy(h2_hbm.at[pl.ds(pl.multiple_of(tok * nt, nt), nt)],
                                  xbuf.at[s, pl.ds(pl.multiple_of(r * nt, nt), nt)], sems.at[s]
                                  ).start(priority=queue)

        if inline:
            for r in range(rb):
                one(r, r % DMA_QUEUES)
        else:
            def group(g, c):
                for i in range(ISSUE_UNROLL):
                    one(g * ISSUE_UNROLL + i, i % DMA_QUEUES)
                return c

            lax.fori_loop(0, rb // ISSUE_UNROLL, group, 0)

    def wait_rows(s):
        pltpu.make_async_copy(h2_hbm.at[pl.ds(0, rb * nt)], xbuf.at[s], sems.at[s]).wait()

    nbuf = xbuf.shape[0]
    ahead = nbuf - 1

    @pl.when(b == 0)
    def _():
        for a in range(ahead):
            start_rows(jnp.minimum(a, n_blocks - 1), a, False)

    for s in range(nbuf):
        @pl.when((b < n_used) & (b % nbuf == s))
        def _(s=s):
            wait_rows(s)
            xb = _load_rows_from_tiles(xbuf.at[s], rb, d).astype(BF16)
            start_rows(jnp.minimum(b + ahead, n_blocks - 1), (s + ahead) % nbuf, True)
            hg = jnp.dot(xb, wg_buf[wslot].astype(BF16), preferred_element_type=F32)
            hu = jnp.dot(xb, wu_buf[wslot].astype(BF16), preferred_element_type=F32)
            _store_rows_as_tiles(y_ref, _bdot(_silu(hg) * hu, wd_buf[wslot]))

    @pl.when(b == n_used - 1)
    def _():
        for a in range(1, nbuf):
            wait_rows((b + a) % nbuf)

    @pl.when(b >= n_used)
    def _():
        y_ref[...] = jnp.zeros_like(y_ref)


def _expert_ffn(h2_tiles, tok_sorted, blk_e, src_start, n_used, winfo, w_gate, w_up, w_down):
    n_blocks = blk_e.shape[0]
    rb = ROW_BLOCK
    _, d, ed = w_gate.shape
    nt = d // LANES
    hbm = pl.BlockSpec(memory_space=pl.ANY)
    grid_spec = pltpu.PrefetchScalarGridSpec(
        num_scalar_prefetch=5,
        grid=(n_blocks,),
        in_specs=[hbm, hbm, hbm, hbm],
        out_specs=pl.BlockSpec((rb * nt, LANES), lambda i, tok, be, st, nu, wi: (i, 0)),
        scratch_shapes=[pltpu.VMEM((FFN_ROW_BUFFERS, rb * nt, LANES), F32),
                        pltpu.VMEM((2, d, ed), w_gate.dtype), pltpu.VMEM((2, d, ed), w_up.dtype),
                        pltpu.VMEM((2, ed, d), w_down.dtype),
                        pltpu.SemaphoreType.DMA((FFN_ROW_BUFFERS,)), pltpu.SemaphoreType.DMA((2,))],
    )
    return pl.pallas_call(
        _ffn_kernel,
        out_shape=jax.ShapeDtypeStruct((n_blocks * rb * nt, LANES), F32),
        grid_spec=grid_spec,
        compiler_params=_cparams(1),
        name="expert_ffn",
    )(tok_sorted, blk_e, src_start, n_used, winfo, h2_tiles, w_gate, w_up, w_down)


def _combine_kernel(dest_ref, y_hbm, base_ref, wt_ref, mod_ref, o_ref, buf, sems):
    i = pl.program_id(0)
    n_steps = pl.num_programs(0)
    tm, d = base_ref.shape
    k = wt_ref.shape[1]
    nt = d // LANES
    n_rows = k * tm
    nbuf = buf.shape[0]
    ahead = nbuf - 1

    def issue(step, s, inline):
        first = step * n_rows

        def one(r, queue):
            src = dest_ref[first + r]
            pltpu.make_async_copy(y_hbm.at[pl.ds(pl.multiple_of(src * nt, nt), nt)],
                                  buf.at[s, pl.ds(pl.multiple_of(r * nt, nt), nt)], sems.at[s]
                                  ).start(priority=queue)

        if inline:
            for r in range(n_rows):
                one(r, r % DMA_QUEUES)
        else:
            def group(g, c):
                for j in range(ISSUE_UNROLL):
                    one(g * ISSUE_UNROLL + j, j % DMA_QUEUES)
                return c

            lax.fori_loop(0, n_rows // ISSUE_UNROLL, group, 0)

    def wait_rows(s):
        pltpu.make_async_copy(y_hbm.at[pl.ds(0, n_rows * nt)], buf.at[s], sems.at[s]).wait()

    @pl.when(i == 0)
    def _():
        for a in range(ahead):
            issue(jnp.minimum(a, n_steps - 1), a, False)

    for s in range(nbuf):
        @pl.when(i % nbuf == s)
        def _(s=s):
            wait_rows(s)
            issue(jnp.minimum(i + ahead, n_steps - 1), (s + ahead) % nbuf, True)
            rows = buf.at[s]
            wts = wt_ref[...]
            gate2 = mod_ref[0, 5:6, :]
            for c in range(nt):
                cols = slice(c * LANES, (c + 1) * LANES)
                routed = jnp.zeros((tm, LANES), F32)
                for kk in range(k):
                    routed = routed + wts[:, kk:kk + 1] * rows[pl.ds(kk * tm * nt + c, tm, stride=nt), :]
                o_ref[:, cols] = base_ref[:, cols] + gate2[:, cols] * routed

    @pl.when(i == n_steps - 1)
    def _():
        for a in range(1, nbuf):
            wait_rows((i + a) % nbuf)


def _combine(y_tiles, dest_t, w_t, base, mod3, seq):
    n, d = base.shape
    tm = COMBINE_TILE
    k = dest_t.shape[0]
    nt = d // LANES
    tiles_per_seq = seq // tm
    dest_tiles = dest_t.reshape(k, n // tm, tm).transpose(1, 0, 2).reshape(n * k)
    grid_spec = pltpu.PrefetchScalarGridSpec(
        num_scalar_prefetch=1,
        grid=(n // tm,),
        in_specs=[pl.BlockSpec(memory_space=pl.ANY),
                  pl.BlockSpec((tm, d), lambda i, dst: (i, 0)),
                  pl.BlockSpec((tm, k), lambda i, dst: (i, 0)),
                  pl.BlockSpec((1, 6, d), lambda i, dst: (i // tiles_per_seq, 0, 0))],
        out_specs=pl.BlockSpec((tm, d), lambda i, dst: (i, 0)),
        scratch_shapes=[pltpu.VMEM((COMBINE_ROW_BUFFERS, k * tm * nt, LANES), F32),
                        pltpu.SemaphoreType.DMA((COMBINE_ROW_BUFFERS,))],
    )
    return pl.pallas_call(
        _combine_kernel,
        out_shape=jax.ShapeDtypeStruct((n, d), F32),
        grid_spec=grid_spec,
        compiler_params=_cparams(1),
        name="combine",
    )(dest_tiles, y_tiles, base, w_t.T, mod3)


def _hybrid_layer(x, cond, rel_bias, w_ada, b_ada, ln1_g, w_in, q_norm_g, k_norm_g,
                  ssm_lambda_re, ssm_lambda_im, ssm_log_dt, ssm_b_re, ssm_b_im, ssm_c_re, ssm_c_im,
                  ssm_d, ssm_w_glu, ssm_b_glu, w_up_attn, w_up_ssm, w_out, ln2_g,
                  w_router, router_bias, w_exp_gate, w_exp_up, w_exp_down,
                  w_sh_gate, w_sh_up, w_sh_down):
    bsz, seq, d = x.shape
    n = bsz * seq
    xf = x.reshape(n, d)
    mod3 = _adaln(cond, w_ada, b_ada).reshape(bsz, 6, d)

    q_gain = jnp.tile(q_norm_g.astype(F32), ATTN_HEADS).reshape(1, ATTN_WIDTH)
    k_gain = jnp.tile(k_norm_g.astype(F32), ATTN_HEADS).reshape(1, ATTN_WIDTH)
    q, k, v, u, ga, gs, kmean = _inproj(xf, mod3, ln1_g, w_in.astype(BF16), q_gain, k_gain, seq)

    attn = _moba_attention(q, k, v, kmean, rel_bias, bsz, seq)
    ops = _s5_operators(ssm_lambda_re, ssm_lambda_im, ssm_log_dt, ssm_b_re, ssm_b_im,
                        ssm_c_re, ssm_c_im, ssm_d, seq // SSM_CHUNK)
    yssm = _s5_scan(u, ops, bsz, seq)

    weights = {
        "glu": ssm_w_glu.astype(BF16), "b_glu": ssm_b_glu.astype(F32).reshape(1, -1),
        "up_attn": w_up_attn.astype(BF16), "up_ssm": w_up_ssm.astype(BF16), "out": w_out.astype(BF16),
        "router_t": w_router.T.astype(BF16),
        "sh_gu": jnp.concatenate([w_sh_gate, w_sh_up], axis=1).astype(BF16),
        "sh_down": w_sh_down.astype(BF16),
    }
    h2, base, scores_t = _mix(xf, attn, yssm, ga, gs, mod3, ln2_g, weights, seq)

    idx_t, w_t, rank_t, counts = _route(scores_t, router_bias)
    n_blocks = -(-(n * TOP_K) // ROW_BLOCK) + N_EXPERTS
    tok_sorted, blk_e, src_start, n_used, winfo, poffs = _dispatch_tables(idx_t, counts, n_blocks)
    dest_t = _dest_slots(idx_t, rank_t, poffs)
    y_tiles = _expert_ffn(h2, tok_sorted, blk_e, src_start, n_used, winfo, w_exp_gate, w_exp_up, w_exp_down)
    out = _combine(y_tiles, dest_t, w_t, base, mod3, seq)
    return out.reshape(bsz, seq, d)


def kernel(x, c, rel_bias, w_ada, b_ada, ln1_g, w_in, q_norm_g, k_norm_g, ssm_lambda_re, ssm_lambda_im, ssm_log_dt, ssm_b_re, ssm_b_im, ssm_c_re, ssm_c_im, ssm_d, ssm_w_glu, ssm_b_glu, w_up_attn, w_up_ssm, w_out, ln2_g, w_router, router_bias, w_exp_gate, w_exp_up, w_exp_down, w_sh_gate, w_sh_up, w_sh_down):
    for l in range(w_ada.shape[0]):
        x = _hybrid_layer(x, c, rel_bias, w_ada[l], b_ada[l], ln1_g[l], w_in[l], q_norm_g[l], k_norm_g[l],
                          ssm_lambda_re[l], ssm_lambda_im[l], ssm_log_dt[l], ssm_b_re[l], ssm_b_im[l],
                          ssm_c_re[l], ssm_c_im[l], ssm_d[l], ssm_w_glu[l], ssm_b_glu[l],
                          w_up_attn[l], w_up_ssm[l], w_out[l], ln2_g[l], w_router[l], router_bias[l],
                          w_exp_gate[l], w_exp_up[l], w_exp_down[l], w_sh_gate[l], w_sh_up[l], w_sh_down[l])
    return x
```

```python
import functools
import math

import numpy as np
import jax
import jax.numpy as jnp
from jax import lax
from jax.experimental import pallas as pl
from jax.experimental.pallas import tpu as pltpu

F32 = jnp.float32
BF16 = jnp.bfloat16

ATTN_HEADS = 8
HEAD_DIM = 64
ATTN_WIDTH = ATTN_HEADS * HEAD_DIM
MOBA_BLOCK = 256
MOBA_TOPK = 3
NUM_BUCKETS = 32
MAX_DISTANCE = 128
SSM_WIDTH = 512
SSM_GROUP = 16
SSM_GROUPS = SSM_WIDTH // SSM_GROUP
SSM_STATE = 64
N_EXPERTS = 256
TOP_K = 8
N_EXPERT_GROUPS = 8
TOPK_GROUPS = 4
GROUP_SIZE = N_EXPERTS // N_EXPERT_GROUPS
EXPERT_DIM = 256
ROUTED_SCALE = 2.5
EPS = 1e-6
MASK_VALUE = -1e30

LANES = 128
HEADS_PER_STEP = LANES // HEAD_DIM
SSM_CHUNK = 16
S5_SEQS_PER_STEP = 2
SUBLANES = 8
BF16_ROWS = 16
ROW_BLOCK = 256
TOKEN_TILE = 512
ROUTE_TILE = 256
COMBINE_TILE = 128
ISSUE_UNROLL = 8
DMA_QUEUES = 2
COMBINE_ROW_BUFFERS = 3
FFN_ROW_BUFFERS = 3
VMEM_LIMIT = 56 * 1024 * 1024


def _store_rows_as_tiles(ref, val):
    rows, d = val.shape
    nt = d // LANES
    for c in range(nt):
        ref[pl.ds(c, rows, stride=nt), :] = val[:, c * LANES:(c + 1) * LANES]


def _load_rows_from_tiles(ref, rows, d):
    nt = d // LANES
    return jnp.concatenate([ref[pl.ds(c, rows, stride=nt), :] for c in range(nt)], axis=1)


def _cparams(n_axes, vmem=VMEM_LIMIT):
    return pltpu.CompilerParams(dimension_semantics=("arbitrary",) * n_axes, vmem_limit_bytes=vmem)


def _sigmoid(x):
    return 1.0 / (1.0 + jnp.exp(-x))


def _silu(x):
    return x * _sigmoid(x)


def _bdot(a, b):
    return jnp.dot(a.astype(BF16), b.astype(BF16), preferred_element_type=F32)


def _bdot_nt(a, b):
    return lax.dot_general(a.astype(BF16), b.astype(BF16), (((1,), (1,)), ((), ())),
                           preferred_element_type=F32)


def _adaln_kernel(c_ref, w_ref, b_ref, o_ref):
    o_ref[...] = _bdot(_silu(c_ref[...]), w_ref[...]) + b_ref[...]


def _adaln(c, w_ada, b_ada):
    bsz, d = c.shape
    n_out = w_ada.shape[1]
    return pl.pallas_call(
        _adaln_kernel,
        out_shape=jax.ShapeDtypeStruct((bsz, n_out), F32),
        grid=(n_out // d,),
        in_specs=[pl.BlockSpec((bsz, d), lambda j: (0, 0)),
                  pl.BlockSpec((d, d), lambda j: (0, j)),
                  pl.BlockSpec((1, d), lambda j: (0, j))],
        out_specs=pl.BlockSpec((bsz, d), lambda j: (0, j)),
        compiler_params=_cparams(1),
        name="adaln",
    )(c, w_ada, b_ada.reshape(1, n_out))


def _modulated_norm(x, gain, shift, scale):
    y = x * lax.rsqrt(jnp.mean(x * x, axis=-1, keepdims=True) + EPS) * gain
    return y * (1.0 + scale) + shift


def _head_norm(t, seg, gain):
    ms = _bdot(t * t, seg)
    return t * lax.rsqrt(ms + EPS) * gain


def _inproj_kernel(x_ref, mod_ref, ln_ref, w_ref, seg_ref, qg_ref, kg_ref,
                   q_ref, k_ref, v_ref, u_ref, ga_ref, gs_ref, km_ref, u_scr):
    aw, sw, d = ATTN_WIDTH, SSM_WIDTH, x_ref.shape[1]
    h = _modulated_norm(x_ref[...], ln_ref[...], mod_ref[0, 0:1, :], mod_ref[0, 1:2, :]).astype(BF16)
    seg = seg_ref[...]
    q = jnp.dot(h, w_ref[:, 0:aw], preferred_element_type=F32)
    q_ref[...] = _head_norm(q, seg, qg_ref[...])
    k = jnp.dot(h, w_ref[:, aw:2 * aw], preferred_element_type=F32)
    kn = _head_norm(k, seg, kg_ref[...])
    k_ref[...] = kn.astype(BF16)
    for blk in range(km_ref.shape[0]):
        km_ref[blk] = jnp.mean(kn[blk * MOBA_BLOCK:(blk + 1) * MOBA_BLOCK, :], axis=0, keepdims=True)
    v_ref[...] = jnp.dot(h, w_ref[:, 2 * aw:3 * aw], preferred_element_type=F32).astype(BF16)
    o = 3 * aw
    u = jnp.dot(h, w_ref[:, o:o + sw], preferred_element_type=F32)
    n_chunk = u_scr.shape[1] // SSM_CHUNK
    for cb in range(sw // LANES):
        u_scr[cb] = u[:, cb * LANES:(cb + 1) * LANES]
        for sg in range(SSM_CHUNK):
            u_ref[cb, :, sg * LANES:(sg + 1) * LANES] = (
                u_scr[cb, pl.ds(sg, n_chunk, stride=SSM_CHUNK), :].astype(BF16))
    o += sw
    ga_ref[...] = jnp.dot(h, w_ref[:, o:o + d], preferred_element_type=F32).astype(BF16)
    o += d
    gs_ref[...] = jnp.dot(h, w_ref[:, o:o + d], preferred_element_type=F32).astype(BF16)


def _inproj(xf, mod3, ln1_g, w_in_b, q_gain, k_gain, seq):
    n, d = xf.shape
    tm = TOKEN_TILE
    assert tm % MOBA_BLOCK == 0 and seq % tm == 0
    blocks_per_tile = tm // MOBA_BLOCK
    tiles_per_seq = seq // tm
    aw, sw = ATTN_WIDTH, SSM_WIDTH
    head_of_lane = np.arange(aw) // HEAD_DIM
    seg = jnp.asarray((head_of_lane[:, None] == head_of_lane[None, :]) / HEAD_DIM, BF16)
    row = lambda i: (i, 0)
    const = lambda i: (0, 0)
    return pl.pallas_call(
        _inproj_kernel,
        out_shape=(jax.ShapeDtypeStruct((n, aw), F32),
                   jax.ShapeDtypeStruct((n, aw), BF16),
                   jax.ShapeDtypeStruct((n, aw), BF16),
                   jax.ShapeDtypeStruct((sw // LANES, n // SSM_CHUNK, SSM_CHUNK * LANES), BF16),
                   jax.ShapeDtypeStruct((n, d), BF16),
                   jax.ShapeDtypeStruct((n, d), BF16),
                   jax.ShapeDtypeStruct((n // MOBA_BLOCK, 1, aw), F32)),
        grid=(n // tm,),
        in_specs=[pl.BlockSpec((tm, d), row),
                  pl.BlockSpec((1, 6, d), lambda i: (i // tiles_per_seq, 0, 0)),
                  pl.BlockSpec((1, d), const),
                  pl.BlockSpec(w_in_b.shape, const),
                  pl.BlockSpec((aw, aw), const),
                  pl.BlockSpec((1, aw), const),
                  pl.BlockSpec((1, aw), const)],
        out_specs=(pl.BlockSpec((tm, aw), row), pl.BlockSpec((tm, aw), row), pl.BlockSpec((tm, aw), row),
                   pl.BlockSpec((sw // LANES, tm // SSM_CHUNK, SSM_CHUNK * LANES), lambda i: (0, i, 0)),
                   pl.BlockSpec((tm, d), row), pl.BlockSpec((tm, d), row),
                   pl.BlockSpec((blocks_per_tile, 1, aw), lambda i: (i, 0, 0))),
        scratch_shapes=[pltpu.VMEM((sw // LANES, tm, LANES), F32)],
        compiler_params=_cparams(1),
        name="inproj",
    )(xf, mod3, ln1_g.reshape(1, d), w_in_b, seg, q_gain, k_gain)


def _t5_bucket(rel):
    n = jnp.maximum(rel, 0)
    max_exact = NUM_BUCKETS // 2
    nf = jnp.maximum(n, 1).astype(F32)
    large = max_exact + (jnp.log(nf / max_exact) / math.log(MAX_DISTANCE / max_exact)
                         * (NUM_BUCKETS - max_exact)).astype(jnp.int32)
    large = jnp.minimum(large, NUM_BUCKETS - 1)
    return jnp.where(n < max_exact, n, large)


def _bias_tables(rel_bias):
    blk = MOBA_BLOCK
    assert blk + 1 >= MAX_DISTANCE
    rel = jnp.arange(blk)[None, :] - jnp.arange(blk)[:, None]
    table = rel_bias.astype(F32)
    table = table - table[NUM_BUCKETS - 1][None, :]

    def lookup(r):
        onehot = jax.nn.one_hot(_t5_bucket(r), NUM_BUCKETS, dtype=F32)
        return jnp.einsum('kqn,nh->hkq', onehot, table, precision=lax.Precision.HIGHEST)

    return lookup(rel), lookup(rel + blk)


def _select_blocks(gate_t, n_past):
    nb, tq = gate_t.shape
    blk = lax.broadcasted_iota(jnp.int32, (nb, tq), 0)
    beaten = jnp.zeros((nb, tq), jnp.int32)
    for m in range(nb):
        gm = gate_t[m:m + 1, :]
        wins = (gm > gate_t) | ((gm == gate_t) & (m < blk))
        beaten = beaten + jnp.where(wins & (m < n_past), 1, 0)
    return jnp.where((blk < n_past) & (beaten < MOBA_TOPK), 1.0, 0.0)


def _attn_kernel(q_ref, k_ref, vt_ref, km_ref, bias_ref, o_ref, sel_ref, s_ref):
    qi = pl.program_id(2)
    tq = q_ref.shape[0]
    blk = MOBA_BLOCK
    hd = HEAD_DIM
    heads = range(HEADS_PER_STEP)
    q = q_ref[...]
    lane = lax.broadcasted_iota(jnp.int32, (tq, LANES), 1)
    kpos = lax.broadcasted_iota(jnp.int32, (blk, tq), 0)
    qpos = lax.broadcasted_iota(jnp.int32, (blk, tq), 1)
    scale = hd ** -0.5
    n_far = jnp.maximum(qi - 1, 0)
    n_pairs = (n_far + 1) // 2
    jp = jnp.maximum(qi - 1, 0)
    k_own = k_ref[pl.ds(pl.multiple_of(qi * blk, blk), blk), :]
    k_prev = k_ref[pl.ds(pl.multiple_of(jp * blk, blk), blk), :]

    qbs = []
    for h in heads:
        in_head = (lane >= h * hd) & (lane < (h + 1) * hd)
        qm = jnp.where(in_head, q, 0.0)
        gate_t = lax.dot_general(km_ref[0], qm, (((1,), (1,)), ((), ())),
                                 precision=lax.Precision.HIGHEST, preferred_element_type=F32)
        sel_ref[h] = _select_blocks(gate_t, qi)
        qbs.append((qm * scale).astype(BF16))

    def pair_scores(j):
        kb = k_ref[pl.ds(pl.multiple_of(j * blk, blk), 2 * blk), :]
        return [_bdot_nt(kb, qbs[h]) for h in heads]

    ones_rows = jnp.ones((BF16_ROWS, blk), BF16)

    def probs(s, m):
        return jnp.exp((s - m).astype(BF16))

    def attend(h, p, blocks):
        acc = None
        for i, j in enumerate(blocks):
            lhs = jnp.concatenate([vt_ref[0, j, h * hd:(h + 1) * hd, :], ones_rows], axis=0)
            part = jnp.dot(lhs, p[i * blk:(i + 1) * blk, :], preferred_element_type=F32)
            acc = part if acc is None else acc + part
        return acc

    for h, s in enumerate(pair_scores(0)):
        s_ref[h] = s

    carries = []
    for h in heads:
        s_prev = _bdot_nt(k_prev, qbs[h]) + bias_ref[h, 0:blk, :]
        s_prev = jnp.where(sel_ref[h, pl.ds(jp, 1), :] > 0.5, s_prev, MASK_VALUE)
        s_own = _bdot_nt(k_own, qbs[h]) + bias_ref[h, blk:2 * blk, :]
        s_own = jnp.where(kpos <= qpos, s_own, MASK_VALUE)
        s = jnp.concatenate([s_prev, s_own], axis=0)
        m = jnp.max(s, axis=0, keepdims=True)
        carries.append((m, attend(h, probs(s, m), (jp, qi))))

    def far_pair(pi, carries):
        j = 2 * pi
        s_cur = [s_ref[h] for h in heads]
        for h, s in enumerate(pair_scores(2 * jnp.minimum(pi + 1, n_pairs - 1))):
            s_ref[h] = s
        second_is_far = j + 1 < n_far
        out = []
        for h in heads:
            m, acc = carries[h]
            c0 = sel_ref[h, pl.ds(j, 1), :] > 0.5
            c1 = (sel_ref[h, pl.ds(j + 1, 1), :] > 0.5) & second_is_far
            chosen = jnp.concatenate([jnp.broadcast_to(c0, (blk, tq)), jnp.broadcast_to(c1, (blk, tq))], axis=0)
            s = jnp.where(chosen, s_cur[h], MASK_VALUE)
            m_new = jnp.maximum(m, jnp.max(s, axis=0, keepdims=True))
            acc = jnp.exp(m - m_new) * acc + attend(h, probs(s, m_new), (j, j + 1))
            out.append((m_new, acc))
        return tuple(out)

    carries = lax.fori_loop(0, n_pairs, far_pair, tuple(carries))
    out_t = jnp.concatenate([acc[:hd] / acc[hd:hd + 1] for _, acc in carries], axis=0)
    o_ref[...] = out_t.T.astype(o_ref.dtype)


def _moba_attention(q, k, v, kmean, rel_bias, bsz, seq):
    n, aw = q.shape
    blk = MOBA_BLOCK
    nb = seq // blk
    assert nb >= 2
    own, prev = _bias_tables(rel_bias)
    bias = jnp.concatenate([prev, own], axis=1)
    hps = HEADS_PER_STEP
    npair = aw // LANES
    vt = v.reshape(bsz, nb, blk, aw).transpose(0, 1, 3, 2)
    return pl.pallas_call(
        _attn_kernel,
        out_shape=jax.ShapeDtypeStruct((n, aw), BF16),
        grid=(bsz, npair, nb),
        in_specs=[pl.BlockSpec((blk, LANES), lambda b, hp, qi: (b * nb + qi, hp)),
                  pl.BlockSpec((seq, LANES), lambda b, hp, qi: (b, hp)),
                  pl.BlockSpec((1, nb, LANES, blk), lambda b, hp, qi: (b, 0, hp, 0)),
                  pl.BlockSpec((1, nb, LANES), lambda b, hp, qi: (b, 0, hp)),
                  pl.BlockSpec((hps, 2 * blk, blk), lambda b, hp, qi: (hp, 0, 0))],
        out_specs=pl.BlockSpec((blk, LANES), lambda b, hp, qi: (b * nb + qi, hp)),
        scratch_shapes=[pltpu.VMEM((hps, nb, blk), F32), pltpu.VMEM((hps, 2 * blk, blk), F32)],
        compiler_params=_cparams(3),
        name="moba_attention",
    )(q, k, vt, kmean.reshape(bsz, nb, aw), bias)


def _s5_operators(lambda_re, lambda_im, log_dt, b_re, b_im, c_re, c_im, d_skip, n_chunks):
    hi = lax.Precision.HIGHEST
    L, G, P, C = SSM_CHUNK, SSM_GROUPS, SSM_STATE, SSM_GROUP
    lam_re = jnp.minimum(lambda_re.astype(F32), -1e-4)
    lam_im = lambda_im.astype(F32)
    dt = jnp.exp(log_dt.astype(F32))[:, None]
    z_re, z_im = lam_re * dt, lam_im * dt

    def a_pow(nvec):
        nv = jnp.asarray(nvec, F32)[:, None, None]
        mag = jnp.exp(nv * z_re)
        return mag * jnp.cos(nv * z_im), mag * jnp.sin(nv * z_im)

    a_re, a_im = a_pow([1.0])
    a_re, a_im = a_re[0], a_im[0]
    den = lam_re * lam_re + lam_im * lam_im
    nr = a_re - 1.0
    coef_re = (nr * lam_re + a_im * lam_im) / den
    coef_im = (a_im * lam_re - nr * lam_im) / den
    br, bi = b_re.astype(F32), b_im.astype(F32)
    bbar_re = coef_re[..., None] * br - coef_im[..., None] * bi
    bbar_im = coef_re[..., None] * bi + coef_im[..., None] * br
    cr, ci = c_re.astype(F32), c_im.astype(F32)

    pw_re, pw_im = a_pow(np.arange(L + 1))
    cb_re = cr[None] * pw_re[:, :, None, :] - ci[None] * pw_im[:, :, None, :]
    cb_im = cr[None] * pw_im[:, :, None, :] + ci[None] * pw_re[:, :, None, :]
    kern = (jnp.einsum('jgop,gpi->gijo', cb_re[:L], bbar_re, precision=hi)
            - jnp.einsum('jgop,gpi->gijo', cb_im[:L], bbar_im, precision=hi)).reshape(G, C, L * C)
    t_op = jnp.stack([jnp.pad(kern[:, :, :(L - s) * C], ((0, 0), (0, 0), (s * C, 0))) for s in range(L)], axis=1)
    d_g = d_skip.astype(F32).reshape(G, 1, C, 1)
    on_diag = (lax.broadcasted_iota(jnp.int32, (1, L, C, L * C), 3)
               == lax.broadcasted_iota(jnp.int32, (1, L, C, L * C), 1) * C
               + lax.broadcasted_iota(jnp.int32, (1, L, C, L * C), 2))
    t_op = (t_op + jnp.where(on_diag, d_g, 0.0)).reshape(G, L * C, L * C)

    rp_re, rp_im = jnp.flip(pw_re[:L], 0), jnp.flip(pw_im[:L], 0)
    p_re = rp_re[..., None] * bbar_re[None] - rp_im[..., None] * bbar_im[None]
    p_im = rp_re[..., None] * bbar_im[None] + rp_im[..., None] * bbar_re[None]
    p_op = jnp.concatenate([p_re, p_im], axis=2)
    p_op = p_op.transpose(1, 0, 3, 2).reshape(G, L * C, 2 * P)

    q_re = cb_re[1:].transpose(1, 3, 0, 2)
    q_im = -cb_im[1:].transpose(1, 3, 0, 2)
    q_op = jnp.concatenate([q_re, q_im], axis=1).reshape(G, 2 * P, L * C)

    n_steps = max(1, int(math.ceil(math.log2(n_chunks))))
    dk_re, dk_im = a_pow([float(L * 2 ** k) for k in range(n_steps)])
    GB = LANES // C
    NB = G // GB
    lc = np.arange(L * C)
    wide = np.arange(L * LANES)
    expand_tc = jnp.asarray((lc[:, None] // C == wide[None, :] // LANES) & (lc[:, None] % C == wide[None, :] % C), BF16)
    st = np.arange(2 * P)
    wide_st = np.arange(2 * GB * P)
    expand_st = jnp.asarray((st[:, None] // P == wide_st[None, :] // (GB * P))
                            & (st[:, None] % P == wide_st[None, :] % P), BF16)
    g_of_wide = (jnp.arange(L * LANES) // C) % GB
    g_of_state = (jnp.arange(2 * GB * P) // P) % GB

    def widen(rows, expand, g_row, g_col):
        full = jnp.einsum('brk,kc->brc', rows.astype(BF16), expand, preferred_element_type=F32)
        return jnp.where(g_row[:, None] == g_col[None, :], full, 0.0).astype(BF16)

    t_rows = t_op.reshape(NB, GB, L, C, L * C).transpose(0, 2, 1, 3, 4).reshape(NB, L * LANES, L * C)
    p_rows = p_op.reshape(NB, GB, L, C, 2 * P).transpose(0, 2, 1, 3, 4).reshape(NB, L * LANES, 2 * P)
    q_rows = q_op.reshape(NB, GB, 2, P, L * C).transpose(0, 2, 1, 3, 4).reshape(NB, 2 * GB * P, L * C)
    t_big = widen(t_rows, expand_tc, g_of_wide, g_of_wide)
    p_big = widen(p_rows, expand_st, g_of_wide, g_of_state)
    q_big = widen(q_rows, expand_tc, g_of_state, g_of_wide)
    dk_re = dk_re.reshape(-1, NB, GB * P)
    dk_im = dk_im.reshape(-1, NB, GB * P)
    a1 = jnp.concatenate([dk_re, dk_re], axis=-1).transpose(1, 0, 2)
    a2 = jnp.concatenate([-dk_im, dk_im], axis=-1).transpose(1, 0, 2)
    return t_big.astype(BF16), p_big.astype(BF16), q_big.astype(BF16), a1, a2


def _s5_kernel(x_ref, t_ref, p_ref, q_ref, a1_ref, a2_ref, y_ref, *, n_chunks):
    x = x_ref[0]
    s = jnp.dot(x, p_ref[0], preferred_element_type=F32)
    rows, width = s.shape
    chunk = lax.broadcasted_iota(jnp.int32, (rows, width), 0) % n_chunks
    a1 = a1_ref[0]
    a2 = a2_ref[0]
    h = jnp.where(chunk >= 1, pltpu.roll(s, 1, axis=0), 0.0)
    for kk in range(a1.shape[0]):
        dist = 2 ** kk
        if dist >= n_chunks:
            break
        hs = jnp.where(chunk >= dist, pltpu.roll(h, dist, axis=0), 0.0)
        h = h + a1[kk:kk + 1, :] * hs + a2[kk:kk + 1, :] * pltpu.roll(hs, width // 2, axis=1)
    hb = h.astype(BF16)
    step = 2 * LANES
    for t in range(x.shape[1] // step):
        hi = (t + 1) * step
        y_ref[0, :, t * step:hi] = (jnp.dot(x[:, :hi], t_ref[0, :hi, t * step:hi], preferred_element_type=F32)
                                    + jnp.dot(hb, q_ref[0, :, t * step:hi], preferred_element_type=F32))


def _s5_scan(x_chunks, ops, bsz, seq):
    t_big, p_big, q_big, a1, a2 = ops
    nblk, rows, w = x_chunks.shape
    nc = seq // SSM_CHUNK
    sw = p_big.shape[2]
    spb = S5_SEQS_PER_STEP if bsz % S5_SEQS_PER_STEP == 0 else 1
    col = lambda cb, b: (cb, 0, 0)
    return pl.pallas_call(
        functools.partial(_s5_kernel, n_chunks=nc),
        out_shape=jax.ShapeDtypeStruct((nblk, rows, w), F32),
        grid=(nblk, bsz // spb),
        in_specs=[pl.BlockSpec((1, spb * nc, w), lambda cb, b: (cb, b, 0)),
                  pl.BlockSpec((1, w, w), col),
                  pl.BlockSpec((1, w, sw), col),
                  pl.BlockSpec((1, sw, w), col),
                  pl.BlockSpec((1,) + a1.shape[1:], col),
                  pl.BlockSpec((1,) + a2.shape[1:], col)],
        out_specs=pl.BlockSpec((1, spb * nc, w), lambda cb, b: (cb, b, 0)),
        compiler_params=_cparams(2),
        name="s5_scan",
    )(x_chunks, t_big, p_big, q_big, a1, a2)


def _gelu_tanh(x):
    return 0.5 * x * (1.0 + jnp.tanh(math.sqrt(2.0 / math.pi) * (x + 0.044715 * (x * x * x))))


def _mix_kernel(x_ref, attn_ref, yssm_ref, ga_ref, gs_ref, mod_ref, ln_ref,
                wglu_ref, bglu_ref, wua_ref, wus_ref, wout_ref, wrt_ref, wsgu_ref, wsd_ref,
                h2_ref, base_ref, score_ref, y_scr):
    n_chunk = yssm_ref.shape[1]
    for cb in range(yssm_ref.shape[0]):
        for tau in range(SSM_CHUNK):
            y_scr[cb, pl.ds(tau, n_chunk, stride=SSM_CHUNK), :] = yssm_ref[cb, :, tau * LANES:(tau + 1) * LANES]
    g = _gelu_tanh(jnp.concatenate([y_scr[cb] for cb in range(yssm_ref.shape[0])], axis=1))
    glu = g * _sigmoid(_bdot(g, wglu_ref[...]) + bglu_ref[...])
    y_attn = jnp.dot(attn_ref[...], wua_ref[...], preferred_element_type=F32)
    y_ssm = _bdot(glu, wus_ref[...])
    mixed = _sigmoid(ga_ref[...].astype(F32)) * y_attn + _sigmoid(gs_ref[...].astype(F32)) * y_ssm
    gate1 = mod_ref[0, 2:3, :]
    x1 = x_ref[...] + gate1 * _bdot(mixed, wout_ref[...])
    h2 = _modulated_norm(x1, ln_ref[...], mod_ref[0, 3:4, :], mod_ref[0, 4:5, :])
    _store_rows_as_tiles(h2_ref, h2)
    h2b = h2.astype(BF16)
    score_ref[...] = _sigmoid(_bdot_nt(wrt_ref[...], h2b))
    gu = jnp.dot(h2b, wsgu_ref[...], preferred_element_type=F32)
    sd = wsd_ref.shape[0]
    shared = _bdot(_silu(gu[:, :sd]) * gu[:, sd:], wsd_ref[...])
    base_ref[...] = x1 + mod_ref[0, 5:6, :] * shared


def _mix(xf, attn, yssm, ga, gs, mod3, ln2_g, w, seq):
    n, d = xf.shape
    tm = TOKEN_TILE
    tiles_per_seq = seq // tm
    row = lambda i: (i, 0)
    const = lambda i: (0, 0)
    nt = d // LANES
    weights = [w["glu"], w["b_glu"], w["up_attn"], w["up_ssm"], w["out"], w["router_t"], w["sh_gu"], w["sh_down"]]
    return pl.pallas_call(
        _mix_kernel,
        out_shape=(jax.ShapeDtypeStruct((n * nt, LANES), F32),
                   jax.ShapeDtypeStruct((n, d), F32),
                   jax.ShapeDtypeStruct((N_EXPERTS, n), F32)),
        grid=(n // tm,),
        in_specs=[pl.BlockSpec((tm, d), row),
                  pl.BlockSpec((tm, attn.shape[1]), row),
                  pl.BlockSpec((yssm.shape[0], tm // SSM_CHUNK, yssm.shape[2]), lambda i: (0, i, 0)),
                  pl.BlockSpec((tm, d), row),
                  pl.BlockSpec((tm, d), row),
                  pl.BlockSpec((1, 6, d), lambda i: (i // tiles_per_seq, 0, 0)),
                  pl.BlockSpec((1, d), const)] + [pl.BlockSpec(a.shape, const) for a in weights],
        out_specs=(pl.BlockSpec((tm * nt, LANES), row), pl.BlockSpec((tm, d), row),
                   pl.BlockSpec((N_EXPERTS, tm), lambda i: (0, i))),
        scratch_shapes=[pltpu.VMEM((yssm.shape[0], tm, LANES), F32)],
        compiler_params=_cparams(1),
        name="mix",
    )(xf, attn, yssm, ga, gs, mod3, ln2_g.reshape(1, d), *weights)


def _route_kernel(score_ref, bias_ref, tri_ref, idx_ref, w_ref, rank_ref, cnt_ref, carry_ref):
    @pl.when(pl.program_id(0) == 0)
    def _():
        carry_ref[...] = jnp.zeros_like(carry_ref)

    scores = score_ref[...]
    ne, tn = scores.shape
    biased = scores + bias_ref[...]
    gsz = GROUP_SIZE
    sub = lax.broadcasted_iota(jnp.int32, (gsz, tn), 0)
    group_score = []
    for g in range(N_EXPERT_GROUPS):
        sg = biased[g * gsz:(g + 1) * gsz, :]
        m1 = jnp.max(sg, axis=0, keepdims=True)
        first = jnp.min(jnp.where(sg == m1, sub, gsz), axis=0, keepdims=True)
        m2 = jnp.max(jnp.where(sub == first, -jnp.inf, sg), axis=0, keepdims=True)
        group_score.append(m1 + m2)
    group_rows = []
    for g in range(N_EXPERT_GROUPS):
        beaten = jnp.zeros((1, tn), jnp.int32)
        for o in range(N_EXPERT_GROUPS):
            if o == g:
                continue
            wins = (group_score[o] > group_score[g])
            if o < g:
                wins = wins | (group_score[o] == group_score[g])
            beaten = beaten + jnp.where(wins, 1, 0)
        group_rows.append(jnp.broadcast_to(beaten < TOPK_GROUPS, (gsz, tn)))
    allowed = jnp.concatenate(group_rows, axis=0)
    cur = jnp.where(allowed, biased, MASK_VALUE)
    eio = lax.broadcasted_iota(jnp.int32, (ne, tn), 0)
    idx_rows, w_rows, hits = [], [], []
    for _ in range(TOP_K):
        vmax = jnp.max(cur, axis=0, keepdims=True)
        eidx = jnp.min(jnp.where(cur == vmax, eio, ne), axis=0, keepdims=True)
        hit = eio == eidx
        w_rows.append(jnp.sum(jnp.where(hit, scores, 0.0), axis=0, keepdims=True))
        idx_rows.append(eidx)
        hits.append(hit)
        cur = jnp.where(hit, -jnp.inf, cur)
    wts = jnp.concatenate(w_rows, axis=0)
    idx_ref[...] = jnp.concatenate(idx_rows, axis=0)
    w_ref[...] = wts / jnp.sum(wts, axis=0, keepdims=True) * ROUTED_SCALE

    onehot = jnp.zeros((ne, tn), F32)
    for hit in hits:
        onehot = onehot + jnp.where(hit, 1.0, 0.0)
    earlier = carry_ref[...] + _bdot(onehot, tri_ref[...])
    rank_rows = [jnp.sum(jnp.where(hit, earlier, 0.0), axis=0, keepdims=True) for hit in hits]
    rank_ref[...] = jnp.concatenate(rank_rows, axis=0).astype(jnp.int32)
    carry_ref[...] = carry_ref[...] + jnp.sum(onehot, axis=1, keepdims=True)
    cnt_ref[...] = carry_ref[...]


def _route(scores_t, router_bias):
    ne, n = scores_t.shape
    tn = ROUTE_TILE
    tri = jnp.asarray(np.arange(tn)[:, None] < np.arange(tn)[None, :], BF16)
    tok = lambda i: (0, i)
    const = lambda i: (0, 0)
    return pl.pallas_call(
        _route_kernel,
        out_shape=(jax.ShapeDtypeStruct((TOP_K, n), jnp.int32), jax.ShapeDtypeStruct((TOP_K, n), F32),
                   jax.ShapeDtypeStruct((TOP_K, n), jnp.int32), jax.ShapeDtypeStruct((ne, 1), F32)),
        grid=(n // tn,),
        in_specs=[pl.BlockSpec((ne, tn), tok), pl.BlockSpec((ne, 1), const), pl.BlockSpec((tn, tn), const)],
        out_specs=(pl.BlockSpec((TOP_K, tn), tok), pl.BlockSpec((TOP_K, tn), tok),
                   pl.BlockSpec((TOP_K, tn), tok), pl.BlockSpec((ne, 1), const)),
        scratch_shapes=[pltpu.VMEM((ne, 1), F32)],
        compiler_params=_cparams(1),
        name="route",
    )(scores_t, router_bias.reshape(ne, 1).astype(F32), tri)


def _dest_kernel(idx_ref, rank_ref, poffs_ref, dest_ref):
    idx = idx_ref[...]
    ne = poffs_ref.shape[0]
    eio = lax.broadcasted_iota(jnp.int32, (ne, idx.shape[1]), 0)
    poffs = poffs_ref[...]
    rows = [jnp.sum(jnp.where(eio == idx[kk:kk + 1, :], poffs, 0.0), axis=0, keepdims=True)
            for kk in range(idx.shape[0])]
    dest_ref[...] = rank_ref[...] + jnp.concatenate(rows, axis=0).astype(jnp.int32)


def _dest_slots(idx_t, rank_t, poffs):
    k, n = idx_t.shape
    tn = ROUTE_TILE
    ne = poffs.shape[0]
    tok = lambda i: (0, i)
    return pl.pallas_call(
        _dest_kernel,
        out_shape=jax.ShapeDtypeStruct((k, n), jnp.int32),
        grid=(n // tn,),
        in_specs=[pl.BlockSpec((k, tn), tok), pl.BlockSpec((k, tn), tok), pl.BlockSpec((ne, 1), lambda i: (0, 0))],
        out_specs=pl.BlockSpec((k, tn), tok),
        compiler_params=_cparams(1),
        name="dest_slots",
    )(idx_t, rank_t, poffs.astype(F32).reshape(ne, 1))


def _dispatch_tables(idx_t, counts, n_blocks):
    k, n = idx_t.shape
    n_assign = k * n
    rb = ROW_BLOCK
    ne = counts.shape[0]
    counts = counts.reshape(ne).astype(jnp.int32)
    offs = jnp.cumsum(counts) - counts
    nblk = (counts + rb - 1) // rb
    bend = jnp.cumsum(nblk)
    bstart = bend - nblk
    blocks = jnp.arange(n_blocks, dtype=jnp.int32)
    blk_e = jnp.minimum(jnp.sum((bend[None, :] <= blocks[:, None]).astype(jnp.int32), axis=1), ne - 1)
    mine = blk_e[:, None] == jnp.arange(ne, dtype=jnp.int32)[None, :]
    src_start = jnp.sum(jnp.where(mine, (offs - bstart * rb)[None, :], 0), axis=1) + blocks * rb
    tok = lax.broadcasted_iota(jnp.int32, (k, n), 1)
    kk = lax.broadcasted_iota(jnp.int32, (k, n), 0)
    assert ne * n_assign < 2 ** 31
    keys = jnp.sort((idx_t * n_assign + tok * k + kk).reshape(n_assign))
    tok_sorted = (keys % n_assign) // k
    experts = jnp.arange(ne, dtype=jnp.int32)
    nonempty = nblk > 0
    run = jnp.cumsum(nonempty.astype(jnp.int32)) - 1
    later = (experts[None, :] > experts[:, None]) & nonempty[None, :]
    next_e = jnp.min(jnp.where(later, experts[None, :], ne), axis=1)
    next_e = jnp.where(next_e == ne, -1, next_e)
    per_expert = (run % 2) * 2 + (next_e + 1) * 4
    winfo = (jnp.sum(jnp.where(mine, per_expert[None, :], 0), axis=1)
             + (blocks == jnp.sum(jnp.where(mine, bstart[None, :], 0), axis=1)).astype(jnp.int32))
    return tok_sorted, blk_e, src_start, bend[ne - 1:ne], winfo.astype(jnp.int32), bstart * rb


def _ffn_kernel(tok_ref, blk_e_ref, start_ref, used_ref, winfo_ref, h2_hbm, wg_hbm, wu_hbm, wd_hbm, y_ref,
                xbuf, wg_buf, wu_buf, wd_buf, sems, wsems):
    b = pl.program_id(0)
    rb = ROW_BLOCK
    d = wg_hbm.shape[1]
    nt = d // LANES
    n_assign = tok_ref.shape[0]
    n_used = used_ref[0]
    n_blocks = pl.num_programs(0)

    info = winfo_ref[b]
    first_of_expert = (info & 1) == 1
    wslot = (info >> 1) & 1
    next_expert = (info >> 2) - 1

    def weight_copies(e, s):
        return [pltpu.make_async_copy(src.at[e], dst.at[s], wsems.at[s])
                for src, dst in ((wg_hbm, wg_buf), (wu_hbm, wu_buf), (wd_hbm, wd_buf))]

    @pl.when(b == 0)
    def _():
        for c in weight_copies(blk_e_ref[0], wslot):
            c.start()

    @pl.when((b < n_used) & first_of_expert)
    def _():
        for c in weight_copies(blk_e_ref[b], wslot):
            c.wait()

        @pl.when(next_expert >= 0)
        def _():
            for c in weight_copies(next_expert, 1 - wslot):
                c.start()

    def start_rows(blk, s, inline):
        start = start_ref[blk]

        def one(r, queue):
            tok = tok_ref[jnp.minimum(start + r, n_assign - 1)]
            pltpu.make_async_copy(h2_hbm.at[pl.ds(pl.multiple_of(tok * nt, nt), nt)],
                                  xbuf.at[s, pl.ds(pl.multiple_of(r * nt, nt), nt)], sems.at[s]
                                  ).start(priority=queue)

        if inline:
            for r in range(rb):
                one(r, r % DMA_QUEUES)
        else:
            def group(g, c):
                for i in range(ISSUE_UNROLL):
                    one(g * ISSUE_UNROLL + i, i % DMA_QUEUES)
                return c

            lax.fori_loop(0, rb // ISSUE_UNROLL, group, 0)

    def wait_rows(s):
        pltpu.make_async_copy(h2_hbm.at[pl.ds(0, rb * nt)], xbuf.at[s], sems.at[s]).wait()

    nbuf = xbuf.shape[0]
    ahead = nbuf - 1

    @pl.when(b == 0)
    def _():
        for a in range(ahead):
            start_rows(jnp.minimum(a, n_blocks - 1), a, False)

    for s in range(nbuf):
        @pl.when((b < n_used) & (b % nbuf == s))
        def _(s=s):
            wait_rows(s)
            xb = _load_rows_from_tiles(xbuf.at[s], rb, d).astype(BF16)
            start_rows(jnp.minimum(b + ahead, n_blocks - 1), (s + ahead) % nbuf, True)
            hg = jnp.dot(xb, wg_buf[wslot].astype(BF16), preferred_element_type=F32)
            hu = jnp.dot(xb, wu_buf[wslot].astype(BF16), preferred_element_type=F32)
            _store_rows_as_tiles(y_ref, _bdot(_silu(hg) * hu, wd_buf[wslot]))

    @pl.when(b == n_used - 1)
    def _():
        for a in range(1, nbuf):
            wait_rows((b + a) % nbuf)

    @pl.when(b >= n_used)
    def _():
        y_ref[...] = jnp.zeros_like(y_ref)


def _expert_ffn(h2_tiles, tok_sorted, blk_e, src_start, n_used, winfo, w_gate, w_up, w_down):
    n_blocks = blk_e.shape[0]
    rb = ROW_BLOCK
    _, d, ed = w_gate.shape
    nt = d // LANES
    hbm = pl.BlockSpec(memory_space=pl.ANY)
    grid_spec = pltpu.PrefetchScalarGridSpec(
        num_scalar_prefetch=5,
        grid=(n_blocks,),
        in_specs=[hbm, hbm, hbm, hbm],
        out_specs=pl.BlockSpec((rb * nt, LANES), lambda i, tok, be, st, nu, wi: (i, 0)),
        scratch_shapes=[pltpu.VMEM((FFN_ROW_BUFFERS, rb * nt, LANES), F32),
                        pltpu.VMEM((2, d, ed), w_gate.dtype), pltpu.VMEM((2, d, ed), w_up.dtype),
                        pltpu.VMEM((2, ed, d), w_down.dtype),
                        pltpu.SemaphoreType.DMA((FFN_ROW_BUFFERS,)), pltpu.SemaphoreType.DMA((2,))],
    )
    return pl.pallas_call(
        _ffn_kernel,
        out_shape=jax.ShapeDtypeStruct((n_blocks * rb * nt, LANES), F32),
        grid_spec=grid_spec,
        compiler_params=_cparams(1),
        name="expert_ffn",
    )(tok_sorted, blk_e, src_start, n_used, winfo, h2_tiles, w_gate, w_up, w_down)


def _combine_kernel(dest_ref, y_hbm, base_ref, wt_ref, mod_ref, o_ref, buf, sems):
    i = pl.program_id(0)
    n_steps = pl.num_programs(0)
    tm, d = base_ref.shape
    k = wt_ref.shape[1]
    nt = d // LANES
    n_rows = k * tm
    nbuf = buf.shape[0]
    ahead = nbuf - 1

    def issue(step, s, inline):
        first = step * n_rows

        def one(r, queue):
            src = dest_ref[first + r]
            pltpu.make_async_copy(y_hbm.at[pl.ds(pl.multiple_of(src * nt, nt), nt)],
                                  buf.at[s, pl.ds(pl.multiple_of(r * nt, nt), nt)], sems.at[s]
                                  ).start(priority=queue)

        if inline:
            for r in range(n_rows):
                one(r, r % DMA_QUEUES)
        else:
            def group(g, c):
                for j in range(ISSUE_UNROLL):
                    one(g * ISSUE_UNROLL + j, j % DMA_QUEUES)
                return c

            lax.fori_loop(0, n_rows // ISSUE_UNROLL, group, 0)

    def wait_rows(s):
        pltpu.make_async_copy(y_hbm.at[pl.ds(0, n_rows * nt)], buf.at[s], sems.at[s]).wait()

    @pl.when(i == 0)
    def _():
        for a in range(ahead):
            issue(jnp.minimum(a, n_steps - 1), a, False)

    for s in range(nbuf):
        @pl.when(i % nbuf == s)
        def _(s=s):
            wait_rows(s)
            issue(jnp.minimum(i + ahead, n_steps - 1), (s + ahead) % nbuf, True)
            rows = buf.at[s]
            wts = wt_ref[...]
            gate2 = mod_ref[0, 5:6, :]
            for c in range(nt):
                cols = slice(c * LANES, (c + 1) * LANES)
                routed = jnp.zeros((tm, LANES), F32)
                for kk in range(k):
                    routed = routed + wts[:, kk:kk + 1] * rows[pl.ds(kk * tm * nt + c, tm, stride=nt), :]
                o_ref[:, cols] = base_ref[:, cols] + gate2[:, cols] * routed

    @pl.when(i == n_steps - 1)
    def _():
        for a in range(1, nbuf):
            wait_rows((i + a) % nbuf)


def _combine(y_tiles, dest_t, w_t, base, mod3, seq):
    n, d = base.shape
    tm = COMBINE_TILE
    k = dest_t.shape[0]
    nt = d // LANES
    tiles_per_seq = seq // tm
    dest_tiles = dest_t.reshape(k, n // tm, tm).transpose(1, 0, 2).reshape(n * k)
    grid_spec = pltpu.PrefetchScalarGridSpec(
        num_scalar_prefetch=1,
        grid=(n // tm,),
        in_specs=[pl.BlockSpec(memory_space=pl.ANY),
                  pl.BlockSpec((tm, d), lambda i, dst: (i, 0)),
                  pl.BlockSpec((tm, k), lambda i, dst: (i, 0)),
                  pl.BlockSpec((1, 6, d), lambda i, dst: (i // tiles_per_seq, 0, 0))],
        out_specs=pl.BlockSpec((tm, d), lambda i, dst: (i, 0)),
        scratch_shapes=[pltpu.VMEM((COMBINE_ROW_BUFFERS, k * tm * nt, LANES), F32),
                        pltpu.SemaphoreType.DMA((COMBINE_ROW_BUFFERS,))],
    )
    return pl.pallas_call(
        _combine_kernel,
        out_shape=jax.ShapeDtypeStruct((n, d), F32),
        grid_spec=grid_spec,
        compiler_params=_cparams(1),
        name="combine",
    )(dest_tiles, y_tiles, base, w_t.T, mod3)


def _hybrid_layer(x, cond, rel_bias, w_ada, b_ada, ln1_g, w_in, q_norm_g, k_norm_g,
                  ssm_lambda_re, ssm_lambda_im, ssm_log_dt, ssm_b_re, ssm_b_im, ssm_c_re, ssm_c_im,
                  ssm_d, ssm_w_glu, ssm_b_glu, w_up_attn, w_up_ssm, w_out, ln2_g,
                  w_router, router_bias, w_exp_gate, w_exp_up, w_exp_down,
                  w_sh_gate, w_sh_up, w_sh_down):
    bsz, seq, d = x.shape
    n = bsz * seq
    xf = x.reshape(n, d)
    mod3 = _adaln(cond, w_ada, b_ada).reshape(bsz, 6, d)

    q_gain = jnp.tile(q_norm_g.astype(F32), ATTN_HEADS).reshape(1, ATTN_WIDTH)
    k_gain = jnp.tile(k_norm_g.astype(F32), ATTN_HEADS).reshape(1, ATTN_WIDTH)
    q, k, v, u, ga, gs, kmean = _inproj(xf, mod3, ln1_g, w_in.astype(BF16), q_gain, k_gain, seq)

    attn = _moba_attention(q, k, v, kmean, rel_bias, bsz, seq)
    ops = _s5_operators(ssm_lambda_re, ssm_lambda_im, ssm_log_dt, ssm_b_re, ssm_b_im,
                        ssm_c_re, ssm_c_im, ssm_d, seq // SSM_CHUNK)
    yssm = _s5_scan(u, ops, bsz, seq)

    weights = {
        "glu": ssm_w_glu.astype(BF16), "b_glu": ssm_b_glu.astype(F32).reshape(1, -1),
        "up_attn": w_up_attn.astype(BF16), "up_ssm": w_up_ssm.astype(BF16), "out": w_out.astype(BF16),
        "router_t": w_router.T.astype(BF16),
        "sh_gu": jnp.concatenate([w_sh_gate, w_sh_up], axis=1).astype(BF16),
        "sh_down": w_sh_down.astype(BF16),
    }
    h2, base, scores_t = _mix(xf, attn, yssm, ga, gs, mod3, ln2_g, weights, seq)

    idx_t, w_t, rank_t, counts = _route(scores_t, router_bias)
    n_blocks = -(-(n * TOP_K) // ROW_BLOCK) + N_EXPERTS
    tok_sorted, blk_e, src_start, n_used, winfo, poffs = _dispatch_tables(idx_t, counts, n_blocks)
    dest_t = _dest_slots(idx_t, rank_t, poffs)
    y_tiles = _expert_ffn(h2, tok_sorted, blk_e, src_start, n_used, winfo, w_exp_gate, w_exp_up, w_exp_down)
    out = _combine(y_tiles, dest_t, w_t, base, mod3, seq)
    return out.reshape(bsz, seq, d)


def kernel(x, c, rel_bias, w_ada, b_ada, ln1_g, w_in, q_norm_g, k_norm_g, ssm_lambda_re, ssm_lambda_im, ssm_log_dt, ssm_b_re, ssm_b_im, ssm_c_re, ssm_c_im, ssm_d, ssm_w_glu, ssm_b_glu, w_up_attn, w_up_ssm, w_out, ln2_g, w_router, router_bias, w_exp_gate, w_exp_up, w_exp_down, w_sh_gate, w_sh_up, w_sh_down):
    for l in range(w_ada.shape[0]):
        x = _hybrid_layer(x, c, rel_bias, w_ada[l], b_ada[l], ln1_g[l], w_in[l], q_norm_g[l], k_norm_g[l],
                          ssm_lambda_re[l], ssm_lambda_im[l], ssm_log_dt[l], ssm_b_re[l], ssm_b_im[l],
                          ssm_c_re[l], ssm_c_im[l], ssm_d[l], ssm_w_glu[l], ssm_b_glu[l],
                          w_up_attn[l], w_up_ssm[l], w_out[l], ln2_g[l], w_router[l], router_bias[l],
                          w_exp_gate[l], w_exp_up[l], w_exp_down[l], w_sh_gate[l], w_sh_up[l], w_sh_down[l])
    return x
```

```python
import functools
import math

import numpy as np
import jax
import jax.numpy as jnp
from jax import lax
from jax.experimental import pallas as pl
from jax.experimental.pallas import tpu as pltpu

F32 = jnp.float32
BF16 = jnp.bfloat16

ATTN_HEADS = 8
HEAD_DIM = 64
ATTN_WIDTH = ATTN_HEADS * HEAD_DIM
MOBA_BLOCK = 256
MOBA_TOPK = 3
NUM_BUCKETS = 32
MAX_DISTANCE = 128
SSM_WIDTH = 512
SSM_GROUP = 16
SSM_GROUPS = SSM_WIDTH // SSM_GROUP
SSM_STATE = 64
N_EXPERTS = 256
TOP_K = 8
N_EXPERT_GROUPS = 8
TOPK_GROUPS = 4
GROUP_SIZE = N_EXPERTS // N_EXPERT_GROUPS
EXPERT_DIM = 256
ROUTED_SCALE = 2.5
EPS = 1e-6
MASK_VALUE = -1e30

LANES = 128
HEADS_PER_STEP = LANES // HEAD_DIM
SSM_CHUNK = 16
S5_SEQS_PER_STEP = 2
SUBLANES = 8
BF16_ROWS = 16
ROW_BLOCK = 256
TOKEN_TILE = 512
ROUTE_TILE = 512
COMBINE_TILE = 128
ISSUE_UNROLL = 8
DMA_QUEUES = 2
COMBINE_ROW_BUFFERS = 3
FFN_ROW_BUFFERS = 3
VMEM_LIMIT = 56 * 1024 * 1024


def _store_rows_as_tiles(ref, val):
    rows, d = val.shape
    nt = d // LANES
    for c in range(nt):
        ref[pl.ds(c, rows, stride=nt), :] = val[:, c * LANES:(c + 1) * LANES]


def _load_rows_from_tiles(ref, rows, d):
    nt = d // LANES
    return jnp.concatenate([ref[pl.ds(c, rows, stride=nt), :] for c in range(nt)], axis=1)


def _cparams(n_axes, vmem=VMEM_LIMIT):
    return pltpu.CompilerParams(dimension_semantics=("arbitrary",) * n_axes, vmem_limit_bytes=vmem)


def _sigmoid(x):
    return 1.0 / (1.0 + jnp.exp(-x))


def _silu(x):
    return x * _sigmoid(x)


def _bdot(a, b):
    return jnp.dot(a.astype(BF16), b.astype(BF16), preferred_element_type=F32)


def _bdot_nt(a, b):
    return lax.dot_general(a.astype(BF16), b.astype(BF16), (((1,), (1,)), ((), ())),
                           preferred_element_type=F32)


def _adaln_kernel(c_ref, w_ref, b_ref, o_ref):
    o_ref[...] = _bdot(_silu(c_ref[...]), w_ref[...]) + b_ref[...]


def _adaln(c, w_ada, b_ada):
    bsz, d = c.shape
    n_out = w_ada.shape[1]
    return pl.pallas_call(
        _adaln_kernel,
        out_shape=jax.ShapeDtypeStruct((bsz, n_out), F32),
        grid=(n_out // d,),
        in_specs=[pl.BlockSpec((bsz, d), lambda j: (0, 0)),
                  pl.BlockSpec((d, d), lambda j: (0, j)),
                  pl.BlockSpec((1, d), lambda j: (0, j))],
        out_specs=pl.BlockSpec((bsz, d), lambda j: (0, j)),
        compiler_params=_cparams(1),
        name="adaln",
    )(c, w_ada, b_ada.reshape(1, n_out))


def _modulated_norm(x, gain, shift, scale):
    y = x * lax.rsqrt(jnp.mean(x * x, axis=-1, keepdims=True) + EPS) * gain
    return y * (1.0 + scale) + shift


def _head_norm(t, seg, gain):
    ms = _bdot(t * t, seg)
    return t * lax.rsqrt(ms + EPS) * gain


def _inproj_kernel(x_ref, mod_ref, ln_ref, w_ref, seg_ref, qg_ref, kg_ref,
                   q_ref, k_ref, v_ref, u_ref, ga_ref, gs_ref, km_ref, u_scr):
    aw, sw, d = ATTN_WIDTH, SSM_WIDTH, x_ref.shape[1]
    h = _modulated_norm(x_ref[...], ln_ref[...], mod_ref[0, 0:1, :], mod_ref[0, 1:2, :]).astype(BF16)
    seg = seg_ref[...]
    q = jnp.dot(h, w_ref[:, 0:aw], preferred_element_type=F32)
    q_ref[...] = _head_norm(q, seg, qg_ref[...])
    k = jnp.dot(h, w_ref[:, aw:2 * aw], preferred_element_type=F32)
    kn = _head_norm(k, seg, kg_ref[...])
    k_ref[...] = kn.astype(BF16)
    for blk in range(km_ref.shape[0]):
        km_ref[blk] = jnp.mean(kn[blk * MOBA_BLOCK:(blk + 1) * MOBA_BLOCK, :], axis=0, keepdims=True)
    v_ref[...] = jnp.dot(h, w_ref[:, 2 * aw:3 * aw], preferred_element_type=F32).astype(BF16)
    o = 3 * aw
    u = jnp.dot(h, w_ref[:, o:o + sw], preferred_element_type=F32)
    n_chunk = u_scr.shape[1] // SSM_CHUNK
    for cb in range(sw // LANES):
        u_scr[cb] = u[:, cb * LANES:(cb + 1) * LANES]
        for sg in range(SSM_CHUNK):
            u_ref[cb, :, sg * LANES:(sg + 1) * LANES] = (
                u_scr[cb, pl.ds(sg, n_chunk, stride=SSM_CHUNK), :].astype(BF16))
    o += sw
    ga_ref[...] = jnp.dot(h, w_ref[:, o:o + d], preferred_element_type=F32).astype(BF16)
    o += d
    gs_ref[...] = jnp.dot(h, w_ref[:, o:o + d], preferred_element_type=F32).astype(BF16)


def _inproj(xf, mod3, ln1_g, w_in_b, q_gain, k_gain, seq):
    n, d = xf.shape
    tm = TOKEN_TILE
    assert tm % MOBA_BLOCK == 0 and seq % tm == 0
    blocks_per_tile = tm // MOBA_BLOCK
    tiles_per_seq = seq // tm
    aw, sw = ATTN_WIDTH, SSM_WIDTH
    head_of_lane = np.arange(aw) // HEAD_DIM
    seg = jnp.asarray((head_of_lane[:, None] == head_of_lane[None, :]) / HEAD_DIM, BF16)
    row = lambda i: (i, 0)
    const = lambda i: (0, 0)
    return pl.pallas_call(
        _inproj_kernel,
        out_shape=(jax.ShapeDtypeStruct((n, aw), F32),
                   jax.ShapeDtypeStruct((n, aw), BF16),
                   jax.ShapeDtypeStruct((n, aw), BF16),
                   jax.ShapeDtypeStruct((sw // LANES, n // SSM_CHUNK, SSM_CHUNK * LANES), BF16),
                   jax.ShapeDtypeStruct((n, d), BF16),
                   jax.ShapeDtypeStruct((n, d), BF16),
                   jax.ShapeDtypeStruct((n // MOBA_BLOCK, 1, aw), F32)),
        grid=(n // tm,),
        in_specs=[pl.BlockSpec((tm, d), row),
                  pl.BlockSpec((1, 6, d), lambda i: (i // tiles_per_seq, 0, 0)),
                  pl.BlockSpec((1, d), const),
                  pl.BlockSpec(w_in_b.shape, const),
                  pl.BlockSpec((aw, aw), const),
                  pl.BlockSpec((1, aw), const),
                  pl.BlockSpec((1, aw), const)],
        out_specs=(pl.BlockSpec((tm, aw), row), pl.BlockSpec((tm, aw), row), pl.BlockSpec((tm, aw), row),
                   pl.BlockSpec((sw // LANES, tm // SSM_CHUNK, SSM_CHUNK * LANES), lambda i: (0, i, 0)),
                   pl.BlockSpec((tm, d), row), pl.BlockSpec((tm, d), row),
                   pl.BlockSpec((blocks_per_tile, 1, aw), lambda i: (i, 0, 0))),
        scratch_shapes=[pltpu.VMEM((sw // LANES, tm, LANES), F32)],
        compiler_params=_cparams(1),
        name="inproj",
    )(xf, mod3, ln1_g.reshape(1, d), w_in_b, seg, q_gain, k_gain)


def _t5_bucket(rel):
    n = jnp.maximum(rel, 0)
    max_exact = NUM_BUCKETS // 2
    nf = jnp.maximum(n, 1).astype(F32)
    large = max_exact + (jnp.log(nf / max_exact) / math.log(MAX_DISTANCE / max_exact)
                         * (NUM_BUCKETS - max_exact)).astype(jnp.int32)
    large = jnp.minimum(large, NUM_BUCKETS - 1)
    return jnp.where(n < max_exact, n, large)


def _bias_tables(rel_bias):
    blk = MOBA_BLOCK
    assert blk + 1 >= MAX_DISTANCE
    rel = jnp.arange(blk)[None, :] - jnp.arange(blk)[:, None]
    table = rel_bias.astype(F32)
    table = table - table[NUM_BUCKETS - 1][None, :]

    def lookup(r):
        onehot = jax.nn.one_hot(_t5_bucket(r), NUM_BUCKETS, dtype=F32)
        return jnp.einsum('kqn,nh->hkq', onehot, table, precision=lax.Precision.HIGHEST)

    return lookup(rel), lookup(rel + blk)


def _select_blocks(gate_t, n_past):
    nb, tq = gate_t.shape
    blk = lax.broadcasted_iota(jnp.int32, (nb, tq), 0)
    beaten = jnp.zeros((nb, tq), jnp.int32)
    for m in range(nb):
        gm = gate_t[m:m + 1, :]
        wins = (gm > gate_t) | ((gm == gate_t) & (m < blk))
        beaten = beaten + jnp.where(wins & (m < n_past), 1, 0)
    return jnp.where((blk < n_past) & (beaten < MOBA_TOPK), 1.0, 0.0)


def _attn_kernel(q_ref, k_ref, vt_ref, km_ref, bias_ref, o_ref, sel_ref, s_ref):
    qi = pl.program_id(2)
    tq = q_ref.shape[0]
    blk = MOBA_BLOCK
    hd = HEAD_DIM
    heads = range(HEADS_PER_STEP)
    q = q_ref[...]
    lane = lax.broadcasted_iota(jnp.int32, (tq, LANES), 1)
    kpos = lax.broadcasted_iota(jnp.int32, (blk, tq), 0)
    qpos = lax.broadcasted_iota(jnp.int32, (blk, tq), 1)
    scale = hd ** -0.5
    n_far = jnp.maximum(qi - 1, 0)
    n_pairs = (n_far + 1) // 2
    jp = jnp.maximum(qi - 1, 0)
    k_own = k_ref[pl.ds(pl.multiple_of(qi * blk, blk), blk), :]
    k_prev = k_ref[pl.ds(pl.multiple_of(jp * blk, blk), blk), :]

    qbs = []
    for h in heads:
        in_head = (lane >= h * hd) & (lane < (h + 1) * hd)
        qm = jnp.where(in_head, q, 0.0)
        gate_t = lax.dot_general(km_ref[0], qm, (((1,), (1,)), ((), ())),
                                 precision=lax.Precision.HIGHEST, preferred_element_type=F32)
        sel_ref[h] = _select_blocks(gate_t, qi)
        qbs.append((qm * scale).astype(BF16))

    def pair_scores(j):
        kb = k_ref[pl.ds(pl.multiple_of(j * blk, blk), 2 * blk), :]
        return [_bdot_nt(kb, qbs[h]) for h in heads]

    ones_rows = jnp.ones((BF16_ROWS, blk), BF16)

    def probs(s, m):
        return jnp.exp((s - m).astype(BF16))

    def attend(h, p, blocks):
        acc = None
        for i, j in enumerate(blocks):
            lhs = jnp.concatenate([vt_ref[0, j, h * hd:(h + 1) * hd, :], ones_rows], axis=0)
            part = jnp.dot(lhs, p[i * blk:(i + 1) * blk, :], preferred_element_type=F32)
            acc = part if acc is None else acc + part
        return acc

    for h, s in enumerate(pair_scores(0)):
        s_ref[h] = s

    carries = []
    for h in heads:
        s_prev = _bdot_nt(k_prev, qbs[h]) + bias_ref[h, 0:blk, :]
        s_prev = jnp.where(sel_ref[h, pl.ds(jp, 1), :] > 0.5, s_prev, MASK_VALUE)
        s_own = _bdot_nt(k_own, qbs[h]) + bias_ref[h, blk:2 * blk, :]
        s_own = jnp.where(kpos <= qpos, s_own, MASK_VALUE)
        s = jnp.concatenate([s_prev, s_own], axis=0)
        m = jnp.max(s, axis=0, keepdims=True)
        carries.append((m, attend(h, probs(s, m), (jp, qi))))

    def far_pair(pi, carries):
        j = 2 * pi
        s_cur = [s_ref[h] for h in heads]
        for h, s in enumerate(pair_scores(2 * jnp.minimum(pi + 1, n_pairs - 1))):
            s_ref[h] = s
        second_is_far = j + 1 < n_far
        out = []
        for h in heads:
            m, acc = carries[h]
            c0 = sel_ref[h, pl.ds(j, 1), :] > 0.5
            c1 = (sel_ref[h, pl.ds(j + 1, 1), :] > 0.5) & second_is_far
            chosen = jnp.concatenate([jnp.broadcast_to(c0, (blk, tq)), jnp.broadcast_to(c1, (blk, tq))], axis=0)
            s = jnp.where(chosen, s_cur[h], MASK_VALUE)
            m_new = jnp.maximum(m, jnp.max(s, axis=0, keepdims=True))
            acc = jnp.exp(m - m_new) * acc + attend(h, probs(s, m_new), (j, j + 1))
            out.append((m_new, acc))
        return tuple(out)

    carries = lax.fori_loop(0, n_pairs, far_pair, tuple(carries))
    out_t = jnp.concatenate([acc[:hd] / acc[hd:hd + 1] for _, acc in carries], axis=0)
    o_ref[...] = out_t.T.astype(o_ref.dtype)


def _moba_attention(q, k, v, kmean, rel_bias, bsz, seq):
    n, aw = q.shape
    blk = MOBA_BLOCK
    nb = seq // blk
    assert nb >= 2
    own, prev = _bias_tables(rel_bias)
    bias = jnp.concatenate([prev, own], axis=1)
    hps = HEADS_PER_STEP
    npair = aw // LANES
    vt = v.reshape(bsz, nb, blk, aw).transpose(0, 1, 3, 2)
    return pl.pallas_call(
        _attn_kernel,
        out_shape=jax.ShapeDtypeStruct((n, aw), BF16),
        grid=(bsz, npair, nb),
        in_specs=[pl.BlockSpec((blk, LANES), lambda b, hp, qi: (b * nb + qi, hp)),
                  pl.BlockSpec((seq, LANES), lambda b, hp, qi: (b, hp)),
                  pl.BlockSpec((1, nb, LANES, blk), lambda b, hp, qi: (b, 0, hp, 0)),
                  pl.BlockSpec((1, nb, LANES), lambda b, hp, qi: (b, 0, hp)),
                  pl.BlockSpec((hps, 2 * blk, blk), lambda b, hp, qi: (hp, 0, 0))],
        out_specs=pl.BlockSpec((blk, LANES), lambda b, hp, qi: (b * nb + qi, hp)),
        scratch_shapes=[pltpu.VMEM((hps, nb, blk), F32), pltpu.VMEM((hps, 2 * blk, blk), F32)],
        compiler_params=_cparams(3),
        name="moba_attention",
    )(q, k, vt, kmean.reshape(bsz, nb, aw), bias)


def _s5_operators(lambda_re, lambda_im, log_dt, b_re, b_im, c_re, c_im, d_skip, n_chunks):
    hi = lax.Precision.HIGHEST
    L, G, P, C = SSM_CHUNK, SSM_GROUPS, SSM_STATE, SSM_GROUP
    lam_re = jnp.minimum(lambda_re.astype(F32), -1e-4)
    lam_im = lambda_im.astype(F32)
    dt = jnp.exp(log_dt.astype(F32))[:, None]
    z_re, z_im = lam_re * dt, lam_im * dt

    def a_pow(nvec):
        nv = jnp.asarray(nvec, F32)[:, None, None]
        mag = jnp.exp(nv * z_re)
        return mag * jnp.cos(nv * z_im), mag * jnp.sin(nv * z_im)

    a_re, a_im = a_pow([1.0])
    a_re, a_im = a_re[0], a_im[0]
    den = lam_re * lam_re + lam_im * lam_im
    nr = a_re - 1.0
    coef_re = (nr * lam_re + a_im * lam_im) / den
    coef_im = (a_im * lam_re - nr * lam_im) / den
    br, bi = b_re.astype(F32), b_im.astype(F32)
    bbar_re = coef_re[..., None] * br - coef_im[..., None] * bi
    bbar_im = coef_re[..., None] * bi + coef_im[..., None] * br
    cr, ci = c_re.astype(F32), c_im.astype(F32)

    pw_re, pw_im = a_pow(np.arange(L + 1))
    cb_re = cr[None] * pw_re[:, :, None, :] - ci[None] * pw_im[:, :, None, :]
    cb_im = cr[None] * pw_im[:, :, None, :] + ci[None] * pw_re[:, :, None, :]
    kern = (jnp.einsum('jgop,gpi->gijo', cb_re[:L], bbar_re, precision=hi)
            - jnp.einsum('jgop,gpi->gijo', cb_im[:L], bbar_im, precision=hi)).reshape(G, C, L * C)
    t_op = jnp.stack([jnp.pad(kern[:, :, :(L - s) * C], ((0, 0), (0, 0), (s * C, 0))) for s in range(L)], axis=1)
    d_g = d_skip.astype(F32).reshape(G, 1, C, 1)
    on_diag = (lax.broadcasted_iota(jnp.int32, (1, L, C, L * C), 3)
               == lax.broadcasted_iota(jnp.int32, (1, L, C, L * C), 1) * C
               + lax.broadcasted_iota(jnp.int32, (1, L, C, L * C), 2))
    t_op = (t_op + jnp.where(on_diag, d_g, 0.0)).reshape(G, L * C, L * C)

    rp_re, rp_im = jnp.flip(pw_re[:L], 0), jnp.flip(pw_im[:L], 0)
    p_re = rp_re[..., None] * bbar_re[None] - rp_im[..., None] * bbar_im[None]
    p_im = rp_re[..., None] * bbar_im[None] + rp_im[..., None] * bbar_re[None]
    p_op = jnp.concatenate([p_re, p_im], axis=2)
    p_op = p_op.transpose(1, 0, 3, 2).reshape(G, L * C, 2 * P)

    q_re = cb_re[1:].transpose(1, 3, 0, 2)
    q_im = -cb_im[1:].transpose(1, 3, 0, 2)
    q_op = jnp.concatenate([q_re, q_im], axis=1).reshape(G, 2 * P, L * C)

    n_steps = max(1, int(math.ceil(math.log2(n_chunks))))
    dk_re, dk_im = a_pow([float(L * 2 ** k) for k in range(n_steps)])
    GB = LANES // C
    NB = G // GB
    lc = np.arange(L * C)
    wide = np.arange(L * LANES)
    expand_tc = jnp.asarray((lc[:, None] // C == wide[None, :] // LANES) & (lc[:, None] % C == wide[None, :] % C), BF16)
    st = np.arange(2 * P)
    wide_st = np.arange(2 * GB * P)
    expand_st = jnp.asarray((st[:, None] // P == wide_st[None, :] // (GB * P))
                            & (st[:, None] % P == wide_st[None, :] % P), BF16)
    g_of_wide = (jnp.arange(L * LANES) // C) % GB
    g_of_state = (jnp.arange(2 * GB * P) // P) % GB

    def widen(rows, expand, g_row, g_col):
        full = jnp.einsum('brk,kc->brc', rows.astype(BF16), expand, preferred_element_type=F32)
        return jnp.where(g_row[:, None] == g_col[None, :], full, 0.0).astype(BF16)

    t_rows = t_op.reshape(NB, GB, L, C, L * C).transpose(0, 2, 1, 3, 4).reshape(NB, L * LANES, L * C)
    p_rows = p_op.reshape(NB, GB, L, C, 2 * P).transpose(0, 2, 1, 3, 4).reshape(NB, L * LANES, 2 * P)
    q_rows = q_op.reshape(NB, GB, 2, P, L * C).transpose(0, 2, 1, 3, 4).reshape(NB, 2 * GB * P, L * C)
    t_big = widen(t_rows, expand_tc, g_of_wide, g_of_wide)
    p_big = widen(p_rows, expand_st, g_of_wide, g_of_state)
    q_big = widen(q_rows, expand_tc, g_of_state, g_of_wide)
    dk_re = dk_re.reshape(-1, NB, GB * P)
    dk_im = dk_im.reshape(-1, NB, GB * P)
    a1 = jnp.concatenate([dk_re, dk_re], axis=-1).transpose(1, 0, 2)
    a2 = jnp.concatenate([-dk_im, dk_im], axis=-1).transpose(1, 0, 2)
    return t_big.astype(BF16), p_big.astype(BF16), q_big.astype(BF16), a1, a2


def _s5_kernel(x_ref, t_ref, p_ref, q_ref, a1_ref, a2_ref, y_ref, *, n_chunks):
    x = x_ref[0]
    s = jnp.dot(x, p_ref[0], preferred_element_type=F32)
    rows, width = s.shape
    chunk = lax.broadcasted_iota(jnp.int32, (rows, width), 0) % n_chunks
    a1 = a1_ref[0]
    a2 = a2_ref[0]
    h = jnp.where(chunk >= 1, pltpu.roll(s, 1, axis=0), 0.0)
    for kk in range(a1.shape[0]):
        dist = 2 ** kk
        if dist >= n_chunks:
            break
        hs = jnp.where(chunk >= dist, pltpu.roll(h, dist, axis=0), 0.0)
        h = h + a1[kk:kk + 1, :] * hs + a2[kk:kk + 1, :] * pltpu.roll(hs, width // 2, axis=1)
    hb = h.astype(BF16)
    step = 2 * LANES
    for t in range(x.shape[1] // step):
        hi = (t + 1) * step
        y_ref[0, :, t * step:hi] = (jnp.dot(x[:, :hi], t_ref[0, :hi, t * step:hi], preferred_element_type=F32)
                                    + jnp.dot(hb, q_ref[0, :, t * step:hi], preferred_element_type=F32))


def _s5_scan(x_chunks, ops, bsz, seq):
    t_big, p_big, q_big, a1, a2 = ops
    nblk, rows, w = x_chunks.shape
    nc = seq // SSM_CHUNK
    sw = p_big.shape[2]
    spb = S5_SEQS_PER_STEP if bsz % S5_SEQS_PER_STEP == 0 else 1
    col = lambda cb, b: (cb, 0, 0)
    return pl.pallas_call(
        functools.partial(_s5_kernel, n_chunks=nc),
        out_shape=jax.ShapeDtypeStruct((nblk, rows, w), F32),
        grid=(nblk, bsz // spb),
        in_specs=[pl.BlockSpec((1, spb * nc, w), lambda cb, b: (cb, b, 0)),
                  pl.BlockSpec((1, w, w), col),
                  pl.BlockSpec((1, w, sw), col),
                  pl.BlockSpec((1, sw, w), col),
                  pl.BlockSpec((1,) + a1.shape[1:], col),
                  pl.BlockSpec((1,) + a2.shape[1:], col)],
        out_specs=pl.BlockSpec((1, spb * nc, w), lambda cb, b: (cb, b, 0)),
        compiler_params=_cparams(2),
        name="s5_scan",
    )(x_chunks, t_big, p_big, q_big, a1, a2)


def _gelu_tanh(x):
    return 0.5 * x * (1.0 + jnp.tanh(math.sqrt(2.0 / math.pi) * (x + 0.044715 * (x * x * x))))


def _mix_kernel(x_ref, attn_ref, yssm_ref, ga_ref, gs_ref, mod_ref, ln_ref,
                wglu_ref, bglu_ref, wua_ref, wus_ref, wout_ref, wrt_ref, wsgu_ref, wsd_ref,
                h2_ref, base_ref, score_ref, y_scr):
    n_chunk = yssm_ref.shape[1]
    for cb in range(yssm_ref.shape[0]):
        for tau in range(SSM_CHUNK):
            y_scr[cb, pl.ds(tau, n_chunk, stride=SSM_CHUNK), :] = yssm_ref[cb, :, tau * LANES:(tau + 1) * LANES]
    g = _gelu_tanh(jnp.concatenate([y_scr[cb] for cb in range(yssm_ref.shape[0])], axis=1))
    glu = g * _sigmoid(_bdot(g, wglu_ref[...]) + bglu_ref[...])
    y_attn = jnp.dot(attn_ref[...], wua_ref[...], preferred_element_type=F32)
    y_ssm = _bdot(glu, wus_ref[...])
    mixed = _sigmoid(ga_ref[...].astype(F32)) * y_attn + _sigmoid(gs_ref[...].astype(F32)) * y_ssm
    gate1 = mod_ref[0, 2:3, :]
    x1 = x_ref[...] + gate1 * _bdot(mixed, wout_ref[...])
    h2 = _modulated_norm(x1, ln_ref[...], mod_ref[0, 3:4, :], mod_ref[0, 4:5, :])
    _store_rows_as_tiles(h2_ref, h2)
    h2b = h2.astype(BF16)
    score_ref[...] = _sigmoid(_bdot_nt(wrt_ref[...], h2b))
    gu = jnp.dot(h2b, wsgu_ref[...], preferred_element_type=F32)
    sd = wsd_ref.shape[0]
    shared = _bdot(_silu(gu[:, :sd]) * gu[:, sd:], wsd_ref[...])
    base_ref[...] = x1 + mod_ref[0, 5:6, :] * shared


def _mix(xf, attn, yssm, ga, gs, mod3, ln2_g, w, seq):
    n, d = xf.shape
    tm = TOKEN_TILE
    tiles_per_seq = seq // tm
    row = lambda i: (i, 0)
    const = lambda i: (0, 0)
    nt = d // LANES
    weights = [w["glu"], w["b_glu"], w["up_attn"], w["up_ssm"], w["out"], w["router_t"], w["sh_gu"], w["sh_down"]]
    return pl.pallas_call(
        _mix_kernel,
        out_shape=(jax.ShapeDtypeStruct((n * nt, LANES), F32),
                   jax.ShapeDtypeStruct((n, d), F32),
                   jax.ShapeDtypeStruct((N_EXPERTS, n), F32)),
        grid=(n // tm,),
        in_specs=[pl.BlockSpec((tm, d), row),
                  pl.BlockSpec((tm, attn.shape[1]), row),
                  pl.BlockSpec((yssm.shape[0], tm // SSM_CHUNK, yssm.shape[2]), lambda i: (0, i, 0)),
                  pl.BlockSpec((tm, d), row),
                  pl.BlockSpec((tm, d), row),
                  pl.BlockSpec((1, 6, d), lambda i: (i // tiles_per_seq, 0, 0)),
                  pl.BlockSpec((1, d), const)] + [pl.BlockSpec(a.shape, const) for a in weights],
        out_specs=(pl.BlockSpec((tm * nt, LANES), row), pl.BlockSpec((tm, d), row),
                   pl.BlockSpec((N_EXPERTS, tm), lambda i: (0, i))),
        scratch_shapes=[pltpu.VMEM((yssm.shape[0], tm, LANES), F32)],
        compiler_params=_cparams(1),
        name="mix",
    )(xf, attn, yssm, ga, gs, mod3, ln2_g.reshape(1, d), *weights)


def _route_kernel(score_ref, bias_ref, tri_ref, idx_ref, w_ref, rank_ref, cnt_ref, carry_ref):
    @pl.when(pl.program_id(0) == 0)
    def _():
        carry_ref[...] = jnp.zeros_like(carry_ref)

    scores = score_ref[...]
    ne, tn = scores.shape
    biased = scores + bias_ref[...]
    gsz = GROUP_SIZE
    sub = lax.broadcasted_iota(jnp.int32, (gsz, tn), 0)
    group_score = []
    for g in range(N_EXPERT_GROUPS):
        sg = biased[g * gsz:(g + 1) * gsz, :]
        m1 = jnp.max(sg, axis=0, keepdims=True)
        first = jnp.min(jnp.where(sg == m1, sub, gsz), axis=0, keepdims=True)
        m2 = jnp.max(jnp.where(sub == first, -jnp.inf, sg), axis=0, keepdims=True)
        group_score.append(m1 + m2)
    group_rows = []
    for g in range(N_EXPERT_GROUPS):
        beaten = jnp.zeros((1, tn), jnp.int32)
        for o in range(N_EXPERT_GROUPS):
            if o == g:
                continue
            wins = (group_score[o] > group_score[g])
            if o < g:
                wins = wins | (group_score[o] == group_score[g])
            beaten = beaten + jnp.where(wins, 1, 0)
        group_rows.append(jnp.broadcast_to(beaten < TOPK_GROUPS, (gsz, tn)))
    allowed = jnp.concatenate(group_rows, axis=0)
    cur = jnp.where(allowed, biased, MASK_VALUE)
    eio = lax.broadcasted_iota(jnp.int32, (ne, tn), 0)
    idx_rows, w_rows, hits = [], [], []
    for _ in range(TOP_K):
        vmax = jnp.max(cur, axis=0, keepdims=True)
        eidx = jnp.min(jnp.where(cur == vmax, eio, ne), axis=0, keepdims=True)
        hit = eio == eidx
        w_rows.append(jnp.sum(jnp.where(hit, scores, 0.0), axis=0, keepdims=True))
        idx_rows.append(eidx)
        hits.append(hit)
        cur = jnp.where(hit, -jnp.inf, cur)
    wts = jnp.concatenate(w_rows, axis=0)
    idx_ref[...] = jnp.concatenate(idx_rows, axis=0)
    w_ref[...] = wts / jnp.sum(wts, axis=0, keepdims=True) * ROUTED_SCALE

    onehot = jnp.zeros((ne, tn), F32)
    for hit in hits:
        onehot = onehot + jnp.where(hit, 1.0, 0.0)
    earlier = carry_ref[...] + _bdot(onehot, tri_ref[...])
    rank_rows = [jnp.sum(jnp.where(hit, earlier, 0.0), axis=0, keepdims=True) for hit in hits]
    rank_ref[...] = jnp.concatenate(rank_rows, axis=0).astype(jnp.int32)
    carry_ref[...] = carry_ref[...] + jnp.sum(onehot, axis=1, keepdims=True)
    cnt_ref[...] = carry_ref[...]


def _route(scores_t, router_bias):
    ne, n = scores_t.shape
    tn = ROUTE_TILE
    tri = jnp.asarray(np.arange(tn)[:, None] < np.arange(tn)[None, :], BF16)
    tok = lambda i: (0, i)
    const = lambda i: (0, 0)
    return pl.pallas_call(
        _route_kernel,
        out_shape=(jax.ShapeDtypeStruct((TOP_K, n), jnp.int32), jax.ShapeDtypeStruct((TOP_K, n), F32),
                   jax.ShapeDtypeStruct((TOP_K, n), jnp.int32), jax.ShapeDtypeStruct((ne, 1), F32)),
        grid=(n // tn,),
        in_specs=[pl.BlockSpec((ne, tn), tok), pl.BlockSpec((ne, 1), const), pl.BlockSpec((tn, tn), const)],
        out_specs=(pl.BlockSpec((TOP_K, tn), tok), pl.BlockSpec((TOP_K, tn), tok),
                   pl.BlockSpec((TOP_K, tn), tok), pl.BlockSpec((ne, 1), const)),
        scratch_shapes=[pltpu.VMEM((ne, 1), F32)],
        compiler_params=_cparams(1),
        name="route",
    )(scores_t, router_bias.reshape(ne, 1).astype(F32), tri)


def _dest_kernel(idx_ref, rank_ref, poffs_ref, dest_ref):
    idx = idx_ref[...]
    ne = poffs_ref.shape[0]
    eio = lax.broadcasted_iota(jnp.int32, (ne, idx.shape[1]), 0)
    poffs = poffs_ref[...]
    rows = [jnp.sum(jnp.where(eio == idx[kk:kk + 1, :], poffs, 0.0), axis=0, keepdims=True)
            for kk in range(idx.shape[0])]
    dest_ref[...] = rank_ref[...] + jnp.concatenate(rows, axis=0).astype(jnp.int32)


def _dest_slots(idx_t, rank_t, poffs):
    k, n = idx_t.shape
    tn = ROUTE_TILE
    ne = poffs.shape[0]
    tok = lambda i: (0, i)
    return pl.pallas_call(
        _dest_kernel,
        out_shape=jax.ShapeDtypeStruct((k, n), jnp.int32),
        grid=(n // tn,),
        in_specs=[pl.BlockSpec((k, tn), tok), pl.BlockSpec((k, tn), tok), pl.BlockSpec((ne, 1), lambda i: (0, 0))],
        out_specs=pl.BlockSpec((k, tn), tok),
        compiler_params=_cparams(1),
        name="dest_slots",
    )(idx_t, rank_t, poffs.astype(F32).reshape(ne, 1))


def _dispatch_tables(idx_t, counts, n_blocks):
    k, n = idx_t.shape
    n_assign = k * n
    rb = ROW_BLOCK
    ne = counts.shape[0]
    counts = counts.reshape(ne).astype(jnp.int32)
    offs = jnp.cumsum(counts) - counts
    nblk = (counts + rb - 1) // rb
    bend = jnp.cumsum(nblk)
    bstart = bend - nblk
    blocks = jnp.arange(n_blocks, dtype=jnp.int32)
    blk_e = jnp.minimum(jnp.sum((bend[None, :] <= blocks[:, None]).astype(jnp.int32), axis=1), ne - 1)
    mine = blk_e[:, None] == jnp.arange(ne, dtype=jnp.int32)[None, :]
    src_start = jnp.sum(jnp.where(mine, (offs - bstart * rb)[None, :], 0), axis=1) + blocks * rb
    tok = lax.broadcasted_iota(jnp.int32, (k, n), 1)
    kk = lax.broadcasted_iota(jnp.int32, (k, n), 0)
    assert ne * n_assign < 2 ** 31
    keys = lax.sort((idx_t * n_assign + tok * k + kk).reshape(n_assign), is_stable=False)
    tok_sorted = (keys % n_assign) // k
    experts = jnp.arange(ne, dtype=jnp.int32)
    nonempty = nblk > 0
    run = jnp.cumsum(nonempty.astype(jnp.int32)) - 1
    later = (experts[None, :] > experts[:, None]) & nonempty[None, :]
    next_e = jnp.min(jnp.where(later, experts[None, :], ne), axis=1)
    next_e = jnp.where(next_e == ne, -1, next_e)
    per_expert = (run % 2) * 2 + (next_e + 1) * 4
    winfo = (jnp.sum(jnp.where(mine, per_expert[None, :], 0), axis=1)
             + (blocks == jnp.sum(jnp.where(mine, bstart[None, :], 0), axis=1)).astype(jnp.int32))
    return tok_sorted, blk_e, src_start, bend[ne - 1:ne], winfo.astype(jnp.int32), bstart * rb


def _ffn_kernel(tok_ref, blk_e_ref, start_ref, used_ref, winfo_ref, h2_hbm, wg_hbm, wu_hbm, wd_hbm, y_ref,
                xbuf, wg_buf, wu_buf, wd_buf, sems, wsems):
    b = pl.program_id(0)
    rb = ROW_BLOCK
    d = wg_hbm.shape[1]
    nt = d // LANES
    n_assign = tok_ref.shape[0]
    n_used = used_ref[0]
    n_blocks = pl.num_programs(0)

    info = winfo_ref[b]
    first_of_expert = (info & 1) == 1
    wslot = (info >> 1) & 1
    next_expert = (info >> 2) - 1

    def weight_copies(e, s):
        return [pltpu.make_async_copy(src.at[e], dst.at[s], wsems.at[s])
                for src, dst in ((wg_hbm, wg_buf), (wu_hbm, wu_buf), (wd_hbm, wd_buf))]

    @pl.when(b == 0)
    def _():
        for c in weight_copies(blk_e_ref[0], wslot):
            c.start()

    @pl.when((b < n_used) & first_of_expert)
    def _():
        for c in weight_copies(blk_e_ref[b], wslot):
            c.wait()

        @pl.when(next_expert >= 0)
        def _():
            for c in weight_copies(next_expert, 1 - wslot):
                c.start()

    def start_rows(blk, s, inline):
        start = start_ref[blk]

        def one(r, queue):
            tok = tok_ref[jnp.minimum(start + r, n_assign - 1)]
            pltpu.make_async_copy(h2_hbm.at[pl.ds(pl.multiple_of(tok * nt, nt), nt)],
                                  xbuf.at[s, pl.ds(pl.multiple_of(r * nt, nt), nt)], sems.at[s]
                                  ).start(priority=queue)

        if inline:
            for r in range(rb):
                one(r, r % DMA_QUEUES)
        else:
            def group(g, c):
                for i in range(ISSUE_UNROLL):
                    one(g * ISSUE_UNROLL + i, i % DMA_QUEUES)
                return c

            lax.fori_loop(0, rb // ISSUE_UNROLL, group, 0)

    def wait_rows(s):
        pltpu.make_async_copy(h2_hbm.at[pl.ds(0, rb * nt)], xbuf.at[s], sems.at[s]).wait()

    nbuf = xbuf.shape[0]
    ahead = nbuf - 1

    @pl.when(b == 0)
    def _():
        for a in range(ahead):
            start_rows(jnp.minimum(a, n_blocks - 1), a, False)

    for s in range(nbuf):
        @pl.when((b < n_used) & (b % nbuf == s))
        def _(s=s):
            wait_rows(s)
            xb = _load_rows_from_tiles(xbuf.at[s], rb, d).astype(BF16)
            start_rows(jnp.minimum(b + ahead, n_blocks - 1), (s + ahead) % nbuf, True)
            hg = jnp.dot(xb, wg_buf[wslot].astype(BF16), preferred_element_type=F32)
            hu = jnp.dot(xb, wu_buf[wslot].astype(BF16), preferred_element_type=F32)
            _store_rows_as_tiles(y_ref, _bdot(_silu(hg) * hu, wd_buf[wslot]))

    @pl.when(b == n_used - 1)
    def _():
        for a in range(1, nbuf):
            wait_rows((b + a) % nbuf)

    @pl.when(b >= n_used)
    def _():
        y_ref[...] = jnp.zeros_like(y_ref)


def _expert_ffn(h2_tiles, tok_sorted, blk_e, src_start, n_used, winfo, w_gate, w_up, w_down):
    n_blocks = blk_e.shape[0]
    rb = ROW_BLOCK
    _, d, ed = w_gate.shape
    nt = d // LANES
    hbm = pl.BlockSpec(memory_space=pl.ANY)
    grid_spec = pltpu.PrefetchScalarGridSpec(
        num_scalar_prefetch=5,
        grid=(n_blocks,),
        in_specs=[hbm, hbm, hbm, hbm],
        out_specs=pl.BlockSpec((rb * nt, LANES), lambda i, tok, be, st, nu, wi: (i, 0)),
        scratch_shapes=[pltpu.VMEM((FFN_ROW_BUFFERS, rb * nt, LANES), F32),
                        pltpu.VMEM((2, d, ed), w_gate.dtype), pltpu.VMEM((2, d, ed), w_up.dtype),
                        pltpu.VMEM((2, ed, d), w_down.dtype),
                        pltpu.SemaphoreType.DMA((FFN_ROW_BUFFERS,)), pltpu.SemaphoreType.DMA((2,))],
    )
    return pl.pallas_call(
        _ffn_kernel,
        out_shape=jax.ShapeDtypeStruct((n_blocks * rb * nt, LANES), F32),
        grid_spec=grid_spec,
        compiler_params=_cparams(1),
        name="expert_ffn",
    )(tok_sorted, blk_e, src_start, n_used, winfo, h2_tiles, w_gate, w_up, w_down)


def _combine_kernel(dest_ref, y_hbm, base_ref, wt_ref, mod_ref, o_ref, buf, sems):
    i = pl.program_id(0)
    n_steps = pl.num_programs(0)
    tm, d = base_ref.shape
    k = wt_ref.shape[1]
    nt = d // LANES
    n_rows = k * tm
    nbuf = buf.shape[0]
    ahead = nbuf - 1

    def issue(step, s, inline):
        first = step * n_rows

        def one(r, queue):
            src = dest_ref[first + r]
            pltpu.make_async_copy(y_hbm.at[pl.ds(pl.multiple_of(src * nt, nt), nt)],
                                  buf.at[s, pl.ds(pl.multiple_of(r * nt, nt), nt)], sems.at[s]
                                  ).start(priority=queue)

        if inline:
            for r in range(n_rows):
                one(r, r % DMA_QUEUES)
        else:
            def group(g, c):
                for j in range(ISSUE_UNROLL):
                    one(g * ISSUE_UNROLL + j, j % DMA_QUEUES)
                return c

            lax.fori_loop(0, n_rows // ISSUE_UNROLL, group, 0)

    def wait_rows(s):
        pltpu.make_async_copy(y_hbm.at[pl.ds(0, n_rows * nt)], buf.at[s], sems.at[s]).wait()

    @pl.when(i == 0)
    def _():
        for a in range(ahead):
            issue(jnp.minimum(a, n_steps - 1), a, False)

    for s in range(nbuf):
        @pl.when(i % nbuf == s)
        def _(s=s):
            wait_rows(s)
            issue(jnp.minimum(i + ahead, n_steps - 1), (s + ahead) % nbuf, True)
            rows = buf.at[s]
            wts = wt_ref[...]
            gate2 = mod_ref[0, 5:6, :]
            for c in range(nt):
                cols = slice(c * LANES, (c + 1) * LANES)
                routed = jnp.zeros((tm, LANES), F32)
                for kk in range(k):
                    routed = routed + wts[:, kk:kk + 1] * rows[pl.ds(kk * tm * nt + c, tm, stride=nt), :]
                o_ref[:, cols] = base_ref[:, cols] + gate2[:, cols] * routed

    @pl.when(i == n_steps - 1)
    def _():
        for a in range(1, nbuf):
            wait_rows((i + a) % nbuf)


def _combine(y_tiles, dest_t, w_t, base, mod3, seq):
    n, d = base.shape
    tm = COMBINE_TILE
    k = dest_t.shape[0]
    nt = d // LANES
    tiles_per_seq = seq // tm
    dest_tiles = dest_t.reshape(k, n // tm, tm).transpose(1, 0, 2).reshape(n * k)
    grid_spec = pltpu.PrefetchScalarGridSpec(
        num_scalar_prefetch=1,
        grid=(n // tm,),
        in_specs=[pl.BlockSpec(memory_space=pl.ANY),
                  pl.BlockSpec((tm, d), lambda i, dst: (i, 0)),
                  pl.BlockSpec((tm, k), lambda i, dst: (i, 0)),
                  pl.BlockSpec((1, 6, d), lambda i, dst: (i // tiles_per_seq, 0, 0))],
        out_specs=pl.BlockSpec((tm, d), lambda i, dst: (i, 0)),
        scratch_shapes=[pltpu.VMEM((COMBINE_ROW_BUFFERS, k * tm * nt, LANES), F32),
                        pltpu.SemaphoreType.DMA((COMBINE_ROW_BUFFERS,))],
    )
    return pl.pallas_call(
        _combine_kernel,
        out_shape=jax.ShapeDtypeStruct((n, d), F32),
        grid_spec=grid_spec,
        compiler_params=_cparams(1),
        name="combine",
    )(dest_tiles, y_tiles, base, w_t.T, mod3)


def _hybrid_layer(x, cond, rel_bias, w_ada, b_ada, ln1_g, w_in, q_norm_g, k_norm_g,
                  ssm_lambda_re, ssm_lambda_im, ssm_log_dt, ssm_b_re, ssm_b_im, ssm_c_re, ssm_c_im,
                  ssm_d, ssm_w_glu, ssm_b_glu, w_up_attn, w_up_ssm, w_out, ln2_g,
                  w_router, router_bias, w_exp_gate, w_exp_up, w_exp_down,
                  w_sh_gate, w_sh_up, w_sh_down):
    bsz, seq, d = x.shape
    n = bsz * seq
    xf = x.reshape(n, d)
    mod3 = _adaln(cond, w_ada, b_ada).reshape(bsz, 6, d)

    q_gain = jnp.tile(q_norm_g.astype(F32), ATTN_HEADS).reshape(1, ATTN_WIDTH)
    k_gain = jnp.tile(k_norm_g.astype(F32), ATTN_HEADS).reshape(1, ATTN_WIDTH)
    q, k, v, u, ga, gs, kmean = _inproj(xf, mod3, ln1_g, w_in.astype(BF16), q_gain, k_gain, seq)

    attn = _moba_attention(q, k, v, kmean, rel_bias, bsz, seq)
    ops = _s5_operators(ssm_lambda_re, ssm_lambda_im, ssm_log_dt, ssm_b_re, ssm_b_im,
                        ssm_c_re, ssm_c_im, ssm_d, seq // SSM_CHUNK)
    yssm = _s5_scan(u, ops, bsz, seq)

    weights = {
        "glu": ssm_w_glu.astype(BF16), "b_glu": ssm_b_glu.astype(F32).reshape(1, -1),
        "up_attn": w_up_attn.astype(BF16), "up_ssm": w_up_ssm.astype(BF16), "out": w_out.astype(BF16),
        "router_t": w_router.T.astype(BF16),
        "sh_gu": jnp.concatenate([w_sh_gate, w_sh_up], axis=1).astype(BF16),
        "sh_down": w_sh_down.astype(BF16),
    }
    h2, base, scores_t = _mix(xf, attn, yssm, ga, gs, mod3, ln2_g, weights, seq)

    idx_t, w_t, rank_t, counts = _route(scores_t, router_bias)
    n_blocks = -(-(n * TOP_K) // ROW_BLOCK) + N_EXPERTS
    tok_sorted, blk_e, src_start, n_used, winfo, poffs = _dispatch_tables(idx_t, counts, n_blocks)
    dest_t = _dest_slots(idx_t, rank_t, poffs)
    y_tiles = _expert_ffn(h2, tok_sorted, blk_e, src_start, n_used, winfo, w_exp_gate, w_exp_up, w_exp_down)
    out = _combine(y_tiles, dest_t, w_t, base, mod3, seq)
    return out.reshape(bsz, seq, d)


def kernel(x, c, rel_bias, w_ada, b_ada, ln1_g, w_in, q_norm_g, k_norm_g, ssm_lambda_re, ssm_lambda_im, ssm_log_dt, ssm_b_re, ssm_b_im, ssm_c_re, ssm_c_im, ssm_d, ssm_w_glu, ssm_b_glu, w_up_attn, w_up_ssm, w_out, ln2_g, w_router, router_bias, w_exp_gate, w_exp_up, w_exp_down, w_sh_gate, w_sh_up, w_sh_down):
    for l in range(w_ada.shape[0]):
        x = _hybrid_layer(x, c, rel_bias, w_ada[l], b_ada[l], ln1_g[l], w_in[l], q_norm_g[l], k_norm_g[l],
                          ssm_lambda_re[l], ssm_lambda_im[l], ssm_log_dt[l], ssm_b_re[l], ssm_b_im[l],
                          ssm_c_re[l], ssm_c_im[l], ssm_d[l], ssm_w_glu[l], ssm_b_glu[l],
                          w_up_attn[l], w_up_ssm[l], w_out[l], ln2_g[l], w_router[l], router_bias[l],
                          w_exp_gate[l], w_exp_up[l], w_exp_down[l], w_sh_gate[l], w_sh_up[l], w_sh_down[l])
    return x
```

```python
import functools
import math

import numpy as np
import jax
import jax.numpy as jnp
from jax import lax
from jax.experimental import pallas as pl
from jax.experimental.pallas import tpu as pltpu

F32 = jnp.float32
BF16 = jnp.bfloat16

ATTN_HEADS = 8
HEAD_DIM = 64
ATTN_WIDTH = ATTN_HEADS * HEAD_DIM
MOBA_BLOCK = 256
MOBA_TOPK = 3
NUM_BUCKETS = 32
MAX_DISTANCE = 128
SSM_WIDTH = 512
SSM_GROUP = 16
SSM_GROUPS = SSM_WIDTH // SSM_GROUP
SSM_STATE = 64
N_EXPERTS = 256
TOP_K = 8
N_EXPERT_GROUPS = 8
TOPK_GROUPS = 4
GROUP_SIZE = N_EXPERTS // N_EXPERT_GROUPS
EXPERT_DIM = 256
ROUTED_SCALE = 2.5
EPS = 1e-6
MASK_VALUE = -1e30

LANES = 128
HEADS_PER_STEP = LANES // HEAD_DIM
SSM_CHUNK = 16
S5_SEQS_PER_STEP = 2
SUBLANES = 8
BF16_ROWS = 16
ROW_BLOCK = 256
TOKEN_TILE = 512
ROUTE_TILE = 512
COMBINE_TILE = 128
ISSUE_UNROLL = 8
DMA_QUEUES = 2
COMBINE_ROW_BUFFERS = 3
FFN_ROW_BUFFERS = 3
VMEM_LIMIT = 56 * 1024 * 1024


def _store_rows_as_tiles(ref, val):
    rows, d = val.shape
    nt = d // LANES
    for c in range(nt):
        ref[pl.ds(c, rows, stride=nt), :] = val[:, c * LANES:(c + 1) * LANES]


def _load_rows_from_tiles(ref, rows, d):
    nt = d // LANES
    return jnp.concatenate([ref[pl.ds(c, rows, stride=nt), :] for c in range(nt)], axis=1)


def _cparams(n_axes, vmem=VMEM_LIMIT):
    return pltpu.CompilerParams(dimension_semantics=("arbitrary",) * n_axes, vmem_limit_bytes=vmem)


def _sigmoid(x):
    return 1.0 / (1.0 + jnp.exp(-x))


def _silu(x):
    return x * _sigmoid(x)


def _bdot(a, b):
    return jnp.dot(a.astype(BF16), b.astype(BF16), preferred_element_type=F32)


def _bdot_nt(a, b):
    return lax.dot_general(a.astype(BF16), b.astype(BF16), (((1,), (1,)), ((), ())),
                           preferred_element_type=F32)


def _adaln_kernel(c_ref, w_ref, b_ref, o_ref):
    o_ref[...] = _bdot(_silu(c_ref[...]), w_ref[...]) + b_ref[...]


def _adaln(c, w_ada, b_ada):
    bsz, d = c.shape
    n_out = w_ada.shape[1]
    return pl.pallas_call(
        _adaln_kernel,
        out_shape=jax.ShapeDtypeStruct((bsz, n_out), F32),
        grid=(n_out // d,),
        in_specs=[pl.BlockSpec((bsz, d), lambda j: (0, 0)),
                  pl.BlockSpec((d, d), lambda j: (0, j)),
                  pl.BlockSpec((1, d), lambda j: (0, j))],
        out_specs=pl.BlockSpec((bsz, d), lambda j: (0, j)),
        compiler_params=_cparams(1),
        name="adaln",
    )(c, w_ada, b_ada.reshape(1, n_out))


def _modulated_norm(x, gain, shift, scale):
    y = x * lax.rsqrt(jnp.mean(x * x, axis=-1, keepdims=True) + EPS) * gain
    return y * (1.0 + scale) + shift


def _head_norm(t, seg, gain):
    ms = _bdot(t * t, seg)
    return t * lax.rsqrt(ms + EPS) * gain


def _inproj_kernel(x_ref, mod_ref, ln_ref, w_ref, seg_ref, qg_ref, kg_ref,
                   q_ref, k_ref, v_ref, u_ref, ga_ref, gs_ref, km_ref, u_scr):
    aw, sw, d = ATTN_WIDTH, SSM_WIDTH, x_ref.shape[1]
    h = _modulated_norm(x_ref[...], ln_ref[...], mod_ref[0, 0:1, :], mod_ref[0, 1:2, :]).astype(BF16)
    seg = seg_ref[...]
    q = jnp.dot(h, w_ref[:, 0:aw], preferred_element_type=F32)
    q_ref[...] = _head_norm(q, seg, qg_ref[...])
    k = jnp.dot(h, w_ref[:, aw:2 * aw], preferred_element_type=F32)
    kn = _head_norm(k, seg, kg_ref[...])
    k_ref[...] = kn.astype(BF16)
    for blk in range(km_ref.shape[0]):
        km_ref[blk] = jnp.mean(kn[blk * MOBA_BLOCK:(blk + 1) * MOBA_BLOCK, :], axis=0, keepdims=True)
    v_ref[...] = jnp.dot(h, w_ref[:, 2 * aw:3 * aw], preferred_element_type=F32).astype(BF16)
    o = 3 * aw
    u = jnp.dot(h, w_ref[:, o:o + sw], preferred_element_type=F32)
    n_chunk = u_scr.shape[1] // SSM_CHUNK
    for cb in range(sw // LANES):
        u_scr[cb] = u[:, cb * LANES:(cb + 1) * LANES]
        for sg in range(SSM_CHUNK):
            u_ref[cb, :, sg * LANES:(sg + 1) * LANES] = (
                u_scr[cb, pl.ds(sg, n_chunk, stride=SSM_CHUNK), :].astype(BF16))
    o += sw
    ga_ref[...] = jnp.dot(h, w_ref[:, o:o + d], preferred_element_type=F32).astype(BF16)
    o += d
    gs_ref[...] = jnp.dot(h, w_ref[:, o:o + d], preferred_element_type=F32).astype(BF16)


def _inproj(xf, mod3, ln1_g, w_in_b, q_gain, k_gain, seq):
    n, d = xf.shape
    tm = TOKEN_TILE
    assert tm % MOBA_BLOCK == 0 and seq % tm == 0
    blocks_per_tile = tm // MOBA_BLOCK
    tiles_per_seq = seq // tm
    aw, sw = ATTN_WIDTH, SSM_WIDTH
    head_of_lane = np.arange(aw) // HEAD_DIM
    seg = jnp.asarray((head_of_lane[:, None] == head_of_lane[None, :]) / HEAD_DIM, BF16)
    row = lambda i: (i, 0)
    const = lambda i: (0, 0)
    return pl.pallas_call(
        _inproj_kernel,
        out_shape=(jax.ShapeDtypeStruct((n, aw), F32),
                   jax.ShapeDtypeStruct((n, aw), BF16),
                   jax.ShapeDtypeStruct((n, aw), BF16),
                   jax.ShapeDtypeStruct((sw // LANES, n // SSM_CHUNK, SSM_CHUNK * LANES), BF16),
                   jax.ShapeDtypeStruct((n, d), BF16),
                   jax.ShapeDtypeStruct((n, d), BF16),
                   jax.ShapeDtypeStruct((n // MOBA_BLOCK, 1, aw), F32)),
        grid=(n // tm,),
        in_specs=[pl.BlockSpec((tm, d), row),
                  pl.BlockSpec((1, 6, d), lambda i: (i // tiles_per_seq, 0, 0)),
                  pl.BlockSpec((1, d), const),
                  pl.BlockSpec(w_in_b.shape, const),
                  pl.BlockSpec((aw, aw), const),
                  pl.BlockSpec((1, aw), const),
                  pl.BlockSpec((1, aw), const)],
        out_specs=(pl.BlockSpec((tm, aw), row), pl.BlockSpec((tm, aw), row), pl.BlockSpec((tm, aw), row),
                   pl.BlockSpec((sw // LANES, tm // SSM_CHUNK, SSM_CHUNK * LANES), lambda i: (0, i, 0)),
                   pl.BlockSpec((tm, d), row), pl.BlockSpec((tm, d), row),
                   pl.BlockSpec((blocks_per_tile, 1, aw), lambda i: (i, 0, 0))),
        scratch_shapes=[pltpu.VMEM((sw // LANES, tm, LANES), F32)],
        compiler_params=_cparams(1),
        name="inproj",
    )(xf, mod3, ln1_g.reshape(1, d), w_in_b, seg, q_gain, k_gain)


def _t5_bucket(rel):
    n = jnp.maximum(rel, 0)
    max_exact = NUM_BUCKETS // 2
    nf = jnp.maximum(n, 1).astype(F32)
    large = max_exact + (jnp.log(nf / max_exact) / math.log(MAX_DISTANCE / max_exact)
                         * (NUM_BUCKETS - max_exact)).astype(jnp.int32)
    large = jnp.minimum(large, NUM_BUCKETS - 1)
    return jnp.where(n < max_exact, n, large)


def _bias_tables(rel_bias):
    blk = MOBA_BLOCK
    assert blk + 1 >= MAX_DISTANCE
    rel = jnp.arange(blk)[None, :] - jnp.arange(blk)[:, None]
    table = rel_bias.astype(F32)
    table = table - table[NUM_BUCKETS - 1][None, :]

    def lookup(r):
        onehot = jax.nn.one_hot(_t5_bucket(r), NUM_BUCKETS, dtype=F32)
        return jnp.einsum('kqn,nh->hkq', onehot, table, precision=lax.Precision.HIGHEST)

    return lookup(rel), lookup(rel + blk)


def _select_blocks(gate_t, n_past):
    nb, tq = gate_t.shape
    blk = lax.broadcasted_iota(jnp.int32, (nb, tq), 0)
    beaten = jnp.zeros((nb, tq), jnp.int32)
    for m in range(nb):
        gm = gate_t[m:m + 1, :]
        wins = (gm > gate_t) | ((gm == gate_t) & (m < blk))
        beaten = beaten + jnp.where(wins & (m < n_past), 1, 0)
    return jnp.where((blk < n_past) & (beaten < MOBA_TOPK), 1.0, 0.0)


def _attn_kernel(q_ref, k_ref, vt_ref, km_ref, bias_ref, o_ref, sel_ref, s_ref):
    qi = pl.program_id(2)
    tq = q_ref.shape[0]
    blk = MOBA_BLOCK
    hd = HEAD_DIM
    heads = range(HEADS_PER_STEP)
    q = q_ref[...]
    lane = lax.broadcasted_iota(jnp.int32, (tq, LANES), 1)
    kpos = lax.broadcasted_iota(jnp.int32, (blk, tq), 0)
    qpos = lax.broadcasted_iota(jnp.int32, (blk, tq), 1)
    scale = hd ** -0.5
    n_far = jnp.maximum(qi - 1, 0)
    n_pairs = (n_far + 1) // 2
    jp = jnp.maximum(qi - 1, 0)
    k_own = k_ref[pl.ds(pl.multiple_of(qi * blk, blk), blk), :]
    k_prev = k_ref[pl.ds(pl.multiple_of(jp * blk, blk), blk), :]

    qbs = []
    for h in heads:
        in_head = (lane >= h * hd) & (lane < (h + 1) * hd)
        qm = jnp.where(in_head, q, 0.0)
        gate_t = lax.dot_general(km_ref[0], qm, (((1,), (1,)), ((), ())),
                                 precision=lax.Precision.HIGHEST, preferred_element_type=F32)
        sel_ref[h] = _select_blocks(gate_t, qi)
        qbs.append((qm * scale).astype(BF16))

    def pair_scores(j):
        kb = k_ref[pl.ds(pl.multiple_of(j * blk, blk), 2 * blk), :]
        return [_bdot_nt(kb, qbs[h]) for h in heads]

    ones_rows = jnp.ones((BF16_ROWS, blk), BF16)

    def probs(s, m):
        return jnp.exp((s - m).astype(BF16))

    def attend(h, p, blocks):
        acc = None
        for i, j in enumerate(blocks):
            lhs = jnp.concatenate([vt_ref[0, j, h * hd:(h + 1) * hd, :], ones_rows], axis=0)
            part = jnp.dot(lhs, p[i * blk:(i + 1) * blk, :], preferred_element_type=F32)
            acc = part if acc is None else acc + part
        return acc

    for h, s in enumerate(pair_scores(0)):
        s_ref[h] = s

    carries = []
    for h in heads:
        s_prev = _bdot_nt(k_prev, qbs[h]) + bias_ref[h, 0:blk, :]
        s_prev = jnp.where(sel_ref[h, pl.ds(jp, 1), :] > 0.5, s_prev, MASK_VALUE)
        s_own = _bdot_nt(k_own, qbs[h]) + bias_ref[h, blk:2 * blk, :]
        s_own = jnp.where(kpos <= qpos, s_own, MASK_VALUE)
        s = jnp.concatenate([s_prev, s_own], axis=0)
        m = jnp.max(s, axis=0, keepdims=True)
        carries.append((m, attend(h, probs(s, m), (jp, qi))))

    def far_pair(pi, carries):
        j = 2 * pi
        s_cur = [s_ref[h] for h in heads]
        for h, s in enumerate(pair_scores(2 * jnp.minimum(pi + 1, n_pairs - 1))):
            s_ref[h] = s
        second_is_far = j + 1 < n_far
        out = []
        for h in heads:
            m, acc = carries[h]
            c0 = sel_ref[h, pl.ds(j, 1), :] > 0.5
            c1 = (sel_ref[h, pl.ds(j + 1, 1), :] > 0.5) & second_is_far
            chosen = jnp.concatenate([jnp.broadcast_to(c0, (blk, tq)), jnp.broadcast_to(c1, (blk, tq))], axis=0)
            s = jnp.where(chosen, s_cur[h], MASK_VALUE)
            m_new = jnp.maximum(m, jnp.max(s, axis=0, keepdims=True))
            acc = jnp.exp(m - m_new) * acc + attend(h, probs(s, m_new), (j, j + 1))
            out.append((m_new, acc))
        return tuple(out)

    carries = lax.fori_loop(0, n_pairs, far_pair, tuple(carries))
    out_t = jnp.concatenate([acc[:hd] / acc[hd:hd + 1] for _, acc in carries], axis=0)
    o_ref[...] = out_t.T.astype(o_ref.dtype)


def _moba_attention(q, k, v, kmean, rel_bias, bsz, seq):
    n, aw = q.shape
    blk = MOBA_BLOCK
    nb = seq // blk
    assert nb >= 2
    own, prev = _bias_tables(rel_bias)
    bias = jnp.concatenate([prev, own], axis=1)
    hps = HEADS_PER_STEP
    npair = aw // LANES
    vt = v.reshape(bsz, nb, blk, aw).transpose(0, 1, 3, 2)
    return pl.pallas_call(
        _attn_kernel,
        out_shape=jax.ShapeDtypeStruct((n, aw), BF16),
        grid=(bsz, npair, nb),
        in_specs=[pl.BlockSpec((blk, LANES), lambda b, hp, qi: (b * nb + qi, hp)),
                  pl.BlockSpec((seq, LANES), lambda b, hp, qi: (b, hp)),
                  pl.BlockSpec((1, nb, LANES, blk), lambda b, hp, qi: (b, 0, hp, 0)),
                  pl.BlockSpec((1, nb, LANES), lambda b, hp, qi: (b, 0, hp)),
                  pl.BlockSpec((hps, 2 * blk, blk), lambda b, hp, qi: (hp, 0, 0))],
        out_specs=pl.BlockSpec((blk, LANES), lambda b, hp, qi: (b * nb + qi, hp)),
        scratch_shapes=[pltpu.VMEM((hps, nb, blk), F32), pltpu.VMEM((hps, 2 * blk, blk), F32)],
        compiler_params=_cparams(3),
        name="moba_attention",
    )(q, k, vt, kmean.reshape(bsz, nb, aw), bias)


def _s5_operators(lambda_re, lambda_im, log_dt, b_re, b_im, c_re, c_im, d_skip, n_chunks):
    hi = lax.Precision.HIGHEST
    L, G, P, C = SSM_CHUNK, SSM_GROUPS, SSM_STATE, SSM_GROUP
    lam_re = jnp.minimum(lambda_re.astype(F32), -1e-4)
    lam_im = lambda_im.astype(F32)
    dt = jnp.exp(log_dt.astype(F32))[:, None]
    z_re, z_im = lam_re * dt, lam_im * dt

    def a_pow(nvec):
        nv = jnp.asarray(nvec, F32)[:, None, None]
        mag = jnp.exp(nv * z_re)
        return mag * jnp.cos(nv * z_im), mag * jnp.sin(nv * z_im)

    a_re, a_im = a_pow([1.0])
    a_re, a_im = a_re[0], a_im[0]
    den = lam_re * lam_re + lam_im * lam_im
    nr = a_re - 1.0
    coef_re = (nr * lam_re + a_im * lam_im) / den
    coef_im = (a_im * lam_re - nr * lam_im) / den
    br, bi = b_re.astype(F32), b_im.astype(F32)
    bbar_re = coef_re[..., None] * br - coef_im[..., None] * bi
    bbar_im = coef_re[..., None] * bi + coef_im[..., None] * br
    cr, ci = c_re.astype(F32), c_im.astype(F32)

    pw_re, pw_im = a_pow(np.arange(L + 1))
    cb_re = cr[None] * pw_re[:, :, None, :] - ci[None] * pw_im[:, :, None, :]
    cb_im = cr[None] * pw_im[:, :, None, :] + ci[None] * pw_re[:, :, None, :]
    kern = jnp.einsum('jgop,gpi->gijo', jnp.concatenate([cb_re[:L], -cb_im[:L]], axis=-1),
                      jnp.concatenate([bbar_re, bbar_im], axis=1), precision=hi).reshape(G, C, L * C)
    t_op = jnp.stack([jnp.pad(kern[:, :, :(L - s) * C], ((0, 0), (0, 0), (s * C, 0))) for s in range(L)], axis=1)
    d_g = d_skip.astype(F32).reshape(G, 1, C, 1)
    on_diag = (lax.broadcasted_iota(jnp.int32, (1, L, C, L * C), 3)
               == lax.broadcasted_iota(jnp.int32, (1, L, C, L * C), 1) * C
               + lax.broadcasted_iota(jnp.int32, (1, L, C, L * C), 2))
    t_op = (t_op + jnp.where(on_diag, d_g, 0.0)).reshape(G, L * C, L * C)

    rp_re, rp_im = jnp.flip(pw_re[:L], 0), jnp.flip(pw_im[:L], 0)
    p_re = rp_re[..., None] * bbar_re[None] - rp_im[..., None] * bbar_im[None]
    p_im = rp_re[..., None] * bbar_im[None] + rp_im[..., None] * bbar_re[None]
    p_op = jnp.concatenate([p_re, p_im], axis=2)
    p_op = p_op.transpose(1, 0, 3, 2).reshape(G, L * C, 2 * P)

    q_re = cb_re[1:].transpose(1, 3, 0, 2)
    q_im = -cb_im[1:].transpose(1, 3, 0, 2)
    q_op = jnp.concatenate([q_re, q_im], axis=1).reshape(G, 2 * P, L * C)

    n_steps = max(1, int(math.ceil(math.log2(n_chunks))))
    dk_re, dk_im = a_pow([float(L * 2 ** k) for k in range(n_steps)])
    GB = LANES // C
    NB = G // GB
    lc = np.arange(L * C)
    wide = np.arange(L * LANES)
    expand_tc = jnp.asarray((lc[:, None] // C == wide[None, :] // LANES) & (lc[:, None] % C == wide[None, :] % C), BF16)
    st = np.arange(2 * P)
    wide_st = np.arange(2 * GB * P)
    expand_st = jnp.asarray((st[:, None] // P == wide_st[None, :] // (GB * P))
                            & (st[:, None] % P == wide_st[None, :] % P), BF16)
    g_of_wide = (jnp.arange(L * LANES) // C) % GB
    g_of_state = (jnp.arange(2 * GB * P) // P) % GB

    def widen(rows, expand, g_row, g_col):
        full = jnp.einsum('brk,kc->brc', rows.astype(BF16), expand, preferred_element_type=F32)
        return jnp.where(g_row[:, None] == g_col[None, :], full, 0.0).astype(BF16)

    t_rows = t_op.reshape(NB, GB, L, C, L * C).transpose(0, 2, 1, 3, 4).reshape(NB, L * LANES, L * C)
    p_rows = p_op.reshape(NB, GB, L, C, 2 * P).transpose(0, 2, 1, 3, 4).reshape(NB, L * LANES, 2 * P)
    q_rows = q_op.reshape(NB, GB, 2, P, L * C).transpose(0, 2, 1, 3, 4).reshape(NB, 2 * GB * P, L * C)
    t_big = widen(t_rows, expand_tc, g_of_wide, g_of_wide)
    p_big = widen(p_rows, expand_st, g_of_wide, g_of_state)
    q_big = widen(q_rows, expand_tc, g_of_state, g_of_wide)
    dk_re = dk_re.reshape(-1, NB, GB * P)
    dk_im = dk_im.reshape(-1, NB, GB * P)
    a1 = jnp.concatenate([dk_re, dk_re], axis=-1).transpose(1, 0, 2)
    a2 = jnp.concatenate([-dk_im, dk_im], axis=-1).transpose(1, 0, 2)
    return t_big.astype(BF16), p_big.astype(BF16), q_big.astype(BF16), a1, a2


def _s5_kernel(x_ref, t_ref, p_ref, q_ref, a1_ref, a2_ref, y_ref, *, n_chunks):
    x = x_ref[0]
    s = jnp.dot(x, p_ref[0], preferred_element_type=F32)
    rows, width = s.shape
    chunk = lax.broadcasted_iota(jnp.int32, (rows, width), 0) % n_chunks
    a1 = a1_ref[0]
    a2 = a2_ref[0]
    h = jnp.where(chunk >= 1, pltpu.roll(s, 1, axis=0), 0.0)
    for kk in range(a1.shape[0]):
        dist = 2 ** kk
        if dist >= n_chunks:
            break
        hs = jnp.where(chunk >= dist, pltpu.roll(h, dist, axis=0), 0.0)
        h = h + a1[kk:kk + 1, :] * hs + a2[kk:kk + 1, :] * pltpu.roll(hs, width // 2, axis=1)
    hb = h.astype(BF16)
    step = 2 * LANES
    for t in range(x.shape[1] // step):
        hi = (t + 1) * step
        y_ref[0, :, t * step:hi] = (jnp.dot(x[:, :hi], t_ref[0, :hi, t * step:hi], preferred_element_type=F32)
                                    + jnp.dot(hb, q_ref[0, :, t * step:hi], preferred_element_type=F32))


def _s5_scan(x_chunks, ops, bsz, seq):
    t_big, p_big, q_big, a1, a2 = ops
    nblk, rows, w = x_chunks.shape
    nc = seq // SSM_CHUNK
    sw = p_big.shape[2]
    spb = S5_SEQS_PER_STEP if bsz % S5_SEQS_PER_STEP == 0 else 1
    col = lambda cb, b: (cb, 0, 0)
    return pl.pallas_call(
        functools.partial(_s5_kernel, n_chunks=nc),
        out_shape=jax.ShapeDtypeStruct((nblk, rows, w), F32),
        grid=(nblk, bsz // spb),
        in_specs=[pl.BlockSpec((1, spb * nc, w), lambda cb, b: (cb, b, 0)),
                  pl.BlockSpec((1, w, w), col),
                  pl.BlockSpec((1, w, sw), col),
                  pl.BlockSpec((1, sw, w), col),
                  pl.BlockSpec((1,) + a1.shape[1:], col),
                  pl.BlockSpec((1,) + a2.shape[1:], col)],
        out_specs=pl.BlockSpec((1, spb * nc, w), lambda cb, b: (cb, b, 0)),
        compiler_params=_cparams(2),
        name="s5_scan",
    )(x_chunks, t_big, p_big, q_big, a1, a2)


def _gelu_tanh(x):
    return 0.5 * x * (1.0 + jnp.tanh(math.sqrt(2.0 / math.pi) * (x + 0.044715 * (x * x * x))))


def _mix_kernel(x_ref, attn_ref, yssm_ref, ga_ref, gs_ref, mod_ref, ln_ref,
                wglu_ref, bglu_ref, wua_ref, wus_ref, wout_ref, wrt_ref, wsgu_ref, wsd_ref,
                h2_ref, base_ref, score_ref, y_scr):
    n_chunk = yssm_ref.shape[1]
    for cb in range(yssm_ref.shape[0]):
        for tau in range(SSM_CHUNK):
            y_scr[cb, pl.ds(tau, n_chunk, stride=SSM_CHUNK), :] = yssm_ref[cb, :, tau * LANES:(tau + 1) * LANES]
    g = _gelu_tanh(jnp.concatenate([y_scr[cb] for cb in range(yssm_ref.shape[0])], axis=1))
    glu = g * _sigmoid(_bdot(g, wglu_ref[...]) + bglu_ref[...])
    y_attn = jnp.dot(attn_ref[...], wua_ref[...], preferred_element_type=F32)
    y_ssm = _bdot(glu, wus_ref[...])
    mixed = _sigmoid(ga_ref[...].astype(F32)) * y_attn + _sigmoid(gs_ref[...].astype(F32)) * y_ssm
    gate1 = mod_ref[0, 2:3, :]
    x1 = x_ref[...] + gate1 * _bdot(mixed, wout_ref[...])
    h2 = _modulated_norm(x1, ln_ref[...], mod_ref[0, 3:4, :], mod_ref[0, 4:5, :])
    _store_rows_as_tiles(h2_ref, h2)
    h2b = h2.astype(BF16)
    score_ref[...] = _sigmoid(_bdot_nt(wrt_ref[...], h2b))
    gu = jnp.dot(h2b, wsgu_ref[...], preferred_element_type=F32)
    sd = wsd_ref.shape[0]
    shared = _bdot(_silu(gu[:, :sd]) * gu[:, sd:], wsd_ref[...])
    base_ref[...] = x1 + mod_ref[0, 5:6, :] * shared


def _mix(xf, attn, yssm, ga, gs, mod3, ln2_g, w, seq):
    n, d = xf.shape
    tm = TOKEN_TILE
    tiles_per_seq = seq // tm
    row = lambda i: (i, 0)
    const = lambda i: (0, 0)
    nt = d // LANES
    weights = [w["glu"], w["b_glu"], w["up_attn"], w["up_ssm"], w["out"], w["router_t"], w["sh_gu"], w["sh_down"]]
    return pl.pallas_call(
        _mix_kernel,
        out_shape=(jax.ShapeDtypeStruct((n * nt, LANES), F32),
                   jax.ShapeDtypeStruct((n, d), F32),
                   jax.ShapeDtypeStruct((N_EXPERTS, n), F32)),
        grid=(n // tm,),
        in_specs=[pl.BlockSpec((tm, d), row),
                  pl.BlockSpec((tm, attn.shape[1]), row),
                  pl.BlockSpec((yssm.shape[0], tm // SSM_CHUNK, yssm.shape[2]), lambda i: (0, i, 0)),
                  pl.BlockSpec((tm, d), row),
                  pl.BlockSpec((tm, d), row),
                  pl.BlockSpec((1, 6, d), lambda i: (i // tiles_per_seq, 0, 0)),
                  pl.BlockSpec((1, d), const)] + [pl.BlockSpec(a.shape, const) for a in weights],
        out_specs=(pl.BlockSpec((tm * nt, LANES), row), pl.BlockSpec((tm, d), row),
                   pl.BlockSpec((N_EXPERTS, tm), lambda i: (0, i))),
        scratch_shapes=[pltpu.VMEM((yssm.shape[0], tm, LANES), F32)],
        compiler_params=_cparams(1),
        name="mix",
    )(xf, attn, yssm, ga, gs, mod3, ln2_g.reshape(1, d), *weights)


def _route_kernel(score_ref, bias_ref, tri_ref, idx_ref, w_ref, rank_ref, cnt_ref, carry_ref):
    @pl.when(pl.program_id(0) == 0)
    def _():
        carry_ref[...] = jnp.zeros_like(carry_ref)

    scores = score_ref[...]
    ne, tn = scores.shape
    biased = scores + bias_ref[...]
    gsz = GROUP_SIZE
    sub = lax.broadcasted_iota(jnp.int32, (gsz, tn), 0)
    group_score = []
    for g in range(N_EXPERT_GROUPS):
        sg = biased[g * gsz:(g + 1) * gsz, :]
        m1 = jnp.max(sg, axis=0, keepdims=True)
        first = jnp.min(jnp.where(sg == m1, sub, gsz), axis=0, keepdims=True)
        m2 = jnp.max(jnp.where(sub == first, -jnp.inf, sg), axis=0, keepdims=True)
        group_score.append(m1 + m2)
    group_rows = []
    for g in range(N_EXPERT_GROUPS):
        beaten = jnp.zeros((1, tn), jnp.int32)
        for o in range(N_EXPERT_GROUPS):
            if o == g:
                continue
            wins = (group_score[o] > group_score[g])
            if o < g:
                wins = wins | (group_score[o] == group_score[g])
            beaten = beaten + jnp.where(wins, 1, 0)
        group_rows.append(jnp.broadcast_to(beaten < TOPK_GROUPS, (gsz, tn)))
    allowed = jnp.concatenate(group_rows, axis=0)
    cur = jnp.where(allowed, biased, MASK_VALUE)
    eio = lax.broadcasted_iota(jnp.int32, (ne, tn), 0)
    idx_rows, w_rows, hits = [], [], []
    for _ in range(TOP_K):
        vmax = jnp.max(cur, axis=0, keepdims=True)
        eidx = jnp.min(jnp.where(cur == vmax, eio, ne), axis=0, keepdims=True)
        hit = eio == eidx
        w_rows.append(jnp.sum(jnp.where(hit, scores, 0.0), axis=0, keepdims=True))
        idx_rows.append(eidx)
        hits.append(hit)
        cur = jnp.where(hit, -jnp.inf, cur)
    wts = jnp.concatenate(w_rows, axis=0)
    idx_ref[...] = jnp.concatenate(idx_rows, axis=0)
    w_ref[...] = wts / jnp.sum(wts, axis=0, keepdims=True) * ROUTED_SCALE

    onehot = jnp.zeros((ne, tn), F32)
    for hit in hits:
        onehot = onehot + jnp.where(hit, 1.0, 0.0)
    earlier = carry_ref[...] + _bdot(onehot, tri_ref[...])
    rank_rows = [jnp.sum(jnp.where(hit, earlier, 0.0), axis=0, keepdims=True) for hit in hits]
    rank_ref[...] = jnp.concatenate(rank_rows, axis=0).astype(jnp.int32)
    carry_ref[...] = carry_ref[...] + jnp.sum(onehot, axis=1, keepdims=True)
    cnt_ref[...] = carry_ref[...]


def _route(scores_t, router_bias):
    ne, n = scores_t.shape
    tn = ROUTE_TILE
    tri = jnp.asarray(np.arange(tn)[:, None] < np.arange(tn)[None, :], BF16)
    tok = lambda i: (0, i)
    const = lambda i: (0, 0)
    return pl.pallas_call(
        _route_kernel,
        out_shape=(jax.ShapeDtypeStruct((TOP_K, n), jnp.int32), jax.ShapeDtypeStruct((TOP_K, n), F32),
                   jax.ShapeDtypeStruct((TOP_K, n), jnp.int32), jax.ShapeDtypeStruct((ne, 1), F32)),
        grid=(n // tn,),
        in_specs=[pl.BlockSpec((ne, tn), tok), pl.BlockSpec((ne, 1), const), pl.BlockSpec((tn, tn), const)],
        out_specs=(pl.BlockSpec((TOP_K, tn), tok), pl.BlockSpec((TOP_K, tn), tok),
                   pl.BlockSpec((TOP_K, tn), tok), pl.BlockSpec((ne, 1), const)),
        scratch_shapes=[pltpu.VMEM((ne, 1), F32)],
        compiler_params=_cparams(1),
        name="route",
    )(scores_t, router_bias.reshape(ne, 1).astype(F32), tri)


def _dest_kernel(idx_ref, rank_ref, poffs_ref, dest_ref):
    idx = idx_ref[...]
    ne = poffs_ref.shape[0]
    eio = lax.broadcasted_iota(jnp.int32, (ne, idx.shape[1]), 0)
    poffs = poffs_ref[...]
    rows = [jnp.sum(jnp.where(eio == idx[kk:kk + 1, :], poffs, 0.0), axis=0, keepdims=True)
            for kk in range(idx.shape[0])]
    dest_ref[...] = rank_ref[...] + jnp.concatenate(rows, axis=0).astype(jnp.int32)


def _dest_slots(idx_t, rank_t, poffs):
    k, n = idx_t.shape
    tn = ROUTE_TILE
    ne = poffs.shape[0]
    tok = lambda i: (0, i)
    return pl.pallas_call(
        _dest_kernel,
        out_shape=jax.ShapeDtypeStruct((k, n), jnp.int32),
        grid=(n // tn,),
        in_specs=[pl.BlockSpec((k, tn), tok), pl.BlockSpec((k, tn), tok), pl.BlockSpec((ne, 1), lambda i: (0, 0))],
        out_specs=pl.BlockSpec((k, tn), tok),
        compiler_params=_cparams(1),
        name="dest_slots",
    )(idx_t, rank_t, poffs.astype(F32).reshape(ne, 1))


def _dispatch_tables(idx_t, counts, n_blocks):
    k, n = idx_t.shape
    n_assign = k * n
    rb = ROW_BLOCK
    ne = counts.shape[0]
    counts = counts.reshape(ne).astype(jnp.int32)
    offs = jnp.cumsum(counts) - counts
    nblk = (counts + rb - 1) // rb
    bend = jnp.cumsum(nblk)
    bstart = bend - nblk
    blocks = jnp.arange(n_blocks, dtype=jnp.int32)
    blk_e = jnp.minimum(jnp.sum((bend[None, :] <= blocks[:, None]).astype(jnp.int32), axis=1), ne - 1)
    mine = blk_e[:, None] == jnp.arange(ne, dtype=jnp.int32)[None, :]
    src_start = jnp.sum(jnp.where(mine, (offs - bstart * rb)[None, :], 0), axis=1) + blocks * rb
    tok = lax.broadcasted_iota(jnp.int32, (k, n), 1)
    kk = lax.broadcasted_iota(jnp.int32, (k, n), 0)
    assert ne * n_assign < 2 ** 31
    keys = lax.sort((idx_t * n_assign + tok * k + kk).reshape(n_assign), is_stable=False)
    tok_sorted = (keys % n_assign) // k
    experts = jnp.arange(ne, dtype=jnp.int32)
    nonempty = nblk > 0
    run = jnp.cumsum(nonempty.astype(jnp.int32)) - 1
    later = (experts[None, :] > experts[:, None]) & nonempty[None, :]
    next_e = jnp.min(jnp.where(later, experts[None, :], ne), axis=1)
    next_e = jnp.where(next_e == ne, -1, next_e)
    per_expert = (run % 2) * 2 + (next_e + 1) * 4
    winfo = (jnp.sum(jnp.where(mine, per_expert[None, :], 0), axis=1)
             + (blocks == jnp.sum(jnp.where(mine, bstart[None, :], 0), axis=1)).astype(jnp.int32))
    return tok_sorted, blk_e, src_start, bend[ne - 1:ne], winfo.astype(jnp.int32), bstart * rb


def _ffn_kernel(tok_ref, blk_e_ref, start_ref, used_ref, winfo_ref, h2_hbm, wg_hbm, wu_hbm, wd_hbm, y_ref,
                xbuf, wg_buf, wu_buf, wd_buf, sems, wsems):
    b = pl.program_id(0)
    rb = ROW_BLOCK
    d = wg_hbm.shape[1]
    nt = d // LANES
    n_assign = tok_ref.shape[0]
    n_used = used_ref[0]
    n_blocks = pl.num_programs(0)

    info = winfo_ref[b]
    first_of_expert = (info & 1) == 1
    wslot = (info >> 1) & 1
    next_expert = (info >> 2) - 1

    def weight_copies(e, s):
        return [pltpu.make_async_copy(src.at[e], dst.at[s], wsems.at[s])
                for src, dst in ((wg_hbm, wg_buf), (wu_hbm, wu_buf), (wd_hbm, wd_buf))]

    @pl.when(b == 0)
    def _():
        for c in weight_copies(blk_e_ref[0], wslot):
            c.start()

    @pl.when((b < n_used) & first_of_expert)
    def _():
        for c in weight_copies(blk_e_ref[b], wslot):
            c.wait()

        @pl.when(next_expert >= 0)
        def _():
            for c in weight_copies(next_expert, 1 - wslot):
                c.start()

    def start_rows(blk, s, inline):
        start = start_ref[blk]

        def one(r, queue):
            tok = tok_ref[jnp.minimum(start + r, n_assign - 1)]
            pltpu.make_async_copy(h2_hbm.at[pl.ds(pl.multiple_of(tok * nt, nt), nt)],
                                  xbuf.at[s, pl.ds(pl.multiple_of(r * nt, nt), nt)], sems.at[s]
                                  ).start(priority=queue)

        if inline:
            for r in range(rb):
                one(r, r % DMA_QUEUES)
        else:
            def group(g, c):
                for i in range(ISSUE_UNROLL):
                    one(g * ISSUE_UNROLL + i, i % DMA_QUEUES)
                return c

            lax.fori_loop(0, rb // ISSUE_UNROLL, group, 0)

    def wait_rows(s):
        pltpu.make_async_copy(h2_hbm.at[pl.ds(0, rb * nt)], xbuf.at[s], sems.at[s]).wait()

    nbuf = xbuf.shape[0]
    ahead = nbuf - 1

    @pl.when(b == 0)
    def _():
        for a in range(ahead):
            start_rows(jnp.minimum(a, n_blocks - 1), a, False)

    for s in range(nbuf):
        @pl.when((b < n_used) & (b % nbuf == s))
        def _(s=s):
            wait_rows(s)
            xb = _load_rows_from_tiles(xbuf.at[s], rb, d).astype(BF16)
            start_rows(jnp.minimum(b + ahead, n_blocks - 1), (s + ahead) % nbuf, True)
            hg = jnp.dot(xb, wg_buf[wslot].astype(BF16), preferred_element_type=F32)
            hu = jnp.dot(xb, wu_buf[wslot].astype(BF16), preferred_element_type=F32)
            _store_rows_as_tiles(y_ref, _bdot(_silu(hg) * hu, wd_buf[wslot]))

    @pl.when(b == n_used - 1)
    def _():
        for a in range(1, nbuf):
            wait_rows((b + a) % nbuf)

    @pl.when(b >= n_used)
    def _():
        y_ref[...] = jnp.zeros_like(y_ref)


def _expert_ffn(h2_tiles, tok_sorted, blk_e, src_start, n_used, winfo, w_gate, w_up, w_down):
    n_blocks = blk_e.shape[0]
    rb = ROW_BLOCK
    _, d, ed = w_gate.shape
    nt = d // LANES
    hbm = pl.BlockSpec(memory_space=pl.ANY)
    grid_spec = pltpu.PrefetchScalarGridSpec(
        num_scalar_prefetch=5,
        grid=(n_blocks,),
        in_specs=[hbm, hbm, hbm, hbm],
        out_specs=pl.BlockSpec((rb * nt, LANES), lambda i, tok, be, st, nu, wi: (i, 0)),
        scratch_shapes=[pltpu.VMEM((FFN_ROW_BUFFERS, rb * nt, LANES), F32),
                        pltpu.VMEM((2, d, ed), w_gate.dtype), pltpu.VMEM((2, d, ed), w_up.dtype),
                        pltpu.VMEM((2, ed, d), w_down.dtype),
                        pltpu.SemaphoreType.DMA((FFN_ROW_BUFFERS,)), pltpu.SemaphoreType.DMA((2,))],
    )
    return pl.pallas_call(
        _ffn_kernel,
        out_shape=jax.ShapeDtypeStruct((n_blocks * rb * nt, LANES), F32),
        grid_spec=grid_spec,
        compiler_params=_cparams(1),
        name="expert_ffn",
    )(tok_sorted, blk_e, src_start, n_used, winfo, h2_tiles, w_gate, w_up, w_down)


def _combine_kernel(dest_ref, y_hbm, base_ref, wt_ref, mod_ref, o_ref, buf, sems):
    i = pl.program_id(0)
    n_steps = pl.num_programs(0)
    tm, d = base_ref.shape
    k = wt_ref.shape[1]
    nt = d // LANES
    n_rows = k * tm
    nbuf = buf.shape[0]
    ahead = nbuf - 1

    def issue(step, s, inline):
        first = step * n_rows

        def one(r, queue):
            src = dest_ref[first + r]
            pltpu.make_async_copy(y_hbm.at[pl.ds(pl.multiple_of(src * nt, nt), nt)],
                                  buf.at[s, pl.ds(pl.multiple_of(r * nt, nt), nt)], sems.at[s]
                                  ).start(priority=queue)

        if inline:
            for r in range(n_rows):
                one(r, r % DMA_QUEUES)
        else:
            def group(g, c):
                for j in range(ISSUE_UNROLL):
                    one(g * ISSUE_UNROLL + j, j % DMA_QUEUES)
                return c

            lax.fori_loop(0, n_rows // ISSUE_UNROLL, group, 0)

    def wait_rows(s):
        pltpu.make_async_copy(y_hbm.at[pl.ds(0, n_rows * nt)], buf.at[s], sems.at[s]).wait()

    @pl.when(i == 0)
    def _():
        for a in range(ahead):
            issue(jnp.minimum(a, n_steps - 1), a, False)

    for s in range(nbuf):
        @pl.when(i % nbuf == s)
        def _(s=s):
            wait_rows(s)
            issue(jnp.minimum(i + ahead, n_steps - 1), (s + ahead) % nbuf, True)
            rows = buf.at[s]
            wts = wt_ref[...]
            gate2 = mod_ref[0, 5:6, :]
            for c in range(nt):
                cols = slice(c * LANES, (c + 1) * LANES)
                routed = jnp.zeros((tm, LANES), F32)
                for kk in range(k):
                    routed = routed + wts[:, kk:kk + 1] * rows[pl.ds(kk * tm * nt + c, tm, stride=nt), :]
                o_ref[:, cols] = base_ref[:, cols] + gate2[:, cols] * routed

    @pl.when(i == n_steps - 1)
    def _():
        for a in range(1, nbuf):
            wait_rows((i + a) % nbuf)


def _combine(y_tiles, dest_t, w_t, base, mod3, seq):
    n, d = base.shape
    tm = COMBINE_TILE
    k = dest_t.shape[0]
    nt = d // LANES
    tiles_per_seq = seq // tm
    dest_tiles = dest_t.reshape(k, n // tm, tm).transpose(1, 0, 2).reshape(n * k)
    grid_spec = pltpu.PrefetchScalarGridSpec(
        num_scalar_prefetch=1,
        grid=(n // tm,),
        in_specs=[pl.BlockSpec(memory_space=pl.ANY),
                  pl.BlockSpec((tm, d), lambda i, dst: (i, 0)),
                  pl.BlockSpec((tm, k), lambda i, dst: (i, 0)),
                  pl.BlockSpec((1, 6, d), lambda i, dst: (i // tiles_per_seq, 0, 0))],
        out_specs=pl.BlockSpec((tm, d), lambda i, dst: (i, 0)),
        scratch_shapes=[pltpu.VMEM((COMBINE_ROW_BUFFERS, k * tm * nt, LANES), F32),
                        pltpu.SemaphoreType.DMA((COMBINE_ROW_BUFFERS,))],
    )
    return pl.pallas_call(
        _combine_kernel,
        out_shape=jax.ShapeDtypeStruct((n, d), F32),
        grid_spec=grid_spec,
        compiler_params=_cparams(1),
        name="combine",
    )(dest_tiles, y_tiles, base, w_t.T, mod3)


def _hybrid_layer(x, cond, rel_bias, w_ada, b_ada, ln1_g, w_in, q_norm_g, k_norm_g,
                  ssm_lambda_re, ssm_lambda_im, ssm_log_dt, ssm_b_re, ssm_b_im, ssm_c_re, ssm_c_im,
                  ssm_d, ssm_w_glu, ssm_b_glu, w_up_attn, w_up_ssm, w_out, ln2_g,
                  w_router, router_bias, w_exp_gate, w_exp_up, w_exp_down,
                  w_sh_gate, w_sh_up, w_sh_down):
    bsz, seq, d = x.shape
    n = bsz * seq
    xf = x.reshape(n, d)
    mod3 = _adaln(cond, w_ada, b_ada).reshape(bsz, 6, d)

    q_gain = jnp.tile(q_norm_g.astype(F32), ATTN_HEADS).reshape(1, ATTN_WIDTH)
    k_gain = jnp.tile(k_norm_g.astype(F32), ATTN_HEADS).reshape(1, ATTN_WIDTH)
    q, k, v, u, ga, gs, kmean = _inproj(xf, mod3, ln1_g, w_in.astype(BF16), q_gain, k_gain, seq)

    attn = _moba_attention(q, k, v, kmean, rel_bias, bsz, seq)
    ops = _s5_operators(ssm_lambda_re, ssm_lambda_im, ssm_log_dt, ssm_b_re, ssm_b_im,
                        ssm_c_re, ssm_c_im, ssm_d, seq // SSM_CHUNK)
    yssm = _s5_scan(u, ops, bsz, seq)

    weights = {
        "glu": ssm_w_glu.astype(BF16), "b_glu": ssm_b_glu.astype(F32).reshape(1, -1),
        "up_attn": w_up_attn.astype(BF16), "up_ssm": w_up_ssm.astype(BF16), "out": w_out.astype(BF16),
        "router_t": w_router.T.astype(BF16),
        "sh_gu": jnp.concatenate([w_sh_gate, w_sh_up], axis=1).astype(BF16),
        "sh_down": w_sh_down.astype(BF16),
    }
    h2, base, scores_t = _mix(xf, attn, yssm, ga, gs, mod3, ln2_g, weights, seq)

    idx_t, w_t, rank_t, counts = _route(scores_t, router_bias)
    n_blocks = -(-(n * TOP_K) // ROW_BLOCK) + N_EXPERTS
    tok_sorted, blk_e, src_start, n_used, winfo, poffs = _dispatch_tables(idx_t, counts, n_blocks)
    dest_t = _dest_slots(idx_t, rank_t, poffs)
    y_tiles = _expert_ffn(h2, tok_sorted, blk_e, src_start, n_used, winfo, w_exp_gate, w_exp_up, w_exp_down)
    out = _combine(y_tiles, dest_t, w_t, base, mod3, seq)
    return out.reshape(bsz, seq, d)


def kernel(x, c, rel_bias, w_ada, b_ada, ln1_g, w_in, q_norm_g, k_norm_g, ssm_lambda_re, ssm_lambda_im, ssm_log_dt, ssm_b_re, ssm_b_im, ssm_c_re, ssm_c_im, ssm_d, ssm_w_glu, ssm_b_glu, w_up_attn, w_up_ssm, w_out, ln2_g, w_router, router_bias, w_exp_gate, w_exp_up, w_exp_down, w_sh_gate, w_sh_up, w_sh_down):
    for l in range(w_ada.shape[0]):
        x = _hybrid_layer(x, c, rel_bias, w_ada[l], b_ada[l], ln1_g[l], w_in[l], q_norm_g[l], k_norm_g[l],
                          ssm_lambda_re[l], ssm_lambda_im[l], ssm_log_dt[l], ssm_b_re[l], ssm_b_im[l],
                          ssm_c_re[l], ssm_c_im[l], ssm_d[l], ssm_w_glu[l], ssm_b_glu[l],
                          w_up_attn[l], w_up_ssm[l], w_out[l], ln2_g[l], w_router[l], router_bias[l],
                          w_exp_gate[l], w_exp_up[l], w_exp_down[l], w_sh_gate[l], w_sh_up[l], w_sh_down[l])
    return x
```
